```python
import jax, jax.numpy as jnp
from jax import lax
import numpy as np

D_MODEL = 1024
BATCH = 1
SEQ = 16384
DEPTH = 1

N_META = 16
CHUNK = 64
EPS = 1e-6
M_HEADS = 4
M_DV = D_MODEL // M_HEADS
M_DQK = M_DV // 2
M_QK = M_HEADS * M_DQK
M_V = M_HEADS * M_DV
CONV_W = 4
F_BIAS = 3.0
G_HEADS = 4
G_DV = D_MODEL // G_HEADS
G_DK = G_DV // 2
G_QK = G_HEADS * G_DK
G_V = G_HEADS * G_DV
G_RANK = 16
G_TAU = 16.0
D_FF = ((8 * D_MODEL + 3 * 256 - 1) // (3 * 256)) * 256
PROJ_WIDTHS = (M_QK, M_QK, M_V, M_HEADS, M_HEADS, M_V, G_QK, G_QK, G_V, G_RANK, G_V, D_MODEL, D_MODEL)
N_PROJ = sum(PROJ_WIDTHS)

kernel_name = 'hybrid_mlstm_gla_block'


def rmsnorm(x, g):
    xf = x.astype(jnp.float32)
    y = xf * lax.rsqrt(jnp.mean(xf * xf, axis=-1, keepdims=True) + EPS)
    return (y * g.astype(jnp.float32)).astype(x.dtype)


def head_rmsnorm(h, g):
    y = h * lax.rsqrt(jnp.mean(h * h, axis=-1, keepdims=True) + EPS)
    return y * g.astype(jnp.float32)


def split_cols(p):
    idx = np.cumsum(np.array(PROJ_WIDTHS))[:-1]
    return jnp.split(p, idx, axis=-1)


def to_chunks(t, n_heads):
    b, tp = t.shape[:2]
    t = t.reshape(b, tp // CHUNK, CHUNK, n_heads, -1)
    return jnp.transpose(t, (0, 3, 1, 2, 4)).astype(jnp.float32)


def from_chunks(t):
    b, h, nc, l, d = t.shape
    return jnp.transpose(t, (0, 2, 3, 1, 4)).reshape(b, nc * l, h, d)


def causal_depthwise_conv(x, w, bias):
    k = w.shape[0]
    y = lax.conv_general_dilated(x, w[:, None, :].astype(x.dtype), window_strides=(1,),
                                 padding=[(k - 1, 0)], dimension_numbers=('NWC', 'WIO', 'NWC'),
                                 feature_group_count=x.shape[-1])
    return y + bias.astype(x.dtype)


def mlstm_chunkwise(q, k, v, logi, logf):
    L = q.shape[3]
    b = jnp.cumsum(logf, axis=-1)
    g = b[..., -1]
    causal = jnp.tril(jnp.ones((L, L), dtype=bool))
    dmat = jnp.where(causal, b[..., :, None] - b[..., None, :] + logi[..., None, :], -jnp.inf)
    wlog = g[..., None] - b + logi

    def step(carry, inp):
        C, n, m = carry
        kc, vc, wc, gc = inp
        m_new = jnp.maximum(gc + m, jnp.max(wc, axis=-1))
        a = jnp.exp(gc + m - m_new)
        w = jnp.exp(wc - m_new[..., None])
        C_new = a[..., None, None] * C + jnp.einsum('bhl,bhlv,bhlk->bhvk', w, vc, kc)
        n_new = a[..., None] * n + jnp.einsum('bhl,bhlk->bhk', w, kc)
        return (C_new, n_new, m_new), (C, n, m)

    bsz, nh, _, _, dqk = q.shape
    dv = v.shape[-1]
    init = (jnp.zeros((bsz, nh, dv, dqk), jnp.float32), jnp.zeros((bsz, nh, dqk), jnp.float32),
            jnp.zeros((bsz, nh), jnp.float32))
    xs = (jnp.moveaxis(k, 2, 0), jnp.moveaxis(v, 2, 0), jnp.moveaxis(wlog, 2, 0), jnp.moveaxis(g, 2, 0))
    _, (Cs, ns, ms) = lax.scan(step, init, xs)
    Cs = jnp.moveaxis(Cs, 0, 2)
    ns = jnp.moveaxis(ns, 0, 2)
    ms = jnp.moveaxis(ms, 0, 2)

    inter_log = b + ms[..., None]
    m_row = jnp.maximum(inter_log, jnp.max(dmat, axis=-1))
    sim = jnp.einsum('bhcjd,bhcsd->bhcjs', q, k)
    wts = jnp.exp(dmat - m_row[..., None]) * sim
    a_inter = jnp.exp(inter_log - m_row)
    num = (a_inter[..., None] * jnp.einsum('bhcvd,bhcjd->bhcjv', Cs, q)
           + jnp.einsum('bhcjs,bhcsv->bhcjv', wts, v))
    den = a_inter * jnp.einsum('bhcd,bhcjd->bhcj', ns, q) + jnp.sum(wts, axis=-1)
    return num / jnp.maximum(jnp.abs(den), jnp.exp(-m_row))[..., None]


def gla_chunked(q, k, v, loga):
    L = q.shape[3]
    bc = jnp.cumsum(loga, axis=3)
    btot = bc[..., -1, :]
    q_dec = q * jnp.exp(bc)
    k_inv = k * jnp.exp(-bc)
    k_end = k * jnp.exp(btot[..., None, :] - bc)
    causal = jnp.tril(jnp.ones((L, L), dtype=bool))
    att = jnp.where(causal, jnp.einsum('bhcjd,bhcsd->bhcjs', q_dec, k_inv), 0.0)
    intra = jnp.einsum('bhcjs,bhcsv->bhcjv', att, v)

    def step(S, inp):
        ke, vc, bt = inp
        S_new = jnp.exp(bt)[..., None] * S + jnp.einsum('bhlk,bhlv->bhkv', ke, vc)
        return S_new, S

    bsz, nh, _, _, dk = q.shape
    dv = v.shape[-1]
    S0 = jnp.zeros((bsz, nh, dk, dv), jnp.float32)
    _, S_prev = lax.scan(step, S0, (jnp.moveaxis(k_end, 2, 0), jnp.moveaxis(v, 2, 0), jnp.moveaxis(btot, 2, 0)))
    S_prev = jnp.moveaxis(S_prev, 0, 2)
    return intra + jnp.einsum('bhcjk,bhckv->bhcjv', q_dec, S_prev)


def setup_inputs(seed: int = 0) -> dict:
    key = jax.random.key(seed)
    ks = jax.random.split(key, 24)
    f32 = jnp.float32
    nrm = lambda k, shape, s: jax.random.normal(k, shape, f32) * s
    m_gate_b = jnp.stack([nrm(ks[6], (DEPTH, M_HEADS), 0.01),
                          F_BIAS + nrm(ks[7], (DEPTH, M_HEADS), 0.1)], axis=1)
    return {
        'x': nrm(ks[0], (BATCH, SEQ, D_MODEL), 1.0),
        'meta_tokens': nrm(ks[1], (N_META, D_MODEL), 1.0),
        'norm1_g': 1.0 + nrm(ks[2], (DEPTH, D_MODEL), 0.02),
        'w_in': nrm(ks[3], (DEPTH, D_MODEL, N_PROJ), D_MODEL ** -0.5),
        'conv_w': nrm(ks[4], (DEPTH, CONV_W, 2 * M_QK), CONV_W ** -0.5),
        'conv_b': nrm(ks[5], (DEPTH, 2 * M_QK), 0.01),
        'm_gate_b': m_gate_b,
        'g_a2': nrm(ks[8], (DEPTH, G_RANK, G_QK), G_RANK ** -0.5),
        'g_a2_b': nrm(ks[9], (DEPTH, G_QK), 0.01),
        'm_head_g': 1.0 + nrm(ks[10], (DEPTH, M_HEADS, M_DV), 0.02),
        'g_head_g': 1.0 + nrm(ks[11], (DEPTH, G_HEADS, G_DV), 0.02),
        'w_branch_m': nrm(ks[12], (DEPTH, M_V, D_MODEL), M_V ** -0.5),
        'w_branch_g': nrm(ks[13], (DEPTH, G_V, D_MODEL), G_V ** -0.5),
        'w_out': nrm(ks[14], (DEPTH, D_MODEL, D_MODEL), D_MODEL ** -0.5),
        'norm2_g': 1.0 + nrm(ks[15], (DEPTH, D_MODEL), 0.02),
        'w_ff_gate': nrm(ks[16], (DEPTH, D_MODEL, D_FF), D_MODEL ** -0.5),
        'w_ff_up': nrm(ks[17], (DEPTH, D_MODEL, D_FF), D_MODEL ** -0.5),
        'w_ff_down': nrm(ks[18], (DEPTH, D_FF, D_MODEL), D_FF ** -0.5),
        'final_g': 1.0 + nrm(ks[19], (D_MODEL,), 0.02),
    }


def reference(x, meta_tokens, norm1_g, w_in, conv_w, conv_b, m_gate_b, g_a2, g_a2_b, m_head_g, g_head_g,
              w_branch_m, w_branch_g, w_out, norm2_g, w_ff_gate, w_ff_up, w_ff_down, final_g):
    f32 = jnp.float32
    bsz, _, d = x.shape
    dt = x.dtype
    n_pad = CHUNK - N_META
    meta = jnp.broadcast_to(meta_tokens.astype(dt)[None], (bsz, N_META, d))
    h = jnp.concatenate([jnp.zeros((bsz, n_pad, d), dt), meta, x], axis=1)
    tp = h.shape[1]
    valid = (jnp.arange(tp) >= n_pad)[None, :, None]

    for l in range(DEPTH):
        xn = rmsnorm(h, norm1_g[l])
        proj = jnp.where(valid, xn @ w_in[l].astype(dt), 0.0).astype(dt)
        (mq, mk, mv, mi, mf, mo, gq, gk, gv, ga, gr, gate_m, gate_g) = split_cols(proj)

        mqk = jax.nn.silu(causal_depthwise_conv(jnp.concatenate([mq, mk], axis=-1), conv_w[l], conv_b[l]))
        mq, mk = jnp.split(mqk, 2, axis=-1)
        logi = jnp.where(valid, mi.astype(f32) + m_gate_b[l, 0], -jnp.inf)
        logf = jnp.where(valid, jax.nn.log_sigmoid(mf.astype(f32) + m_gate_b[l, 1]), 0.0)
        hm = mlstm_chunkwise(to_chunks(mq, M_HEADS) * (M_DQK ** -0.5), to_chunks(mk, M_HEADS),
                             to_chunks(mv, M_HEADS), to_chunks(logi, M_HEADS)[..., 0],
                             to_chunks(logf, M_HEADS)[..., 0])
        hm = head_rmsnorm(from_chunks(hm), m_head_g[l]) * jax.nn.sigmoid(mo.astype(f32)).reshape(bsz, tp, M_HEADS, M_DV)
        y_m = hm.reshape(bsz, tp, M_V).astype(dt)

        za = ga @ g_a2[l].astype(dt) + g_a2_b[l].astype(dt)
        loga = jnp.where(valid, jax.nn.log_sigmoid(za.astype(f32)) / G_TAU, 0.0)
        hg = gla_chunked(to_chunks(gq, G_HEADS) * (G_DK ** -0.5), to_chunks(gk, G_HEADS),
                         to_chunks(gv, G_HEADS), to_chunks(loga, G_HEADS))
        hg = head_rmsnorm(from_chunks(hg), g_head_g[l]) * jax.nn.silu(gr.astype(f32)).reshape(bsz, tp, G_HEADS, G_DV)
        y_g = hg.reshape(bsz, tp, G_V).astype(dt)

        merged = (jax.nn.sigmoid(gate_m) * (y_m @ w_branch_m[l].astype(dt))
                  + jax.nn.sigmoid(gate_g) * (y_g @ w_branch_g[l].astype(dt)))
        h = h + merged @ w_out[l].astype(dt)

        hn = rmsnorm(h, norm2_g[l])
        ff = jax.nn.silu(hn @ w_ff_gate[l].astype(dt)) * (hn @ w_ff_up[l].astype(dt))
        h = h + ff @ w_ff_down[l].astype(dt)

    out = rmsnorm(h, final_g)
    return out[:, CHUNK:, :]
```

```python
import functools

import jax
import jax.numpy as jnp
from jax import lax
from jax.experimental import pallas as pl
from jax.experimental.pallas import tpu as pltpu

F32 = jnp.float32
BF16 = jnp.bfloat16

D_MODEL = 1024
N_META = 16
CHUNK = 64
EPS = 1e-6
HEADS = 4
DV = D_MODEL // HEADS
DK = DV // 2
QK = HEADS * DK
G_RANK = 16
G_TAU = 16.0
CONV_W = 4
LANES = 128
SUBLANES = 8
NEG_BIG = -1e30
VMEM_LIMIT = 56 * 1024 * 1024

O_MQK, O_MV, O_MO = 0, 1024, 2048
O_GQ, O_GK, O_GV, O_GR = 3072, 3584, 4096, 5120
O_GATE_M, O_GATE_G = 6144, 7168
O_S1, O_S2 = 8192, 8320
N_PACK = 8448
GA_LANE = 8

NT_DIMS = (((1,), (1,)), ((), ()))
TN_DIMS = (((0,), (0,)), ((), ()))


def _dot(a, b):
    return jnp.dot(a, b, preferred_element_type=F32)


def _dot_nt(a, b):
    return lax.dot_general(a, b, NT_DIMS, preferred_element_type=F32)


def _dot_tn(a, b):
    return lax.dot_general(a, b, TN_DIMS, preferred_element_type=F32)


def _split3(x):
    hi = x.astype(BF16)
    r = x - hi.astype(F32)
    mid = r.astype(BF16)
    lo = (r - mid.astype(F32)).astype(BF16)
    return hi, mid, lo


def _chunk_cumsum(tri, x):
    hi, mid, lo = _split3(x)
    return _dot(tri, hi) + _dot(tri, mid) + _dot(tri, lo)


def _log_sigmoid(x):
    return jnp.minimum(x, 0.0) - jnp.log(1.0 + jnp.exp(-jnp.abs(x)))


def _sigmoid(x):
    return 1.0 / (1.0 + jnp.exp(-x))


def _outer_sum_operand(col, first_lane, lane):
    hi, mid, lo = _split3(col)
    pieces = jnp.where(lane == first_lane, hi.astype(F32),
                       jnp.where(lane == first_lane + 1, mid.astype(F32), lo.astype(F32)))
    in_pieces = (lane >= first_lane) & (lane < first_lane + 3)
    return jnp.where(in_pieces, pieces, jnp.where(lane < 6, 1.0, 0.0)).astype(BF16)


def _mixer_kernel(x_ref, g1_ref, w_ref, sb_ref, a2_ref, a2b_ref, cw_ref, cb_ref, mhg_ref, ghg_ref,
                  wbm_ref, wbg_ref, wout_ref, tail0_ref, c0_ref, n0_ref, m0_ref, s0_ref,
                  h1_ref, tail_ref, c_ref, n_ref, m_ref, s_ref,
                  xn_s, qk_s, q_s, k_s, v_s, li_s, b_s, gq_s, gk_s, bc_s, hm_s, hg_s,
                  *, tm, meta):
    nc = tm // CHUNK

    @pl.when(pl.program_id(0) == 0)
    def _():
        qk_s[0:SUBLANES, :] = tail0_ref[...]
        c_ref[...] = c0_ref[...]
        n_ref[...] = n0_ref[...]
        m_ref[...] = m0_ref[...]
        s_ref[...] = s0_ref[...]

    x = x_ref[...]
    xn = x * lax.rsqrt(jnp.mean(x * x, axis=-1, keepdims=True) + EPS) * g1_ref[...]
    if meta:
        valid = lax.broadcasted_iota(jnp.int32, (tm, 1), 0) >= (CHUNK - N_META)
        xn = jnp.where(valid, xn, 0.0)
    xn_s[...] = xn.astype(BF16)

    def proj(off, width):
        return _dot(xn_s[...], w_ref[:, off:off + width])

    row = lax.broadcasted_iota(jnp.int32, (tm, tm), 0)
    col = lax.broadcasted_iota(jnp.int32, (tm, tm), 1)
    tri = jnp.where((col <= row) & (col >= (row & -CHUNK)), 1.0, 0.0).astype(BF16)
    crow = lax.broadcasted_iota(jnp.int32, (CHUNK, CHUNK), 0)
    ccol = lax.broadcasted_iota(jnp.int32, (CHUNK, CHUNK), 1)
    causal = ccol <= crow
    lane = lax.broadcasted_iota(jnp.int32, (CHUNK, LANES), 1)

    qk_s[SUBLANES:SUBLANES + tm, :] = proj(O_MQK, 2 * QK)
    conv = cb_ref[...]
    for j in range(CONV_W):
        conv = conv + cw_ref[j:j + 1, :] * qk_s[pl.ds(SUBLANES - (CONV_W - 1) + j, tm), :]
    tail_new = qk_s[tm:tm + SUBLANES, :]
    qk_s[0:SUBLANES, :] = tail_new
    tail_ref[...] = tail_new
    conv = conv * _sigmoid(conv)
    q_s[...] = (conv[:, :QK] * (DK ** -0.5)).astype(BF16)
    k_s[...] = conv[:, QK:].astype(BF16)
    v_s[...] = proj(O_MV, D_MODEL).astype(BF16)

    s1 = proj(O_S1, LANES)
    logi = s1 + sb_ref[0:1, :]
    logf = _log_sigmoid(proj(O_S2, LANES) + sb_ref[1:2, :])
    if meta:
        logi = jnp.where(valid, logi, NEG_BIG)
        logf = jnp.where(valid, logf, 0.0)
    li_s[...] = logi
    b_s[...] = _chunk_cumsum(tri, logf)

    def mlstm_chunk(c, carry):
        r = pl.multiple_of(c * CHUNK, CHUNK)
        b_all = b_s[pl.ds(r, CHUNK), :]
        c_all = li_s[pl.ds(r, CHUNK), :] - b_all
        g_all = b_all[CHUNK - 1:CHUNK, :]
        wlog_all = g_all + c_all
        m_prev = m_ref[0:1, :]
        inter_log_all = b_all + m_prev
        m_new = jnp.maximum(g_all + m_prev, jnp.max(wlog_all, axis=0, keepdims=True))
        a_all = jnp.exp(g_all + m_prev - m_new)
        w_all = jnp.exp(wlog_all - m_new)
        m_ref[...] = jnp.broadcast_to(m_new, (SUBLANES, LANES))
        for h in range(HEADS):
            dsl = slice(h * DK, (h + 1) * DK)
            vsl = slice(h * DV, (h + 1) * DV)
            q = q_s[pl.ds(r, CHUNK), dsl]
            k = k_s[pl.ds(r, CHUNK), dsl]
            v = v_s[pl.ds(r, CHUNK), vsl]
            bcol = b_all[:, h:h + 1]
            dmat = _dot_nt(_outer_sum_operand(bcol, 0, lane),
                           _outer_sum_operand(c_all[:, h:h + 1], 3, lane))
            dmat = jnp.where(causal, dmat, -jnp.inf)
            inter_log = inter_log_all[:, h:h + 1]
            m_row = jnp.maximum(inter_log, jnp.max(dmat, axis=-1, keepdims=True))
            wts = jnp.exp(dmat - m_row) * _dot_nt(q, k)
            a_inter = jnp.exp(inter_log - m_row)
            cst = c_ref[h]
            nst = n_ref[h, 0:1, :]
            num = a_inter * _dot_nt(q, cst.astype(BF16)) + _dot(wts.astype(BF16), v)
            den = (a_inter * jnp.sum(q.astype(F32) * nst, axis=-1, keepdims=True)
                   + jnp.sum(wts, axis=-1, keepdims=True))
            hm_s[pl.ds(r, CHUNK), vsl] = num / jnp.maximum(jnp.abs(den), jnp.exp(-m_row))
            wk = w_all[:, h:h + 1] * k.astype(F32)
            a = a_all[:, h:h + 1]
            c_ref[h] = a * cst + _dot_tn(v, wk.astype(BF16))
            n_ref[h] = jnp.broadcast_to(a * nst + jnp.sum(wk, axis=0, keepdims=True), (SUBLANES, DK))
        return carry

    lax.fori_loop(0, nc, mlstm_chunk, 0)

    gq_s[...] = proj(O_GQ, QK) * (DK ** -0.5)
    gk_s[...] = proj(O_GK, QK)
    v_s[...] = proj(O_GV, D_MODEL).astype(BF16)
    za = _dot(s1.astype(BF16), a2_ref[...]) + a2b_ref[...]
    loga = _log_sigmoid(za) / G_TAU
    if meta:
        loga = jnp.where(valid, loga, 0.0)
    bc_s[...] = _chunk_cumsum(tri, loga)

    def gla_chunk(c, carry):
        r = pl.multiple_of(c * CHUNK, CHUNK)
        for h in range(HEADS):
            dsl = slice(h * DK, (h + 1) * DK)
            vsl = slice(h * DV, (h + 1) * DV)
            bc = bc_s[pl.ds(r, CHUNK), dsl]
            btot = bc[CHUNK - 1:CHUNK, :]
            gq = gq_s[pl.ds(r, CHUNK), dsl]
            gk = gk_s[pl.ds(r, CHUNK), dsl]
            v = v_s[pl.ds(r, CHUNK), vsl]
            q_dec = (gq * jnp.exp(bc)).astype(BF16)
            k_inv = (gk * jnp.exp(-bc)).astype(BF16)
            k_end = (gk * jnp.exp(btot - bc)).astype(BF16)
            att = jnp.where(causal, _dot_nt(q_dec, k_inv), 0.0)
            sst = s_ref[h]
            hg_s[pl.ds(r, CHUNK), vsl] = _dot(att.astype(BF16), v) + _dot_nt(q_dec, sst.astype(BF16))
            s_ref[h] = jnp.exp(btot) * sst + _dot_tn(v, k_end)
        return carry

    lax.fori_loop(0, nc, gla_chunk, 0)

    if meta:
        h1_ref[...] = x
        return

    def head_norm(src, gain_ref):
        outs = []
        for h in range(HEADS):
            t = src[:, h * DV:(h + 1) * DV]
            outs.append(t * lax.rsqrt(jnp.mean(t * t, axis=-1, keepdims=True) + EPS))
        return jnp.concatenate(outs, axis=-1) * gain_ref[...]

    y_m = head_norm(hm_s, mhg_ref) * _sigmoid(proj(O_MO, D_MODEL))
    branch_m = _dot(y_m.astype(BF16), wbm_ref[...])
    merged = _sigmoid(proj(O_GATE_M, D_MODEL)) * branch_m
    gr = proj(O_GR, D_MODEL)
    y_g = head_norm(hg_s, ghg_ref) * (gr * _sigmoid(gr))
    branch_g = _dot(y_g.astype(BF16), wbg_ref[...])
    merged = merged + _sigmoid(proj(O_GATE_G, D_MODEL)) * branch_g
    h1_ref[...] = x + _dot(merged.astype(BF16), wout_ref[...])


def _const_spec(shape):
    nd = len(shape)
    return pl.BlockSpec(shape, lambda i: (0,) * nd, pipeline_mode=pl.Buffered(1))


def _mixer_call(x2, consts, state, *, tm, meta):
    t = x2.shape[0]
    tail0, c0, n0, m0, s0 = state
    state_shapes = [jax.ShapeDtypeStruct(a.shape, F32) for a in state]
    row_spec = pl.BlockSpec((tm, D_MODEL), lambda i: (i, 0))
    state_specs = [pl.BlockSpec(a.shape, lambda i, nd=a.ndim: (0,) * nd) for a in state]
    return pl.pallas_call(
        functools.partial(_mixer_kernel, tm=tm, meta=meta),
        grid=(t // tm,),
        in_specs=[row_spec] + [_const_spec(c.shape) for c in consts] + state_specs,
        out_specs=[row_spec] + state_specs,
        out_shape=[jax.ShapeDtypeStruct((t, D_MODEL), F32)] + state_shapes,
        scratch_shapes=[
            pltpu.VMEM((tm, D_MODEL), BF16),
            pltpu.VMEM((tm + SUBLANES, 2 * QK), F32),
            pltpu.VMEM((tm, QK), BF16),
            pltpu.VMEM((tm, QK), BF16),
            pltpu.VMEM((tm, D_MODEL), BF16),
            pltpu.VMEM((tm, LANES), F32),
            pltpu.VMEM((tm, LANES), F32),
            pltpu.VMEM((tm, QK), F32),
            pltpu.VMEM((tm, QK), F32),
            pltpu.VMEM((tm, QK), F32),
            pltpu.VMEM((tm, D_MODEL), F32),
            pltpu.VMEM((tm, D_MODEL), F32),
        ],
        compiler_params=pltpu.CompilerParams(
            dimension_semantics=("arbitrary",), vmem_limit_bytes=VMEM_LIMIT),
        name="mixer_meta" if meta else "mixer",
    )(x2, *consts, *state)


def _ffn_kernel(h_ref, g2_ref, wg_ref, wu_ref, wd_ref, gf_ref, o_ref, *, fc):
    h = h_ref[...]
    hn = (h * lax.rsqrt(jnp.mean(h * h, axis=-1, keepdims=True) + EPS) * g2_ref[...]).astype(BF16)
    acc = h
    for f in range(0, wg_ref.shape[1], fc):
        gate = _dot(hn, wg_ref[:, f:f + fc])
        up = _dot(hn, wu_ref[:, f:f + fc])
        acc = acc + _dot((gate * _sigmoid(gate) * up).astype(BF16), wd_ref[f:f + fc, :])
    o_ref[...] = acc * lax.rsqrt(jnp.mean(acc * acc, axis=-1, keepdims=True) + EPS) * gf_ref[...]


def _ffn_call(h1, g2, wg, wu, wd, gf, *, tm, fc):
    t = h1.shape[0]
    row_spec = pl.BlockSpec((tm, D_MODEL), lambda i: (i, 0))
    return pl.pallas_call(
        functools.partial(_ffn_kernel, fc=fc),
        grid=(t // tm,),
        in_specs=[row_spec] + [_const_spec(a.shape) for a in (g2, wg, wu, wd, gf)],
        out_specs=row_spec,
        out_shape=jax.ShapeDtypeStruct((t, D_MODEL), F32),
        compiler_params=pltpu.CompilerParams(
            dimension_semantics=("arbitrary",), vmem_limit_bytes=VMEM_LIMIT),
        name="ffn",
    )(h1, g2, wg, wu, wd, gf)


def _pack_w_in(w):
    mq, mk, mv, mi, mf, mo, gq, gk, gv, ga, gr, gate_m, gate_g = jnp.split(
        w, [512, 1024, 2048, 2052, 2056, 3080, 3592, 4104, 5128, 5144, 6168, 7192], axis=1)
    d = w.shape[0]
    s1 = jnp.concatenate([mi, jnp.zeros((d, GA_LANE - HEADS), w.dtype), ga,
                          jnp.zeros((d, LANES - GA_LANE - G_RANK), w.dtype)], axis=1)
    s2 = jnp.concatenate([mf, jnp.zeros((d, LANES - HEADS), w.dtype)], axis=1)
    return jnp.concatenate([mq, mk, mv, mo, gq, gk, gv, gr, gate_m, gate_g, s1, s2], axis=1).astype(BF16)


def kernel(x, meta_tokens, norm1_g, w_in, conv_w, conv_b, m_gate_b, g_a2, g_a2_b, m_head_g, g_head_g,
           w_branch_m, w_branch_g, w_out, norm2_g, w_ff_gate, w_ff_up, w_ff_down, final_g):
    bsz, seq, d = x.shape
    assert bsz == 1 and d == D_MODEL and norm1_g.shape[0] == 1 and seq % CHUNK == 0
    row = lambda a: a.reshape(1, -1).astype(F32)

    gate_bias = jnp.zeros((SUBLANES, LANES), F32)
    gate_bias = gate_bias.at[0, :HEADS].set(m_gate_b[0, 0]).at[1, :HEADS].set(m_gate_b[0, 1])
    a2 = jnp.zeros((LANES, QK), F32).at[GA_LANE:GA_LANE + G_RANK].set(g_a2[0]).astype(BF16)
    conv_w8 = jnp.zeros((SUBLANES, 2 * QK), F32).at[:CONV_W].set(conv_w[0])
    consts = (row(norm1_g[0]), _pack_w_in(w_in[0]), gate_bias, a2, row(g_a2_b[0]), conv_w8, row(conv_b[0]),
              row(m_head_g[0]), row(g_head_g[0]),
              w_branch_m[0].astype(BF16), w_branch_g[0].astype(BF16), w_out[0].astype(BF16))

    zero_state = (jnp.zeros((SUBLANES, 2 * QK), F32), jnp.zeros((HEADS, DV, DK), F32),
                  jnp.zeros((HEADS, SUBLANES, DK), F32), jnp.zeros((SUBLANES, LANES), F32),
                  jnp.zeros((HEADS, DV, DK), F32))
    lead = jnp.concatenate([jnp.zeros((CHUNK - N_META, d), F32), meta_tokens.astype(F32)], axis=0)
    _, *state = _mixer_call(lead, consts, zero_state, tm=CHUNK, meta=True)
    h1, *_ = _mixer_call(x[0], consts, tuple(state), tm=256, meta=False)
    out = _ffn_call(h1, row(norm2_g[0]), w_ff_gate[0].astype(BF16), w_ff_up[0].astype(BF16),
                    w_ff_down[0].astype(BF16), row(final_g), tm=512, fc=256)
    return out[None]
```

```python
import functools

import jax
import jax.numpy as jnp
from jax import lax
from jax.experimental import pallas as pl
from jax.experimental.pallas import tpu as pltpu

F32 = jnp.float32
BF16 = jnp.bfloat16

D_MODEL = 1024
N_META = 16
CHUNK = 64
EPS = 1e-6
HEADS = 4
DV = D_MODEL // HEADS
DK = DV // 2
QK = HEADS * DK
G_RANK = 16
G_TAU = 16.0
CONV_W = 4
LANES = 128
SUBLANES = 8
BF16_ROWS = 16
NEG_BIG = -1e30
VMEM_LIMIT = 56 * 1024 * 1024
CN = DV + LANES

O_MQK, O_MV, O_MO = 0, 1024, 2048
O_GQ, O_GK, O_GV, O_GR = 3072, 3584, 4096, 5120
O_GATE_M, O_GATE_G = 6144, 7168
O_S1, O_S2 = 8192, 8320
N_PACK = 8448
GA_LANE = 8

NT_DIMS = (((1,), (1,)), ((), ()))
TN_DIMS = (((0,), (0,)), ((), ()))


def _dot(a, b):
    return jnp.dot(a, b, preferred_element_type=F32)


def _dot_nt(a, b):
    return lax.dot_general(a, b, NT_DIMS, preferred_element_type=F32)


def _dot_tn(a, b):
    return lax.dot_general(a, b, TN_DIMS, preferred_element_type=F32)


def _split3(x):
    hi = x.astype(BF16)
    r = x - hi.astype(F32)
    mid = r.astype(BF16)
    lo = (r - mid.astype(F32)).astype(BF16)
    return jnp.concatenate([hi, mid, lo], axis=1)


def _log_sigmoid(x):
    return jnp.minimum(x, 0.0) - jnp.log(1.0 + jnp.exp(-jnp.abs(x)))


def _sigmoid(x):
    return 1.0 / (1.0 + jnp.exp(-x))


def _mixer_kernel(x_ref, g1_ref, w_ref, sb_ref, a2_ref, a2b_ref, cw_ref, cb_ref, mhg_ref, ghg_ref,
                  wbm_ref, wbg_ref, wout_ref, tail0_ref, cn0_ref, m0_ref, st0_ref,
                  h1_ref, tail_ref, cn_ref, m_ref, st_ref,
                  xn_s, qk_s, q_s, k_s, wk_s, vo_s, gqd_s, gki_s, gke_s, gv_s, hm_s, hg_s,
                  *, tm, meta):
    nc = tm // CHUNK
    pairs = [(c, h) for c in range(nc) for h in range(HEADS)]

    @pl.when(pl.program_id(0) == 0)
    def _():
        qk_s[0:SUBLANES, :] = tail0_ref[...]
        cn_ref[...] = cn0_ref[...]
        m_ref[...] = m0_ref[...]
        st_ref[...] = st0_ref[...]
        for h in range(HEADS):
            vo_s[:, h * CN + DV:(h + 1) * CN] = jnp.ones((tm, LANES), BF16)

    x = x_ref[...]
    xn = x * lax.rsqrt(jnp.mean(x * x, axis=-1, keepdims=True) + EPS) * g1_ref[...]
    if meta:
        valid = lax.broadcasted_iota(jnp.int32, (tm, 1), 0) >= (CHUNK - N_META)
        xn = jnp.where(valid, xn, 0.0)
    xn_s[...] = xn.astype(BF16)

    def proj(off, width):
        return _dot(xn_s[...], w_ref[:, off:off + width])

    def rows(c):
        return slice(c * CHUNK, (c + 1) * CHUNK)

    row = lax.broadcasted_iota(jnp.int32, (tm, tm), 0)
    col = lax.broadcasted_iota(jnp.int32, (tm, tm), 1)
    same_chunk = (col >= (row & -CHUNK)) & (col < (row & -CHUNK) + CHUNK)
    tri = jnp.where(same_chunk & (col <= row), 1.0, 0.0).astype(BF16)
    tri_rev = jnp.where(same_chunk & (col > row), 1.0, 0.0).astype(BF16)
    crow = lax.broadcasted_iota(jnp.int32, (CHUNK, CHUNK), 0)
    ccol = lax.broadcasted_iota(jnp.int32, (CHUNK, CHUNK), 1)
    causal = ccol <= crow
    sr = lax.broadcasted_iota(jnp.int32, (3 * LANES, HEADS * LANES), 0)
    sc = lax.broadcasted_iota(jnp.int32, (3 * LANES, HEADS * LANES), 1)
    spread = jnp.where((sr & (LANES - 1)) == (sc >> 7), 1.0, 0.0).astype(BF16)
    pr = lax.broadcasted_iota(jnp.int32, (HEADS * CHUNK, 3 * LANES), 0)
    pc = lax.broadcasted_iota(jnp.int32, (HEADS * CHUNK, 3 * LANES), 1)
    pick = jnp.where((pc & (LANES - 1)) == (pr >> 6), 1.0, 0.0).astype(BF16)

    def lane_tiles(x128):
        return _dot(_split3(x128), spread)

    s1 = proj(O_S1, LANES)
    logi = s1 + sb_ref[0:1, :]
    logf = _log_sigmoid(proj(O_S2, LANES) + sb_ref[1:2, :])
    if meta:
        logi = jnp.where(valid, logi, NEG_BIG)
        logf = jnp.where(valid, logf, 0.0)
    b3 = _dot(tri, _split3(logf))
    b_all = b3[:, :LANES] + b3[:, LANES:2 * LANES] + b3[:, 2 * LANES:]
    c_all = logi - b_all
    m_run = m_ref[0:1, :]
    inter_parts, w_parts, a_rows = [], [], []
    for c in range(nc):
        b_c = b_all[rows(c), :]
        g_c = b_c[CHUNK - 1:CHUNK, :]
        wlog = g_c + c_all[rows(c), :]
        m_next = jnp.maximum(g_c + m_run, jnp.max(wlog, axis=0, keepdims=True))
        inter_parts.append(b_c + m_run)
        w_parts.append(jnp.exp(wlog - m_next))
        a_rows.append(jnp.exp(g_c + m_run - m_next))
        m_run = m_next
    m_ref[...] = jnp.broadcast_to(m_run, (SUBLANES, LANES))
    b_t = lane_tiles(b_all)
    il_t = lane_tiles(jnp.concatenate(inter_parts, axis=0))
    w_t = lane_tiles(jnp.concatenate(w_parts, axis=0))
    a_t = lane_tiles(jnp.concatenate(
        [jnp.broadcast_to(a, (BF16_ROWS, LANES)) for a in a_rows], axis=0))

    qk_s[SUBLANES:SUBLANES + tm, :] = proj(O_MQK, 2 * QK)
    conv = cb_ref[...]
    for j in range(CONV_W):
        conv = conv + cw_ref[j:j + 1, :] * qk_s[pl.ds(SUBLANES - (CONV_W - 1) + j, tm), :]
    tail_new = qk_s[tm:tm + SUBLANES, :]
    qk_s[0:SUBLANES, :] = tail_new
    tail_ref[...] = tail_new
    conv = conv * _sigmoid(conv)
    q_s[...] = conv[:, :QK] * (DK ** -0.5)
    k_s[...] = conv[:, QK:].astype(BF16)
    wk_s[...] = (w_t * conv[:, QK:]).astype(BF16)
    mv = proj(O_MV, D_MODEL).astype(BF16)
    for h in range(HEADS):
        vo_s[:, h * CN:h * CN + DV] = mv[:, h * DV:(h + 1) * DV]

    za = _dot(s1.astype(BF16), a2_ref[...]) + a2b_ref[...]
    loga = _log_sigmoid(za) / G_TAU
    if meta:
        loga = jnp.where(valid, loga, 0.0)
    hi = loga.astype(BF16)
    r1 = loga - hi.astype(F32)
    mid = r1.astype(BF16)
    lo = (r1 - mid.astype(F32)).astype(BF16)
    bc = _dot(tri, hi) + _dot(tri, mid) + _dot(tri, lo)
    rest = _dot(tri_rev, hi) + _dot(tri_rev, mid) + _dot(tri_rev, lo)
    gk = proj(O_GK, QK)
    gqd_s[...] = (proj(O_GQ, QK) * (DK ** -0.5) * jnp.exp(bc)).astype(BF16)
    gki_s[...] = (gk * jnp.exp(-bc)).astype(BF16)
    gke_s[...] = (gk * jnp.exp(rest)).astype(BF16)
    gv_s[...] = proj(O_GV, D_MODEL).astype(BF16)
    e_tot = [jnp.exp(bc[(c + 1) * CHUNK - 1:(c + 1) * CHUNK, :]) for c in range(nc)]

    c3 = _split3(c_all)
    r_blk = [_dot_nt(pick, c3[rows(c), :]) for c in range(nc)]
    dmat, rmax, sim = {}, {}, {}
    for (c, h) in pairs:
        d = b_t[rows(c), h * LANES:h * LANES + CHUNK] + r_blk[c][h * CHUNK:(h + 1) * CHUNK, :]
        d = jnp.where(causal, d, -jnp.inf)
        dmat[c, h] = d
        rmax[c, h] = jnp.max(d, axis=-1, keepdims=True)
    for (c, h) in pairs:
        sim[c, h] = _dot_nt(q_s[rows(c), h * DK:(h + 1) * DK].astype(BF16), k_s[rows(c), h * DK:(h + 1) * DK])
    lhs, emr, upd = {}, {}, {}
    zeros_pad = jnp.zeros((CHUNK, CHUNK), BF16)
    for (c, h) in pairs:
        il = il_t[rows(c), h * LANES:(h + 1) * LANES]
        m_row = jnp.maximum(il, rmax[c, h])
        wts = jnp.exp(dmat[c, h] - m_row[:, :CHUNK]) * sim[c, h]
        aq = jnp.exp(il - m_row) * q_s[rows(c), h * DK:(h + 1) * DK]
        lhs[c, h] = jnp.concatenate([aq.astype(BF16), wts.astype(BF16), zeros_pad], axis=1)
        emr[c, h] = jnp.exp(-m_row)
    for (c, h) in pairs:
        upd[c, h] = _dot_tn(wk_s[rows(c), h * DK:(h + 1) * DK], vo_s[rows(c), h * CN:(h + 1) * CN])
    zeros_rows = jnp.zeros((CHUNK, CN), BF16)
    res = {}
    for h in range(HEADS):
        state = cn_ref[h]
        for c in range(nc):
            rhs = jnp.concatenate([state.astype(BF16), vo_s[rows(c), h * CN:(h + 1) * CN], zeros_rows], axis=0)
            res[c, h] = _dot(lhs[c, h], rhs)
            a = a_t[BF16_ROWS * c:BF16_ROWS * c + 1, h * LANES:(h + 1) * LANES]
            state = jnp.concatenate([a] * (CN // LANES), axis=1) * state + upd[c, h]
        cn_ref[h] = state
    for (c, h) in pairs:
        inv = 1.0 / jnp.maximum(jnp.abs(res[c, h][:, DV:]), emr[c, h])
        hm_s[rows(c), h * DV:h * DV + LANES] = res[c, h][:, :LANES] * inv
        hm_s[rows(c), h * DV + LANES:(h + 1) * DV] = res[c, h][:, LANES:DV] * inv

    att, gupd = {}, {}
    for (c, h) in pairs:
        a_ch = _dot_nt(gqd_s[rows(c), h * DK:(h + 1) * DK], gki_s[rows(c), h * DK:(h + 1) * DK])
        att[c, h] = jnp.where(causal, a_ch, 0.0).astype(BF16)
    for (c, h) in pairs:
        gupd[c, h] = _dot_tn(gv_s[rows(c), h * DV:(h + 1) * DV], gke_s[rows(c), h * DK:(h + 1) * DK])
    for h in range(HEADS):
        sst = st_ref[h]
        for c in range(nc):
            hg_s[rows(c), h * DV:(h + 1) * DV] = (
                _dot(att[c, h], gv_s[rows(c), h * DV:(h + 1) * DV])
                + _dot_nt(gqd_s[rows(c), h * DK:(h + 1) * DK], sst.astype(BF16)))
            sst = e_tot[c][:, h * DK:(h + 1) * DK] * sst + gupd[c, h]
        st_ref[h] = sst

    if meta:
        h1_ref[...] = x
        return

    def head_norm(src, gain_ref):
        outs = []
        for h in range(HEADS):
            t = src[:, h * DV:(h + 1) * DV]
            outs.append(t * lax.rsqrt(jnp.mean(t * t, axis=-1, keepdims=True) + EPS))
        return jnp.concatenate(outs, axis=-1) * gain_ref[...]

    y_m = head_norm(hm_s, mhg_ref) * _sigmoid(proj(O_MO, D_MODEL))
    branch_m = _dot(y_m.astype(BF16), wbm_ref[...])
    merged = _sigmoid(proj(O_GATE_M, D_MODEL)) * branch_m
    gr = proj(O_GR, D_MODEL)
    y_g = head_norm(hg_s, ghg_ref) * (gr * _sigmoid(gr))
    branch_g = _dot(y_g.astype(BF16), wbg_ref[...])
    merged = merged + _sigmoid(proj(O_GATE_G, D_MODEL)) * branch_g
    h1_ref[...] = x + _dot(merged.astype(BF16), wout_ref[...])


def _const_spec(shape):
    nd = len(shape)
    return pl.BlockSpec(shape, lambda i: (0,) * nd, pipeline_mode=pl.Buffered(1))


def _mixer_call(x2, consts, state, *, tm, meta):
    t = x2.shape[0]
    state_shapes = [jax.ShapeDtypeStruct(a.shape, F32) for a in state]
    row_spec = pl.BlockSpec((tm, D_MODEL), lambda i: (i, 0))
    state_specs = [pl.BlockSpec(a.shape, lambda i, nd=a.ndim: (0,) * nd) for a in state]
    return pl.pallas_call(
        functools.partial(_mixer_kernel, tm=tm, meta=meta),
        grid=(t // tm,),
        in_specs=[row_spec] + [_const_spec(c.shape) for c in consts] + state_specs,
        out_specs=[row_spec] + state_specs,
        out_shape=[jax.ShapeDtypeStruct((t, D_MODEL), F32)] + state_shapes,
        scratch_shapes=[
            pltpu.VMEM((tm, D_MODEL), BF16),
            pltpu.VMEM((tm + SUBLANES, 2 * QK), F32),
            pltpu.VMEM((tm, QK), F32),
            pltpu.VMEM((tm, QK), BF16),
            pltpu.VMEM((tm, QK), BF16),
            pltpu.VMEM((tm, HEADS * CN), BF16),
            pltpu.VMEM((tm, QK), BF16),
            pltpu.VMEM((tm, QK), BF16),
            pltpu.VMEM((tm, QK), BF16),
            pltpu.VMEM((tm, D_MODEL), BF16),
            pltpu.VMEM((tm, D_MODEL), F32),
            pltpu.VMEM((tm, D_MODEL), F32),
        ],
        compiler_params=pltpu.CompilerParams(
            dimension_semantics=("arbitrary",), vmem_limit_bytes=VMEM_LIMIT),
        name="mixer_meta" if meta else "mixer",
    )(x2, *consts, *state)


def _ffn_kernel(h_ref, g2_ref, wg_ref, wu_ref, wd_ref, gf_ref, o_ref, *, fc):
    h = h_ref[...]
    hn = (h * lax.rsqrt(jnp.mean(h * h, axis=-1, keepdims=True) + EPS) * g2_ref[...]).astype(BF16)
    acc = h
    for f in range(0, wg_ref.shape[1], fc):
        gate = _dot(hn, wg_ref[:, f:f + fc])
        up = _dot(hn, wu_ref[:, f:f + fc])
        acc = acc + _dot((gate * _sigmoid(gate) * up).astype(BF16), wd_ref[f:f + fc, :])
    o_ref[...] = acc * lax.rsqrt(jnp.mean(acc * acc, axis=-1, keepdims=True) + EPS) * gf_ref[...]


def _ffn_call(h1, g2, wg, wu, wd, gf, *, tm, fc):
    t = h1.shape[0]
    row_spec = pl.BlockSpec((tm, D_MODEL), lambda i: (i, 0))
    return pl.pallas_call(
        functools.partial(_ffn_kernel, fc=fc),
        grid=(t // tm,),
        in_specs=[row_spec] + [_const_spec(a.shape) for a in (g2, wg, wu, wd, gf)],
        out_specs=row_spec,
        out_shape=jax.ShapeDtypeStruct((t, D_MODEL), F32),
        compiler_params=pltpu.CompilerParams(
            dimension_semantics=("arbitrary",), vmem_limit_bytes=VMEM_LIMIT),
        name="ffn",
    )(h1, g2, wg, wu, wd, gf)


def _pack_w_in(w):
    mq, mk, mv, mi, mf, mo, gq, gk, gv, ga, gr, gate_m, gate_g = jnp.split(
        w, [512, 1024, 2048, 2052, 2056, 3080, 3592, 4104, 5128, 5144, 6168, 7192], axis=1)
    d = w.shape[0]
    s1 = jnp.concatenate([mi, jnp.zeros((d, GA_LANE - HEADS), w.dtype), ga,
                          jnp.zeros((d, LANES - GA_LANE - G_RANK), w.dtype)], axis=1)
    s2 = jnp.concatenate([mf, jnp.zeros((d, LANES - HEADS), w.dtype)], axis=1)
    return jnp.concatenate([mq, mk, mv, mo, gq, gk, gv, gr, gate_m, gate_g, s1, s2], axis=1).astype(BF16)


def kernel(x, meta_tokens, norm1_g, w_in, conv_w, conv_b, m_gate_b, g_a2, g_a2_b, m_head_g, g_head_g,
           w_branch_m, w_branch_g, w_out, norm2_g, w_ff_gate, w_ff_up, w_ff_down, final_g):
    bsz, seq, d = x.shape
    assert bsz == 1 and d == D_MODEL and norm1_g.shape[0] == 1 and seq % CHUNK == 0
    row = lambda a: a.reshape(1, -1).astype(F32)

    gate_bias = jnp.zeros((SUBLANES, LANES), F32)
    gate_bias = gate_bias.at[0, :HEADS].set(m_gate_b[0, 0]).at[1, :HEADS].set(m_gate_b[0, 1])
    a2 = jnp.zeros((LANES, QK), F32).at[GA_LANE:GA_LANE + G_RANK].set(g_a2[0]).astype(BF16)
    conv_w8 = jnp.zeros((SUBLANES, 2 * QK), F32).at[:CONV_W].set(conv_w[0])
    consts = (row(norm1_g[0]), _pack_w_in(w_in[0]), gate_bias, a2, row(g_a2_b[0]), conv_w8, row(conv_b[0]),
              row(m_head_g[0]), row(g_head_g[0]),
              w_branch_m[0].astype(BF16), w_branch_g[0].astype(BF16), w_out[0].astype(BF16))

    zero_state = (jnp.zeros((SUBLANES, 2 * QK), F32), jnp.zeros((HEADS, DK, CN), F32),
                  jnp.zeros((SUBLANES, LANES), F32), jnp.zeros((HEADS, DV, DK), F32))
    lead = jnp.concatenate([jnp.zeros((CHUNK - N_META, d), F32), meta_tokens.astype(F32)], axis=0)
    _, *state = _mixer_call(lead, consts, zero_state, tm=CHUNK, meta=True)
    h1, *_ = _mixer_call(x[0], consts, tuple(state), tm=256, meta=False)
    out = _ffn_call(h1, row(norm2_g[0]), w_ff_gate[0].astype(BF16), w_ff_up[0].astype(BF16),
                    w_ff_down[0].astype(BF16), row(final_g), tm=512, fc=256)
    return out[None]
```

```python
import functools

import jax
import jax.numpy as jnp
from jax import lax
from jax.experimental import pallas as pl
from jax.experimental.pallas import tpu as pltpu

F32 = jnp.float32
BF16 = jnp.bfloat16

D_MODEL = 1024
N_META = 16
CHUNK = 64
EPS = 1e-6
HEADS = 4
DV = D_MODEL // HEADS
DK = DV // 2
QK = HEADS * DK
G_RANK = 16
G_TAU = 16.0
CONV_W = 4
LANES = 128
SUBLANES = 8
BF16_ROWS = 16
NEG_BIG = -1e30
VMEM_LIMIT = 56 * 1024 * 1024
CN = DV + LANES

W_IN_RANGES = ((0, 2048), (2056, 5128), (5144, 8216))
O_MQK, O_MV = (0, 0), (0, 1024)
O_MO, O_GQ, O_GK, O_GV = (1, 0), (1, 1024), (1, 1536), (1, 2048)
O_GR, O_GATE_M, O_GATE_G = (2, 0), (2, 1024), (2, 2048)
O_S1, O_S2 = (3, 0), (3, LANES)
GA_LANE = 8

NT_DIMS = (((1,), (1,)), ((), ()))
TN_DIMS = (((0,), (0,)), ((), ()))


def _dot(a, b):
    return jnp.dot(a, b, preferred_element_type=F32)


def _dot_nt(a, b):
    return lax.dot_general(a, b, NT_DIMS, preferred_element_type=F32)


def _dot_tn(a, b):
    return lax.dot_general(a, b, TN_DIMS, preferred_element_type=F32)


def _split3(x):
    hi = x.astype(BF16)
    r = x - hi.astype(F32)
    mid = r.astype(BF16)
    lo = (r - mid.astype(F32)).astype(BF16)
    return jnp.concatenate([hi, mid, lo], axis=1)


def _log_sigmoid(x):
    return jnp.minimum(x, 0.0) - jnp.log(1.0 + jnp.exp(-jnp.abs(x)))


def _sigmoid(x):
    return 1.0 / (1.0 + jnp.exp(-x))


def _mixer_kernel(x_ref, g1_ref, wa_ref, wb_ref, wc_ref, ws_ref, sb_ref, a2_ref, a2b_ref, cw_ref, cb_ref, mhg_ref, ghg_ref,
                  wbm_ref, wbg_ref, wout_ref, tail0_ref, cn0_ref, m0_ref, st0_ref,
                  h1_ref, tail_ref, cn_ref, m_ref, st_ref,
                  xn_s, qk_s, q_s, k_s, wk_s, vo_s, gqd_s, gki_s, gke_s, gv_s, hm_s, hg_s,
                  *, tm, meta):
    nc = tm // CHUNK
    pairs = [(c, h) for c in range(nc) for h in range(HEADS)]

    @pl.when(pl.program_id(0) == 0)
    def _():
        qk_s[0:SUBLANES, :] = tail0_ref[...]
        cn_ref[...] = cn0_ref[...]
        m_ref[...] = m0_ref[...]
        st_ref[...] = st0_ref[...]
        for h in range(HEADS):
            vo_s[:, h * CN + DV:(h + 1) * CN] = jnp.ones((tm, LANES), BF16)

    x = x_ref[...]
    xn = x * lax.rsqrt(jnp.mean(x * x, axis=-1, keepdims=True) + EPS) * g1_ref[...]
    if meta:
        valid = lax.broadcasted_iota(jnp.int32, (tm, 1), 0) >= (CHUNK - N_META)
        xn = jnp.where(valid, xn, 0.0)
    xn_s[...] = xn.astype(BF16)

    w_refs = (wa_ref, wb_ref, wc_ref, ws_ref)

    def proj(seg, width):
        return _dot(xn_s[...], w_refs[seg[0]][:, seg[1]:seg[1] + width])

    def rows(c):
        return slice(c * CHUNK, (c + 1) * CHUNK)

    row = lax.broadcasted_iota(jnp.int32, (tm, tm), 0)
    col = lax.broadcasted_iota(jnp.int32, (tm, tm), 1)
    same_chunk = (col >= (row & -CHUNK)) & (col < (row & -CHUNK) + CHUNK)
    tri = jnp.where(same_chunk & (col <= row), 1.0, 0.0).astype(BF16)
    crow = lax.broadcasted_iota(jnp.int32, (CHUNK, CHUNK), 0)
    ccol = lax.broadcasted_iota(jnp.int32, (CHUNK, CHUNK), 1)
    causal = ccol <= crow
    sr = lax.broadcasted_iota(jnp.int32, (3 * LANES, HEADS * LANES), 0)
    sc = lax.broadcasted_iota(jnp.int32, (3 * LANES, HEADS * LANES), 1)
    spread = jnp.where((sr & (LANES - 1)) == (sc >> 7), 1.0, 0.0).astype(BF16)

    def lane_tiles(x128):
        return _dot(_split3(x128), spread)

    s1 = proj(O_S1, LANES)
    logi = s1 + sb_ref[0:1, :]
    logf = _log_sigmoid(proj(O_S2, LANES) + sb_ref[1:2, :])
    if meta:
        logi = jnp.where(valid, logi, NEG_BIG)
        logf = jnp.where(valid, logf, 0.0)
    b3 = _dot(tri, _split3(logf))
    b_all = b3[:, :LANES] + b3[:, LANES:2 * LANES] + b3[:, 2 * LANES:]
    c_all = logi - b_all
    m_run = m_ref[0:1, :]
    inter_parts, w_parts, a_rows = [], [], []
    for c in range(nc):
        b_c = b_all[rows(c), :]
        g_c = b_c[CHUNK - 1:CHUNK, :]
        wlog = g_c + c_all[rows(c), :]
        m_next = jnp.maximum(g_c + m_run, jnp.max(wlog, axis=0, keepdims=True))
        inter_parts.append(b_c + m_run)
        w_parts.append(jnp.exp(wlog - m_next))
        a_rows.append(jnp.exp(g_c + m_run - m_next))
        m_run = m_next
    m_ref[...] = jnp.broadcast_to(m_run, (SUBLANES, LANES))
    b_t = lane_tiles(b_all)
    il_t = lane_tiles(jnp.concatenate(inter_parts, axis=0))
    w_t = lane_tiles(jnp.concatenate(w_parts, axis=0))
    a_t = lane_tiles(jnp.concatenate(
        [jnp.broadcast_to(a, (BF16_ROWS, LANES)) for a in a_rows], axis=0))

    qk_s[SUBLANES:SUBLANES + tm, :] = proj(O_MQK, 2 * QK)
    conv = cb_ref[...]
    for j in range(CONV_W):
        conv = conv + cw_ref[j:j + 1, :] * qk_s[pl.ds(SUBLANES - (CONV_W - 1) + j, tm), :]
    tail_new = qk_s[tm:tm + SUBLANES, :]
    qk_s[0:SUBLANES, :] = tail_new
    tail_ref[...] = tail_new
    conv = conv * _sigmoid(conv)
    q_s[...] = conv[:, :QK] * (DK ** -0.5)
    k_s[...] = conv[:, QK:].astype(BF16)
    wk_s[...] = (w_t * conv[:, QK:]).astype(BF16)
    mv = proj(O_MV, D_MODEL).astype(BF16)
    for h in range(HEADS):
        vo_s[:, h * CN:h * CN + DV] = mv[:, h * DV:(h + 1) * DV]

    za = _dot(s1.astype(BF16), a2_ref[...]) + a2b_ref[...]
    loga = _log_sigmoid(za) / G_TAU
    if meta:
        loga = jnp.where(valid, loga, 0.0)
    hi = loga.astype(BF16)
    r1 = loga - hi.astype(F32)
    mid = r1.astype(BF16)
    lo = (r1 - mid.astype(F32)).astype(BF16)
    bc = _dot(tri, hi) + _dot(tri, mid) + _dot(tri, lo)
    b_tot = [bc[(c + 1) * CHUNK - 1:(c + 1) * CHUNK, :] for c in range(nc)]
    rest = jnp.concatenate([b_tot[c] - bc[rows(c), :] for c in range(nc)], axis=0)
    gk = proj(O_GK, QK)
    gqd_s[...] = (proj(O_GQ, QK) * (DK ** -0.5) * jnp.exp(bc)).astype(BF16)
    gki_s[...] = (gk * jnp.exp(-bc)).astype(BF16)
    gke_s[...] = (gk * jnp.exp(rest)).astype(BF16)
    gv_s[...] = proj(O_GV, D_MODEL).astype(BF16)
    e_tot = [jnp.exp(b_tot[c]) for c in range(nc)]

    pad_rows = -tm % LANES
    c_pad = jnp.concatenate([c_all, jnp.zeros((pad_rows, LANES), F32)], axis=0) if pad_rows else c_all
    c_row = c_pad.T
    dmat, rmax, sim = {}, {}, {}
    for (c, h) in pairs:
        d = b_t[rows(c), h * LANES:h * LANES + CHUNK] + c_row[h:h + 1, c * CHUNK:(c + 1) * CHUNK]
        d = jnp.where(causal, d, -jnp.inf)
        dmat[c, h] = d
        rmax[c, h] = jnp.max(d, axis=-1, keepdims=True)
    for (c, h) in pairs:
        sim[c, h] = _dot_nt(q_s[rows(c), h * DK:(h + 1) * DK].astype(BF16), k_s[rows(c), h * DK:(h + 1) * DK])
    lhs, emr, upd = {}, {}, {}
    zeros_pad = jnp.zeros((CHUNK, CHUNK), BF16)
    for (c, h) in pairs:
        il = il_t[rows(c), h * LANES:(h + 1) * LANES]
        m_row = jnp.maximum(il, rmax[c, h])
        wts = jnp.exp(dmat[c, h] - m_row[:, :CHUNK]) * sim[c, h]
        aq = jnp.exp(il - m_row) * q_s[rows(c), h * DK:(h + 1) * DK]
        lhs[c, h] = jnp.concatenate([aq.astype(BF16), wts.astype(BF16), zeros_pad], axis=1)
        emr[c, h] = jnp.exp(-m_row)
    for (c, h) in pairs:
        upd[c, h] = _dot_tn(wk_s[rows(c), h * DK:(h + 1) * DK], vo_s[rows(c), h * CN:(h + 1) * CN])
    zeros_rows = jnp.zeros((CHUNK, CN), BF16)
    res = {}
    for h in range(HEADS):
        state = cn_ref[h]
        for c in range(nc):
            rhs = jnp.concatenate([state.astype(BF16), vo_s[rows(c), h * CN:(h + 1) * CN], zeros_rows], axis=0)
            res[c, h] = _dot(lhs[c, h], rhs)
            a = a_t[BF16_ROWS * c:BF16_ROWS * c + 1, h * LANES:(h + 1) * LANES]
            state = jnp.concatenate([a] * (CN // LANES), axis=1) * state + upd[c, h]
        cn_ref[h] = state
    for (c, h) in pairs:
        inv = 1.0 / jnp.maximum(jnp.abs(res[c, h][:, DV:]), emr[c, h])
        hm_s[rows(c), h * DV:h * DV + LANES] = res[c, h][:, :LANES] * inv
        hm_s[rows(c), h * DV + LANES:(h + 1) * DV] = res[c, h][:, LANES:DV] * inv

    att, gupd = {}, {}
    for (c, h) in pairs:
        a_ch = _dot_nt(gqd_s[rows(c), h * DK:(h + 1) * DK], gki_s[rows(c), h * DK:(h + 1) * DK])
        att[c, h] = jnp.where(causal, a_ch, 0.0).astype(BF16)
    for (c, h) in pairs:
        gupd[c, h] = _dot_tn(gv_s[rows(c), h * DV:(h + 1) * DV], gke_s[rows(c), h * DK:(h + 1) * DK])
    for h in range(HEADS):
        sst = st_ref[h]
        for c in range(nc):
            hg_s[rows(c), h * DV:(h + 1) * DV] = (
                _dot(att[c, h], gv_s[rows(c), h * DV:(h + 1) * DV])
                + _dot_nt(gqd_s[rows(c), h * DK:(h + 1) * DK], sst.astype(BF16)))
            sst = e_tot[c][:, h * DK:(h + 1) * DK] * sst + gupd[c, h]
        st_ref[h] = sst

    if meta:
        h1_ref[...] = x
        return

    def head_norm(src, gain_ref):
        outs = []
        for h in range(HEADS):
            t = src[:, h * DV:(h + 1) * DV]
            outs.append(t * lax.rsqrt(jnp.mean(t * t, axis=-1, keepdims=True) + EPS))
        return jnp.concatenate(outs, axis=-1) * gain_ref[...]

    y_m = head_norm(hm_s, mhg_ref) * _sigmoid(proj(O_MO, D_MODEL))
    branch_m = _dot(y_m.astype(BF16), wbm_ref[...])
    merged = _sigmoid(proj(O_GATE_M, D_MODEL)) * branch_m
    gr = proj(O_GR, D_MODEL)
    y_g = head_norm(hg_s, ghg_ref) * (gr * _sigmoid(gr))
    branch_g = _dot(y_g.astype(BF16), wbg_ref[...])
    merged = merged + _sigmoid(proj(O_GATE_G, D_MODEL)) * branch_g
    h1_ref[...] = x + _dot(merged.astype(BF16), wout_ref[...])


def _const_spec(shape):
    nd = len(shape)
    return pl.BlockSpec(shape, lambda i: (0,) * nd, pipeline_mode=pl.Buffered(1))


def _mixer_call(x2, consts, state, *, tm, meta):
    t = x2.shape[0]
    state_shapes = [jax.ShapeDtypeStruct(a.shape, F32) for a in state]
    row_spec = pl.BlockSpec((tm, D_MODEL), lambda i: (i, 0))
    state_specs = [pl.BlockSpec(a.shape, lambda i, nd=a.ndim: (0,) * nd) for a in state]
    return pl.pallas_call(
        functools.partial(_mixer_kernel, tm=tm, meta=meta),
        grid=(t // tm,),
        in_specs=[row_spec] + [_const_spec(c.shape) for c in consts] + state_specs,
        out_specs=[row_spec] + state_specs,
        out_shape=[jax.ShapeDtypeStruct((t, D_MODEL), F32)] + state_shapes,
        scratch_shapes=[
            pltpu.VMEM((tm, D_MODEL), BF16),
            pltpu.VMEM((tm + SUBLANES, 2 * QK), F32),
            pltpu.VMEM((tm, QK), F32),
            pltpu.VMEM((tm, QK), BF16),
            pltpu.VMEM((tm, QK), BF16),
            pltpu.VMEM((tm, HEADS * CN), BF16),
            pltpu.VMEM((tm, QK), BF16),
            pltpu.VMEM((tm, QK), BF16),
            pltpu.VMEM((tm, QK), BF16),
            pltpu.VMEM((tm, D_MODEL), BF16),
            pltpu.VMEM((tm, D_MODEL), F32),
            pltpu.VMEM((tm, D_MODEL), F32),
        ],
        compiler_params=pltpu.CompilerParams(
            dimension_semantics=("arbitrary",), vmem_limit_bytes=VMEM_LIMIT),
        name="mixer_meta" if meta else "mixer",
    )(x2, *consts, *state)


def _ffn_kernel(h_ref, g2_ref, wg_ref, wu_ref, wd_ref, gf_ref, o_ref, *, fc):
    h = h_ref[...]
    hn = (h * lax.rsqrt(jnp.mean(h * h, axis=-1, keepdims=True) + EPS) * g2_ref[...]).astype(BF16)
    acc = h
    for f in range(0, wg_ref.shape[1], fc):
        gate = _dot(hn, wg_ref[:, f:f + fc])
        up = _dot(hn, wu_ref[:, f:f + fc])
        acc = acc + _dot((gate * _sigmoid(gate) * up).astype(BF16), wd_ref[f:f + fc, :])
    o_ref[...] = acc * lax.rsqrt(jnp.mean(acc * acc, axis=-1, keepdims=True) + EPS) * gf_ref[...]


def _ffn_call(h1, g2, wg, wu, wd, gf, *, tm, fc):
    t = h1.shape[0]
    row_spec = pl.BlockSpec((tm, D_MODEL), lambda i: (i, 0))
    return pl.pallas_call(
        functools.partial(_ffn_kernel, fc=fc),
        grid=(t // tm,),
        in_specs=[row_spec] + [_const_spec(a.shape) for a in (g2, wg, wu, wd, gf)],
        out_specs=row_spec,
        out_shape=jax.ShapeDtypeStruct((t, D_MODEL), F32),
        compiler_params=pltpu.CompilerParams(
            dimension_semantics=("arbitrary",), vmem_limit_bytes=VMEM_LIMIT),
        name="ffn",
    )(h1, g2, wg, wu, wd, gf)


def _split_w_in(w):
    d = w.shape[0]
    wide = [w[:, a:b].astype(BF16) for a, b in W_IN_RANGES]
    zeros = lambda n: jnp.zeros((d, n), w.dtype)
    small = jnp.concatenate([w[:, 2048:2052], zeros(GA_LANE - HEADS), w[:, 5128:5144],
                             zeros(LANES - GA_LANE - G_RANK), w[:, 2052:2056], zeros(LANES - HEADS)], axis=1)
    return wide + [small.astype(BF16)]


def kernel(x, meta_tokens, norm1_g, w_in, conv_w, conv_b, m_gate_b, g_a2, g_a2_b, m_head_g, g_head_g,
           w_branch_m, w_branch_g, w_out, norm2_g, w_ff_gate, w_ff_up, w_ff_down, final_g):
    bsz, seq, d = x.shape
    assert bsz == 1 and d == D_MODEL and norm1_g.shape[0] == 1 and seq % CHUNK == 0
    row = lambda a: a.reshape(1, -1).astype(F32)

    gate_bias = jnp.zeros((SUBLANES, LANES), F32)
    gate_bias = gate_bias.at[0, :HEADS].set(m_gate_b[0, 0]).at[1, :HEADS].set(m_gate_b[0, 1])
    a2 = jnp.zeros((LANES, QK), F32).at[GA_LANE:GA_LANE + G_RANK].set(g_a2[0]).astype(BF16)
    conv_w8 = jnp.zeros((SUBLANES, 2 * QK), F32).at[:CONV_W].set(conv_w[0])
    consts = (row(norm1_g[0]), *_split_w_in(w_in[0]), gate_bias, a2, row(g_a2_b[0]), conv_w8, row(conv_b[0]),
              row(m_head_g[0]), row(g_head_g[0]),
              w_branch_m[0].astype(BF16), w_branch_g[0].astype(BF16), w_out[0].astype(BF16))

    zero_state = (jnp.zeros((SUBLANES, 2 * QK), F32), jnp.zeros((HEADS, DK, CN), F32),
                  jnp.zeros((SUBLANES, LANES), F32), jnp.zeros((HEADS, DV, DK), F32))
    lead = jnp.concatenate([jnp.zeros((CHUNK - N_META, d), F32), meta_tokens.astype(F32)], axis=0)
    _, *state = _mixer_call(lead, consts, zero_state, tm=CHUNK, meta=True)
    h1, *_ = _mixer_call(x[0], consts, tuple(state), tm=256, meta=False)
    out = _ffn_call(h1, row(norm2_g[0]), w_ff_gate[0].astype(BF16), w_ff_up[0].astype(BF16),
                    w_ff_down[0].astype(BF16), row(final_g), tm=512, fc=256)
    return out[None]
```

```python
import functools

import jax
import jax.numpy as jnp
from jax import lax
from jax.experimental import pallas as pl
from jax.experimental.pallas import tpu as pltpu

F32 = jnp.float32
BF16 = jnp.bfloat16

D_MODEL = 1024
N_META = 16
CHUNK = 64
RCHUNK = 128
EPS = 1e-6
HEADS = 4
DV = D_MODEL // HEADS
DK = DV // 2
QK = HEADS * DK
G_RANK = 16
G_TAU = 16.0
CONV_W = 4
LANES = 128
SUBLANES = 8
BF16_ROWS = 16
NEG_BIG = -1e30
VMEM_LIMIT = 56 * 1024 * 1024
CN = DV + LANES

W_IN_RANGES = ((0, 2048), (2056, 5128), (5144, 8216))
O_MQK, O_MV = (0, 0), (0, 1024)
O_MO, O_GQ, O_GK, O_GV = (1, 0), (1, 1024), (1, 1536), (1, 2048)
O_GR, O_GATE_M, O_GATE_G = (2, 0), (2, 1024), (2, 2048)
O_S1, O_S2 = (3, 0), (3, LANES)
GA_LANE = 8

NT_DIMS = (((1,), (1,)), ((), ()))
TN_DIMS = (((0,), (0,)), ((), ()))


def _dot(a, b):
    return jnp.dot(a, b, preferred_element_type=F32)


def _dot_nt(a, b):
    return lax.dot_general(a, b, NT_DIMS, preferred_element_type=F32)


def _dot_tn(a, b):
    return lax.dot_general(a, b, TN_DIMS, preferred_element_type=F32)


def _split3(x):
    hi = x.astype(BF16)
    r = x - hi.astype(F32)
    mid = r.astype(BF16)
    lo = (r - mid.astype(F32)).astype(BF16)
    return jnp.concatenate([hi, mid, lo], axis=1)


def _log_sigmoid(x):
    return jnp.minimum(x, 0.0) - jnp.log(1.0 + jnp.exp(-jnp.abs(x)))


def _sigmoid(x):
    return 1.0 / (1.0 + jnp.exp(-x))


def _mixer_kernel(x_ref, g1_ref, wa_ref, wb_ref, wc_ref, ws_ref, sb_ref, a2_ref, a2b_ref, cw_ref, cb_ref,
                  mhg_ref, ghg_ref, wbm_ref, wbg_ref, wout_ref, tail0_ref, cn0_ref, m0_ref, s0_ref,
                  h1_ref, tail_ref, cn_ref, m_ref, s_ref,
                  xn_s, qk_s, q_s, k_s, wk_s, vo_s, gq64_s, gq128_s, gki_s, gke64_s, gke128_s, gv_s, hm_s, hg_s,
                  *, tm, meta):
    nc = tm // RCHUNK
    pairs = [(c, h) for c in range(nc) for h in range(HEADS)]

    @pl.when(pl.program_id(0) == 0)
    def _():
        qk_s[0:SUBLANES, :] = tail0_ref[...]
        cn_ref[...] = cn0_ref[...]
        m_ref[...] = m0_ref[...]
        s_ref[...] = s0_ref[...]
        for h in range(HEADS):
            vo_s[:, h * CN + DV:(h + 1) * CN] = jnp.ones((tm, LANES), BF16)

    x = x_ref[...]
    xn = x * lax.rsqrt(jnp.mean(x * x, axis=-1, keepdims=True) + EPS) * g1_ref[...]
    if meta:
        valid = lax.broadcasted_iota(jnp.int32, (tm, 1), 0) >= (tm - N_META)
        xn = jnp.where(valid, xn, 0.0)
    xn_s[...] = xn.astype(BF16)

    w_refs = (wa_ref, wb_ref, wc_ref, ws_ref)

    def proj(seg, width):
        return _dot(xn_s[...], w_refs[seg[0]][:, seg[1]:seg[1] + width])

    def rows(c):
        return slice(c * RCHUNK, (c + 1) * RCHUNK)

    def half(c, second):
        lo = c * RCHUNK + (CHUNK if second else 0)
        return slice(lo, lo + CHUNK)

    row = lax.broadcasted_iota(jnp.int32, (tm, tm), 0)
    col = lax.broadcasted_iota(jnp.int32, (tm, tm), 1)
    tri_r = jnp.where((col <= row) & (col >= (row & -RCHUNK)), 1.0, 0.0).astype(BF16)
    tri_c = jnp.where((col <= row) & (col >= (row & -CHUNK)), 1.0, 0.0).astype(BF16)
    crow = lax.broadcasted_iota(jnp.int32, (RCHUNK, RCHUNK), 0)
    ccol = lax.broadcasted_iota(jnp.int32, (RCHUNK, RCHUNK), 1)
    causal = ccol <= crow
    sr = lax.broadcasted_iota(jnp.int32, (3 * LANES, HEADS * LANES), 0)
    sc = lax.broadcasted_iota(jnp.int32, (3 * LANES, HEADS * LANES), 1)
    spread = jnp.where((sr & (LANES - 1)) == (sc >> 7), 1.0, 0.0).astype(BF16)

    def lane_tiles(x128):
        return _dot(_split3(x128), spread)

    s1 = proj(O_S1, LANES)
    logi = s1 + sb_ref[0:1, :]
    logf = _log_sigmoid(proj(O_S2, LANES) + sb_ref[1:2, :])
    if meta:
        logi = jnp.where(valid, logi, NEG_BIG)
        logf = jnp.where(valid, logf, 0.0)
    b3 = _dot(tri_r, _split3(logf))
    b_all = b3[:, :LANES] + b3[:, LANES:2 * LANES] + b3[:, 2 * LANES:]
    c_all = logi - b_all
    m_run = m_ref[0:1, :]
    inter_parts, w_parts, a_rows = [], [], []
    for c in range(nc):
        b_c = b_all[rows(c), :]
        g_c = b_c[RCHUNK - 1:RCHUNK, :]
        wlog = g_c + c_all[rows(c), :]
        m_next = jnp.maximum(g_c + m_run, jnp.max(wlog, axis=0, keepdims=True))
        inter_parts.append(b_c + m_run)
        w_parts.append(jnp.exp(wlog - m_next))
        a_rows.append(jnp.exp(g_c + m_run - m_next))
        m_run = m_next
    m_ref[...] = jnp.broadcast_to(m_run, (SUBLANES, LANES))
    b_t = lane_tiles(b_all)
    il_t = lane_tiles(jnp.concatenate(inter_parts, axis=0))
    w_t = lane_tiles(jnp.concatenate(w_parts, axis=0))
    a_t = lane_tiles(jnp.concatenate(
        [jnp.broadcast_to(a, (BF16_ROWS, LANES)) for a in a_rows], axis=0))

    qk_s[SUBLANES:SUBLANES + tm, :] = proj(O_MQK, 2 * QK)
    conv = cb_ref[...]
    for j in range(CONV_W):
        conv = conv + cw_ref[j:j + 1, :] * qk_s[pl.ds(SUBLANES - (CONV_W - 1) + j, tm), :]
    tail_new = qk_s[tm:tm + SUBLANES, :]
    qk_s[0:SUBLANES, :] = tail_new
    tail_ref[...] = tail_new
    conv = conv * _sigmoid(conv)
    q_s[...] = conv[:, :QK] * (DK ** -0.5)
    k_s[...] = conv[:, QK:].astype(BF16)
    wk_s[...] = (w_t * conv[:, QK:]).astype(BF16)
    mv = proj(O_MV, D_MODEL).astype(BF16)
    for h in range(HEADS):
        vo_s[:, h * CN:h * CN + DV] = mv[:, h * DV:(h + 1) * DV]

    za = _dot(s1.astype(BF16), a2_ref[...]) + a2b_ref[...]
    loga = _log_sigmoid(za) / G_TAU
    if meta:
        loga = jnp.where(valid, loga, 0.0)
    hi = loga.astype(BF16)
    r1 = loga - hi.astype(F32)
    mid = r1.astype(BF16)
    lo = (r1 - mid.astype(F32)).astype(BF16)
    bc = _dot(tri_c, hi) + _dot(tri_c, mid) + _dot(tri_c, lo)
    e_blk = jnp.exp(bc)
    tot_a = [bc[c * RCHUNK + CHUNK - 1:c * RCHUNK + CHUNK, :] for c in range(nc)]
    tot_b = [bc[(c + 1) * RCHUNK - 1:(c + 1) * RCHUNK, :] for c in range(nc)]
    e_a = [jnp.exp(t) for t in tot_a]
    e_b = [jnp.exp(t) for t in tot_b]
    rest = jnp.concatenate([t - bc[half(c, s), :] for c in range(nc) for s, t in ((0, tot_a[c]), (1, tot_b[c]))],
                           axis=0)
    e_rest = jnp.exp(rest)
    ones_row = jnp.ones((1, QK), F32)
    to_chunk = jnp.concatenate([jnp.broadcast_to(f, (CHUNK, QK)) for c in range(nc) for f in (ones_row, e_a[c])],
                               axis=0)
    from_blk = jnp.concatenate([jnp.broadcast_to(f, (CHUNK, QK)) for c in range(nc) for f in (e_b[c], ones_row)],
                               axis=0)
    gq = proj(O_GQ, QK) * (DK ** -0.5)
    gk = proj(O_GK, QK)
    gq64_s[...] = (gq * e_blk).astype(BF16)
    gq128_s[...] = (gq * e_blk * to_chunk).astype(BF16)
    gki_s[...] = (gk * jnp.exp(-bc)).astype(BF16)
    gke64_s[...] = (gk * e_rest).astype(BF16)
    gke128_s[...] = (gk * e_rest * from_blk).astype(BF16)
    gv_s[...] = proj(O_GV, D_MODEL).astype(BF16)
    e_rows = [(e_a[c] * e_b[c])[:, h * DK:(h + 1) * DK] for (c, h) in pairs]
    e_cols = jnp.concatenate(e_rows + [jnp.zeros((LANES - len(pairs), DK), F32)], axis=0).T

    c_row = c_all.T
    dmat, rmax, sim = {}, {}, {}
    for (c, h) in pairs:
        d = b_t[rows(c), h * LANES:(h + 1) * LANES] + c_row[h:h + 1, rows(c)]
        d = jnp.where(causal, d, -jnp.inf)
        dmat[c, h] = d
        rmax[c, h] = jnp.max(d, axis=-1, keepdims=True)
    for (c, h) in pairs:
        sim[c, h] = _dot_nt(q_s[rows(c), h * DK:(h + 1) * DK].astype(BF16), k_s[rows(c), h * DK:(h + 1) * DK])
    lhs, emr, upd = {}, {}, {}
    for (c, h) in pairs:
        il = il_t[rows(c), h * LANES:(h + 1) * LANES]
        m_row = jnp.maximum(il, rmax[c, h])
        wts = jnp.exp(dmat[c, h] - m_row) * sim[c, h]
        aq = jnp.exp(il - m_row) * q_s[rows(c), h * DK:(h + 1) * DK]
        lhs[c, h] = jnp.concatenate([aq.astype(BF16), wts.astype(BF16)], axis=1)
        emr[c, h] = jnp.exp(-m_row)
    for (c, h) in pairs:
        upd[c, h] = _dot_tn(wk_s[rows(c), h * DK:(h + 1) * DK], vo_s[rows(c), h * CN:(h + 1) * CN])
    res = {}
    for h in range(HEADS):
        state = cn_ref[h]
        for c in range(nc):
            rhs = jnp.concatenate([state.astype(BF16), vo_s[rows(c), h * CN:(h + 1) * CN]], axis=0)
            res[c, h] = _dot(lhs[c, h], rhs)
            a = a_t[BF16_ROWS * c:BF16_ROWS * c + 1, h * LANES:(h + 1) * LANES]
            state = jnp.concatenate([a] * (CN // LANES), axis=1) * state + upd[c, h]
        cn_ref[h] = state
    for (c, h) in pairs:
        inv = 1.0 / jnp.maximum(jnp.abs(res[c, h][:, DV:]), emr[c, h])
        hm_s[rows(c), h * DV:h * DV + LANES] = res[c, h][:, :LANES] * inv
        hm_s[rows(c), h * DV + LANES:(h + 1) * DV] = res[c, h][:, LANES:DV] * inv

    att, gupd = {}, {}
    zblk = jnp.zeros((CHUNK, DK), BF16)
    for (c, h) in pairs:
        dsl = slice(h * DK, (h + 1) * DK)
        ra, rb = half(c, 0), half(c, 1)
        qp = jnp.concatenate([jnp.concatenate([gq64_s[ra, dsl], zblk], axis=1),
                              jnp.concatenate([zblk, gq64_s[rb, dsl]], axis=1)], axis=0)
        kp = jnp.concatenate([jnp.concatenate([gki_s[ra, dsl], gke64_s[ra, dsl]], axis=1),
                              jnp.concatenate([zblk, gki_s[rb, dsl]], axis=1)], axis=0)
        att[c, h] = jnp.where(causal, _dot_nt(qp, kp), 0.0).astype(BF16)
    for (c, h) in pairs:
        gupd[c, h] = _dot_tn(gke128_s[rows(c), h * DK:(h + 1) * DK], gv_s[rows(c), h * DV:(h + 1) * DV])
    for h in range(HEADS):
        sst = s_ref[h]
        for c in range(nc):
            lhs_g = jnp.concatenate([gq128_s[rows(c), h * DK:(h + 1) * DK], att[c, h]], axis=1)
            rhs_g = jnp.concatenate([sst.astype(BF16), gv_s[rows(c), h * DV:(h + 1) * DV]], axis=0)
            hg_s[rows(c), h * DV:(h + 1) * DV] = _dot(lhs_g, rhs_g)
            i = c * HEADS + h
            sst = e_cols[:, i:i + 1] * sst + gupd[c, h]
        s_ref[h] = sst

    if meta:
        h1_ref[...] = x
        return

    def head_norm(src, gain_ref):
        outs = []
        for h in range(HEADS):
            t = src[:, h * DV:(h + 1) * DV]
            outs.append(t * lax.rsqrt(jnp.mean(t * t, axis=-1, keepdims=True) + EPS))
        return jnp.concatenate(outs, axis=-1) * gain_ref[...]

    y_m = head_norm(hm_s, mhg_ref) * _sigmoid(proj(O_MO, D_MODEL))
    branch_m = _dot(y_m.astype(BF16), wbm_ref[...])
    merged = _sigmoid(proj(O_GATE_M, D_MODEL)) * branch_m
    gr = proj(O_GR, D_MODEL)
    y_g = head_norm(hg_s, ghg_ref) * (gr * _sigmoid(gr))
    branch_g = _dot(y_g.astype(BF16), wbg_ref[...])
    merged = merged + _sigmoid(proj(O_GATE_G, D_MODEL)) * branch_g
    h1_ref[...] = x + _dot(merged.astype(BF16), wout_ref[...])


def _const_spec(shape):
    nd = len(shape)
    return pl.BlockSpec(shape, lambda i: (0,) * nd, pipeline_mode=pl.Buffered(1))


def _mixer_call(x2, consts, state, *, tm, meta):
    t = x2.shape[0]
    state_shapes = [jax.ShapeDtypeStruct(a.shape, F32) for a in state]
    row_spec = pl.BlockSpec((tm, D_MODEL), lambda i: (i, 0))
    state_specs = [pl.BlockSpec(a.shape, lambda i, nd=a.ndim: (0,) * nd) for a in state]
    return pl.pallas_call(
        functools.partial(_mixer_kernel, tm=tm, meta=meta),
        grid=(t // tm,),
        in_specs=[row_spec] + [_const_spec(c.shape) for c in consts] + state_specs,
        out_specs=[row_spec] + state_specs,
        out_shape=[jax.ShapeDtypeStruct((t, D_MODEL), F32)] + state_shapes,
        scratch_shapes=[
            pltpu.VMEM((tm, D_MODEL), BF16),
            pltpu.VMEM((tm + SUBLANES, 2 * QK), F32),
            pltpu.VMEM((tm, QK), F32),
            pltpu.VMEM((tm, QK), BF16),
            pltpu.VMEM((tm, QK), BF16),
            pltpu.VMEM((tm, HEADS * CN), BF16),
            pltpu.VMEM((tm, QK), BF16),
            pltpu.VMEM((tm, QK), BF16),
            pltpu.VMEM((tm, QK), BF16),
            pltpu.VMEM((tm, QK), BF16),
            pltpu.VMEM((tm, QK), BF16),
            pltpu.VMEM((tm, D_MODEL), BF16),
            pltpu.VMEM((tm, D_MODEL), F32),
            pltpu.VMEM((tm, D_MODEL), F32),
        ],
        compiler_params=pltpu.CompilerParams(
            dimension_semantics=("arbitrary",), vmem_limit_bytes=VMEM_LIMIT),
        name="mixer_meta" if meta else "mixer",
    )(x2, *consts, *state)


def _ffn_kernel(h_ref, g2_ref, wg_ref, wu_ref, wd_ref, gf_ref, o_ref, *, fc):
    h = h_ref[...]
    hn = (h * lax.rsqrt(jnp.mean(h * h, axis=-1, keepdims=True) + EPS) * g2_ref[...]).astype(BF16)
    acc = h
    for f in range(0, wg_ref.shape[1], fc):
        gate = _dot(hn, wg_ref[:, f:f + fc])
        up = _dot(hn, wu_ref[:, f:f + fc])
        acc = acc + _dot((gate * _sigmoid(gate) * up).astype(BF16), wd_ref[f:f + fc, :])
    o_ref[...] = acc * lax.rsqrt(jnp.mean(acc * acc, axis=-1, keepdims=True) + EPS) * gf_ref[...]


def _ffn_call(h1, g2, wg, wu, wd, gf, *, tm, fc):
    t = h1.shape[0]
    row_spec = pl.BlockSpec((tm, D_MODEL), lambda i: (i, 0))
    return pl.pallas_call(
        functools.partial(_ffn_kernel, fc=fc),
        grid=(t // tm,),
        in_specs=[row_spec] + [_const_spec(a.shape) for a in (g2, wg, wu, wd, gf)],
        out_specs=row_spec,
        out_shape=jax.ShapeDtypeStruct((t, D_MODEL), F32),
        compiler_params=pltpu.CompilerParams(
            dimension_semantics=("arbitrary",), vmem_limit_bytes=VMEM_LIMIT),
        name="ffn",
    )(h1, g2, wg, wu, wd, gf)


def _split_w_in(w):
    d = w.shape[0]
    wide = [w[:, a:b].astype(BF16) for a, b in W_IN_RANGES]
    zeros = lambda n: jnp.zeros((d, n), w.dtype)
    small = jnp.concatenate([w[:, 2048:2052], zeros(GA_LANE - HEADS), w[:, 5128:5144],
                             zeros(LANES - GA_LANE - G_RANK), w[:, 2052:2056], zeros(LANES - HEADS)], axis=1)
    return wide + [small.astype(BF16)]


def kernel(x, meta_tokens, norm1_g, w_in, conv_w, conv_b, m_gate_b, g_a2, g_a2_b, m_head_g, g_head_g,
           w_branch_m, w_branch_g, w_out, norm2_g, w_ff_gate, w_ff_up, w_ff_down, final_g):
    bsz, seq, d = x.shape
    assert bsz == 1 and d == D_MODEL and norm1_g.shape[0] == 1 and seq % CHUNK == 0
    row = lambda a: a.reshape(1, -1).astype(F32)

    gate_bias = jnp.zeros((SUBLANES, LANES), F32)
    gate_bias = gate_bias.at[0, :HEADS].set(m_gate_b[0, 0]).at[1, :HEADS].set(m_gate_b[0, 1])
    a2 = jnp.zeros((LANES, QK), F32).at[GA_LANE:GA_LANE + G_RANK].set(g_a2[0]).astype(BF16)
    conv_w8 = jnp.zeros((SUBLANES, 2 * QK), F32).at[:CONV_W].set(conv_w[0])
    consts = (row(norm1_g[0]), *_split_w_in(w_in[0]), gate_bias, a2, row(g_a2_b[0]), conv_w8, row(conv_b[0]),
              row(m_head_g[0]), row(g_head_g[0]),
              w_branch_m[0].astype(BF16), w_branch_g[0].astype(BF16), w_out[0].astype(BF16))

    zero_state = (jnp.zeros((SUBLANES, 2 * QK), F32), jnp.zeros((HEADS, DK, CN), F32),
                  jnp.zeros((SUBLANES, LANES), F32), jnp.zeros((HEADS, DK, DV), F32))
    lead = jnp.concatenate([jnp.zeros((RCHUNK - N_META, d), F32), meta_tokens.astype(F32)], axis=0)
    _, *state = _mixer_call(lead, consts, zero_state, tm=RCHUNK, meta=True)
    h1, *_ = _mixer_call(x[0], consts, tuple(state), tm=256, meta=False)
    out = _ffn_call(h1, row(norm2_g[0]), w_ff_gate[0].astype(BF16), w_ff_up[0].astype(BF16),
                    w_ff_down[0].astype(BF16), row(final_g), tm=512, fc=256)
    return out[None]
```

```python
import functools

import jax
import jax.numpy as jnp
from jax import lax
from jax.experimental import pallas as pl
from jax.experimental.pallas import tpu as pltpu

F32 = jnp.float32
BF16 = jnp.bfloat16

D_MODEL = 1024
N_META = 16
CHUNK = 64
RCHUNK = 128
EPS = 1e-6
HEADS = 4
DV = D_MODEL // HEADS
DK = DV // 2
QK = HEADS * DK
G_RANK = 16
G_TAU = 16.0
CONV_W = 4
LANES = 128
SUBLANES = 8
BF16_ROWS = 16
NEG_BIG = -1e30
VMEM_LIMIT = 56 * 1024 * 1024
CN = DV + LANES

W_IN_RANGES = ((0, 2048), (2056, 5128), (5144, 8216))
O_MQK, O_MV = (0, 0), (0, 1024)
O_MO, O_GQ, O_GK, O_GV = (1, 0), (1, 1024), (1, 1536), (1, 2048)
O_GR, O_GATE_M, O_GATE_G = (2, 0), (2, 1024), (2, 2048)
O_S1 = (3, 0)
GA_LANE = 8

NT_DIMS = (((1,), (1,)), ((), ()))
TN_DIMS = (((0,), (0,)), ((), ()))


def _dot(a, b):
    return jnp.dot(a, b, preferred_element_type=F32)


def _dot_nt(a, b):
    return lax.dot_general(a, b, NT_DIMS, preferred_element_type=F32)


def _dot_tn(a, b):
    return lax.dot_general(a, b, TN_DIMS, preferred_element_type=F32)


def _pieces(x):
    hi = x.astype(BF16).astype(F32)
    mid = (x - hi).astype(BF16).astype(F32)
    lo = (x - hi - mid).astype(BF16).astype(F32)
    return [hi, mid, lo]


def _log_sigmoid(x):
    return jnp.minimum(x, 0.0) - jnp.log(1.0 + jnp.exp(-jnp.abs(x)))


def _sigmoid(x):
    return 1.0 / (1.0 + jnp.exp(-x))


def _mixer_kernel(x_ref, g1_ref, wa_ref, wb_ref, wc_ref, ws_ref, wst_ref, sb_ref, a2_ref, a2b_ref, cw_ref, cb_ref,
                  mhg_ref, ghg_ref, wbm_ref, wbg_ref, wout_ref, tail0_ref, cn0_ref, m0_ref, s0_ref,
                  h1_ref, tail_ref, cn_ref, m_ref, s_ref,
                  xn_s, qk_s, q_s, k_s, wk_s, vo_s, gq64_s, gq128_s, gki_s, gke64_s, gke128_s, gv_s, hm_s, hg_s,
                  ga_s, gb_s, gc_s, gd_s, gqk_s, tric_s, triu_s, spread_s, *, tm, meta):
    nc = tm // RCHUNK
    pairs = [(c, h) for c in range(nc) for h in range(HEADS)]

    @pl.when(pl.program_id(0) == 0)
    def _():
        qk_s[0:SUBLANES, :] = tail0_ref[...]
        cn_ref[...] = cn0_ref[...]
        m_ref[...] = m0_ref[...]
        s_ref[...] = s0_ref[...]
        for h in range(HEADS):
            vo_s[:, h * CN + DV:(h + 1) * CN] = jnp.ones((tm, LANES), BF16)
        row = lax.broadcasted_iota(jnp.int32, (tm, tm), 0)
        col = lax.broadcasted_iota(jnp.int32, (tm, tm), 1)
        tric_s[...] = jnp.where((col <= row) & (col >= (row & -CHUNK)), 1.0, 0.0).astype(BF16)
        triu_s[...] = jnp.where((row <= col) & (row >= (col & -RCHUNK)), 1.0, 0.0).astype(BF16)
        sr = lax.broadcasted_iota(jnp.int32, (LANES, 3 * HEADS * LANES), 0)
        sc = lax.broadcasted_iota(jnp.int32, (LANES, 3 * HEADS * LANES), 1)
        blk = (sc >> 9) * (3 * SUBLANES)
        spread_s[...] = jnp.where(
            (sr >= blk) & (sr < blk + 3 * SUBLANES) & ((sr & (SUBLANES - 1)) == ((sc >> 7) & (HEADS - 1))),
            1.0, 0.0).astype(BF16)

    x = x_ref[...]
    xn = x * lax.rsqrt(jnp.mean(x * x, axis=-1, keepdims=True) + EPS) * g1_ref[...]
    if meta:
        valid = lax.broadcasted_iota(jnp.int32, (tm, 1), 0) >= (tm - N_META)
        xn = jnp.where(valid, xn, 0.0)
    xn_s[...] = xn.astype(BF16)

    w_refs = (wa_ref, wb_ref, wc_ref, ws_ref)

    def proj(seg, width):
        return _dot(xn_s[...], w_refs[seg[0]][:, seg[1]:seg[1] + width])

    def rows(c):
        return slice(c * RCHUNK, (c + 1) * RCHUNK)

    def half(c, second):
        lo = c * RCHUNK + (CHUNK if second else 0)
        return slice(lo, lo + CHUNK)

    crow = lax.broadcasted_iota(jnp.int32, (RCHUNK, RCHUNK), 0)
    ccol = lax.broadcasted_iota(jnp.int32, (RCHUNK, RCHUNK), 1)
    causal = ccol <= crow
    tri_c = tric_s[...]
    tri_up = triu_s[...]
    spread = spread_s[...]

    def st_qk(i, v):
        qk_s[SUBLANES:SUBLANES + tm, i * DV:(i + 1) * DV] = v

    def st_mv(i, v):
        vo_s[:, i * CN:i * CN + DV] = v.astype(BF16)

    def st_gqk(i, v):
        gqk_s[:, i * DV:(i + 1) * DV] = v

    def st_gv(i, v):
        gv_s[:, i * DV:(i + 1) * DV] = v.astype(BF16)

    def st_mo(i, v):
        ga_s[:, i * DV:(i + 1) * DV] = _sigmoid(v)

    def st_gm(i, v):
        gb_s[:, i * DV:(i + 1) * DV] = _sigmoid(v)

    def st_gr(i, v):
        gc_s[:, i * DV:(i + 1) * DV] = v * _sigmoid(v)

    def st_gg(i, v):
        gd_s[:, i * DV:(i + 1) * DV] = _sigmoid(v)

    groups = [(O_MQK, st_qk), (O_MV, st_mv), (O_GQ, st_gqk), (O_GV, st_gv)]
    if not meta:
        groups += [(O_MO, st_mo), (O_GATE_M, st_gm), (O_GR, st_gr), (O_GATE_G, st_gg)]
    queue = [(seg, i, store) for seg, store in groups for i in range(D_MODEL // DV)]
    emitted = [0]

    def fill(n):
        for _ in range(n):
            if emitted[0] < len(queue):
                seg, i, store = queue[emitted[0]]
                store(i, proj((seg[0], seg[1] + i * DV), DV))
                emitted[0] += 1

    def finish_groups(n):
        fill(n * (D_MODEL // DV) - emitted[0])

    gates = _dot_nt(wst_ref[...], xn_s[...]) + jnp.concatenate([sb_ref[...]] * (tm // LANES), axis=1)
    s1 = proj(O_S1, LANES)
    fill(2)
    logi = gates[:SUBLANES, :]
    logf = _log_sigmoid(gates[SUBLANES:, :])
    if meta:
        valid_t = lax.broadcasted_iota(jnp.int32, (1, tm), 1) >= (tm - N_META)
        logi = jnp.where(valid_t, logi, NEG_BIG)
        logf = jnp.where(valid_t, logf, 0.0)
    zero8 = jnp.zeros((SUBLANES, tm), F32)
    b4 = _dot(jnp.concatenate(_pieces(logf) + [zero8], axis=0).astype(BF16), tri_up)
    fill(2)
    b_all = b4[:SUBLANES] + b4[SUBLANES:2 * SUBLANES] + b4[2 * SUBLANES:3 * SUBLANES]
    c_all = logi - b_all
    m_run = m_ref[...]
    inter_parts, w_parts, a_chunk = [], [], []
    for c in range(nc):
        b_c = b_all[:, rows(c)]
        g_c = jnp.broadcast_to(b_c[:, RCHUNK - 1:RCHUNK], (SUBLANES, LANES))
        wlog = g_c + c_all[:, rows(c)]
        m_next = jnp.maximum(g_c + m_run, jnp.max(wlog, axis=1, keepdims=True))
        inter_parts.append(b_c + m_run)
        w_parts.append(jnp.exp(wlog - m_next))
        a_chunk.append(jnp.exp(g_c + m_run - m_next))
        m_run = m_next
    m_ref[...] = m_run
    stacked = jnp.concatenate(
        _pieces(b_all) + _pieces(jnp.concatenate(inter_parts, axis=1)) + _pieces(jnp.concatenate(w_parts, axis=1))
        + [jnp.zeros((LANES - 9 * SUBLANES, tm), F32)], axis=0)
    stacked_t = stacked.T.astype(BF16)
    fill(2)
    tiles = _dot(stacked_t, spread)
    fill(2)
    b_t, il_t, w_t = (tiles[:, i * QK:(i + 1) * QK] for i in range(3))

    finish_groups(1)
    for blk in range(2 * QK // DV):
        csl = slice(blk * DV, (blk + 1) * DV)
        conv = cb_ref[:, csl]
        for j in range(CONV_W):
            conv = conv + cw_ref[j:j + 1, csl] * qk_s[pl.ds(SUBLANES - (CONV_W - 1) + j, tm), csl]
        conv = conv * _sigmoid(conv)
        if blk < QK // DV:
            q_s[:, csl] = conv * (DK ** -0.5)
        else:
            ksl = slice(blk * DV - QK, (blk + 1) * DV - QK)
            k_s[:, ksl] = conv.astype(BF16)
            wk_s[:, ksl] = (w_t[:, ksl] * conv).astype(BF16)
        fill(1)
    tail_new = qk_s[tm:tm + SUBLANES, :]
    qk_s[0:SUBLANES, :] = tail_new
    tail_ref[...] = tail_new

    za = _dot(s1.astype(BF16), a2_ref[...]) + a2b_ref[...]
    fill(1)
    loga = _log_sigmoid(za) / G_TAU
    if meta:
        loga = jnp.where(valid, loga, 0.0)
    hi = loga.astype(BF16)
    r1 = loga - hi.astype(F32)
    mid = r1.astype(BF16)
    lo = (r1 - mid.astype(F32)).astype(BF16)
    bc = _dot(tri_c, hi) + _dot(tri_c, mid) + _dot(tri_c, lo)
    fill(2)
    e_blk = jnp.exp(bc)
    tot_a = [bc[c * RCHUNK + CHUNK - 1:c * RCHUNK + CHUNK, :] for c in range(nc)]
    tot_b = [bc[(c + 1) * RCHUNK - 1:(c + 1) * RCHUNK, :] for c in range(nc)]
    e_a = [jnp.exp(t) for t in tot_a]
    e_b = [jnp.exp(t) for t in tot_b]
    rest = jnp.concatenate([t - bc[half(c, s), :] for c in range(nc) for s, t in ((0, tot_a[c]), (1, tot_b[c]))],
                           axis=0)
    e_rest = jnp.exp(rest)
    ones_row = jnp.ones((1, QK), F32)
    to_chunk = jnp.concatenate([jnp.broadcast_to(f, (CHUNK, QK)) for c in range(nc) for f in (ones_row, e_a[c])],
                               axis=0)
    from_blk = jnp.concatenate([jnp.broadcast_to(f, (CHUNK, QK)) for c in range(nc) for f in (e_b[c], ones_row)],
                               axis=0)
    finish_groups(3)
    gq = gqk_s[:, :QK] * (DK ** -0.5)
    gk = gqk_s[:, QK:]
    gq64_s[...] = (gq * e_blk).astype(BF16)
    gq128_s[...] = (gq * e_blk * to_chunk).astype(BF16)
    gki_s[...] = (gk * jnp.exp(-bc)).astype(BF16)
    gke64_s[...] = (gk * e_rest).astype(BF16)
    gke128_s[...] = (gk * e_rest * from_blk).astype(BF16)
    fill(2)
    e_rows = [(e_a[c] * e_b[c])[:, h * DK:(h + 1) * DK] for (c, h) in pairs]
    e_cols = jnp.concatenate(e_rows + [jnp.zeros((LANES - len(pairs), DK), F32)], axis=0).T

    finish_groups(2)
    dmat, rmax, sim = {}, {}, {}
    for (c, h) in pairs:
        d = b_t[rows(c), h * LANES:(h + 1) * LANES] + c_all[h:h + 1, rows(c)]
        d = jnp.where(causal, d, -jnp.inf)
        dmat[c, h] = d
        rmax[c, h] = jnp.max(d, axis=-1, keepdims=True)
    fill(1)
    for (c, h) in pairs:
        sim[c, h] = _dot_nt(q_s[rows(c), h * DK:(h + 1) * DK].astype(BF16), k_s[rows(c), h * DK:(h + 1) * DK])
    fill(1)
    lhs, emr, upd = {}, {}, {}
    for (c, h) in pairs:
        il = il_t[rows(c), h * LANES:(h + 1) * LANES]
        m_row = jnp.maximum(il, rmax[c, h])
        wts = jnp.exp(dmat[c, h] - m_row) * sim[c, h]
        aq = jnp.exp(il - m_row) * q_s[rows(c), h * DK:(h + 1) * DK]
        lhs[c, h] = jnp.concatenate([aq.astype(BF16), wts.astype(BF16)], axis=1)
        emr[c, h] = jnp.exp(-m_row)
    fill(1)
    for (c, h) in pairs:
        upd[c, h] = _dot_tn(wk_s[rows(c), h * DK:(h + 1) * DK], vo_s[rows(c), h * CN:(h + 1) * CN])
    fill(1)
    res = {}
    for h in range(HEADS):
        state = cn_ref[h]
        for c in range(nc):
            rhs = jnp.concatenate([state.astype(BF16), vo_s[rows(c), h * CN:(h + 1) * CN]], axis=0)
            res[c, h] = _dot(lhs[c, h], rhs)
            a = a_chunk[c][h:h + 1, :]
            state = jnp.concatenate([a] * (CN // LANES), axis=1) * state + upd[c, h]
        cn_ref[h] = state
        fill(1)
    for (c, h) in pairs:
        inv = 1.0 / jnp.maximum(jnp.abs(res[c, h][:, DV:]), emr[c, h])
        hm_s[rows(c), h * DV:h * DV + LANES] = res[c, h][:, :LANES] * inv
        hm_s[rows(c), h * DV + LANES:(h + 1) * DV] = res[c, h][:, LANES:DV] * inv
    fill(1)

    finish_groups(4)
    att, gupd = {}, {}
    zblk = jnp.zeros((CHUNK, DK), BF16)
    for (c, h) in pairs:
        dsl = slice(h * DK, (h + 1) * DK)
        ra, rb = half(c, 0), half(c, 1)
        qp = jnp.concatenate([jnp.concatenate([gq64_s[ra, dsl], zblk], axis=1),
                              jnp.concatenate([zblk, gq64_s[rb, dsl]], axis=1)], axis=0)
        kp = jnp.concatenate([jnp.concatenate([gki_s[ra, dsl], gke64_s[ra, dsl]], axis=1),
                              jnp.concatenate([zblk, gki_s[rb, dsl]], axis=1)], axis=0)
        att[c, h] = jnp.where(causal, _dot_nt(qp, kp), 0.0).astype(BF16)
    fill(1)
    for (c, h) in pairs:
        gupd[c, h] = _dot_tn(gke128_s[rows(c), h * DK:(h + 1) * DK], gv_s[rows(c), h * DV:(h + 1) * DV])
    fill(1)
    for h in range(HEADS):
        sst = s_ref[h]
        for c in range(nc):
            lhs_g = jnp.concatenate([gq128_s[rows(c), h * DK:(h + 1) * DK], att[c, h]], axis=1)
            rhs_g = jnp.concatenate([sst.astype(BF16), gv_s[rows(c), h * DV:(h + 1) * DV]], axis=0)
            hg_s[rows(c), h * DV:(h + 1) * DV] = _dot(lhs_g, rhs_g)
            i = c * HEADS + h
            sst = e_cols[:, i:i + 1] * sst + gupd[c, h]
        s_ref[h] = sst
        fill(1)

    if meta:
        h1_ref[...] = x
        return

    def head_norm(src, gain_ref):
        outs = []
        for h in range(HEADS):
            t = src[:, h * DV:(h + 1) * DV]
            outs.append(t * lax.rsqrt(jnp.mean(t * t, axis=-1, keepdims=True) + EPS))
        return jnp.concatenate(outs, axis=-1) * gain_ref[...]

    finish_groups(len(groups))
    y_m = head_norm(hm_s, mhg_ref) * ga_s[...]
    merged = gb_s[...] * _dot(y_m.astype(BF16), wbm_ref[...])
    y_g = head_norm(hg_s, ghg_ref) * gc_s[...]
    merged = merged + gd_s[...] * _dot(y_g.astype(BF16), wbg_ref[...])
    h1_ref[...] = x + _dot(merged.astype(BF16), wout_ref[...])


def _const_spec(shape):
    nd = len(shape)
    return pl.BlockSpec(shape, lambda i: (0,) * nd, pipeline_mode=pl.Buffered(1))


def _mixer_call(x2, consts, state, *, tm, meta):
    t = x2.shape[0]
    state_shapes = [jax.ShapeDtypeStruct(a.shape, F32) for a in state]
    row_spec = pl.BlockSpec((tm, D_MODEL), lambda i: (i, 0))
    state_specs = [pl.BlockSpec(a.shape, lambda i, nd=a.ndim: (0,) * nd) for a in state]
    return pl.pallas_call(
        functools.partial(_mixer_kernel, tm=tm, meta=meta),
        grid=(t // tm,),
        in_specs=[row_spec] + [_const_spec(c.shape) for c in consts] + state_specs,
        out_specs=[row_spec] + state_specs,
        out_shape=[jax.ShapeDtypeStruct((t, D_MODEL), F32)] + state_shapes,
        scratch_shapes=[
            pltpu.VMEM((tm, D_MODEL), BF16),
            pltpu.VMEM((tm + SUBLANES, 2 * QK), F32),
            pltpu.VMEM((tm, QK), F32),
            pltpu.VMEM((tm, QK), BF16),
            pltpu.VMEM((tm, QK), BF16),
            pltpu.VMEM((tm, HEADS * CN), BF16),
            pltpu.VMEM((tm, QK), BF16),
            pltpu.VMEM((tm, QK), BF16),
            pltpu.VMEM((tm, QK), BF16),
            pltpu.VMEM((tm, QK), BF16),
            pltpu.VMEM((tm, QK), BF16),
            pltpu.VMEM((tm, D_MODEL), BF16),
            pltpu.VMEM((tm, D_MODEL), F32),
            pltpu.VMEM((tm, D_MODEL), F32),
            pltpu.VMEM((tm, D_MODEL), F32),
            pltpu.VMEM((tm, D_MODEL), F32),
            pltpu.VMEM((tm, D_MODEL), F32),
            pltpu.VMEM((tm, D_MODEL), F32),
            pltpu.VMEM((tm, D_MODEL), F32),
            pltpu.VMEM((tm, tm), BF16),
            pltpu.VMEM((tm, tm), BF16),
            pltpu.VMEM((LANES, 3 * HEADS * LANES), BF16),
        ],
        compiler_params=pltpu.CompilerParams(
            dimension_semantics=("arbitrary",), vmem_limit_bytes=VMEM_LIMIT),
        name="mixer_meta" if meta else "mixer",
    )(x2, *consts, *state)


def _ffn_kernel(h_ref, g2_ref, wg_ref, wu_ref, wd_ref, gf_ref, o_ref, *, fc):
    h = h_ref[...]
    hn = (h * lax.rsqrt(jnp.mean(h * h, axis=-1, keepdims=True) + EPS) * g2_ref[...]).astype(BF16)
    acc = h
    for f in range(0, wg_ref.shape[1], fc):
        gate = _dot(hn, wg_ref[:, f:f + fc])
        up = _dot(hn, wu_ref[:, f:f + fc])
        acc = acc + _dot((gate * _sigmoid(gate) * up).astype(BF16), wd_ref[f:f + fc, :])
    o_ref[...] = acc * lax.rsqrt(jnp.mean(acc * acc, axis=-1, keepdims=True) + EPS) * gf_ref[...]


def _ffn_call(h1, g2, wg, wu, wd, gf, *, tm, fc):
    t = h1.shape[0]
    row_spec = pl.BlockSpec((tm, D_MODEL), lambda i: (i, 0))
    return pl.pallas_call(
        functools.partial(_ffn_kernel, fc=fc),
        grid=(t // tm,),
        in_specs=[row_spec] + [_const_spec(a.shape) for a in (g2, wg, wu, wd, gf)],
        out_specs=row_spec,
        out_shape=jax.ShapeDtypeStruct((t, D_MODEL), F32),
        compiler_params=pltpu.CompilerParams(
            dimension_semantics=("arbitrary",), vmem_limit_bytes=VMEM_LIMIT),
        name="ffn",
    )(h1, g2, wg, wu, wd, gf)


def _split_w_in(w):
    d = w.shape[0]
    wide = [w[:, a:b].astype(BF16) for a, b in W_IN_RANGES]
    zeros = lambda n: jnp.zeros((d, n), w.dtype)
    small = jnp.concatenate([zeros(GA_LANE), w[:, 5128:5144], zeros(LANES - GA_LANE - G_RANK)], axis=1)
    gates_t = jnp.concatenate([w[:, 2048:2052], zeros(SUBLANES - HEADS), w[:, 2052:2056], zeros(SUBLANES - HEADS)],
                              axis=1).T
    return wide + [small.astype(BF16), gates_t.astype(BF16)]


def kernel(x, meta_tokens, norm1_g, w_in, conv_w, conv_b, m_gate_b, g_a2, g_a2_b, m_head_g, g_head_g,
           w_branch_m, w_branch_g, w_out, norm2_g, w_ff_gate, w_ff_up, w_ff_down, final_g):
    bsz, seq, d = x.shape
    assert bsz == 1 and d == D_MODEL and norm1_g.shape[0] == 1 and seq % CHUNK == 0
    row = lambda a: a.reshape(1, -1).astype(F32)

    gate_bias = jnp.zeros((2 * SUBLANES, LANES), F32)
    gate_bias = gate_bias.at[:HEADS].set(m_gate_b[0, 0][:, None]).at[SUBLANES:SUBLANES + HEADS].set(m_gate_b[0, 1][:, None])
    a2 = jnp.zeros((LANES, QK), F32).at[GA_LANE:GA_LANE + G_RANK].set(g_a2[0]).astype(BF16)
    conv_w8 = jnp.zeros((SUBLANES, 2 * QK), F32).at[:CONV_W].set(conv_w[0])
    consts = (row(norm1_g[0]), *_split_w_in(w_in[0]), gate_bias, a2, row(g_a2_b[0]), conv_w8, row(conv_b[0]),
              row(m_head_g[0]), row(g_head_g[0]),
              w_branch_m[0].astype(BF16), w_branch_g[0].astype(BF16), w_out[0].astype(BF16))

    zero_state = (jnp.zeros((SUBLANES, 2 * QK), F32), jnp.zeros((HEADS, DK, CN), F32),
                  jnp.zeros((SUBLANES, LANES), F32), jnp.zeros((HEADS, DK, DV), F32))
    lead = jnp.concatenate([jnp.zeros((RCHUNK - N_META, d), F32), meta_tokens.astype(F32)], axis=0)
    _, *state = _mixer_call(lead, consts, zero_state, tm=RCHUNK, meta=True)
    h1, *_ = _mixer_call(x[0], consts, tuple(state), tm=256, meta=False)
    out = _ffn_call(h1, row(norm2_g[0]), w_ff_gate[0].astype(BF16), w_ff_up[0].astype(BF16),
                    w_ff_down[0].astype(BF16), row(final_g), tm=512, fc=256)
    return out[None]
```

```python
import functools

import jax
import jax.numpy as jnp
from jax import lax
from jax.experimental import pallas as pl
from jax.experimental.pallas import tpu as pltpu

F32 = jnp.float32
BF16 = jnp.bfloat16

D_MODEL = 1024
N_META = 16
CHUNK = 64
RCHUNK = 128
EPS = 1e-6
HEADS = 4
DV = D_MODEL // HEADS
DK = DV // 2
QK = HEADS * DK
G_RANK = 16
G_TAU = 16.0
CONV_W = 4
LANES = 128
SUBLANES = 8
BF16_ROWS = 16
NEG_BIG = -1e30
VMEM_LIMIT = 56 * 1024 * 1024
CN = DV + LANES

W_IN_RANGES = ((0, 2048), (2056, 5128), (5144, 8216))
O_MQK, O_MV = (0, 0), (0, 1024)
O_MO, O_GQ, O_GK, O_GV = (1, 0), (1, 1024), (1, 1536), (1, 2048)
O_GR, O_GATE_M, O_GATE_G = (2, 0), (2, 1024), (2, 2048)
O_S1 = (3, 0)
GA_LANE = 8

NT_DIMS = (((1,), (1,)), ((), ()))
TN_DIMS = (((0,), (0,)), ((), ()))


def _dot(a, b):
    return jnp.dot(a, b, preferred_element_type=F32)


def _dot_nt(a, b):
    return lax.dot_general(a, b, NT_DIMS, preferred_element_type=F32)


def _dot_tn(a, b):
    return lax.dot_general(a, b, TN_DIMS, preferred_element_type=F32)


def _pieces(x):
    hi = x.astype(BF16).astype(F32)
    mid = (x - hi).astype(BF16).astype(F32)
    lo = (x - hi - mid).astype(BF16).astype(F32)
    return [hi, mid, lo]


def _log_sigmoid(x):
    return jnp.minimum(x, 0.0) - jnp.log(1.0 + jnp.exp(-jnp.abs(x)))


def _sigmoid(x):
    return 1.0 / (1.0 + jnp.exp(-x))


def _mixer_kernel(x_ref, xnext_ref, g1_ref, wa_ref, wb_ref, wc_ref, ws_ref, wst_ref, sb_ref, a2_ref, a2b_ref, cw_ref, cb_ref,
                  mhg_ref, ghg_ref, wbm_ref, wbg_ref, wout_ref, tail0_ref, cn0_ref, m0_ref, s0_ref,
                  h1_ref, tail_ref, cn_ref, m_ref, s_ref,
                  xn_s, qk_s, q_s, k_s, wk_s, vo_s, gq64_s, gq128_s, gki_s, gke64_s, gke128_s, gv_s, hm_s, hg_s,
                  ga_s, gb_s, gc_s, gd_s, gqk_s, tric_s, triu_s, spread_s, *, tm, meta):
    nc = tm // RCHUNK
    pairs = [(c, h) for c in range(nc) for h in range(HEADS)]

    def normed(ref):
        v = ref[...]
        vn = v * lax.rsqrt(jnp.mean(v * v, axis=-1, keepdims=True) + EPS) * g1_ref[...]
        if meta:
            vn = jnp.where(valid, vn, 0.0)
        return vn.astype(BF16)

    if meta:
        valid = lax.broadcasted_iota(jnp.int32, (tm, 1), 0) >= (tm - N_META)

    @pl.when(pl.program_id(0) == 0)
    def _():
        xn_s[...] = normed(x_ref)
        qk_s[0:SUBLANES, :] = tail0_ref[...]
        cn_ref[...] = cn0_ref[...]
        m_ref[...] = m0_ref[...]
        s_ref[...] = s0_ref[...]
        for h in range(HEADS):
            vo_s[:, h * CN + DV:(h + 1) * CN] = jnp.ones((tm, LANES), BF16)
        row = lax.broadcasted_iota(jnp.int32, (tm, tm), 0)
        col = lax.broadcasted_iota(jnp.int32, (tm, tm), 1)
        tric_s[...] = jnp.where((col <= row) & (col >= (row & -CHUNK)), 1.0, 0.0).astype(BF16)
        triu_s[...] = jnp.where((row <= col) & (row >= (col & -RCHUNK)), 1.0, 0.0).astype(BF16)
        sr = lax.broadcasted_iota(jnp.int32, (LANES, 3 * HEADS * LANES), 0)
        sc = lax.broadcasted_iota(jnp.int32, (LANES, 3 * HEADS * LANES), 1)
        blk = (sc >> 9) * (3 * SUBLANES)
        spread_s[...] = jnp.where(
            (sr >= blk) & (sr < blk + 3 * SUBLANES) & ((sr & (SUBLANES - 1)) == ((sc >> 7) & (HEADS - 1))),
            1.0, 0.0).astype(BF16)

    w_refs = (wa_ref, wb_ref, wc_ref, ws_ref)

    def proj(seg, width):
        return _dot(xn_s[...], w_refs[seg[0]][:, seg[1]:seg[1] + width])

    def rows(c):
        return slice(c * RCHUNK, (c + 1) * RCHUNK)

    def half(c, second):
        lo = c * RCHUNK + (CHUNK if second else 0)
        return slice(lo, lo + CHUNK)

    crow = lax.broadcasted_iota(jnp.int32, (RCHUNK, RCHUNK), 0)
    ccol = lax.broadcasted_iota(jnp.int32, (RCHUNK, RCHUNK), 1)
    causal = ccol <= crow
    tri_c = tric_s[...]
    tri_up = triu_s[...]
    spread = spread_s[...]

    def st_qk(i, v):
        qk_s[SUBLANES:SUBLANES + tm, i * DV:(i + 1) * DV] = v

    def st_mv(i, v):
        vo_s[:, i * CN:i * CN + DV] = v.astype(BF16)

    def st_gqk(i, v):
        gqk_s[:, i * DV:(i + 1) * DV] = v

    def st_gv(i, v):
        gv_s[:, i * DV:(i + 1) * DV] = v.astype(BF16)

    def st_mo(i, v):
        ga_s[:, i * DV:(i + 1) * DV] = _sigmoid(v)

    def st_gm(i, v):
        gb_s[:, i * DV:(i + 1) * DV] = _sigmoid(v)

    def st_gr(i, v):
        gc_s[:, i * DV:(i + 1) * DV] = v * _sigmoid(v)

    def st_gg(i, v):
        gd_s[:, i * DV:(i + 1) * DV] = _sigmoid(v)

    groups = [(O_MQK, st_qk), (O_MV, st_mv), (O_GQ, st_gqk), (O_GV, st_gv)]
    if not meta:
        groups += [(O_MO, st_mo), (O_GATE_M, st_gm), (O_GR, st_gr), (O_GATE_G, st_gg)]
    queue = [(seg, i, store) for seg, store in groups for i in range(D_MODEL // DV)]
    emitted = [0]

    def fill(n):
        for _ in range(n):
            if emitted[0] < len(queue):
                seg, i, store = queue[emitted[0]]
                store(i, proj((seg[0], seg[1] + i * DV), DV))
                emitted[0] += 1

    def finish_groups(n):
        fill(n * (D_MODEL // DV) - emitted[0])

    gates = _dot_nt(wst_ref[...], xn_s[...]) + jnp.concatenate([sb_ref[...]] * (tm // LANES), axis=1)
    s1 = proj(O_S1, LANES)
    fill(2)
    logi = gates[:SUBLANES, :]
    logf = _log_sigmoid(gates[SUBLANES:, :])
    if meta:
        valid_t = lax.broadcasted_iota(jnp.int32, (1, tm), 1) >= (tm - N_META)
        logi = jnp.where(valid_t, logi, NEG_BIG)
        logf = jnp.where(valid_t, logf, 0.0)
    zero8 = jnp.zeros((SUBLANES, tm), F32)
    b4 = _dot(jnp.concatenate(_pieces(logf) + [zero8], axis=0).astype(BF16), tri_up)
    fill(2)
    b_all = b4[:SUBLANES] + b4[SUBLANES:2 * SUBLANES] + b4[2 * SUBLANES:3 * SUBLANES]
    c_all = logi - b_all
    m_run = m_ref[...]
    inter_parts, w_parts, a_chunk = [], [], []
    for c in range(nc):
        b_c = b_all[:, rows(c)]
        g_c = jnp.broadcast_to(b_c[:, RCHUNK - 1:RCHUNK], (SUBLANES, LANES))
        wlog = g_c + c_all[:, rows(c)]
        m_next = jnp.maximum(g_c + m_run, jnp.max(wlog, axis=1, keepdims=True))
        inter_parts.append(b_c + m_run)
        w_parts.append(jnp.exp(wlog - m_next))
        a_chunk.append(jnp.exp(g_c + m_run - m_next))
        m_run = m_next
    m_ref[...] = m_run
    stacked = jnp.concatenate(
        _pieces(b_all) + _pieces(jnp.concatenate(inter_parts, axis=1)) + _pieces(jnp.concatenate(w_parts, axis=1))
        + [jnp.zeros((LANES - 9 * SUBLANES, tm), F32)], axis=0)
    stacked_t = stacked.T.astype(BF16)
    fill(2)
    tiles = _dot(stacked_t, spread)
    fill(2)
    b_t, il_t, w_t = (tiles[:, i * QK:(i + 1) * QK] for i in range(3))

    finish_groups(1)
    for blk in range(2 * QK // DV):
        csl = slice(blk * DV, (blk + 1) * DV)
        conv = cb_ref[:, csl]
        for j in range(CONV_W):
            conv = conv + cw_ref[j:j + 1, csl] * qk_s[pl.ds(SUBLANES - (CONV_W - 1) + j, tm), csl]
        conv = conv * _sigmoid(conv)
        if blk < QK // DV:
            q_s[:, csl] = conv * (DK ** -0.5)
        else:
            ksl = slice(blk * DV - QK, (blk + 1) * DV - QK)
            k_s[:, ksl] = conv.astype(BF16)
            wk_s[:, ksl] = (w_t[:, ksl] * conv).astype(BF16)
        fill(1)
    tail_new = qk_s[tm:tm + SUBLANES, :]
    qk_s[0:SUBLANES, :] = tail_new
    tail_ref[...] = tail_new

    za = _dot(s1.astype(BF16), a2_ref[...]) + a2b_ref[...]
    fill(1)
    loga = _log_sigmoid(za) / G_TAU
    if meta:
        loga = jnp.where(valid, loga, 0.0)
    hi = loga.astype(BF16)
    r1 = loga - hi.astype(F32)
    mid = r1.astype(BF16)
    lo = (r1 - mid.astype(F32)).astype(BF16)
    bc = _dot(tri_c, hi) + _dot(tri_c, mid) + _dot(tri_c, lo)
    fill(2)
    e_blk = jnp.exp(bc)
    tot_a = [bc[c * RCHUNK + CHUNK - 1:c * RCHUNK + CHUNK, :] for c in range(nc)]
    tot_b = [bc[(c + 1) * RCHUNK - 1:(c + 1) * RCHUNK, :] for c in range(nc)]
    e_a = [jnp.exp(t) for t in tot_a]
    e_b = [jnp.exp(t) for t in tot_b]
    rest = jnp.concatenate([t - bc[half(c, s), :] for c in range(nc) for s, t in ((0, tot_a[c]), (1, tot_b[c]))],
                           axis=0)
    e_rest = jnp.exp(rest)
    ones_row = jnp.ones((1, QK), F32)
    to_chunk = jnp.concatenate([jnp.broadcast_to(f, (CHUNK, QK)) for c in range(nc) for f in (ones_row, e_a[c])],
                               axis=0)
    from_blk = jnp.concatenate([jnp.broadcast_to(f, (CHUNK, QK)) for c in range(nc) for f in (e_b[c], ones_row)],
                               axis=0)
    finish_groups(3)
    gq = gqk_s[:, :QK] * (DK ** -0.5)
    gk = gqk_s[:, QK:]
    gq64_s[...] = (gq * e_blk).astype(BF16)
    gq128_s[...] = (gq * e_blk * to_chunk).astype(BF16)
    gki_s[...] = (gk * jnp.exp(-bc)).astype(BF16)
    gke64_s[...] = (gk * e_rest).astype(BF16)
    gke128_s[...] = (gk * e_rest * from_blk).astype(BF16)
    fill(2)
    e_rows = [(e_a[c] * e_b[c])[:, h * DK:(h + 1) * DK] for (c, h) in pairs]
    e_cols = jnp.concatenate(e_rows + [jnp.zeros((LANES - len(pairs), DK), F32)], axis=0).T

    finish_groups(2)
    dmat, rmax, sim = {}, {}, {}
    for (c, h) in pairs:
        d = b_t[rows(c), h * LANES:(h + 1) * LANES] + c_all[h:h + 1, rows(c)]
        d = jnp.where(causal, d, -jnp.inf)
        dmat[c, h] = d
        rmax[c, h] = jnp.max(d, axis=-1, keepdims=True)
    fill(1)
    for (c, h) in pairs:
        sim[c, h] = _dot_nt(q_s[rows(c), h * DK:(h + 1) * DK].astype(BF16), k_s[rows(c), h * DK:(h + 1) * DK])
    fill(1)
    lhs, emr, upd = {}, {}, {}
    for (c, h) in pairs:
        il = il_t[rows(c), h * LANES:(h + 1) * LANES]
        m_row = jnp.maximum(il, rmax[c, h])
        wts = jnp.exp(dmat[c, h] - m_row) * sim[c, h]
        aq = jnp.exp(il - m_row) * q_s[rows(c), h * DK:(h + 1) * DK]
        lhs[c, h] = jnp.concatenate([aq.astype(BF16), wts.astype(BF16)], axis=1)
        emr[c, h] = jnp.exp(-m_row)
    fill(1)
    for (c, h) in pairs:
        upd[c, h] = _dot_tn(wk_s[rows(c), h * DK:(h + 1) * DK], vo_s[rows(c), h * CN:(h + 1) * CN])
    fill(1)
    res = {}
    for h in range(HEADS):
        state = cn_ref[h]
        for c in range(nc):
            rhs = jnp.concatenate([state.astype(BF16), vo_s[rows(c), h * CN:(h + 1) * CN]], axis=0)
            res[c, h] = _dot(lhs[c, h], rhs)
            a = a_chunk[c][h:h + 1, :]
            state = jnp.concatenate([a] * (CN // LANES), axis=1) * state + upd[c, h]
        cn_ref[h] = state
        fill(1)
    for (c, h) in pairs:
        inv = 1.0 / jnp.maximum(jnp.abs(res[c, h][:, DV:]), emr[c, h])
        hm_s[rows(c), h * DV:h * DV + LANES] = res[c, h][:, :LANES] * inv
        hm_s[rows(c), h * DV + LANES:(h + 1) * DV] = res[c, h][:, LANES:DV] * inv
    fill(1)

    finish_groups(4)
    att, gupd = {}, {}
    zblk = jnp.zeros((CHUNK, DK), BF16)
    for (c, h) in pairs:
        dsl = slice(h * DK, (h + 1) * DK)
        ra, rb = half(c, 0), half(c, 1)
        qp = jnp.concatenate([jnp.concatenate([gq64_s[ra, dsl], zblk], axis=1),
                              jnp.concatenate([zblk, gq64_s[rb, dsl]], axis=1)], axis=0)
        kp = jnp.concatenate([jnp.concatenate([gki_s[ra, dsl], gke64_s[ra, dsl]], axis=1),
                              jnp.concatenate([zblk, gki_s[rb, dsl]], axis=1)], axis=0)
        att[c, h] = jnp.where(causal, _dot_nt(qp, kp), 0.0).astype(BF16)
    fill(1)
    for (c, h) in pairs:
        gupd[c, h] = _dot_tn(gke128_s[rows(c), h * DK:(h + 1) * DK], gv_s[rows(c), h * DV:(h + 1) * DV])
    fill(1)
    for h in range(HEADS):
        sst = s_ref[h]
        for c in range(nc):
            lhs_g = jnp.concatenate([gq128_s[rows(c), h * DK:(h + 1) * DK], att[c, h]], axis=1)
            rhs_g = jnp.concatenate([sst.astype(BF16), gv_s[rows(c), h * DV:(h + 1) * DV]], axis=0)
            hg_s[rows(c), h * DV:(h + 1) * DV] = _dot(lhs_g, rhs_g)
            i = c * HEADS + h
            sst = e_cols[:, i:i + 1] * sst + gupd[c, h]
        s_ref[h] = sst
        fill(1)

    if meta:
        h1_ref[...] = x_ref[...]
        return

    def head_norm(src, gain_ref):
        outs = []
        for h in range(HEADS):
            t = src[:, h * DV:(h + 1) * DV]
            outs.append(t * lax.rsqrt(jnp.mean(t * t, axis=-1, keepdims=True) + EPS))
        return jnp.concatenate(outs, axis=-1) * gain_ref[...]

    finish_groups(len(groups))
    y_m = head_norm(hm_s, mhg_ref) * ga_s[...]
    merged = gb_s[...] * _dot(y_m.astype(BF16), wbm_ref[...])
    xn_s[...] = normed(xnext_ref)
    y_g = head_norm(hg_s, ghg_ref) * gc_s[...]
    merged = merged + gd_s[...] * _dot(y_g.astype(BF16), wbg_ref[...])
    h1_ref[...] = x_ref[...] + _dot(merged.astype(BF16), wout_ref[...])


def _const_spec(shape):
    nd = len(shape)
    return pl.BlockSpec(shape, lambda i: (0,) * nd, pipeline_mode=pl.Buffered(1))


def _mixer_call(x2, consts, state, *, tm, meta):
    t = x2.shape[0]
    state_shapes = [jax.ShapeDtypeStruct(a.shape, F32) for a in state]
    row_spec = pl.BlockSpec((tm, D_MODEL), lambda i: (i, 0))
    next_spec = pl.BlockSpec((tm, D_MODEL), lambda i: (jnp.minimum(i + 1, t // tm - 1), 0))
    state_specs = [pl.BlockSpec(a.shape, lambda i, nd=a.ndim: (0,) * nd) for a in state]
    return pl.pallas_call(
        functools.partial(_mixer_kernel, tm=tm, meta=meta),
        grid=(t // tm,),
        in_specs=[row_spec, next_spec] + [_const_spec(c.shape) for c in consts] + state_specs,
        out_specs=[row_spec] + state_specs,
        out_shape=[jax.ShapeDtypeStruct((t, D_MODEL), F32)] + state_shapes,
        scratch_shapes=[
            pltpu.VMEM((tm, D_MODEL), BF16),
            pltpu.VMEM((tm + SUBLANES, 2 * QK), F32),
            pltpu.VMEM((tm, QK), F32),
            pltpu.VMEM((tm, QK), BF16),
            pltpu.VMEM((tm, QK), BF16),
            pltpu.VMEM((tm, HEADS * CN), BF16),
            pltpu.VMEM((tm, QK), BF16),
            pltpu.VMEM((tm, QK), BF16),
            pltpu.VMEM((tm, QK), BF16),
            pltpu.VMEM((tm, QK), BF16),
            pltpu.VMEM((tm, QK), BF16),
            pltpu.VMEM((tm, D_MODEL), BF16),
            pltpu.VMEM((tm, D_MODEL), F32),
            pltpu.VMEM((tm, D_MODEL), F32),
            pltpu.VMEM((tm, D_MODEL), F32),
            pltpu.VMEM((tm, D_MODEL), F32),
            pltpu.VMEM((tm, D_MODEL), F32),
            pltpu.VMEM((tm, D_MODEL), F32),
            pltpu.VMEM((tm, D_MODEL), F32),
            pltpu.VMEM((tm, tm), BF16),
            pltpu.VMEM((tm, tm), BF16),
            pltpu.VMEM((LANES, 3 * HEADS * LANES), BF16),
        ],
        compiler_params=pltpu.CompilerParams(
            dimension_semantics=("arbitrary",), vmem_limit_bytes=VMEM_LIMIT),
        name="mixer_meta" if meta else "mixer",
    )(x2, x2, *consts, *state)


def _ffn_kernel(h_ref, g2_ref, wg_ref, wu_ref, wd_ref, gf_ref, o_ref, *, fc):
    h = h_ref[...]
    hn = (h * lax.rsqrt(jnp.mean(h * h, axis=-1, keepdims=True) + EPS) * g2_ref[...]).astype(BF16)
    acc = h
    for f in range(0, wg_ref.shape[1], fc):
        gate = _dot(hn, wg_ref[:, f:f + fc])
        up = _dot(hn, wu_ref[:, f:f + fc])
        acc = acc + _dot((gate * _sigmoid(gate) * up).astype(BF16), wd_ref[f:f + fc, :])
    o_ref[...] = acc * lax.rsqrt(jnp.mean(acc * acc, axis=-1, keepdims=True) + EPS) * gf_ref[...]


def _ffn_call(h1, g2, wg, wu, wd, gf, *, tm, fc):
    t = h1.shape[0]
    row_spec = pl.BlockSpec((tm, D_MODEL), lambda i: (i, 0))
    return pl.pallas_call(
        functools.partial(_ffn_kernel, fc=fc),
        grid=(t // tm,),
        in_specs=[row_spec] + [_const_spec(a.shape) for a in (g2, wg, wu, wd, gf)],
        out_specs=row_spec,
        out_shape=jax.ShapeDtypeStruct((t, D_MODEL), F32),
        compiler_params=pltpu.CompilerParams(
            dimension_semantics=("arbitrary",), vmem_limit_bytes=VMEM_LIMIT),
        name="ffn",
    )(h1, g2, wg, wu, wd, gf)


def _split_w_in(w):
    d = w.shape[0]
    wide = [w[:, a:b].astype(BF16) for a, b in W_IN_RANGES]
    zeros = lambda n: jnp.zeros((d, n), w.dtype)
    small = jnp.concatenate([zeros(GA_LANE), w[:, 5128:5144], zeros(LANES - GA_LANE - G_RANK)], axis=1)
    gates_t = jnp.concatenate([w[:, 2048:2052], zeros(SUBLANES - HEADS), w[:, 2052:2056], zeros(SUBLANES - HEADS)],
                              axis=1).T
    return wide + [small.astype(BF16), gates_t.astype(BF16)]


def kernel(x, meta_tokens, norm1_g, w_in, conv_w, conv_b, m_gate_b, g_a2, g_a2_b, m_head_g, g_head_g,
           w_branch_m, w_branch_g, w_out, norm2_g, w_ff_gate, w_ff_up, w_ff_down, final_g):
    bsz, seq, d = x.shape
    assert bsz == 1 and d == D_MODEL and norm1_g.shape[0] == 1 and seq % CHUNK == 0
    row = lambda a: a.reshape(1, -1).astype(F32)

    gate_bias = jnp.zeros((2 * SUBLANES, LANES), F32)
    gate_bias = gate_bias.at[:HEADS].set(m_gate_b[0, 0][:, None]).at[SUBLANES:SUBLANES + HEADS].set(m_gate_b[0, 1][:, None])
    a2 = jnp.zeros((LANES, QK), F32).at[GA_LANE:GA_LANE + G_RANK].set(g_a2[0]).astype(BF16)
    conv_w8 = jnp.zeros((SUBLANES, 2 * QK), F32).at[:CONV_W].set(conv_w[0])
    consts = (row(norm1_g[0]), *_split_w_in(w_in[0]), gate_bias, a2, row(g_a2_b[0]), conv_w8, row(conv_b[0]),
              row(m_head_g[0]), row(g_head_g[0]),
              w_branch_m[0].astype(BF16), w_branch_g[0].astype(BF16), w_out[0].astype(BF16))

    zero_state = (jnp.zeros((SUBLANES, 2 * QK), F32), jnp.zeros((HEADS, DK, CN), F32),
                  jnp.zeros((SUBLANES, LANES), F32), jnp.zeros((HEADS, DK, DV), F32))
    lead = jnp.concatenate([jnp.zeros((RCHUNK - N_META, d), F32), meta_tokens.astype(F32)], axis=0)
    _, *state = _mixer_call(lead, consts, zero_state, tm=RCHUNK, meta=True)
    h1, *_ = _mixer_call(x[0], consts, tuple(state), tm=256, meta=False)
    out = _ffn_call(h1, row(norm2_g[0]), w_ff_gate[0].astype(BF16), w_ff_up[0].astype(BF16),
                    w_ff_down[0].astype(BF16), row(final_g), tm=512, fc=256)
    return out[None]
```

```python
import functools

import jax
import jax.numpy as jnp
from jax import lax
from jax.experimental import pallas as pl
from jax.experimental.pallas import tpu as pltpu

F32 = jnp.float32
BF16 = jnp.bfloat16

D_MODEL = 1024
N_META = 16
CHUNK = 64
RCHUNK = 128
EPS = 1e-6
HEADS = 4
DV = D_MODEL // HEADS
DK = DV // 2
QK = HEADS * DK
G_RANK = 16
G_TAU = 16.0
CONV_W = 4
LANES = 128
SUBLANES = 8
BF16_ROWS = 16
NEG_BIG = -1e30
VMEM_LIMIT = 56 * 1024 * 1024
CN = DV + LANES

W_IN_RANGES = ((0, 2048), (2056, 5128), (5144, 8216))
O_MQK, O_MV = (0, 0), (0, 1024)
O_MO, O_GQ, O_GK, O_GV = (1, 0), (1, 1024), (1, 1536), (1, 2048)
O_GR, O_GATE_M, O_GATE_G = (2, 0), (2, 1024), (2, 2048)
O_S1 = (3, 0)
GA_LANE = 8
MF_LANE = 24
LEAD_UNUSED = (3, 12, 13, 14)

NT_DIMS = (((1,), (1,)), ((), ()))
TN_DIMS = (((0,), (0,)), ((), ()))


def _dot(a, b):
    return jnp.dot(a, b, preferred_element_type=F32)


def _dot_nt(a, b):
    return lax.dot_general(a, b, NT_DIMS, preferred_element_type=F32)


def _dot_tn(a, b):
    return lax.dot_general(a, b, TN_DIMS, preferred_element_type=F32)


def _pieces(x):
    hi = x.astype(BF16).astype(F32)
    mid = (x - hi).astype(BF16).astype(F32)
    lo = (x - hi - mid).astype(BF16).astype(F32)
    return [hi, mid, lo]


def _log_sigmoid(x):
    return jnp.minimum(x, 0.0) - jnp.log(1.0 + jnp.exp(-jnp.abs(x)))


def _sigmoid(x):
    return 1.0 / (1.0 + jnp.exp(-x))


def _mixer_kernel(x_ref, g1_ref, wa_ref, wb_ref, wc_ref, ws_ref, sb_ref, a2_ref, a2b_ref, cw_ref, cb_ref,
                  mhg_ref, ghg_ref, wbm_ref, wbg_ref, wout_ref, tail0_ref, cn0_ref, m0_ref, s0_ref,
                  h1_ref, tail_ref, cn_ref, m_ref, s_ref,
                  xn_s, qk_s, q_s, k_s, wk_s, vo_s, gq64_s, gq128_s, gki_s, gke64_s, gke128_s, gv_s, hm_s, hg_s,
                  ga_s, gb_s, gc_s, gd_s, gqk_s, tric_s, triu_s, spread_s, wst_s, *, tm, meta):
    nc = tm // RCHUNK
    pairs = [(c, h) for c in range(nc) for h in range(HEADS)]

    @pl.when(pl.program_id(0) == 0)
    def _():
        qk_s[0:SUBLANES, :] = tail0_ref[...]
        cn_ref[...] = cn0_ref[...]
        m_ref[...] = m0_ref[...]
        s_ref[...] = s0_ref[...]
        for h in range(HEADS):
            vo_s[:, h * CN + DV:(h + 1) * CN] = jnp.ones((tm, LANES), BF16)
        ws_t = ws_ref[...].astype(F32).T
        wst_s[...] = jnp.concatenate([ws_t[:SUBLANES], ws_t[MF_LANE:MF_LANE + SUBLANES]], axis=0).astype(BF16)
        row = lax.broadcasted_iota(jnp.int32, (tm, tm), 0)
        col = lax.broadcasted_iota(jnp.int32, (tm, tm), 1)
        tric_s[...] = jnp.where((col <= row) & (col >= (row & -CHUNK)), 1.0, 0.0).astype(BF16)
        triu_s[...] = jnp.where((row <= col) & (row >= (col & -RCHUNK)), 1.0, 0.0).astype(BF16)
        sr = lax.broadcasted_iota(jnp.int32, (LANES, 3 * HEADS * LANES), 0)
        sc = lax.broadcasted_iota(jnp.int32, (LANES, 3 * HEADS * LANES), 1)
        blk = (sc >> 9) * (3 * SUBLANES)
        spread_s[...] = jnp.where(
            (sr >= blk) & (sr < blk + 3 * SUBLANES) & ((sr & (SUBLANES - 1)) == ((sc >> 7) & (HEADS - 1))),
            1.0, 0.0).astype(BF16)

    x = x_ref[...]
    xn = x * lax.rsqrt(jnp.mean(x * x, axis=-1, keepdims=True) + EPS) * g1_ref[...]
    if meta:
        valid = lax.broadcasted_iota(jnp.int32, (tm, 1), 0) >= (tm - N_META)
        xn = jnp.where(valid, xn, 0.0)
    xn_s[...] = xn.astype(BF16)

    w_refs = (wa_ref, wb_ref, wc_ref, ws_ref)

    def proj(seg, width):
        return _dot(xn_s[...], w_refs[seg[0]][:, seg[1]:seg[1] + width])

    def rows(c):
        return slice(c * RCHUNK, (c + 1) * RCHUNK)

    def half(c, second):
        lo = c * RCHUNK + (CHUNK if second else 0)
        return slice(lo, lo + CHUNK)

    crow = lax.broadcasted_iota(jnp.int32, (RCHUNK, RCHUNK), 0)
    ccol = lax.broadcasted_iota(jnp.int32, (RCHUNK, RCHUNK), 1)
    causal = ccol <= crow
    tri_c = tric_s[...]
    tri_up = triu_s[...]
    spread = spread_s[...]

    def st_qk(i, v):
        qk_s[SUBLANES:SUBLANES + tm, i * DV:(i + 1) * DV] = v

    def st_mv(i, v):
        vo_s[:, i * CN:i * CN + DV] = v.astype(BF16)

    def st_gqk(i, v):
        gqk_s[:, i * DV:(i + 1) * DV] = v

    def st_gv(i, v):
        gv_s[:, i * DV:(i + 1) * DV] = v.astype(BF16)

    def st_mo(i, v):
        ga_s[:, i * DV:(i + 1) * DV] = _sigmoid(v)

    def st_gm(i, v):
        gb_s[:, i * DV:(i + 1) * DV] = _sigmoid(v)

    def st_gr(i, v):
        gc_s[:, i * DV:(i + 1) * DV] = v * _sigmoid(v)

    def st_gg(i, v):
        gd_s[:, i * DV:(i + 1) * DV] = _sigmoid(v)

    groups = [(O_MQK, st_qk), (O_MV, st_mv), (O_GQ, st_gqk), (O_GV, st_gv)]
    if not meta:
        groups += [(O_MO, st_mo), (O_GATE_M, st_gm), (O_GR, st_gr), (O_GATE_G, st_gg)]
    queue = [(seg, i, store) for seg, store in groups for i in range(D_MODEL // DV)]
    emitted = [0]

    def fill(n):
        for _ in range(n):
            if emitted[0] < len(queue):
                seg, i, store = queue[emitted[0]]
                store(i, proj((seg[0], seg[1] + i * DV), DV))
                emitted[0] += 1

    def finish_groups(n):
        fill(n * (D_MODEL // DV) - emitted[0])

    gates = _dot_nt(wst_s[...], xn_s[...]) + jnp.concatenate([sb_ref[...]] * (tm // LANES), axis=1)
    s1 = proj(O_S1, LANES)
    fill(2)
    logi = gates[:SUBLANES, :]
    logf = _log_sigmoid(gates[SUBLANES:, :])
    if meta:
        valid_t = lax.broadcasted_iota(jnp.int32, (1, tm), 1) >= (tm - N_META)
        logi = jnp.where(valid_t, logi, NEG_BIG)
        logf = jnp.where(valid_t, logf, 0.0)
    zero8 = jnp.zeros((SUBLANES, tm), F32)
    b4 = _dot(jnp.concatenate(_pieces(logf) + [zero8], axis=0).astype(BF16), tri_up)
    fill(2)
    b_all = b4[:SUBLANES] + b4[SUBLANES:2 * SUBLANES] + b4[2 * SUBLANES:3 * SUBLANES]
    c_all = logi - b_all
    m_run = m_ref[...]
    inter_parts, w_parts, a_chunk = [], [], []
    for c in range(nc):
        b_c = b_all[:, rows(c)]
        g_c = jnp.broadcast_to(b_c[:, RCHUNK - 1:RCHUNK], (SUBLANES, LANES))
        wlog = g_c + c_all[:, rows(c)]
        m_next = jnp.maximum(g_c + m_run, jnp.max(wlog, axis=1, keepdims=True))
        inter_parts.append(b_c + m_run)
        w_parts.append(jnp.exp(wlog - m_next))
        a_chunk.append(jnp.exp(g_c + m_run - m_next))
        m_run = m_next
    m_ref[...] = m_run
    stacked = jnp.concatenate(
        _pieces(b_all) + _pieces(jnp.concatenate(inter_parts, axis=1)) + _pieces(jnp.concatenate(w_parts, axis=1))
        + [jnp.zeros((LANES - 9 * SUBLANES, tm), F32)], axis=0)
    stacked_t = stacked.T.astype(BF16)
    fill(2)
    tiles = _dot(stacked_t, spread)
    fill(2)
    b_t, il_t, w_t = (tiles[:, i * QK:(i + 1) * QK] for i in range(3))

    finish_groups(1)
    for blk in range(2 * QK // DV):
        csl = slice(blk * DV, (blk + 1) * DV)
        conv = cb_ref[:, csl]
        for j in range(CONV_W):
            conv = conv + cw_ref[j:j + 1, csl] * qk_s[pl.ds(SUBLANES - (CONV_W - 1) + j, tm), csl]
        conv = conv * _sigmoid(conv)
        if blk < QK // DV:
            q_s[:, csl] = conv * (DK ** -0.5)
        else:
            ksl = slice(blk * DV - QK, (blk + 1) * DV - QK)
            k_s[:, ksl] = conv.astype(BF16)
            wk_s[:, ksl] = (w_t[:, ksl] * conv).astype(BF16)
        fill(1)
    tail_new = qk_s[tm:tm + SUBLANES, :]
    qk_s[0:SUBLANES, :] = tail_new
    tail_ref[...] = tail_new

    za = _dot(s1.astype(BF16), a2_ref[...]) + a2b_ref[...]
    fill(1)
    loga = _log_sigmoid(za) / G_TAU
    if meta:
        loga = jnp.where(valid, loga, 0.0)
    hi = loga.astype(BF16)
    r1 = loga - hi.astype(F32)
    mid = r1.astype(BF16)
    lo = (r1 - mid.astype(F32)).astype(BF16)
    bc = _dot(tri_c, hi) + _dot(tri_c, mid) + _dot(tri_c, lo)
    fill(2)
    e_blk = jnp.exp(bc)
    tot_a = [bc[c * RCHUNK + CHUNK - 1:c * RCHUNK + CHUNK, :] for c in range(nc)]
    tot_b = [bc[(c + 1) * RCHUNK - 1:(c + 1) * RCHUNK, :] for c in range(nc)]
    e_a = [jnp.exp(t) for t in tot_a]
    e_b = [jnp.exp(t) for t in tot_b]
    rest = jnp.concatenate([t - bc[half(c, s), :] for c in range(nc) for s, t in ((0, tot_a[c]), (1, tot_b[c]))],
                           axis=0)
    e_rest = jnp.exp(rest)
    ones_row = jnp.ones((1, QK), F32)
    to_chunk = jnp.concatenate([jnp.broadcast_to(f, (CHUNK, QK)) for c in range(nc) for f in (ones_row, e_a[c])],
                               axis=0)
    from_blk = jnp.concatenate([jnp.broadcast_to(f, (CHUNK, QK)) for c in range(nc) for f in (e_b[c], ones_row)],
                               axis=0)
    finish_groups(3)
    gq = gqk_s[:, :QK] * (DK ** -0.5)
    gk = gqk_s[:, QK:]
    gq64_s[...] = (gq * e_blk).astype(BF16)
    gq128_s[...] = (gq * e_blk * to_chunk).astype(BF16)
    gki_s[...] = (gk * jnp.exp(-bc)).astype(BF16)
    gke64_s[...] = (gk * e_rest).astype(BF16)
    gke128_s[...] = (gk * e_rest * from_blk).astype(BF16)
    fill(2)
    e_rows = [(e_a[c] * e_b[c])[:, h * DK:(h + 1) * DK] for (c, h) in pairs]
    e_cols = jnp.concatenate(e_rows + [jnp.zeros((LANES - len(pairs), DK), F32)], axis=0).T

    finish_groups(2)
    dmat, rmax, sim = {}, {}, {}
    for (c, h) in pairs:
        d = b_t[rows(c), h * LANES:(h + 1) * LANES] + c_all[h:h + 1, rows(c)]
        d = jnp.where(causal, d, -jnp.inf)
        dmat[c, h] = d
        rmax[c, h] = jnp.max(d, axis=-1, keepdims=True)
    fill(1)
    for (c, h) in pairs:
        sim[c, h] = _dot_nt(q_s[rows(c), h * DK:(h + 1) * DK].astype(BF16), k_s[rows(c), h * DK:(h + 1) * DK])
    fill(1)
    lhs, emr, upd = {}, {}, {}
    for (c, h) in pairs:
        il = il_t[rows(c), h * LANES:(h + 1) * LANES]
        m_row = jnp.maximum(il, rmax[c, h])
        wts = jnp.exp(dmat[c, h] - m_row) * sim[c, h]
        aq = jnp.exp(il - m_row) * q_s[rows(c), h * DK:(h + 1) * DK]
        lhs[c, h] = jnp.concatenate([aq.astype(BF16), wts.astype(BF16)], axis=1)
        emr[c, h] = jnp.exp(-m_row)
    fill(1)
    for (c, h) in pairs:
        upd[c, h] = _dot_tn(wk_s[rows(c), h * DK:(h + 1) * DK], vo_s[rows(c), h * CN:(h + 1) * CN])
    fill(1)
    res = {}
    for h in range(HEADS):
        state = cn_ref[h]
        for c in range(nc):
            rhs = jnp.concatenate([state.astype(BF16), vo_s[rows(c), h * CN:(h + 1) * CN]], axis=0)
            res[c, h] = _dot(lhs[c, h], rhs)
            a = a_chunk[c][h:h + 1, :]
            state = jnp.concatenate([a] * (CN // LANES), axis=1) * state + upd[c, h]
        cn_ref[h] = state
        fill(1)
    for (c, h) in pairs:
        inv = 1.0 / jnp.maximum(jnp.abs(res[c, h][:, DV:]), emr[c, h])
        hm_s[rows(c), h * DV:h * DV + LANES] = res[c, h][:, :LANES] * inv
        hm_s[rows(c), h * DV + LANES:(h + 1) * DV] = res[c, h][:, LANES:DV] * inv
    fill(1)

    finish_groups(4)
    att, gupd = {}, {}
    zblk = jnp.zeros((CHUNK, DK), BF16)
    for (c, h) in pairs:
        dsl = slice(h * DK, (h + 1) * DK)
        ra, rb = half(c, 0), half(c, 1)
        qp = jnp.concatenate([jnp.concatenate([gq64_s[ra, dsl], zblk], axis=1),
                              jnp.concatenate([zblk, gq64_s[rb, dsl]], axis=1)], axis=0)
        kp = jnp.concatenate([jnp.concatenate([gki_s[ra, dsl], gke64_s[ra, dsl]], axis=1),
                              jnp.concatenate([zblk, gki_s[rb, dsl]], axis=1)], axis=0)
        att[c, h] = jnp.where(causal, _dot_nt(qp, kp), 0.0).astype(BF16)
    fill(1)
    for (c, h) in pairs:
        gupd[c, h] = _dot_tn(gke128_s[rows(c), h * DK:(h + 1) * DK], gv_s[rows(c), h * DV:(h + 1) * DV])
    fill(1)
    for h in range(HEADS):
        sst = s_ref[h]
        for c in range(nc):
            lhs_g = jnp.concatenate([gq128_s[rows(c), h * DK:(h + 1) * DK], att[c, h]], axis=1)
            rhs_g = jnp.concatenate([sst.astype(BF16), gv_s[rows(c), h * DV:(h + 1) * DV]], axis=0)
            hg_s[rows(c), h * DV:(h + 1) * DV] = _dot(lhs_g, rhs_g)
            i = c * HEADS + h
            sst = e_cols[:, i:i + 1] * sst + gupd[c, h]
        s_ref[h] = sst
        fill(1)

    if meta:
        h1_ref[...] = x
        return

    def head_norm(src, gain_ref):
        outs = []
        for h in range(HEADS):
            t = src[:, h * DV:(h + 1) * DV]
            outs.append(t * lax.rsqrt(jnp.mean(t * t, axis=-1, keepdims=True) + EPS))
        return jnp.concatenate(outs, axis=-1) * gain_ref[...]

    finish_groups(len(groups))
    y_m = head_norm(hm_s, mhg_ref) * ga_s[...]
    merged = gb_s[...] * _dot(y_m.astype(BF16), wbm_ref[...])
    y_g = head_norm(hg_s, ghg_ref) * gc_s[...]
    merged = merged + gd_s[...] * _dot(y_g.astype(BF16), wbg_ref[...])
    h1_ref[...] = x + _dot(merged.astype(BF16), wout_ref[...])


def _const_spec(shape):
    nd = len(shape)
    return pl.BlockSpec(shape, lambda i: (0,) * nd, pipeline_mode=pl.Buffered(1))


def _mixer_call(x2, consts, state, *, tm, meta):
    t = x2.shape[0]
    state_shapes = [jax.ShapeDtypeStruct(a.shape, F32) for a in state]
    row_spec = pl.BlockSpec((tm, D_MODEL), lambda i: (i, 0))
    state_specs = [pl.BlockSpec(a.shape, lambda i, nd=a.ndim: (0,) * nd) for a in state]
    return pl.pallas_call(
        functools.partial(_mixer_kernel, tm=tm, meta=meta),
        grid=(t // tm,),
        in_specs=[row_spec] + [_const_spec(c.shape) for c in consts] + state_specs,
        out_specs=[row_spec] + state_specs,
        out_shape=[jax.ShapeDtypeStruct((t, D_MODEL), F32)] + state_shapes,
        scratch_shapes=[
            pltpu.VMEM((tm, D_MODEL), BF16),
            pltpu.VMEM((tm + SUBLANES, 2 * QK), F32),
            pltpu.VMEM((tm, QK), F32),
            pltpu.VMEM((tm, QK), BF16),
            pltpu.VMEM((tm, QK), BF16),
            pltpu.VMEM((tm, HEADS * CN), BF16),
            pltpu.VMEM((tm, QK), BF16),
            pltpu.VMEM((tm, QK), BF16),
            pltpu.VMEM((tm, QK), BF16),
            pltpu.VMEM((tm, QK), BF16),
            pltpu.VMEM((tm, QK), BF16),
            pltpu.VMEM((tm, D_MODEL), BF16),
            pltpu.VMEM((tm, D_MODEL), F32),
            pltpu.VMEM((tm, D_MODEL), F32),
            pltpu.VMEM((tm, D_MODEL), F32),
            pltpu.VMEM((tm, D_MODEL), F32),
            pltpu.VMEM((tm, D_MODEL), F32),
            pltpu.VMEM((tm, D_MODEL), F32),
            pltpu.VMEM((tm, D_MODEL), F32),
            pltpu.VMEM((tm, tm), BF16),
            pltpu.VMEM((tm, tm), BF16),
            pltpu.VMEM((LANES, 3 * HEADS * LANES), BF16),
            pltpu.VMEM((2 * SUBLANES, D_MODEL), BF16),
        ],
        compiler_params=pltpu.CompilerParams(
            dimension_semantics=("arbitrary",), vmem_limit_bytes=VMEM_LIMIT),
        name="mixer_meta" if meta else "mixer",
    )(x2, *consts, *state)


def _ffn_kernel(h_ref, g2_ref, wg_ref, wu_ref, wd_ref, gf_ref, o_ref, *, fc):
    h = h_ref[...]
    hn = (h * lax.rsqrt(jnp.mean(h * h, axis=-1, keepdims=True) + EPS) * g2_ref[...]).astype(BF16)
    acc = h
    for f in range(0, wg_ref.shape[1], fc):
        gate = _dot(hn, wg_ref[:, f:f + fc])
        up = _dot(hn, wu_ref[:, f:f + fc])
        acc = acc + _dot((gate * _sigmoid(gate) * up).astype(BF16), wd_ref[f:f + fc, :])
    o_ref[...] = acc * lax.rsqrt(jnp.mean(acc * acc, axis=-1, keepdims=True) + EPS) * gf_ref[...]


def _ffn_call(h1, g2, wg, wu, wd, gf, *, tm, fc):
    t = h1.shape[0]
    row_spec = pl.BlockSpec((tm, D_MODEL), lambda i: (i, 0))
    return pl.pallas_call(
        functools.partial(_ffn_kernel, fc=fc),
        grid=(t // tm,),
        in_specs=[row_spec] + [_const_spec(a.shape) for a in (g2, wg, wu, wd, gf)],
        out_specs=row_spec,
        out_shape=jax.ShapeDtypeStruct((t, D_MODEL), F32),
        compiler_params=pltpu.CompilerParams(
            dimension_semantics=("arbitrary",), vmem_limit_bytes=VMEM_LIMIT),
        name="ffn",
    )(h1, g2, wg, wu, wd, gf)


def _split_w_in(w):
    d = w.shape[0]
    wide = [w[:, a:b].astype(BF16) for a, b in W_IN_RANGES]
    zeros = lambda n: jnp.zeros((d, n), w.dtype)
    small = jnp.concatenate([w[:, 2048:2052], zeros(GA_LANE - HEADS), w[:, 5128:5144], w[:, 2052:2056],
                             zeros(LANES - MF_LANE - HEADS)], axis=1)
    return wide + [small.astype(BF16)]


def kernel(x, meta_tokens, norm1_g, w_in, conv_w, conv_b, m_gate_b, g_a2, g_a2_b, m_head_g, g_head_g,
           w_branch_m, w_branch_g, w_out, norm2_g, w_ff_gate, w_ff_up, w_ff_down, final_g):
    bsz, seq, d = x.shape
    assert bsz == 1 and d == D_MODEL and norm1_g.shape[0] == 1 and seq % CHUNK == 0
    row = lambda a: a.reshape(1, -1).astype(F32)

    gate_bias = jnp.zeros((2 * SUBLANES, LANES), F32)
    gate_bias = gate_bias.at[:HEADS].set(m_gate_b[0, 0][:, None]).at[SUBLANES:SUBLANES + HEADS].set(m_gate_b[0, 1][:, None])
    a2 = jnp.zeros((LANES, QK), F32).at[GA_LANE:GA_LANE + G_RANK].set(g_a2[0]).astype(BF16)
    conv_w8 = jnp.zeros((SUBLANES, 2 * QK), F32).at[:CONV_W].set(conv_w[0])
    consts = (row(norm1_g[0]), *_split_w_in(w_in[0]), gate_bias, a2, row(g_a2_b[0]), conv_w8, row(conv_b[0]),
              row(m_head_g[0]), row(g_head_g[0]),
              w_branch_m[0].astype(BF16), w_branch_g[0].astype(BF16), w_out[0].astype(BF16))

    zero_state = (jnp.zeros((SUBLANES, 2 * QK), F32), jnp.zeros((HEADS, DK, CN), F32),
                  jnp.zeros((SUBLANES, LANES), F32), jnp.zeros((HEADS, DK, DV), F32))
    lead = jnp.concatenate([jnp.zeros((RCHUNK - N_META, d), F32), meta_tokens.astype(F32)], axis=0)
    unused = jnp.zeros((SUBLANES, LANES), BF16)
    lead_consts = tuple(unused if i in LEAD_UNUSED else c for i, c in enumerate(consts))
    _, *state = _mixer_call(lead, lead_consts, zero_state, tm=RCHUNK, meta=True)
    h1, *_ = _mixer_call(x[0], consts, tuple(state), tm=256, meta=False)
    out = _ffn_call(h1, row(norm2_g[0]), w_ff_gate[0].astype(BF16), w_ff_up[0].astype(BF16),
                    w_ff_down[0].astype(BF16), row(final_g), tm=512, fc=256)
    return out[None]
```

```python
import functools

import jax
import jax.numpy as jnp
from jax import lax
from jax.experimental import pallas as pl
from jax.experimental.pallas import tpu as pltpu

F32 = jnp.float32
BF16 = jnp.bfloat16

D_MODEL = 1024
N_META = 16
CHUNK = 64
RCHUNK = 128
EPS = 1e-6
HEADS = 4
DV = D_MODEL // HEADS
DK = DV // 2
QK = HEADS * DK
G_RANK = 16
G_TAU = 16.0
CONV_W = 4
LANES = 128
SUBLANES = 8
BF16_ROWS = 16
NEG_BIG = -1e30
VMEM_LIMIT = 60 * 1024 * 1024
CN = DV + LANES

W_IN_RANGES = ((0, 2048), (2056, 5128), (5144, 8216))
O_MQK, O_MV = (0, 0), (0, 1024)
O_MO, O_GQ, O_GK, O_GV = (1, 0), (1, 1024), (1, 1536), (1, 2048)
O_GR, O_GATE_M, O_GATE_G = (2, 0), (2, 1024), (2, 2048)
O_S1 = (3, 0)
GA_LANE = 8
MF_LANE = 24
LEAD_UNUSED = (3, 12, 13, 14)

NT_DIMS = (((1,), (1,)), ((), ()))
TN_DIMS = (((0,), (0,)), ((), ()))


def _dot(a, b):
    return jnp.dot(a, b, preferred_element_type=F32)


def _dot_nt(a, b):
    return lax.dot_general(a, b, NT_DIMS, preferred_element_type=F32)


def _dot_tn(a, b):
    return lax.dot_general(a, b, TN_DIMS, preferred_element_type=F32)


def _pieces(x):
    hi = x.astype(BF16).astype(F32)
    mid = (x - hi).astype(BF16).astype(F32)
    lo = (x - hi - mid).astype(BF16).astype(F32)
    return [hi, mid, lo]


def _log_sigmoid(x):
    return jnp.minimum(x, 0.0) - jnp.log(1.0 + jnp.exp(-jnp.abs(x)))


def _sigmoid(x):
    return 1.0 / (1.0 + jnp.exp(-x))


def _mixer_kernel(x_ref, g1_ref, wa_ref, wb_ref, wc_ref, ws_ref, sb_ref, a2_ref, a2b_ref, cw_ref, cb_ref,
                  mhg_ref, ghg_ref, wbm_ref, wbg_ref, wout_ref, tail0_ref, cn0_ref, m0_ref, s0_ref,
                  h1_ref, tail_ref, cn_ref, m_ref, s_ref,
                  xn_s, qk_s, q_s, k_s, wk_s, vo_s, gq64_s, gq128_s, gki_s, gke64_s, gke128_s, gv_s, hm_s, hg_s,
                  ga_s, gb_s, gc_s, gd_s, gqk_s, tric_s, triu_s, spread_s, wst_s, *, tm, meta):
    nc = tm // RCHUNK
    pairs = [(c, h) for c in range(nc) for h in range(HEADS)]

    @pl.when(pl.program_id(0) == 0)
    def _():
        qk_s[0:SUBLANES, :] = tail0_ref[...]
        cn_ref[...] = cn0_ref[...]
        m_ref[...] = m0_ref[...]
        s_ref[...] = s0_ref[...]
        for h in range(HEADS):
            vo_s[:, h * CN + DV:(h + 1) * CN] = jnp.ones((tm, LANES), BF16)
        ws_t = ws_ref[...].astype(F32).T
        wst_s[...] = jnp.concatenate([ws_t[:SUBLANES], ws_t[MF_LANE:MF_LANE + SUBLANES]], axis=0).astype(BF16)
        row = lax.broadcasted_iota(jnp.int32, (tm, tm), 0)
        col = lax.broadcasted_iota(jnp.int32, (tm, tm), 1)
        tric_s[...] = jnp.where((col <= row) & (col >= (row & -CHUNK)), 1.0, 0.0).astype(BF16)
        triu_s[...] = jnp.where((row <= col) & (row >= (col & -RCHUNK)), 1.0, 0.0).astype(BF16)
        sr = lax.broadcasted_iota(jnp.int32, (LANES, 3 * HEADS * LANES), 0)
        sc = lax.broadcasted_iota(jnp.int32, (LANES, 3 * HEADS * LANES), 1)
        blk = (sc >> 9) * (3 * SUBLANES)
        spread_s[...] = jnp.where(
            (sr >= blk) & (sr < blk + 3 * SUBLANES) & ((sr & (SUBLANES - 1)) == ((sc >> 7) & (HEADS - 1))),
            1.0, 0.0).astype(BF16)

    x = x_ref[...]
    xn = x * lax.rsqrt(jnp.mean(x * x, axis=-1, keepdims=True) + EPS) * g1_ref[...]
    if meta:
        valid = lax.broadcasted_iota(jnp.int32, (tm, 1), 0) >= (tm - N_META)
        xn = jnp.where(valid, xn, 0.0)
    xn_s[...] = xn.astype(BF16)

    w_refs = (wa_ref, wb_ref, wc_ref, ws_ref)

    def proj(seg, width):
        return _dot(xn_s[...], w_refs[seg[0]][:, seg[1]:seg[1] + width])

    def rows(c):
        return slice(c * RCHUNK, (c + 1) * RCHUNK)

    def half(c, second):
        lo = c * RCHUNK + (CHUNK if second else 0)
        return slice(lo, lo + CHUNK)

    crow = lax.broadcasted_iota(jnp.int32, (RCHUNK, RCHUNK), 0)
    ccol = lax.broadcasted_iota(jnp.int32, (RCHUNK, RCHUNK), 1)
    causal = ccol <= crow
    tri_c = tric_s[...]
    tri_up = triu_s[...]
    spread = spread_s[...]

    def st_qk(i, v):
        qk_s[SUBLANES:SUBLANES + tm, i * DV:(i + 1) * DV] = v

    def st_mv(i, v):
        vo_s[:, i * CN:i * CN + DV] = v.astype(BF16)

    def st_gqk(i, v):
        gqk_s[:, i * DV:(i + 1) * DV] = v

    def st_gv(i, v):
        gv_s[:, i * DV:(i + 1) * DV] = v.astype(BF16)

    def st_mo(i, v):
        ga_s[:, i * DV:(i + 1) * DV] = _sigmoid(v)

    def st_gm(i, v):
        gb_s[:, i * DV:(i + 1) * DV] = _sigmoid(v)

    def st_gr(i, v):
        gc_s[:, i * DV:(i + 1) * DV] = v * _sigmoid(v)

    def st_gg(i, v):
        gd_s[:, i * DV:(i + 1) * DV] = _sigmoid(v)

    groups = [(O_MQK, st_qk), (O_MV, st_mv), (O_GQ, st_gqk), (O_GV, st_gv)]
    if not meta:
        groups += [(O_MO, st_mo), (O_GATE_M, st_gm), (O_GR, st_gr), (O_GATE_G, st_gg)]
    queue = [(seg, i, store) for seg, store in groups for i in range(D_MODEL // DV)]
    emitted = [0]

    def fill(n):
        for _ in range(n):
            if emitted[0] < len(queue):
                seg, i, store = queue[emitted[0]]
                store(i, proj((seg[0], seg[1] + i * DV), DV))
                emitted[0] += 1

    def finish_groups(n):
        fill(n * (D_MODEL // DV) - emitted[0])

    gates = _dot_nt(wst_s[...], xn_s[...]) + jnp.concatenate([sb_ref[...]] * (tm // LANES), axis=1)
    s1 = proj(O_S1, LANES)
    fill(2)
    logi = gates[:SUBLANES, :]
    logf = _log_sigmoid(gates[SUBLANES:, :])
    if meta:
        valid_t = lax.broadcasted_iota(jnp.int32, (1, tm), 1) >= (tm - N_META)
        logi = jnp.where(valid_t, logi, NEG_BIG)
        logf = jnp.where(valid_t, logf, 0.0)
    zero8 = jnp.zeros((SUBLANES, tm), F32)
    b4 = _dot(jnp.concatenate(_pieces(logf) + [zero8], axis=0).astype(BF16), tri_up)
    fill(2)
    b_all = b4[:SUBLANES] + b4[SUBLANES:2 * SUBLANES] + b4[2 * SUBLANES:3 * SUBLANES]
    c_all = logi - b_all
    m_run = m_ref[...]
    inter_parts, w_parts, a_chunk = [], [], []
    for c in range(nc):
        b_c = b_all[:, rows(c)]
        g_c = jnp.broadcast_to(b_c[:, RCHUNK - 1:RCHUNK], (SUBLANES, LANES))
        wlog = g_c + c_all[:, rows(c)]
        m_next = jnp.maximum(g_c + m_run, jnp.max(wlog, axis=1, keepdims=True))
        inter_parts.append(b_c + m_run)
        w_parts.append(jnp.exp(wlog - m_next))
        a_chunk.append(jnp.exp(g_c + m_run - m_next))
        m_run = m_next
    m_ref[...] = m_run
    stacked = jnp.concatenate(
        _pieces(b_all) + _pieces(jnp.concatenate(inter_parts, axis=1)) + _pieces(jnp.concatenate(w_parts, axis=1))
        + [jnp.zeros((LANES - 9 * SUBLANES, tm), F32)], axis=0)
    stacked_t = stacked.T.astype(BF16)
    fill(2)
    tiles = _dot(stacked_t, spread)
    fill(2)
    b_t, il_t, w_t = (tiles[:, i * QK:(i + 1) * QK] for i in range(3))

    finish_groups(1)
    for blk in range(2 * QK // DV):
        csl = slice(blk * DV, (blk + 1) * DV)
        conv = cb_ref[:, csl]
        for j in range(CONV_W):
            conv = conv + cw_ref[j:j + 1, csl] * qk_s[pl.ds(SUBLANES - (CONV_W - 1) + j, tm), csl]
        conv = conv * _sigmoid(conv)
        if blk < QK // DV:
            q_s[:, csl] = conv * (DK ** -0.5)
        else:
            ksl = slice(blk * DV - QK, (blk + 1) * DV - QK)
            k_s[:, ksl] = conv.astype(BF16)
            wk_s[:, ksl] = (w_t[:, ksl] * conv).astype(BF16)
        fill(1)
    tail_new = qk_s[tm:tm + SUBLANES, :]
    qk_s[0:SUBLANES, :] = tail_new
    tail_ref[...] = tail_new

    za = _dot(s1.astype(BF16), a2_ref[...]) + a2b_ref[...]
    fill(1)
    loga = _log_sigmoid(za) / G_TAU
    if meta:
        loga = jnp.where(valid, loga, 0.0)
    hi = loga.astype(BF16)
    r1 = loga - hi.astype(F32)
    mid = r1.astype(BF16)
    lo = (r1 - mid.astype(F32)).astype(BF16)
    bc = _dot(tri_c, hi) + _dot(tri_c, mid) + _dot(tri_c, lo)
    fill(2)
    e_blk = jnp.exp(bc)
    tot_a = [bc[c * RCHUNK + CHUNK - 1:c * RCHUNK + CHUNK, :] for c in range(nc)]
    tot_b = [bc[(c + 1) * RCHUNK - 1:(c + 1) * RCHUNK, :] for c in range(nc)]
    e_a = [jnp.exp(t) for t in tot_a]
    e_b = [jnp.exp(t) for t in tot_b]
    rest = jnp.concatenate([t - bc[half(c, s), :] for c in range(nc) for s, t in ((0, tot_a[c]), (1, tot_b[c]))],
                           axis=0)
    e_rest = jnp.exp(rest)
    ones_row = jnp.ones((1, QK), F32)
    to_chunk = jnp.concatenate([jnp.broadcast_to(f, (CHUNK, QK)) for c in range(nc) for f in (ones_row, e_a[c])],
                               axis=0)
    from_blk = jnp.concatenate([jnp.broadcast_to(f, (CHUNK, QK)) for c in range(nc) for f in (e_b[c], ones_row)],
                               axis=0)
    finish_groups(3)
    gq = gqk_s[:, :QK] * (DK ** -0.5)
    gk = gqk_s[:, QK:]
    gq64_s[...] = (gq * e_blk).astype(BF16)
    gq128_s[...] = (gq * e_blk * to_chunk).astype(BF16)
    gki_s[...] = (gk * jnp.exp(-bc)).astype(BF16)
    gke64_s[...] = (gk * e_rest).astype(BF16)
    gke128_s[...] = (gk * e_rest * from_blk).astype(BF16)
    fill(2)
    e_rows = [(e_a[c] * e_b[c])[:, h * DK:(h + 1) * DK] for (c, h) in pairs]
    e_cols = jnp.concatenate(e_rows + [jnp.zeros((LANES - len(pairs), DK), F32)], axis=0).T

    finish_groups(2)
    dmat, rmax, sim = {}, {}, {}
    for (c, h) in pairs:
        d = b_t[rows(c), h * LANES:(h + 1) * LANES] + c_all[h:h + 1, rows(c)]
        d = jnp.where(causal, d, -jnp.inf)
        dmat[c, h] = d
        rmax[c, h] = jnp.max(d, axis=-1, keepdims=True)
    fill(1)
    for (c, h) in pairs:
        sim[c, h] = _dot_nt(q_s[rows(c), h * DK:(h + 1) * DK].astype(BF16), k_s[rows(c), h * DK:(h + 1) * DK])
    fill(1)
    lhs, emr, upd = {}, {}, {}
    for (c, h) in pairs:
        il = il_t[rows(c), h * LANES:(h + 1) * LANES]
        m_row = jnp.maximum(il, rmax[c, h])
        wts = jnp.exp(dmat[c, h] - m_row) * sim[c, h]
        aq = jnp.exp(il - m_row) * q_s[rows(c), h * DK:(h + 1) * DK]
        lhs[c, h] = jnp.concatenate([aq.astype(BF16), wts.astype(BF16)], axis=1)
        emr[c, h] = jnp.exp(-m_row)
    fill(1)
    for (c, h) in pairs:
        upd[c, h] = _dot_tn(wk_s[rows(c), h * DK:(h + 1) * DK], vo_s[rows(c), h * CN:(h + 1) * CN])
    fill(1)
    res = {}
    for h in range(HEADS):
        state = cn_ref[h]
        for c in range(nc):
            rhs = jnp.concatenate([state.astype(BF16), vo_s[rows(c), h * CN:(h + 1) * CN]], axis=0)
            res[c, h] = _dot(lhs[c, h], rhs)
            a = a_chunk[c][h:h + 1, :]
            state = jnp.concatenate([a] * (CN // LANES), axis=1) * state + upd[c, h]
        cn_ref[h] = state
        fill(1)
    for (c, h) in pairs:
        inv = 1.0 / jnp.maximum(jnp.abs(res[c, h][:, DV:]), emr[c, h])
        hm_s[rows(c), h * DV:h * DV + LANES] = res[c, h][:, :LANES] * inv
        hm_s[rows(c), h * DV + LANES:(h + 1) * DV] = res[c, h][:, LANES:DV] * inv
    fill(1)

    finish_groups(4)
    att, gupd = {}, {}
    zblk = jnp.zeros((CHUNK, DK), BF16)
    for (c, h) in pairs:
        dsl = slice(h * DK, (h + 1) * DK)
        ra, rb = half(c, 0), half(c, 1)
        qp = jnp.concatenate([jnp.concatenate([gq64_s[ra, dsl], zblk], axis=1),
                              jnp.concatenate([zblk, gq64_s[rb, dsl]], axis=1)], axis=0)
        kp = jnp.concatenate([jnp.concatenate([gki_s[ra, dsl], gke64_s[ra, dsl]], axis=1),
                              jnp.concatenate([zblk, gki_s[rb, dsl]], axis=1)], axis=0)
        att[c, h] = jnp.where(causal, _dot_nt(qp, kp), 0.0).astype(BF16)
    fill(1)
    for (c, h) in pairs:
        gupd[c, h] = _dot_tn(gke128_s[rows(c), h * DK:(h + 1) * DK], gv_s[rows(c), h * DV:(h + 1) * DV])
    fill(1)
    for h in range(HEADS):
        sst = s_ref[h]
        for c in range(nc):
            lhs_g = jnp.concatenate([gq128_s[rows(c), h * DK:(h + 1) * DK], att[c, h]], axis=1)
            rhs_g = jnp.concatenate([sst.astype(BF16), gv_s[rows(c), h * DV:(h + 1) * DV]], axis=0)
            hg_s[rows(c), h * DV:(h + 1) * DV] = _dot(lhs_g, rhs_g)
            i = c * HEADS + h
            sst = e_cols[:, i:i + 1] * sst + gupd[c, h]
        s_ref[h] = sst
        fill(1)

    if meta:
        h1_ref[...] = x
        return

    def head_norm(src, gain_ref):
        outs = []
        for h in range(HEADS):
            t = src[:, h * DV:(h + 1) * DV]
            outs.append(t * lax.rsqrt(jnp.mean(t * t, axis=-1, keepdims=True) + EPS))
        return jnp.concatenate(outs, axis=-1) * gain_ref[...]

    finish_groups(len(groups))
    y_m = head_norm(hm_s, mhg_ref) * ga_s[...]
    merged = gb_s[...] * _dot(y_m.astype(BF16), wbm_ref[...])
    y_g = head_norm(hg_s, ghg_ref) * gc_s[...]
    merged = merged + gd_s[...] * _dot(y_g.astype(BF16), wbg_ref[...])
    h1_ref[...] = x + _dot(merged.astype(BF16), wout_ref[...])


def _const_spec(shape):
    nd = len(shape)
    return pl.BlockSpec(shape, lambda i: (0,) * nd, pipeline_mode=pl.Buffered(1))


def _mixer_call(x2, consts, state, *, tm, meta):
    t = x2.shape[0]
    state_shapes = [jax.ShapeDtypeStruct(a.shape, F32) for a in state]
    row_spec = pl.BlockSpec((tm, D_MODEL), lambda i: (i, 0))
    state_specs = [pl.BlockSpec(a.shape, lambda i, nd=a.ndim: (0,) * nd) for a in state]
    return pl.pallas_call(
        functools.partial(_mixer_kernel, tm=tm, meta=meta),
        grid=(t // tm,),
        in_specs=[row_spec] + [_const_spec(c.shape) for c in consts] + state_specs,
        out_specs=[row_spec] + state_specs,
        out_shape=[jax.ShapeDtypeStruct((t, D_MODEL), F32)] + state_shapes,
        scratch_shapes=[
            pltpu.VMEM((tm, D_MODEL), BF16),
            pltpu.VMEM((tm + SUBLANES, 2 * QK), F32),
            pltpu.VMEM((tm, QK), F32),
            pltpu.VMEM((tm, QK), BF16),
            pltpu.VMEM((tm, QK), BF16),
            pltpu.VMEM((tm, HEADS * CN), BF16),
            pltpu.VMEM((tm, QK), BF16),
            pltpu.VMEM((tm, QK), BF16),
            pltpu.VMEM((tm, QK), BF16),
            pltpu.VMEM((tm, QK), BF16),
            pltpu.VMEM((tm, QK), BF16),
            pltpu.VMEM((tm, D_MODEL), BF16),
            pltpu.VMEM((tm, D_MODEL), F32),
            pltpu.VMEM((tm, D_MODEL), F32),
            pltpu.VMEM((tm, D_MODEL), F32),
            pltpu.VMEM((tm, D_MODEL), F32),
            pltpu.VMEM((tm, D_MODEL), F32),
            pltpu.VMEM((tm, D_MODEL), F32),
            pltpu.VMEM((tm, D_MODEL), F32),
            pltpu.VMEM((tm, tm), BF16),
            pltpu.VMEM((tm, tm), BF16),
            pltpu.VMEM((LANES, 3 * HEADS * LANES), BF16),
            pltpu.VMEM((2 * SUBLANES, D_MODEL), BF16),
        ],
        compiler_params=pltpu.CompilerParams(
            dimension_semantics=("arbitrary",), vmem_limit_bytes=VMEM_LIMIT),
        name="mixer_meta" if meta else "mixer",
    )(x2, *consts, *state)


def _ffn_kernel(h_ref, g2_ref, wg_ref, wu_ref, wd_ref, gf_ref, o_ref, *, fc):
    h = h_ref[...]
    hn = (h * lax.rsqrt(jnp.mean(h * h, axis=-1, keepdims=True) + EPS) * g2_ref[...]).astype(BF16)
    acc = h
    for f in range(0, wg_ref.shape[1], fc):
        gate = _dot(hn, wg_ref[:, f:f + fc])
        up = _dot(hn, wu_ref[:, f:f + fc])
        acc = acc + _dot((gate * _sigmoid(gate) * up).astype(BF16), wd_ref[f:f + fc, :])
    o_ref[...] = acc * lax.rsqrt(jnp.mean(acc * acc, axis=-1, keepdims=True) + EPS) * gf_ref[...]


def _ffn_call(h1, g2, wg, wu, wd, gf, *, tm, fc):
    t = h1.shape[0]
    row_spec = pl.BlockSpec((tm, D_MODEL), lambda i: (i, 0))
    return pl.pallas_call(
        functools.partial(_ffn_kernel, fc=fc),
        grid=(t // tm,),
        in_specs=[row_spec] + [_const_spec(a.shape) for a in (g2, wg, wu, wd, gf)],
        out_specs=row_spec,
        out_shape=jax.ShapeDtypeStruct((t, D_MODEL), F32),
        compiler_params=pltpu.CompilerParams(
            dimension_semantics=("arbitrary",), vmem_limit_bytes=VMEM_LIMIT),
        name="ffn",
    )(h1, g2, wg, wu, wd, gf)


def _split_w_in(w):
    d = w.shape[0]
    wide = [w[:, a:b].astype(BF16) for a, b in W_IN_RANGES]
    zeros = lambda n: jnp.zeros((d, n), w.dtype)
    small = jnp.concatenate([w[:, 2048:2052], zeros(GA_LANE - HEADS), w[:, 5128:5144], w[:, 2052:2056],
                             zeros(LANES - MF_LANE - HEADS)], axis=1)
    return wide + [small.astype(BF16)]


def kernel(x, meta_tokens, norm1_g, w_in, conv_w, conv_b, m_gate_b, g_a2, g_a2_b, m_head_g, g_head_g,
           w_branch_m, w_branch_g, w_out, norm2_g, w_ff_gate, w_ff_up, w_ff_down, final_g):
    bsz, seq, d = x.shape
    assert bsz == 1 and d == D_MODEL and norm1_g.shape[0] == 1 and seq % CHUNK == 0
    row = lambda a: a.reshape(1, -1).astype(F32)

    gate_bias = jnp.zeros((2 * SUBLANES, LANES), F32)
    gate_bias = gate_bias.at[:HEADS].set(m_gate_b[0, 0][:, None]).at[SUBLANES:SUBLANES + HEADS].set(m_gate_b[0, 1][:, None])
    a2 = jnp.zeros((LANES, QK), F32).at[GA_LANE:GA_LANE + G_RANK].set(g_a2[0]).astype(BF16)
    conv_w8 = jnp.zeros((SUBLANES, 2 * QK), F32).at[:CONV_W].set(conv_w[0])
    consts = (row(norm1_g[0]), *_split_w_in(w_in[0]), gate_bias, a2, row(g_a2_b[0]), conv_w8, row(conv_b[0]),
              row(m_head_g[0]), row(g_head_g[0]),
              w_branch_m[0].astype(BF16), w_branch_g[0].astype(BF16), w_out[0].astype(BF16))

    zero_state = (jnp.zeros((SUBLANES, 2 * QK), F32), jnp.zeros((HEADS, DK, CN), F32),
                  jnp.zeros((SUBLANES, LANES), F32), jnp.zeros((HEADS, DK, DV), F32))
    lead = jnp.concatenate([jnp.zeros((RCHUNK - N_META, d), F32), meta_tokens.astype(F32)], axis=0)
    unused = jnp.zeros((SUBLANES, LANES), BF16)
    lead_consts = tuple(unused if i in LEAD_UNUSED else c for i, c in enumerate(consts))
    _, *state = _mixer_call(lead, lead_consts, zero_state, tm=RCHUNK, meta=True)
    h1, *_ = _mixer_call(x[0], consts, tuple(state), tm=512, meta=False)
    out = _ffn_call(h1, row(norm2_g[0]), w_ff_gate[0].astype(BF16), w_ff_up[0].astype(BF16),
                    w_ff_down[0].astype(BF16), row(final_g), tm=512, fc=256)
    return out[None]
```

```python
import functools

import jax
import jax.numpy as jnp
from jax import lax
from jax.experimental import pallas as pl
from jax.experimental.pallas import tpu as pltpu

F32 = jnp.float32
BF16 = jnp.bfloat16

D_MODEL = 1024
N_META = 16
CHUNK = 64
RCHUNK = 128
EPS = 1e-6
HEADS = 4
DV = D_MODEL // HEADS
DK = DV // 2
QK = HEADS * DK
G_RANK = 16
G_TAU = 16.0
CONV_W = 4
LANES = 128
SUBLANES = 8
BF16_ROWS = 16
NEG_BIG = -1e30
VMEM_LIMIT = 60 * 1024 * 1024
CN = DV + LANES

W_IN_RANGES = ((0, 2048), (2056, 5128), (5144, 8216))
O_MQK, O_MV = (0, 0), (0, 1024)
O_MO, O_GQ, O_GK, O_GV = (1, 0), (1, 1024), (1, 1536), (1, 2048)
O_GR, O_GATE_M, O_GATE_G = (2, 0), (2, 1024), (2, 2048)
O_S1 = (3, 0)
GA_LANE = 8
MF_LANE = 24
LEAD_UNUSED = (3, 12, 13, 14)

NT_DIMS = (((1,), (1,)), ((), ()))
TN_DIMS = (((0,), (0,)), ((), ()))


def _dot(a, b):
    return jnp.dot(a, b, preferred_element_type=F32)


def _dot_nt(a, b):
    return lax.dot_general(a, b, NT_DIMS, preferred_element_type=F32)


def _dot_tn(a, b):
    return lax.dot_general(a, b, TN_DIMS, preferred_element_type=F32)


def _pieces(x):
    hi = x.astype(BF16).astype(F32)
    mid = (x - hi).astype(BF16).astype(F32)
    lo = (x - hi - mid).astype(BF16).astype(F32)
    return [hi, mid, lo]


def _log_sigmoid(x):
    return jnp.minimum(x, 0.0) - jnp.log(1.0 + jnp.exp(-jnp.abs(x)))


def _sigmoid(x):
    return 1.0 / (1.0 + jnp.exp(-x))


def _mixer_kernel(x_ref, g1_ref, wa_ref, wb_ref, wc_ref, ws_ref, sb_ref, a2_ref, a2b_ref, cw_ref, cb_ref,
                  mhg_ref, ghg_ref, wbm_ref, wbg_ref, wout_ref, tail0_ref, cn0_ref, m0_ref, s0_ref,
                  h1_ref, tail_ref, cn_ref, m_ref, s_ref,
                  xn_s, qk_s, q_s, k_s, wk_s, vo_s, gq64_s, gq128_s, gki_s, gke64_s, gke128_s, gv_s, hm_s, hg_s,
                  ga_s, gb_s, gc_s, gd_s, gqk_s, tric_s, triu_s, spread_s, wst_s, *, tm, meta):
    nc = tm // RCHUNK
    pairs = [(c, h) for c in range(nc) for h in range(HEADS)]

    @pl.when(pl.program_id(0) == 0)
    def _():
        qk_s[0:SUBLANES, :] = tail0_ref[...]
        cn_ref[...] = cn0_ref[...]
        m_ref[...] = m0_ref[...]
        s_ref[...] = s0_ref[...]
        for h in range(HEADS):
            vo_s[:, h * CN + DV:(h + 1) * CN] = jnp.ones((tm, LANES), BF16)
        ws_t = ws_ref[...].astype(F32).T
        wst_s[...] = jnp.concatenate([ws_t[:SUBLANES], ws_t[MF_LANE:MF_LANE + SUBLANES]], axis=0).astype(BF16)
        row = lax.broadcasted_iota(jnp.int32, (tm, tm), 0)
        col = lax.broadcasted_iota(jnp.int32, (tm, tm), 1)
        tric_s[...] = jnp.where((col <= row) & (col >= (row & -CHUNK)), 1.0, 0.0).astype(BF16)
        triu_s[...] = jnp.where((row <= col) & (row >= (col & -RCHUNK)), 1.0, 0.0).astype(BF16)
        sr = lax.broadcasted_iota(jnp.int32, (LANES, 3 * HEADS * LANES), 0)
        sc = lax.broadcasted_iota(jnp.int32, (LANES, 3 * HEADS * LANES), 1)
        blk = (sc >> 9) * (3 * SUBLANES)
        spread_s[...] = jnp.where(
            (sr >= blk) & (sr < blk + 3 * SUBLANES) & ((sr & (SUBLANES - 1)) == ((sc >> 7) & (HEADS - 1))),
            1.0, 0.0).astype(BF16)

    x = x_ref[...]
    xn = x * lax.rsqrt(jnp.mean(x * x, axis=-1, keepdims=True) + EPS) * g1_ref[...]
    if meta:
        valid = lax.broadcasted_iota(jnp.int32, (tm, 1), 0) >= (tm - N_META)
        xn = jnp.where(valid, xn, 0.0)
    xn_s[...] = xn.astype(BF16)

    w_refs = (wa_ref, wb_ref, wc_ref, ws_ref)

    def proj(seg, width):
        return _dot(xn_s[...], w_refs[seg[0]][:, seg[1]:seg[1] + width])

    def rows(c):
        return slice(c * RCHUNK, (c + 1) * RCHUNK)

    def half(c, second):
        lo = c * RCHUNK + (CHUNK if second else 0)
        return slice(lo, lo + CHUNK)

    crow = lax.broadcasted_iota(jnp.int32, (RCHUNK, RCHUNK), 0)
    ccol = lax.broadcasted_iota(jnp.int32, (RCHUNK, RCHUNK), 1)
    causal = ccol <= crow
    tri_c = tric_s[...]
    tri_up = triu_s[...]
    spread = spread_s[...]

    def st_qk(i, v):
        qk_s[SUBLANES:SUBLANES + tm, i * DV:(i + 1) * DV] = v

    def st_mv(i, v):
        vo_s[:, i * CN:i * CN + DV] = v.astype(BF16)

    def st_gqk(i, v):
        gqk_s[:, i * DV:(i + 1) * DV] = v

    def st_gv(i, v):
        gv_s[:, i * DV:(i + 1) * DV] = v.astype(BF16)

    def st_mo(i, v):
        ga_s[:, i * DV:(i + 1) * DV] = _sigmoid(v)

    def st_gm(i, v):
        gb_s[:, i * DV:(i + 1) * DV] = _sigmoid(v)

    def st_gr(i, v):
        gc_s[:, i * DV:(i + 1) * DV] = v * _sigmoid(v)

    def st_gg(i, v):
        gd_s[:, i * DV:(i + 1) * DV] = _sigmoid(v)

    groups = [(O_MQK, st_qk), (O_MV, st_mv), (O_GQ, st_gqk), (O_GV, st_gv)]
    if not meta:
        groups += [(O_MO, st_mo), (O_GATE_M, st_gm), (O_GR, st_gr), (O_GATE_G, st_gg)]
    queue = [(seg, i, store) for seg, store in groups for i in range(D_MODEL // DV)]
    emitted = [0]

    def fill(n):
        for _ in range(n):
            if emitted[0] < len(queue):
                seg, i, store = queue[emitted[0]]
                store(i, proj((seg[0], seg[1] + i * DV), DV))
                emitted[0] += 1

    def finish_groups(n):
        fill(n * (D_MODEL // DV) - emitted[0])

    gates = _dot_nt(wst_s[...], xn_s[...]) + jnp.concatenate([sb_ref[...]] * (tm // LANES), axis=1)
    s1 = proj(O_S1, LANES)
    fill(2)
    logi = gates[:SUBLANES, :]
    logf = _log_sigmoid(gates[SUBLANES:, :])
    if meta:
        valid_t = lax.broadcasted_iota(jnp.int32, (1, tm), 1) >= (tm - N_META)
        logi = jnp.where(valid_t, logi, NEG_BIG)
        logf = jnp.where(valid_t, logf, 0.0)
    zero8 = jnp.zeros((SUBLANES, tm), F32)
    b4 = _dot(jnp.concatenate(_pieces(logf) + [zero8], axis=0).astype(BF16), tri_up)
    fill(2)
    b_all = b4[:SUBLANES] + b4[SUBLANES:2 * SUBLANES] + b4[2 * SUBLANES:3 * SUBLANES]
    c_all = logi - b_all
    m_run = m_ref[...]
    inter_parts, w_parts, a_chunk = [], [], []
    for c in range(nc):
        b_c = b_all[:, rows(c)]
        g_c = jnp.broadcast_to(b_c[:, RCHUNK - 1:RCHUNK], (SUBLANES, LANES))
        wlog = g_c + c_all[:, rows(c)]
        m_next = jnp.maximum(g_c + m_run, jnp.max(wlog, axis=1, keepdims=True))
        inter_parts.append(b_c + m_run)
        w_parts.append(jnp.exp(wlog - m_next))
        a_chunk.append(jnp.exp(g_c + m_run - m_next))
        m_run = m_next
    m_ref[...] = m_run
    stacked = jnp.concatenate(
        _pieces(b_all) + _pieces(jnp.concatenate(inter_parts, axis=1)) + _pieces(jnp.concatenate(w_parts, axis=1))
        + [jnp.zeros((LANES - 9 * SUBLANES, tm), F32)], axis=0)
    stacked_t = stacked.T.astype(BF16)
    fill(2)
    tiles = _dot(stacked_t, spread)
    fill(2)
    b_t, il_t, w_t = (tiles[:, i * QK:(i + 1) * QK] for i in range(3))

    finish_groups(1)
    for blk in range(2 * QK // DV):
        csl = slice(blk * DV, (blk + 1) * DV)
        conv = cb_ref[:, csl]
        for j in range(CONV_W):
            conv = conv + cw_ref[j:j + 1, csl] * qk_s[pl.ds(SUBLANES - (CONV_W - 1) + j, tm), csl]
        conv = conv * _sigmoid(conv)
        if blk < QK // DV:
            q_s[:, csl] = conv * (DK ** -0.5)
        else:
            ksl = slice(blk * DV - QK, (blk + 1) * DV - QK)
            k_s[:, ksl] = conv.astype(BF16)
            wk_s[:, ksl] = (w_t[:, ksl] * conv).astype(BF16)
        fill(1)
    tail_new = qk_s[tm:tm + SUBLANES, :]
    qk_s[0:SUBLANES, :] = tail_new
    tail_ref[...] = tail_new

    za = _dot(s1.astype(BF16), a2_ref[...]) + a2b_ref[...]
    fill(1)
    loga = _log_sigmoid(za) / G_TAU
    if meta:
        loga = jnp.where(valid, loga, 0.0)
    hi = loga.astype(BF16)
    r1 = loga - hi.astype(F32)
    mid = r1.astype(BF16)
    lo = (r1 - mid.astype(F32)).astype(BF16)
    bc = _dot(tri_c, hi) + _dot(tri_c, mid) + _dot(tri_c, lo)
    fill(2)
    e_blk = jnp.exp(bc)
    tot_a = [bc[c * RCHUNK + CHUNK - 1:c * RCHUNK + CHUNK, :] for c in range(nc)]
    tot_b = [bc[(c + 1) * RCHUNK - 1:(c + 1) * RCHUNK, :] for c in range(nc)]
    e_a = [jnp.exp(t) for t in tot_a]
    e_b = [jnp.exp(t) for t in tot_b]
    rest = jnp.concatenate([t - bc[half(c, s), :] for c in range(nc) for s, t in ((0, tot_a[c]), (1, tot_b[c]))],
                           axis=0)
    e_rest = jnp.exp(rest)
    ones_row = jnp.ones((1, QK), F32)
    to_chunk = jnp.concatenate([jnp.broadcast_to(f, (CHUNK, QK)) for c in range(nc) for f in (ones_row, e_a[c])],
                               axis=0)
    from_blk = jnp.concatenate([jnp.broadcast_to(f, (CHUNK, QK)) for c in range(nc) for f in (e_b[c], ones_row)],
                               axis=0)
    finish_groups(3)
    gq = gqk_s[:, :QK] * (DK ** -0.5)
    gk = gqk_s[:, QK:]
    gq64_s[...] = (gq * e_blk).astype(BF16)
    gq128_s[...] = (gq * e_blk * to_chunk).astype(BF16)
    gki_s[...] = (gk * jnp.exp(-bc)).astype(BF16)
    gke64_s[...] = (gk * e_rest).astype(BF16)
    gke128_s[...] = (gk * e_rest * from_blk).astype(BF16)
    fill(2)
    e_rows = [(e_a[c] * e_b[c])[:, h * DK:(h + 1) * DK] for (c, h) in pairs]
    e_cols = jnp.concatenate(e_rows + [jnp.zeros((LANES - len(pairs), DK), F32)], axis=0).T

    finish_groups(2)
    dmat, rmax, sim = {}, {}, {}
    for (c, h) in pairs:
        d = b_t[rows(c), h * LANES:(h + 1) * LANES] + c_all[h:h + 1, rows(c)]
        d = jnp.where(causal, d, -jnp.inf)
        dmat[c, h] = d
        rmax[c, h] = jnp.max(d, axis=-1, keepdims=True)
    fill(1)
    for (c, h) in pairs:
        sim[c, h] = _dot_nt(q_s[rows(c), h * DK:(h + 1) * DK].astype(BF16), k_s[rows(c), h * DK:(h + 1) * DK])
    fill(1)
    lhs, emr, upd = {}, {}, {}
    for (c, h) in pairs:
        il = il_t[rows(c), h * LANES:(h + 1) * LANES]
        m_row = jnp.maximum(il, rmax[c, h])
        wts = jnp.exp(dmat[c, h] - m_row) * sim[c, h]
        aq = jnp.exp(il - m_row) * q_s[rows(c), h * DK:(h + 1) * DK]
        lhs[c, h] = jnp.concatenate([aq.astype(BF16), wts.astype(BF16)], axis=1)
        emr[c, h] = jnp.exp(-m_row)
    fill(1)
    for (c, h) in pairs:
        upd[c, h] = _dot_tn(wk_s[rows(c), h * DK:(h + 1) * DK], vo_s[rows(c), h * CN:(h + 1) * CN])
    fill(1)
    res = {}
    for h in range(HEADS):
        state = cn_ref[h]
        for c in range(nc):
            rhs = jnp.concatenate([state.astype(BF16), vo_s[rows(c), h * CN:(h + 1) * CN]], axis=0)
            res[c, h] = _dot(lhs[c, h], rhs)
            a = a_chunk[c][h:h + 1, :]
            state = jnp.concatenate([a] * (CN // LANES), axis=1) * state + upd[c, h]
        cn_ref[h] = state
        fill(1)
    for (c, h) in pairs:
        inv = 1.0 / jnp.maximum(jnp.abs(res[c, h][:, DV:]), emr[c, h])
        hm_s[rows(c), h * DV:h * DV + LANES] = res[c, h][:, :LANES] * inv
        hm_s[rows(c), h * DV + LANES:(h + 1) * DV] = res[c, h][:, LANES:DV] * inv
    fill(1)

    finish_groups(4)
    att, gupd = {}, {}
    zblk = jnp.zeros((CHUNK, DK), BF16)
    for (c, h) in pairs:
        dsl = slice(h * DK, (h + 1) * DK)
        ra, rb = half(c, 0), half(c, 1)
        qp = jnp.concatenate([jnp.concatenate([gq64_s[ra, dsl], zblk], axis=1),
                              jnp.concatenate([zblk, gq64_s[rb, dsl]], axis=1)], axis=0)
        kp = jnp.concatenate([jnp.concatenate([gki_s[ra, dsl], gke64_s[ra, dsl]], axis=1),
                              jnp.concatenate([zblk, gki_s[rb, dsl]], axis=1)], axis=0)
        att[c, h] = jnp.where(causal, _dot_nt(qp, kp), 0.0).astype(BF16)
    fill(1)
    for (c, h) in pairs:
        gupd[c, h] = _dot_tn(gke128_s[rows(c), h * DK:(h + 1) * DK], gv_s[rows(c), h * DV:(h + 1) * DV])
    fill(1)
    for h in range(HEADS):
        sst = s_ref[h]
        for c in range(nc):
            lhs_g = jnp.concatenate([gq128_s[rows(c), h * DK:(h + 1) * DK], att[c, h]], axis=1)
            rhs_g = jnp.concatenate([sst.astype(BF16), gv_s[rows(c), h * DV:(h + 1) * DV]], axis=0)
            hg_s[rows(c), h * DV:(h + 1) * DV] = _dot(lhs_g, rhs_g)
            i = c * HEADS + h
            sst = e_cols[:, i:i + 1] * sst + gupd[c, h]
        s_ref[h] = sst
        fill(1)

    if meta:
        h1_ref[...] = x
        return

    def head_norm(src, gain_ref):
        outs = []
        for h in range(HEADS):
            t = src[:, h * DV:(h + 1) * DV]
            outs.append(t * lax.rsqrt(jnp.mean(t * t, axis=-1, keepdims=True) + EPS))
        return jnp.concatenate(outs, axis=-1) * gain_ref[...]

    finish_groups(len(groups))
    y_m = head_norm(hm_s, mhg_ref) * ga_s[...]
    merged = gb_s[...] * _dot(y_m.astype(BF16), wbm_ref[...])
    y_g = head_norm(hg_s, ghg_ref) * gc_s[...]
    merged = merged + gd_s[...] * _dot(y_g.astype(BF16), wbg_ref[...])
    h1_ref[...] = x + _dot(merged.astype(BF16), wout_ref[...])


def _const_spec(shape):
    nd = len(shape)
    return pl.BlockSpec(shape, lambda i: (0,) * nd, pipeline_mode=pl.Buffered(1))


def _mixer_call(x2, consts, state, *, tm, meta):
    t = x2.shape[0]
    state_shapes = [jax.ShapeDtypeStruct(a.shape, F32) for a in state]
    row_spec = pl.BlockSpec((tm, D_MODEL), lambda i: (i, 0))
    state_specs = [pl.BlockSpec(a.shape, lambda i, nd=a.ndim: (0,) * nd) for a in state]
    return pl.pallas_call(
        functools.partial(_mixer_kernel, tm=tm, meta=meta),
        grid=(t // tm,),
        in_specs=[row_spec] + [_const_spec(c.shape) for c in consts] + state_specs,
        out_specs=[row_spec] + state_specs,
        out_shape=[jax.ShapeDtypeStruct((t, D_MODEL), F32)] + state_shapes,
        scratch_shapes=[
            pltpu.VMEM((tm, D_MODEL), BF16),
            pltpu.VMEM((tm + SUBLANES, 2 * QK), F32),
            pltpu.VMEM((tm, QK), F32),
            pltpu.VMEM((tm, QK), BF16),
            pltpu.VMEM((tm, QK), BF16),
            pltpu.VMEM((tm, HEADS * CN), BF16),
            pltpu.VMEM((tm, QK), BF16),
            pltpu.VMEM((tm, QK), BF16),
            pltpu.VMEM((tm, QK), BF16),
            pltpu.VMEM((tm, QK), BF16),
            pltpu.VMEM((tm, QK), BF16),
            pltpu.VMEM((tm, D_MODEL), BF16),
            pltpu.VMEM((tm, D_MODEL), F32),
            pltpu.VMEM((tm, D_MODEL), F32),
            pltpu.VMEM((tm, D_MODEL), F32),
            pltpu.VMEM((tm, D_MODEL), F32),
            pltpu.VMEM((tm, D_MODEL), F32),
            pltpu.VMEM((tm, D_MODEL), F32),
            pltpu.VMEM((tm, D_MODEL), F32),
            pltpu.VMEM((tm, tm), BF16),
            pltpu.VMEM((tm, tm), BF16),
            pltpu.VMEM((LANES, 3 * HEADS * LANES), BF16),
            pltpu.VMEM((2 * SUBLANES, D_MODEL), BF16),
        ],
        compiler_params=pltpu.CompilerParams(
            dimension_semantics=("arbitrary",), vmem_limit_bytes=VMEM_LIMIT),
        name="mixer_meta" if meta else "mixer",
    )(x2, *consts, *state)


def _ffn_kernel(h_ref, g2_ref, wg_ref, wu_ref, wd_ref, gf_ref, o_ref, *, fc):
    h = h_ref[...]
    hn = (h * lax.rsqrt(jnp.mean(h * h, axis=-1, keepdims=True) + EPS) * g2_ref[...]).astype(BF16)
    acc = h
    for f in range(0, wg_ref.shape[1], fc):
        gate = _dot(hn, wg_ref[:, f:f + fc])
        up = _dot(hn, wu_ref[:, f:f + fc])
        acc = acc + _dot((gate * _sigmoid(gate) * up).astype(BF16), wd_ref[f:f + fc, :])
    o_ref[...] = acc * lax.rsqrt(jnp.mean(acc * acc, axis=-1, keepdims=True) + EPS) * gf_ref[...]


def _ffn_call(h1, g2, wg, wu, wd, gf, *, tm, fc):
    t = h1.shape[0]
    row_spec = pl.BlockSpec((tm, D_MODEL), lambda i: (i, 0))
    return pl.pallas_call(
        functools.partial(_ffn_kernel, fc=fc),
        grid=(t // tm,),
        in_specs=[row_spec] + [_const_spec(a.shape) for a in (g2, wg, wu, wd, gf)],
        out_specs=row_spec,
        out_shape=jax.ShapeDtypeStruct((t, D_MODEL), F32),
        compiler_params=pltpu.CompilerParams(
            dimension_semantics=("arbitrary",), vmem_limit_bytes=VMEM_LIMIT),
        name="ffn",
    )(h1, g2, wg, wu, wd, gf)


def _split_w_in_kernel(w_ref, wa_ref, wb_ref, wc_ref, ws_ref):
    for (a, b), o_ref in zip(W_IN_RANGES, (wa_ref, wb_ref, wc_ref)):
        o_ref[...] = w_ref[:, a:b].astype(BF16)
    zeros = lambda n: jnp.zeros((w_ref.shape[0], n), F32)
    ws_ref[...] = jnp.concatenate(
        [w_ref[:, 2048:2052], zeros(GA_LANE - HEADS), w_ref[:, 5128:5144], w_ref[:, 2052:2056],
         zeros(LANES - MF_LANE - HEADS)], axis=1).astype(BF16)


def _split_w_in(w, *, rows=128):
    d, n = w.shape
    widths = [b - a for a, b in W_IN_RANGES] + [LANES]
    return pl.pallas_call(
        _split_w_in_kernel,
        grid=(d // rows,),
        in_specs=[pl.BlockSpec((rows, n), lambda i: (i, 0))],
        out_specs=[pl.BlockSpec((rows, c), lambda i: (i, 0)) for c in widths],
        out_shape=[jax.ShapeDtypeStruct((d, c), BF16) for c in widths],
        compiler_params=pltpu.CompilerParams(dimension_semantics=("arbitrary",)),
        name="split_w_in",
    )(w)


def kernel(x, meta_tokens, norm1_g, w_in, conv_w, conv_b, m_gate_b, g_a2, g_a2_b, m_head_g, g_head_g,
           w_branch_m, w_branch_g, w_out, norm2_g, w_ff_gate, w_ff_up, w_ff_down, final_g):
    bsz, seq, d = x.shape
    assert bsz == 1 and d == D_MODEL and norm1_g.shape[0] == 1 and seq % CHUNK == 0
    row = lambda a: a.reshape(1, -1).astype(F32)

    gate_bias = jnp.zeros((2 * SUBLANES, LANES), F32)
    gate_bias = gate_bias.at[:HEADS].set(m_gate_b[0, 0][:, None]).at[SUBLANES:SUBLANES + HEADS].set(m_gate_b[0, 1][:, None])
    a2 = jnp.zeros((LANES, QK), F32).at[GA_LANE:GA_LANE + G_RANK].set(g_a2[0]).astype(BF16)
    conv_w8 = jnp.zeros((SUBLANES, 2 * QK), F32).at[:CONV_W].set(conv_w[0])
    consts = (row(norm1_g[0]), *_split_w_in(w_in[0]), gate_bias, a2, row(g_a2_b[0]), conv_w8, row(conv_b[0]),
              row(m_head_g[0]), row(g_head_g[0]),
              w_branch_m[0].astype(BF16), w_branch_g[0].astype(BF16), w_out[0].astype(BF16))

    zero_state = (jnp.zeros((SUBLANES, 2 * QK), F32), jnp.zeros((HEADS, DK, CN), F32),
                  jnp.zeros((SUBLANES, LANES), F32), jnp.zeros((HEADS, DK, DV), F32))
    lead = jnp.concatenate([jnp.zeros((RCHUNK - N_META, d), F32), meta_tokens.astype(F32)], axis=0)
    unused = jnp.zeros((SUBLANES, LANES), BF16)
    lead_consts = tuple(unused if i in LEAD_UNUSED else c for i, c in enumerate(consts))
    _, *state = _mixer_call(lead, lead_consts, zero_state, tm=RCHUNK, meta=True)
    h1, *_ = _mixer_call(x[0], consts, tuple(state), tm=512, meta=False)
    out = _ffn_call(h1, row(norm2_g[0]), w_ff_gate[0].astype(BF16), w_ff_up[0].astype(BF16),
                    w_ff_down[0].astype(BF16), row(final_g), tm=512, fc=256)
    return out[None]
```

```python
import functools

import jax
import jax.numpy as jnp
from jax import lax
from jax.experimental import pallas as pl
from jax.experimental.pallas import tpu as pltpu

F32 = jnp.float32
BF16 = jnp.bfloat16

D_MODEL = 1024
N_META = 16
CHUNK = 64
RCHUNK = 128
EPS = 1e-6
HEADS = 4
DV = D_MODEL // HEADS
DK = DV // 2
QK = HEADS * DK
G_RANK = 16
G_TAU = 16.0
CONV_W = 4
LANES = 128
SUBLANES = 8
BF16_ROWS = 16
NEG_BIG = -1e30
VMEM_LIMIT = 60 * 1024 * 1024
CN = DV + LANES

W_IN_RANGES = ((0, 2048), (2056, 5128), (5144, 8216))
O_MQK, O_MV = (0, 0), (0, 1024)
O_MO, O_GQ, O_GK, O_GV = (1, 0), (1, 1024), (1, 1536), (1, 2048)
O_GR, O_GATE_M, O_GATE_G = (2, 0), (2, 1024), (2, 2048)
O_S1 = (3, 0)
GA_LANE = 8
MF_LANE = 24
LEAD_UNUSED = (3, 12, 13, 14)

NT_DIMS = (((1,), (1,)), ((), ()))
TN_DIMS = (((0,), (0,)), ((), ()))


def _dot(a, b):
    return jnp.dot(a, b, preferred_element_type=F32)


def _dot_nt(a, b):
    return lax.dot_general(a, b, NT_DIMS, preferred_element_type=F32)


def _dot_tn(a, b):
    return lax.dot_general(a, b, TN_DIMS, preferred_element_type=F32)


def _pieces(x):
    hi = x.astype(BF16).astype(F32)
    mid = (x - hi).astype(BF16).astype(F32)
    lo = (x - hi - mid).astype(BF16).astype(F32)
    return [hi, mid, lo]


def _log_sigmoid(x):
    return jnp.minimum(x, 0.0) - jnp.log(1.0 + jnp.exp(-jnp.abs(x)))


def _sigmoid(x):
    return 1.0 / (1.0 + jnp.exp(-x))


def _mixer_kernel(x_ref, g1_ref, wa_ref, wb_ref, wc_ref, ws_ref, sb_ref, a2_ref, a2b_ref, cw_ref, cb_ref,
                  mhg_ref, ghg_ref, wbm_ref, wbg_ref, wout_ref, tail0_ref, cn0_ref, m0_ref, s0_ref,
                  h1_ref, tail_ref, cn_ref, m_ref, s_ref,
                  xn_s, qk_s, q_s, k_s, wk_s, vo_s, gq64_s, gq128_s, gki_s, gke64_s, gke128_s, gv_s, hm_s, hg_s,
                  ga_s, gb_s, gc_s, gd_s, gqk_s, tric_s, triu_s, spread_s, wst_s, *, tm, meta):
    nc = tm // RCHUNK
    pairs = [(c, h) for c in range(nc) for h in range(HEADS)]

    @pl.when(pl.program_id(0) == 0)
    def _():
        qk_s[0:SUBLANES, :] = tail0_ref[...]
        cn_ref[...] = cn0_ref[...]
        m_ref[...] = m0_ref[...]
        s_ref[...] = s0_ref[...]
        for h in range(HEADS):
            vo_s[:, h * CN + DV:(h + 1) * CN] = jnp.ones((tm, LANES), BF16)
        ws_t = ws_ref[...].astype(F32).T
        wst_s[...] = jnp.concatenate([ws_t[:SUBLANES], ws_t[MF_LANE:MF_LANE + SUBLANES],
                                      ws_t[GA_LANE:GA_LANE + G_RANK]], axis=0).astype(BF16)
        row = lax.broadcasted_iota(jnp.int32, (tm, tm), 0)
        col = lax.broadcasted_iota(jnp.int32, (tm, tm), 1)
        tric_s[...] = jnp.where((col <= row) & (col >= (row & -CHUNK)), 1.0, 0.0)[:RCHUNK, :RCHUNK].astype(BF16)
        triu_s[...] = jnp.where((row <= col) & (row >= (col & -RCHUNK)), 1.0, 0.0).astype(BF16)
        sr = lax.broadcasted_iota(jnp.int32, (LANES, 3 * HEADS * LANES), 0)
        sc = lax.broadcasted_iota(jnp.int32, (LANES, 3 * HEADS * LANES), 1)
        blk = (sc >> 9) * (3 * SUBLANES)
        spread_s[...] = jnp.where(
            (sr >= blk) & (sr < blk + 3 * SUBLANES) & ((sr & (SUBLANES - 1)) == ((sc >> 7) & (HEADS - 1))),
            1.0, 0.0).astype(BF16)

    x = x_ref[...]
    xn = x * lax.rsqrt(jnp.mean(x * x, axis=-1, keepdims=True) + EPS) * g1_ref[...]
    if meta:
        valid = lax.broadcasted_iota(jnp.int32, (tm, 1), 0) >= (tm - N_META)
        xn = jnp.where(valid, xn, 0.0)
    xn_s[...] = xn.astype(BF16)

    w_refs = (wa_ref, wb_ref, wc_ref, ws_ref)

    def proj(seg, width):
        return _dot(xn_s[...], w_refs[seg[0]][:, seg[1]:seg[1] + width])

    def rows(c):
        return slice(c * RCHUNK, (c + 1) * RCHUNK)

    def half(c, second):
        lo = c * RCHUNK + (CHUNK if second else 0)
        return slice(lo, lo + CHUNK)

    crow = lax.broadcasted_iota(jnp.int32, (RCHUNK, RCHUNK), 0)
    ccol = lax.broadcasted_iota(jnp.int32, (RCHUNK, RCHUNK), 1)
    causal = ccol <= crow
    tri_c = tric_s[...]
    tri_up = triu_s[...]
    spread = spread_s[...]

    def st_qk(i, v):
        qk_s[SUBLANES:SUBLANES + tm, i * DV:(i + 1) * DV] = v

    def st_mv(i, v):
        vo_s[:, i * CN:i * CN + DV] = v.astype(BF16)

    def st_gqk(i, v):
        gqk_s[:, i * DV:(i + 1) * DV] = v

    def st_gv(i, v):
        gv_s[:, i * DV:(i + 1) * DV] = v.astype(BF16)

    def st_mo(i, v):
        ga_s[:, i * DV:(i + 1) * DV] = _sigmoid(v)

    def st_gm(i, v):
        gb_s[:, i * DV:(i + 1) * DV] = _sigmoid(v)

    def st_gr(i, v):
        gc_s[:, i * DV:(i + 1) * DV] = v * _sigmoid(v)

    def st_gg(i, v):
        gd_s[:, i * DV:(i + 1) * DV] = _sigmoid(v)

    groups = [(O_MQK, st_qk), (O_MV, st_mv), (O_GQ, st_gqk), (O_GV, st_gv)]
    if not meta:
        groups += [(O_MO, st_mo), (O_GATE_M, st_gm), (O_GR, st_gr), (O_GATE_G, st_gg)]
    queue = [(seg, i, store) for seg, store in groups for i in range(D_MODEL // DV)]
    emitted = [0]

    def fill(n):
        for _ in range(n):
            if emitted[0] < len(queue):
                seg, i, store = queue[emitted[0]]
                store(i, proj((seg[0], seg[1] + i * DV), DV))
                emitted[0] += 1

    def finish_groups(n):
        fill(n * (D_MODEL // DV) - emitted[0])

    narrow = _dot_nt(wst_s[...], xn_s[...])
    gates = narrow[:2 * SUBLANES, :] + jnp.concatenate([sb_ref[...]] * (tm // LANES), axis=1)
    fill(2)
    s1 = jnp.concatenate([jnp.zeros((GA_LANE, tm), F32), narrow[2 * SUBLANES:, :],
                          jnp.zeros((LANES - GA_LANE - G_RANK, tm), F32)], axis=0).T
    logi = gates[:SUBLANES, :]
    logf = _log_sigmoid(gates[SUBLANES:, :])
    if meta:
        valid_t = lax.broadcasted_iota(jnp.int32, (1, tm), 1) >= (tm - N_META)
        logi = jnp.where(valid_t, logi, NEG_BIG)
        logf = jnp.where(valid_t, logf, 0.0)
    zero8 = jnp.zeros((SUBLANES, tm), F32)
    b4 = _dot(jnp.concatenate(_pieces(logf) + [zero8], axis=0).astype(BF16), tri_up)
    fill(2)
    b_all = b4[:SUBLANES] + b4[SUBLANES:2 * SUBLANES] + b4[2 * SUBLANES:3 * SUBLANES]
    c_all = logi - b_all
    m_run = m_ref[...]
    inter_parts, w_parts, a_chunk = [], [], []
    for c in range(nc):
        b_c = b_all[:, rows(c)]
        g_c = jnp.broadcast_to(b_c[:, RCHUNK - 1:RCHUNK], (SUBLANES, LANES))
        wlog = g_c + c_all[:, rows(c)]
        m_next = jnp.maximum(g_c + m_run, jnp.max(wlog, axis=1, keepdims=True))
        inter_parts.append(b_c + m_run)
        w_parts.append(jnp.exp(wlog - m_next))
        a_chunk.append(jnp.exp(g_c + m_run - m_next))
        m_run = m_next
    m_ref[...] = m_run
    stacked = jnp.concatenate(
        _pieces(b_all) + _pieces(jnp.concatenate(inter_parts, axis=1)) + _pieces(jnp.concatenate(w_parts, axis=1))
        + [jnp.zeros((LANES - 9 * SUBLANES, tm), F32)], axis=0)
    stacked_t = stacked.T.astype(BF16)
    fill(2)
    tiles = _dot(stacked_t, spread)
    fill(2)
    b_t, il_t, w_t = (tiles[:, i * QK:(i + 1) * QK] for i in range(3))

    finish_groups(1)
    for blk in range(2 * QK // DV):
        csl = slice(blk * DV, (blk + 1) * DV)
        conv = cb_ref[:, csl]
        for j in range(CONV_W):
            conv = conv + cw_ref[j:j + 1, csl] * qk_s[pl.ds(SUBLANES - (CONV_W - 1) + j, tm), csl]
        conv = conv * _sigmoid(conv)
        if blk < QK // DV:
            q_s[:, csl] = conv * (DK ** -0.5)
        else:
            ksl = slice(blk * DV - QK, (blk + 1) * DV - QK)
            k_s[:, ksl] = conv.astype(BF16)
            wk_s[:, ksl] = (w_t[:, ksl] * conv).astype(BF16)
        fill(1)
    tail_new = qk_s[tm:tm + SUBLANES, :]
    qk_s[0:SUBLANES, :] = tail_new
    tail_ref[...] = tail_new

    za = _dot(s1.astype(BF16), a2_ref[...]) + a2b_ref[...]
    fill(1)
    loga = _log_sigmoid(za) / G_TAU
    if meta:
        loga = jnp.where(valid, loga, 0.0)
    hi = loga.astype(BF16)
    r1 = loga - hi.astype(F32)
    mid = r1.astype(BF16)
    lo = (r1 - mid.astype(F32)).astype(BF16)
    bc = jnp.concatenate([_dot(tri_c, hi[rows(c), :]) + _dot(tri_c, mid[rows(c), :]) + _dot(tri_c, lo[rows(c), :])
                          for c in range(nc)], axis=0)
    fill(2)
    e_blk = jnp.exp(bc)
    tot_a = [bc[c * RCHUNK + CHUNK - 1:c * RCHUNK + CHUNK, :] for c in range(nc)]
    tot_b = [bc[(c + 1) * RCHUNK - 1:(c + 1) * RCHUNK, :] for c in range(nc)]
    e_a = [jnp.exp(t) for t in tot_a]
    e_b = [jnp.exp(t) for t in tot_b]
    rest = jnp.concatenate([t - bc[half(c, s), :] for c in range(nc) for s, t in ((0, tot_a[c]), (1, tot_b[c]))],
                           axis=0)
    e_rest = jnp.exp(rest)
    ones_row = jnp.ones((1, QK), F32)
    to_chunk = jnp.concatenate([jnp.broadcast_to(f, (CHUNK, QK)) for c in range(nc) for f in (ones_row, e_a[c])],
                               axis=0)
    from_blk = jnp.concatenate([jnp.broadcast_to(f, (CHUNK, QK)) for c in range(nc) for f in (e_b[c], ones_row)],
                               axis=0)
    finish_groups(3)
    gq = gqk_s[:, :QK] * (DK ** -0.5)
    gk = gqk_s[:, QK:]
    gq64_s[...] = (gq * e_blk).astype(BF16)
    gq128_s[...] = (gq * e_blk * to_chunk).astype(BF16)
    gki_s[...] = (gk * jnp.exp(-bc)).astype(BF16)
    gke64_s[...] = (gk * e_rest).astype(BF16)
    gke128_s[...] = (gk * e_rest * from_blk).astype(BF16)
    fill(2)
    e_rows = [(e_a[c] * e_b[c])[:, h * DK:(h + 1) * DK] for (c, h) in pairs]
    e_cols = jnp.concatenate(e_rows + [jnp.zeros((LANES - len(pairs), DK), F32)], axis=0).T

    finish_groups(2)
    dmat, rmax, sim = {}, {}, {}
    for (c, h) in pairs:
        d = b_t[rows(c), h * LANES:(h + 1) * LANES] + c_all[h:h + 1, rows(c)]
        d = jnp.where(causal, d, -jnp.inf)
        dmat[c, h] = d
        rmax[c, h] = jnp.max(d, axis=-1, keepdims=True)
    fill(1)
    for (c, h) in pairs:
        sim[c, h] = _dot_nt(q_s[rows(c), h * DK:(h + 1) * DK].astype(BF16), k_s[rows(c), h * DK:(h + 1) * DK])
    fill(1)
    lhs, emr, upd = {}, {}, {}
    for (c, h) in pairs:
        il = il_t[rows(c), h * LANES:(h + 1) * LANES]
        m_row = jnp.maximum(il, rmax[c, h])
        wts = jnp.exp(dmat[c, h] - m_row) * sim[c, h]
        aq = jnp.exp(il - m_row) * q_s[rows(c), h * DK:(h + 1) * DK]
        lhs[c, h] = jnp.concatenate([aq.astype(BF16), wts.astype(BF16)], axis=1)
        emr[c, h] = jnp.exp(-m_row)
    fill(1)
    for (c, h) in pairs:
        upd[c, h] = _dot_tn(wk_s[rows(c), h * DK:(h + 1) * DK], vo_s[rows(c), h * CN:(h + 1) * CN])
    fill(1)
    res = {}
    for h in range(HEADS):
        state = cn_ref[h]
        for c in range(nc):
            rhs = jnp.concatenate([state.astype(BF16), vo_s[rows(c), h * CN:(h + 1) * CN]], axis=0)
            res[c, h] = _dot(lhs[c, h], rhs)
            a = a_chunk[c][h:h + 1, :]
            state = jnp.concatenate([a] * (CN // LANES), axis=1) * state + upd[c, h]
        cn_ref[h] = state
        fill(1)
    for (c, h) in pairs:
        inv = 1.0 / jnp.maximum(jnp.abs(res[c, h][:, DV:]), emr[c, h])
        hm_s[rows(c), h * DV:h * DV + LANES] = res[c, h][:, :LANES] * inv
        hm_s[rows(c), h * DV + LANES:(h + 1) * DV] = res[c, h][:, LANES:DV] * inv
    fill(1)

    finish_groups(4)
    att, gupd = {}, {}
    zblk = jnp.zeros((CHUNK, DK), BF16)
    for (c, h) in pairs:
        dsl = slice(h * DK, (h + 1) * DK)
        ra, rb = half(c, 0), half(c, 1)
        qp = jnp.concatenate([jnp.concatenate([gq64_s[ra, dsl], zblk], axis=1),
                              jnp.concatenate([zblk, gq64_s[rb, dsl]], axis=1)], axis=0)
        kp = jnp.concatenate([jnp.concatenate([gki_s[ra, dsl], gke64_s[ra, dsl]], axis=1),
                              jnp.concatenate([zblk, gki_s[rb, dsl]], axis=1)], axis=0)
        att[c, h] = jnp.where(causal, _dot_nt(qp, kp), 0.0).astype(BF16)
    fill(1)
    for (c, h) in pairs:
        gupd[c, h] = _dot_tn(gke128_s[rows(c), h * DK:(h + 1) * DK], gv_s[rows(c), h * DV:(h + 1) * DV])
    fill(1)
    for h in range(HEADS):
        sst = s_ref[h]
        for c in range(nc):
            lhs_g = jnp.concatenate([gq128_s[rows(c), h * DK:(h + 1) * DK], att[c, h]], axis=1)
            rhs_g = jnp.concatenate([sst.astype(BF16), gv_s[rows(c), h * DV:(h + 1) * DV]], axis=0)
            hg_s[rows(c), h * DV:(h + 1) * DV] = _dot(lhs_g, rhs_g)
            i = c * HEADS + h
            sst = e_cols[:, i:i + 1] * sst + gupd[c, h]
        s_ref[h] = sst
        fill(1)

    if meta:
        h1_ref[...] = x
        return

    def head_norm(src, gain_ref):
        outs = []
        for h in range(HEADS):
            t = src[:, h * DV:(h + 1) * DV]
            outs.append(t * lax.rsqrt(jnp.mean(t * t, axis=-1, keepdims=True) + EPS))
        return jnp.concatenate(outs, axis=-1) * gain_ref[...]

    finish_groups(len(groups))
    y_m = head_norm(hm_s, mhg_ref) * ga_s[...]
    merged = gb_s[...] * _dot(y_m.astype(BF16), wbm_ref[...])
    y_g = head_norm(hg_s, ghg_ref) * gc_s[...]
    merged = merged + gd_s[...] * _dot(y_g.astype(BF16), wbg_ref[...])
    h1_ref[...] = x + _dot(merged.astype(BF16), wout_ref[...])


def _const_spec(shape):
    nd = len(shape)
    return pl.BlockSpec(shape, lambda i: (0,) * nd, pipeline_mode=pl.Buffered(1))


def _mixer_call(x2, consts, state, *, tm, meta):
    t = x2.shape[0]
    state_shapes = [jax.ShapeDtypeStruct(a.shape, F32) for a in state]
    row_spec = pl.BlockSpec((tm, D_MODEL), lambda i: (i, 0))
    state_specs = [pl.BlockSpec(a.shape, lambda i, nd=a.ndim: (0,) * nd) for a in state]
    return pl.pallas_call(
        functools.partial(_mixer_kernel, tm=tm, meta=meta),
        grid=(t // tm,),
        in_specs=[row_spec] + [_const_spec(c.shape) for c in consts] + state_specs,
        out_specs=[row_spec] + state_specs,
        out_shape=[jax.ShapeDtypeStruct((t, D_MODEL), F32)] + state_shapes,
        scratch_shapes=[
            pltpu.VMEM((tm, D_MODEL), BF16),
            pltpu.VMEM((tm + SUBLANES, 2 * QK), F32),
            pltpu.VMEM((tm, QK), F32),
            pltpu.VMEM((tm, QK), BF16),
            pltpu.VMEM((tm, QK), BF16),
            pltpu.VMEM((tm, HEADS * CN), BF16),
            pltpu.VMEM((tm, QK), BF16),
            pltpu.VMEM((tm, QK), BF16),
            pltpu.VMEM((tm, QK), BF16),
            pltpu.VMEM((tm, QK), BF16),
            pltpu.VMEM((tm, QK), BF16),
            pltpu.VMEM((tm, D_MODEL), BF16),
            pltpu.VMEM((tm, D_MODEL), F32),
            pltpu.VMEM((tm, D_MODEL), F32),
            pltpu.VMEM((tm, D_MODEL), F32),
            pltpu.VMEM((tm, D_MODEL), F32),
            pltpu.VMEM((tm, D_MODEL), F32),
            pltpu.VMEM((tm, D_MODEL), F32),
            pltpu.VMEM((tm, D_MODEL), F32),
            pltpu.VMEM((RCHUNK, RCHUNK), BF16),
            pltpu.VMEM((tm, tm), BF16),
            pltpu.VMEM((LANES, 3 * HEADS * LANES), BF16),
            pltpu.VMEM((2 * SUBLANES + G_RANK, D_MODEL), BF16),
        ],
        compiler_params=pltpu.CompilerParams(
            dimension_semantics=("arbitrary",), vmem_limit_bytes=VMEM_LIMIT),
        name="mixer_meta" if meta else "mixer",
    )(x2, *consts, *state)


def _ffn_kernel(h_ref, g2_ref, wg_ref, wu_ref, wd_ref, gf_ref, o_ref, *, fc, parts):
    rows = h_ref.shape[0] // parts
    blocks = [slice(p * rows, (p + 1) * rows) for p in range(parts)]
    acc, hn = [], []
    for blk in blocks:
        h = h_ref[blk, :]
        hn.append((h * lax.rsqrt(jnp.mean(h * h, axis=-1, keepdims=True) + EPS) * g2_ref[...]).astype(BF16))
        acc.append(h)
    for f in range(0, wg_ref.shape[1], fc):
        for p in range(parts):
            gate = _dot(hn[p], wg_ref[:, f:f + fc])
            up = _dot(hn[p], wu_ref[:, f:f + fc])
            acc[p] = acc[p] + _dot((gate * _sigmoid(gate) * up).astype(BF16), wd_ref[f:f + fc, :])
    for p, blk in enumerate(blocks):
        a = acc[p]
        o_ref[blk, :] = a * lax.rsqrt(jnp.mean(a * a, axis=-1, keepdims=True) + EPS) * gf_ref[...]


def _ffn_call(h1, g2, wg, wu, wd, gf, *, tm, fc, parts):
    t = h1.shape[0]
    row_spec = pl.BlockSpec((tm, D_MODEL), lambda i: (i, 0))
    return pl.pallas_call(
        functools.partial(_ffn_kernel, fc=fc, parts=parts),
        grid=(t // tm,),
        in_specs=[row_spec] + [_const_spec(a.shape) for a in (g2, wg, wu, wd, gf)],
        out_specs=row_spec,
        out_shape=jax.ShapeDtypeStruct((t, D_MODEL), F32),
        compiler_params=pltpu.CompilerParams(
            dimension_semantics=("arbitrary",), vmem_limit_bytes=VMEM_LIMIT),
        name="ffn",
    )(h1, g2, wg, wu, wd, gf)


def _split_w_in(w):
    d = w.shape[0]
    wide = [w[:, a:b].astype(BF16) for a, b in W_IN_RANGES]
    zeros = lambda n: jnp.zeros((d, n), w.dtype)
    small = jnp.concatenate([w[:, 2048:2052], zeros(GA_LANE - HEADS), w[:, 5128:5144], w[:, 2052:2056],
                             zeros(LANES - MF_LANE - HEADS)], axis=1)
    return wide + [small.astype(BF16)]


def kernel(x, meta_tokens, norm1_g, w_in, conv_w, conv_b, m_gate_b, g_a2, g_a2_b, m_head_g, g_head_g,
           w_branch_m, w_branch_g, w_out, norm2_g, w_ff_gate, w_ff_up, w_ff_down, final_g):
    bsz, seq, d = x.shape
    assert bsz == 1 and d == D_MODEL and norm1_g.shape[0] == 1 and seq % CHUNK == 0
    row = lambda a: a.reshape(1, -1).astype(F32)

    gate_bias = jnp.zeros((2 * SUBLANES, LANES), F32)
    gate_bias = gate_bias.at[:HEADS].set(m_gate_b[0, 0][:, None]).at[SUBLANES:SUBLANES + HEADS].set(m_gate_b[0, 1][:, None])
    a2 = jnp.zeros((LANES, QK), F32).at[GA_LANE:GA_LANE + G_RANK].set(g_a2[0]).astype(BF16)
    conv_w8 = jnp.zeros((SUBLANES, 2 * QK), F32).at[:CONV_W].set(conv_w[0])
    consts = (row(norm1_g[0]), *_split_w_in(w_in[0]), gate_bias, a2, row(g_a2_b[0]), conv_w8, row(conv_b[0]),
              row(m_head_g[0]), row(g_head_g[0]),
              w_branch_m[0].astype(BF16), w_branch_g[0].astype(BF16), w_out[0].astype(BF16))

    zero_state = (jnp.zeros((SUBLANES, 2 * QK), F32), jnp.zeros((HEADS, DK, CN), F32),
                  jnp.zeros((SUBLANES, LANES), F32), jnp.zeros((HEADS, DK, DV), F32))
    lead = jnp.concatenate([jnp.zeros((RCHUNK - N_META, d), F32), meta_tokens.astype(F32)], axis=0)
    unused = jnp.zeros((SUBLANES, LANES), BF16)
    lead_consts = tuple(unused if i in LEAD_UNUSED else c for i, c in enumerate(consts))
    _, *state = _mixer_call(lead, lead_consts, zero_state, tm=RCHUNK, meta=True)
    h1, *_ = _mixer_call(x[0], consts, tuple(state), tm=512, meta=False)
    out = _ffn_call(h1, row(norm2_g[0]), w_ff_gate[0].astype(BF16), w_ff_up[0].astype(BF16),
                    w_ff_down[0].astype(BF16), row(final_g), tm=1024, fc=256, parts=2)
    return out[None]
```

```python
import functools

import jax
import jax.numpy as jnp
from jax import lax
from jax.experimental import pallas as pl
from jax.experimental.pallas import tpu as pltpu

F32 = jnp.float32
BF16 = jnp.bfloat16

D_MODEL = 1024
N_META = 16
CHUNK = 64
RCHUNK = 128
EPS = 1e-6
HEADS = 4
DV = D_MODEL // HEADS
DK = DV // 2
QK = HEADS * DK
G_RANK = 16
G_TAU = 16.0
CONV_W = 4
LANES = 128
SUBLANES = 8
BF16_ROWS = 16
NEG_BIG = -1e30
VMEM_LIMIT = 60 * 1024 * 1024
CN = DV + LANES

W_IN_RANGES = ((0, 2048), (2056, 5128), (5144, 8216))
O_MQK, O_MV = (0, 0), (0, 1024)
O_MO, O_GQ, O_GK, O_GV = (1, 0), (1, 1024), (1, 1536), (1, 2048)
O_GR, O_GATE_M, O_GATE_G = (2, 0), (2, 1024), (2, 2048)
O_S1 = (3, 0)
GA_LANE = 8
MF_LANE = 24
LEAD_UNUSED = (3, 12, 13, 14)

NT_DIMS = (((1,), (1,)), ((), ()))
TN_DIMS = (((0,), (0,)), ((), ()))


def _dot(a, b):
    return jnp.dot(a, b, preferred_element_type=F32)


def _dot_nt(a, b):
    return lax.dot_general(a, b, NT_DIMS, preferred_element_type=F32)


def _dot_tn(a, b):
    return lax.dot_general(a, b, TN_DIMS, preferred_element_type=F32)


def _pieces(x):
    hi = x.astype(BF16).astype(F32)
    mid = (x - hi).astype(BF16).astype(F32)
    lo = (x - hi - mid).astype(BF16).astype(F32)
    return [hi, mid, lo]


def _log_sigmoid(x):
    return jnp.minimum(x, 0.0) - jnp.log(1.0 + jnp.exp(-jnp.abs(x)))


def _sigmoid(x):
    return 1.0 / (1.0 + jnp.exp(-x))


def _mixer_kernel(x_ref, g1_ref, wa_ref, wb_ref, wc_ref, ws_ref, sb_ref, a2_ref, a2b_ref, cw_ref, cb_ref,
                  mhg_ref, ghg_ref, wbm_ref, wbg_ref, wout_ref, tail0_ref, cn0_ref, m0_ref, s0_ref,
                  h1_ref, tail_ref, cn_ref, m_ref, s_ref,
                  xn_s, qk_s, q_s, k_s, wk_s, vo_s, gq64_s, gq128_s, gki_s, gke64_s, gke128_s, gv_s, hm_s, hg_s,
                  ga_s, gb_s, gc_s, gd_s, gqk_s, tric_s, triu_s, spread_s, wst_s, *, tm, meta):
    nc = tm // RCHUNK
    pairs = [(c, h) for c in range(nc) for h in range(HEADS)]

    @pl.when(pl.program_id(0) == 0)
    def _():
        qk_s[0:SUBLANES, :] = tail0_ref[...]
        cn_ref[...] = cn0_ref[...]
        m_ref[...] = m0_ref[...]
        s_ref[...] = s0_ref[...]
        for h in range(HEADS):
            vo_s[:, h * CN + DV:(h + 1) * CN] = jnp.ones((tm, LANES), BF16)
        ws_t = ws_ref[...].astype(F32).T
        wst_s[...] = jnp.concatenate([ws_t[:SUBLANES], ws_t[MF_LANE:MF_LANE + SUBLANES],
                                      ws_t[GA_LANE:GA_LANE + G_RANK]], axis=0).astype(BF16)
        row = lax.broadcasted_iota(jnp.int32, (tm, tm), 0)
        col = lax.broadcasted_iota(jnp.int32, (tm, tm), 1)
        tric_s[...] = jnp.where((col <= row) & (col >= (row & -CHUNK)), 1.0, 0.0)[:RCHUNK, :RCHUNK].astype(BF16)
        triu_s[...] = jnp.where((row <= col) & (row >= (col & -RCHUNK)), 1.0, 0.0).astype(BF16)
        sr = lax.broadcasted_iota(jnp.int32, (LANES, 3 * HEADS * LANES), 0)
        sc = lax.broadcasted_iota(jnp.int32, (LANES, 3 * HEADS * LANES), 1)
        blk = (sc >> 9) * (3 * SUBLANES)
        spread_s[...] = jnp.where(
            (sr >= blk) & (sr < blk + 3 * SUBLANES) & ((sr & (SUBLANES - 1)) == ((sc >> 7) & (HEADS - 1))),
            1.0, 0.0).astype(BF16)

    x = x_ref[...]
    xn = x * lax.rsqrt(jnp.mean(x * x, axis=-1, keepdims=True) + EPS) * g1_ref[...]
    if meta:
        valid = lax.broadcasted_iota(jnp.int32, (tm, 1), 0) >= (tm - N_META)
        xn = jnp.where(valid, xn, 0.0)
    xn_s[...] = xn.astype(BF16)

    w_refs = (wa_ref, wb_ref, wc_ref, ws_ref)

    def proj(seg, width):
        return _dot(xn_s[...], w_refs[seg[0]][:, seg[1]:seg[1] + width])

    def rows(c):
        return slice(c * RCHUNK, (c + 1) * RCHUNK)

    def half(c, second):
        lo = c * RCHUNK + (CHUNK if second else 0)
        return slice(lo, lo + CHUNK)

    crow = lax.broadcasted_iota(jnp.int32, (RCHUNK, RCHUNK), 0)
    ccol = lax.broadcasted_iota(jnp.int32, (RCHUNK, RCHUNK), 1)
    causal = ccol <= crow
    tri_c = tric_s[...]
    tri_up = triu_s[...]
    spread = spread_s[...]

    def st_qk(i, v):
        qk_s[SUBLANES:SUBLANES + tm, i * DV:(i + 1) * DV] = v

    def st_mv(i, v):
        vo_s[:, i * CN:i * CN + DV] = v.astype(BF16)

    def st_gqk(i, v):
        gqk_s[:, i * DV:(i + 1) * DV] = v

    def st_gv(i, v):
        gv_s[:, i * DV:(i + 1) * DV] = v.astype(BF16)

    def st_mo(i, v):
        ga_s[:, i * DV:(i + 1) * DV] = _sigmoid(v)

    def st_gm(i, v):
        gb_s[:, i * DV:(i + 1) * DV] = _sigmoid(v)

    def st_gr(i, v):
        gc_s[:, i * DV:(i + 1) * DV] = v * _sigmoid(v)

    def st_gg(i, v):
        gd_s[:, i * DV:(i + 1) * DV] = _sigmoid(v)

    groups = [(O_MQK, st_qk), (O_MV, st_mv), (O_GQ, st_gqk), (O_GV, st_gv)]
    if not meta:
        groups += [(O_MO, st_mo), (O_GATE_M, st_gm), (O_GR, st_gr), (O_GATE_G, st_gg)]
    queue = [(seg, i, store) for seg, store in groups for i in range(D_MODEL // DV)]
    emitted = [0]

    def fill(n):
        for _ in range(n):
            if emitted[0] < len(queue):
                seg, i, store = queue[emitted[0]]
                store(i, proj((seg[0], seg[1] + i * DV), DV))
                emitted[0] += 1

    def finish_groups(n):
        fill(n * (D_MODEL // DV) - emitted[0])

    narrow = _dot_nt(wst_s[...], xn_s[...])
    gates = narrow[:2 * SUBLANES, :] + jnp.concatenate([sb_ref[...]] * (tm // LANES), axis=1)
    fill(2)
    s1 = jnp.concatenate([jnp.zeros((GA_LANE, tm), F32), narrow[2 * SUBLANES:, :],
                          jnp.zeros((LANES - GA_LANE - G_RANK, tm), F32)], axis=0).T
    logi = gates[:SUBLANES, :]
    logf = _log_sigmoid(gates[SUBLANES:, :])
    if meta:
        valid_t = lax.broadcasted_iota(jnp.int32, (1, tm), 1) >= (tm - N_META)
        logi = jnp.where(valid_t, logi, NEG_BIG)
        logf = jnp.where(valid_t, logf, 0.0)
    zero8 = jnp.zeros((SUBLANES, tm), F32)
    b4 = _dot(jnp.concatenate(_pieces(logf) + [zero8], axis=0).astype(BF16), tri_up)
    fill(2)
    b_all = b4[:SUBLANES] + b4[SUBLANES:2 * SUBLANES] + b4[2 * SUBLANES:3 * SUBLANES]
    c_all = logi - b_all
    m_run = m_ref[...]
    inter_parts, w_parts, a_chunk = [], [], []
    for c in range(nc):
        b_c = b_all[:, rows(c)]
        g_c = jnp.broadcast_to(b_c[:, RCHUNK - 1:RCHUNK], (SUBLANES, LANES))
        wlog = g_c + c_all[:, rows(c)]
        m_next = jnp.maximum(g_c + m_run, jnp.max(wlog, axis=1, keepdims=True))
        inter_parts.append(b_c + m_run)
        w_parts.append(jnp.exp(wlog - m_next))
        a_chunk.append(jnp.exp(g_c + m_run - m_next))
        m_run = m_next
    m_ref[...] = m_run
    stacked = jnp.concatenate(
        _pieces(b_all) + _pieces(jnp.concatenate(inter_parts, axis=1)) + _pieces(jnp.concatenate(w_parts, axis=1))
        + [jnp.zeros((LANES - 9 * SUBLANES, tm), F32)], axis=0)
    stacked_t = stacked.T.astype(BF16)
    fill(2)
    tiles = _dot(stacked_t, spread)
    fill(2)
    b_t, il_t, w_t = (tiles[:, i * QK:(i + 1) * QK] for i in range(3))

    finish_groups(1)
    for blk in range(2 * QK // DV):
        csl = slice(blk * DV, (blk + 1) * DV)
        conv = cb_ref[:, csl]
        for j in range(CONV_W):
            conv = conv + cw_ref[j:j + 1, csl] * qk_s[pl.ds(SUBLANES - (CONV_W - 1) + j, tm), csl]
        conv = conv * _sigmoid(conv)
        if blk < QK // DV:
            q_s[:, csl] = conv * (DK ** -0.5)
        else:
            ksl = slice(blk * DV - QK, (blk + 1) * DV - QK)
            k_s[:, ksl] = conv.astype(BF16)
            wk_s[:, ksl] = (w_t[:, ksl] * conv).astype(BF16)
        fill(1)
    tail_new = qk_s[tm:tm + SUBLANES, :]
    qk_s[0:SUBLANES, :] = tail_new
    tail_ref[...] = tail_new

    za = _dot(s1.astype(BF16), a2_ref[...]) + a2b_ref[...]
    fill(1)
    loga = _log_sigmoid(za) / G_TAU
    if meta:
        loga = jnp.where(valid, loga, 0.0)
    hi = loga.astype(BF16)
    r1 = loga - hi.astype(F32)
    mid = r1.astype(BF16)
    lo = (r1 - mid.astype(F32)).astype(BF16)
    bc = jnp.concatenate([_dot(tri_c, hi[rows(c), :]) + _dot(tri_c, mid[rows(c), :]) + _dot(tri_c, lo[rows(c), :])
                          for c in range(nc)], axis=0)
    fill(2)
    e_blk = jnp.exp(bc)
    tot_a = [bc[c * RCHUNK + CHUNK - 1:c * RCHUNK + CHUNK, :] for c in range(nc)]
    tot_b = [bc[(c + 1) * RCHUNK - 1:(c + 1) * RCHUNK, :] for c in range(nc)]
    e_a = [jnp.exp(t) for t in tot_a]
    e_b = [jnp.exp(t) for t in tot_b]
    rest = jnp.concatenate([t - bc[half(c, s), :] for c in range(nc) for s, t in ((0, tot_a[c]), (1, tot_b[c]))],
                           axis=0)
    e_rest = jnp.exp(rest)
    ones_row = jnp.ones((1, QK), F32)
    to_chunk = jnp.concatenate([jnp.broadcast_to(f, (CHUNK, QK)) for c in range(nc) for f in (ones_row, e_a[c])],
                               axis=0)
    from_blk = jnp.concatenate([jnp.broadcast_to(f, (CHUNK, QK)) for c in range(nc) for f in (e_b[c], ones_row)],
                               axis=0)
    finish_groups(3)
    gq = gqk_s[:, :QK] * (DK ** -0.5)
    gk = gqk_s[:, QK:]
    gq64_s[...] = (gq * e_blk).astype(BF16)
    gq128_s[...] = (gq * e_blk * to_chunk).astype(BF16)
    gki_s[...] = (gk * jnp.exp(-bc)).astype(BF16)
    gke64_s[...] = (gk * e_rest).astype(BF16)
    gke128_s[...] = (gk * e_rest * from_blk).astype(BF16)
    fill(2)
    e_rows = [(e_a[c] * e_b[c])[:, h * DK:(h + 1) * DK] for (c, h) in pairs]
    e_cols = jnp.concatenate(e_rows + [jnp.zeros((LANES - len(pairs), DK), F32)], axis=0).T

    finish_groups(2)
    dmat, rmax, sim = {}, {}, {}
    for (c, h) in pairs:
        d = b_t[rows(c), h * LANES:(h + 1) * LANES] + c_all[h:h + 1, rows(c)]
        d = jnp.where(causal, d, -jnp.inf)
        dmat[c, h] = d
        rmax[c, h] = jnp.max(d, axis=-1, keepdims=True)
    fill(1)
    for (c, h) in pairs:
        sim[c, h] = _dot_nt(q_s[rows(c), h * DK:(h + 1) * DK].astype(BF16), k_s[rows(c), h * DK:(h + 1) * DK])
    fill(1)
    lhs, emr, upd = {}, {}, {}
    for (c, h) in pairs:
        il = il_t[rows(c), h * LANES:(h + 1) * LANES]
        m_row = jnp.maximum(il, rmax[c, h])
        wts = jnp.exp(dmat[c, h] - m_row) * sim[c, h]
        aq = jnp.exp(il - m_row) * q_s[rows(c), h * DK:(h + 1) * DK]
        lhs[c, h] = jnp.concatenate([aq.astype(BF16), wts.astype(BF16)], axis=1)
        emr[c, h] = jnp.exp(-m_row)
    fill(1)
    for (c, h) in pairs:
        upd[c, h] = _dot_tn(wk_s[rows(c), h * DK:(h + 1) * DK], vo_s[rows(c), h * CN:(h + 1) * CN])
    fill(1)
    res = {}
    for h in range(HEADS):
        state = cn_ref[h]
        for c in range(nc):
            rhs = jnp.concatenate([state.astype(BF16), vo_s[rows(c), h * CN:(h + 1) * CN]], axis=0)
            res[c, h] = _dot(lhs[c, h], rhs)
            a = a_chunk[c][h:h + 1, :]
            state = jnp.concatenate([a] * (CN // LANES), axis=1) * state + upd[c, h]
        cn_ref[h] = state
        fill(1)
    for (c, h) in pairs:
        inv = 1.0 / jnp.maximum(jnp.abs(res[c, h][:, DV:]), emr[c, h])
        hm_s[rows(c), h * DV:h * DV + LANES] = res[c, h][:, :LANES] * inv
        hm_s[rows(c), h * DV + LANES:(h + 1) * DV] = res[c, h][:, LANES:DV] * inv
    fill(1)

    finish_groups(4)
    att, gupd = {}, {}
    zblk = jnp.zeros((CHUNK, DK), BF16)
    for (c, h) in pairs:
        dsl = slice(h * DK, (h + 1) * DK)
        ra, rb = half(c, 0), half(c, 1)
        qp = jnp.concatenate([jnp.concatenate([gq64_s[ra, dsl], zblk], axis=1),
                              jnp.concatenate([zblk, gq64_s[rb, dsl]], axis=1)], axis=0)
        kp = jnp.concatenate([jnp.concatenate([gki_s[ra, dsl], gke64_s[ra, dsl]], axis=1),
                              jnp.concatenate([zblk, gki_s[rb, dsl]], axis=1)], axis=0)
        att[c, h] = jnp.where(causal, _dot_nt(qp, kp), 0.0).astype(BF16)
    fill(1)
    for (c, h) in pairs:
        gupd[c, h] = _dot_tn(gke128_s[rows(c), h * DK:(h + 1) * DK], gv_s[rows(c), h * DV:(h + 1) * DV])
    fill(1)
    for h in range(HEADS):
        sst = s_ref[h]
        for c in range(nc):
            lhs_g = jnp.concatenate([gq128_s[rows(c), h * DK:(h + 1) * DK], att[c, h]], axis=1)
            rhs_g = jnp.concatenate([sst.astype(BF16), gv_s[rows(c), h * DV:(h + 1) * DV]], axis=0)
            hg_s[rows(c), h * DV:(h + 1) * DV] = _dot(lhs_g, rhs_g)
            i = c * HEADS + h
            sst = e_cols[:, i:i + 1] * sst + gupd[c, h]
        s_ref[h] = sst
        fill(1)

    if meta:
        h1_ref[...] = x
        return

    def head_norm(src, gain_ref):
        outs = []
        for h in range(HEADS):
            t = src[:, h * DV:(h + 1) * DV]
            outs.append(t * lax.rsqrt(jnp.mean(t * t, axis=-1, keepdims=True) + EPS))
        return jnp.concatenate(outs, axis=-1) * gain_ref[...]

    finish_groups(len(groups))
    y_m = head_norm(hm_s, mhg_ref) * ga_s[...]
    merged = gb_s[...] * _dot(y_m.astype(BF16), wbm_ref[...])
    y_g = head_norm(hg_s, ghg_ref) * gc_s[...]
    merged = merged + gd_s[...] * _dot(y_g.astype(BF16), wbg_ref[...])
    h1_ref[...] = x + _dot(merged.astype(BF16), wout_ref[...])


def _const_spec(shape):
    nd = len(shape)
    return pl.BlockSpec(shape, lambda i: (0,) * nd, pipeline_mode=pl.Buffered(1))


def _mixer_call(x2, consts, state, *, tm, meta):
    t = x2.shape[0]
    state_shapes = [jax.ShapeDtypeStruct(a.shape, F32) for a in state]
    row_spec = pl.BlockSpec((tm, D_MODEL), lambda i: (i, 0))
    state_specs = [pl.BlockSpec(a.shape, lambda i, nd=a.ndim: (0,) * nd) for a in state]
    return pl.pallas_call(
        functools.partial(_mixer_kernel, tm=tm, meta=meta),
        grid=(t // tm,),
        in_specs=[row_spec] + [_const_spec(c.shape) for c in consts] + state_specs,
        out_specs=[row_spec] + state_specs,
        out_shape=[jax.ShapeDtypeStruct((t, D_MODEL), F32)] + state_shapes,
        scratch_shapes=[
            pltpu.VMEM((tm, D_MODEL), BF16),
            pltpu.VMEM((tm + SUBLANES, 2 * QK), F32),
            pltpu.VMEM((tm, QK), F32),
            pltpu.VMEM((tm, QK), BF16),
            pltpu.VMEM((tm, QK), BF16),
            pltpu.VMEM((tm, HEADS * CN), BF16),
            pltpu.VMEM((tm, QK), BF16),
            pltpu.VMEM((tm, QK), BF16),
            pltpu.VMEM((tm, QK), BF16),
            pltpu.VMEM((tm, QK), BF16),
            pltpu.VMEM((tm, QK), BF16),
            pltpu.VMEM((tm, D_MODEL), BF16),
            pltpu.VMEM((tm, D_MODEL), F32),
            pltpu.VMEM((tm, D_MODEL), F32),
            pltpu.VMEM((tm, D_MODEL), F32),
            pltpu.VMEM((tm, D_MODEL), F32),
            pltpu.VMEM((tm, D_MODEL), F32),
            pltpu.VMEM((tm, D_MODEL), F32),
            pltpu.VMEM((tm, D_MODEL), F32),
            pltpu.VMEM((RCHUNK, RCHUNK), BF16),
            pltpu.VMEM((tm, tm), BF16),
            pltpu.VMEM((LANES, 3 * HEADS * LANES), BF16),
            pltpu.VMEM((2 * SUBLANES + G_RANK, D_MODEL), BF16),
        ],
        compiler_params=pltpu.CompilerParams(
            dimension_semantics=("arbitrary",), vmem_limit_bytes=VMEM_LIMIT),
        name="mixer_meta" if meta else "mixer",
    )(x2, *consts, *state)


def _ffn_kernel(h_ref, g2_ref, wg_hbm, wu_hbm, wd_hbm, gf_ref, o_ref,
                wg_ref, wu_ref, wd_ref, sg, su, sd, sem, *, fc, parts):
    n_chunks = wg_ref.shape[1] // fc

    def chunk_copies(f, slot):
        cols = pl.ds(f * fc, fc)
        return (pltpu.make_async_copy(wg_hbm.at[:, cols], sg.at[slot], sem.at[slot, 0]),
                pltpu.make_async_copy(wu_hbm.at[:, cols], su.at[slot], sem.at[slot, 1]),
                pltpu.make_async_copy(wd_hbm.at[cols, :], sd.at[slot], sem.at[slot, 2]))

    @pl.when(pl.program_id(0) == 0)
    def _():
        for cp in chunk_copies(0, 0):
            cp.start()
        for f in range(n_chunks):
            slot = f % 2
            if f + 1 < n_chunks:
                for cp in chunk_copies(f + 1, 1 - slot):
                    cp.start()
            for cp in chunk_copies(f, slot):
                cp.wait()
            wg_ref[:, f * fc:(f + 1) * fc] = sg[slot].astype(BF16)
            wu_ref[:, f * fc:(f + 1) * fc] = su[slot].astype(BF16)
            wd_ref[f * fc:(f + 1) * fc, :] = sd[slot].astype(BF16)

    rows = h_ref.shape[0] // parts
    blocks = [slice(p * rows, (p + 1) * rows) for p in range(parts)]
    acc, hn = [], []
    for blk in blocks:
        h = h_ref[blk, :]
        hn.append((h * lax.rsqrt(jnp.mean(h * h, axis=-1, keepdims=True) + EPS) * g2_ref[...]).astype(BF16))
        acc.append(h)
    for f in range(0, wg_ref.shape[1], fc):
        for p in range(parts):
            gate = _dot(hn[p], wg_ref[:, f:f + fc])
            up = _dot(hn[p], wu_ref[:, f:f + fc])
            acc[p] = acc[p] + _dot((gate * _sigmoid(gate) * up).astype(BF16), wd_ref[f:f + fc, :])
    for p, blk in enumerate(blocks):
        a = acc[p]
        o_ref[blk, :] = a * lax.rsqrt(jnp.mean(a * a, axis=-1, keepdims=True) + EPS) * gf_ref[...]


def _ffn_call(h1, g2, wg, wu, wd, gf, *, tm, fc, parts):
    t = h1.shape[0]
    d_ff = wg.shape[1]
    row_spec = pl.BlockSpec((tm, D_MODEL), lambda i: (i, 0))
    hbm_spec = pl.BlockSpec(memory_space=pl.ANY)
    return pl.pallas_call(
        functools.partial(_ffn_kernel, fc=fc, parts=parts),
        grid=(t // tm,),
        in_specs=[row_spec, _const_spec(g2.shape), hbm_spec, hbm_spec, hbm_spec, _const_spec(gf.shape)],
        out_specs=row_spec,
        out_shape=jax.ShapeDtypeStruct((t, D_MODEL), F32),
        scratch_shapes=[
            pltpu.VMEM((D_MODEL, d_ff), BF16),
            pltpu.VMEM((D_MODEL, d_ff), BF16),
            pltpu.VMEM((d_ff, D_MODEL), BF16),
            pltpu.VMEM((2, D_MODEL, fc), F32),
            pltpu.VMEM((2, D_MODEL, fc), F32),
            pltpu.VMEM((2, fc, D_MODEL), F32),
            pltpu.SemaphoreType.DMA((2, 3)),
        ],
        compiler_params=pltpu.CompilerParams(
            dimension_semantics=("arbitrary",), vmem_limit_bytes=VMEM_LIMIT),
        name="ffn",
    )(h1, g2, wg, wu, wd, gf)


def _split_w_in(w):
    d = w.shape[0]
    wide = [w[:, a:b].astype(BF16) for a, b in W_IN_RANGES]
    zeros = lambda n: jnp.zeros((d, n), w.dtype)
    small = jnp.concatenate([w[:, 2048:2052], zeros(GA_LANE - HEADS), w[:, 5128:5144], w[:, 2052:2056],
                             zeros(LANES - MF_LANE - HEADS)], axis=1)
    return wide + [small.astype(BF16)]


def kernel(x, meta_tokens, norm1_g, w_in, conv_w, conv_b, m_gate_b, g_a2, g_a2_b, m_head_g, g_head_g,
           w_branch_m, w_branch_g, w_out, norm2_g, w_ff_gate, w_ff_up, w_ff_down, final_g):
    bsz, seq, d = x.shape
    assert bsz == 1 and d == D_MODEL and norm1_g.shape[0] == 1 and seq % CHUNK == 0
    row = lambda a: a.reshape(1, -1).astype(F32)

    gate_bias = jnp.zeros((2 * SUBLANES, LANES), F32)
    gate_bias = gate_bias.at[:HEADS].set(m_gate_b[0, 0][:, None]).at[SUBLANES:SUBLANES + HEADS].set(m_gate_b[0, 1][:, None])
    a2 = jnp.zeros((LANES, QK), F32).at[GA_LANE:GA_LANE + G_RANK].set(g_a2[0]).astype(BF16)
    conv_w8 = jnp.zeros((SUBLANES, 2 * QK), F32).at[:CONV_W].set(conv_w[0])
    consts = (row(norm1_g[0]), *_split_w_in(w_in[0]), gate_bias, a2, row(g_a2_b[0]), conv_w8, row(conv_b[0]),
              row(m_head_g[0]), row(g_head_g[0]),
              w_branch_m[0].astype(BF16), w_branch_g[0].astype(BF16), w_out[0].astype(BF16))

    zero_state = (jnp.zeros((SUBLANES, 2 * QK), F32), jnp.zeros((HEADS, DK, CN), F32),
                  jnp.zeros((SUBLANES, LANES), F32), jnp.zeros((HEADS, DK, DV), F32))
    lead = jnp.concatenate([jnp.zeros((RCHUNK - N_META, d), F32), meta_tokens.astype(F32)], axis=0)
    unused = jnp.zeros((SUBLANES, LANES), BF16)
    lead_consts = tuple(unused if i in LEAD_UNUSED else c for i, c in enumerate(consts))
    _, *state = _mixer_call(lead, lead_consts, zero_state, tm=RCHUNK, meta=True)
    h1, *_ = _mixer_call(x[0], consts, tuple(state), tm=512, meta=False)
    out = _ffn_call(h1, row(norm2_g[0]), w_ff_gate[0].astype(F32), w_ff_up[0].astype(F32),
                    w_ff_down[0].astype(F32), row(final_g), tm=1024, fc=256, parts=2)
    return out[None]
```

```python
import functools

import jax
import jax.numpy as jnp
from jax import lax
from jax.experimental import pallas as pl
from jax.experimental.pallas import tpu as pltpu

F32 = jnp.float32
BF16 = jnp.bfloat16

D_MODEL = 1024
N_META = 16
CHUNK = 64
RCHUNK = 128
EPS = 1e-6
HEADS = 4
DV = D_MODEL // HEADS
DK = DV // 2
QK = HEADS * DK
G_RANK = 16
G_TAU = 16.0
CONV_W = 4
LANES = 128
SUBLANES = 8
BF16_ROWS = 16
NEG_BIG = -1e30
VMEM_LIMIT = 60 * 1024 * 1024
CN = DV + LANES

W_IN_RANGES = ((0, 2048), (2056, 5128), (5144, 8216))
N_PACK = sum(b - a for a, b in W_IN_RANGES)
O_MQK, O_MV, O_MO = 0, 1024, 2048
O_GQ, O_GK, O_GV = 3072, 3584, 4096
O_GR, O_GATE_M, O_GATE_G = 5120, 6144, 7168
MI_COL, MF_COL, GA_COL = 2048, 2052, 5128
GA_LANE = 8
PIECE = 256
LEAD_UNUSED = (10, 11, 12)

NT_DIMS = (((1,), (1,)), ((), ()))
TN_DIMS = (((0,), (0,)), ((), ()))


def _dot(a, b):
    return jnp.dot(a, b, preferred_element_type=F32)


def _dot_nt(a, b):
    return lax.dot_general(a, b, NT_DIMS, preferred_element_type=F32)


def _dot_tn(a, b):
    return lax.dot_general(a, b, TN_DIMS, preferred_element_type=F32)


def _pieces(x):
    hi = x.astype(BF16).astype(F32)
    mid = (x - hi).astype(BF16).astype(F32)
    lo = (x - hi - mid).astype(BF16).astype(F32)
    return [hi, mid, lo]


def _log_sigmoid(x):
    return jnp.minimum(x, 0.0) - jnp.log(1.0 + jnp.exp(-jnp.abs(x)))


def _sigmoid(x):
    return 1.0 / (1.0 + jnp.exp(-x))


def _mixer_kernel(x_ref, g1_ref, w_ref, wst_ref, sb_ref, a2_ref, a2b_ref, cw_ref, cb_ref,
                  mhg_ref, ghg_ref, wbm_ref, wbg_ref, wout_ref, tail0_ref, cn0_ref, m0_ref, s0_ref,
                  h1_ref, tail_ref, cn_ref, m_ref, s_ref,
                  xn_s, qk_s, q_s, k_s, wk_s, vo_s, gq64_s, gq128_s, gki_s, gke64_s, gke128_s, gv_s, hm_s, hg_s,
                  ga_s, gb_s, gc_s, gd_s, gqk_s, tric_s, triu_s, spread_s, *, tm, meta):
    nc = tm // RCHUNK
    pairs = [(c, h) for c in range(nc) for h in range(HEADS)]

    @pl.when(pl.program_id(0) == 0)
    def _():
        qk_s[0:SUBLANES, :] = tail0_ref[...]
        cn_ref[...] = cn0_ref[...]
        m_ref[...] = m0_ref[...]
        s_ref[...] = s0_ref[...]
        for h in range(HEADS):
            vo_s[:, h * CN + DV:(h + 1) * CN] = jnp.ones((tm, LANES), BF16)
        row = lax.broadcasted_iota(jnp.int32, (tm, tm), 0)
        col = lax.broadcasted_iota(jnp.int32, (tm, tm), 1)
        tric_s[...] = jnp.where((col <= row) & (col >= (row & -CHUNK)), 1.0, 0.0)[:RCHUNK, :RCHUNK].astype(BF16)
        triu_s[...] = jnp.where((row <= col) & (row >= (col & -RCHUNK)), 1.0, 0.0).astype(BF16)
        sr = lax.broadcasted_iota(jnp.int32, (LANES, 3 * HEADS * LANES), 0)
        sc = lax.broadcasted_iota(jnp.int32, (LANES, 3 * HEADS * LANES), 1)
        blk = (sc >> 9) * (3 * SUBLANES)
        spread_s[...] = jnp.where(
            (sr >= blk) & (sr < blk + 3 * SUBLANES) & ((sr & (SUBLANES - 1)) == ((sc >> 7) & (HEADS - 1))),
            1.0, 0.0).astype(BF16)

    x = x_ref[...]
    xn = x * lax.rsqrt(jnp.mean(x * x, axis=-1, keepdims=True) + EPS) * g1_ref[...]
    if meta:
        valid = lax.broadcasted_iota(jnp.int32, (tm, 1), 0) >= (tm - N_META)
        xn = jnp.where(valid, xn, 0.0)
    xn_s[...] = xn.astype(BF16)

    def proj(off, width):
        return _dot(xn_s[...], w_ref[:, off:off + width])

    def rows(c):
        return slice(c * RCHUNK, (c + 1) * RCHUNK)

    def half(c, second):
        lo = c * RCHUNK + (CHUNK if second else 0)
        return slice(lo, lo + CHUNK)

    crow = lax.broadcasted_iota(jnp.int32, (RCHUNK, RCHUNK), 0)
    ccol = lax.broadcasted_iota(jnp.int32, (RCHUNK, RCHUNK), 1)
    causal = ccol <= crow
    tri_c = tric_s[...]
    tri_up = triu_s[...]
    spread = spread_s[...]

    def st_qk(i, v):
        qk_s[SUBLANES:SUBLANES + tm, i * DV:(i + 1) * DV] = v

    def st_mv(i, v):
        vo_s[:, i * CN:i * CN + DV] = v.astype(BF16)

    def st_gqk(i, v):
        gqk_s[:, i * DV:(i + 1) * DV] = v

    def st_gv(i, v):
        gv_s[:, i * DV:(i + 1) * DV] = v.astype(BF16)

    def st_mo(i, v):
        ga_s[:, i * DV:(i + 1) * DV] = _sigmoid(v)

    def st_gm(i, v):
        gb_s[:, i * DV:(i + 1) * DV] = _sigmoid(v)

    def st_gr(i, v):
        gc_s[:, i * DV:(i + 1) * DV] = v * _sigmoid(v)

    def st_gg(i, v):
        gd_s[:, i * DV:(i + 1) * DV] = _sigmoid(v)

    groups = [(O_MQK, st_qk), (O_MV, st_mv), (O_GQ, st_gqk), (O_GV, st_gv)]
    if not meta:
        groups += [(O_MO, st_mo), (O_GATE_M, st_gm), (O_GR, st_gr), (O_GATE_G, st_gg)]
    queue = [(seg, i, store) for seg, store in groups for i in range(D_MODEL // DV)]
    emitted = [0]

    def fill(n):
        for _ in range(n):
            if emitted[0] < len(queue):
                seg, i, store = queue[emitted[0]]
                store(i, proj(seg + i * DV, DV))
                emitted[0] += 1

    def finish_groups(n):
        fill(n * (D_MODEL // DV) - emitted[0])

    narrow = _dot_nt(wst_ref[...], xn_s[...])
    gates = narrow[:2 * SUBLANES, :] + jnp.concatenate([sb_ref[...]] * (tm // LANES), axis=1)
    fill(2)
    s1 = jnp.concatenate([jnp.zeros((GA_LANE, tm), F32), narrow[2 * SUBLANES:, :],
                          jnp.zeros((LANES - GA_LANE - G_RANK, tm), F32)], axis=0).T
    logi = gates[:SUBLANES, :]
    logf = _log_sigmoid(gates[SUBLANES:, :])
    if meta:
        valid_t = lax.broadcasted_iota(jnp.int32, (1, tm), 1) >= (tm - N_META)
        logi = jnp.where(valid_t, logi, NEG_BIG)
        logf = jnp.where(valid_t, logf, 0.0)
    zero8 = jnp.zeros((SUBLANES, tm), F32)
    b4 = _dot(jnp.concatenate(_pieces(logf) + [zero8], axis=0).astype(BF16), tri_up)
    fill(2)
    b_all = b4[:SUBLANES] + b4[SUBLANES:2 * SUBLANES] + b4[2 * SUBLANES:3 * SUBLANES]
    c_all = logi - b_all
    m_run = m_ref[...]
    inter_parts, w_parts, a_chunk = [], [], []
    for c in range(nc):
        b_c = b_all[:, rows(c)]
        g_c = jnp.broadcast_to(b_c[:, RCHUNK - 1:RCHUNK], (SUBLANES, LANES))
        wlog = g_c + c_all[:, rows(c)]
        m_next = jnp.maximum(g_c + m_run, jnp.max(wlog, axis=1, keepdims=True))
        inter_parts.append(b_c + m_run)
        w_parts.append(jnp.exp(wlog - m_next))
        a_chunk.append(jnp.exp(g_c + m_run - m_next))
        m_run = m_next
    m_ref[...] = m_run
    stacked = jnp.concatenate(
        _pieces(b_all) + _pieces(jnp.concatenate(inter_parts, axis=1)) + _pieces(jnp.concatenate(w_parts, axis=1))
        + [jnp.zeros((LANES - 9 * SUBLANES, tm), F32)], axis=0)
    stacked_t = stacked.T.astype(BF16)
    fill(2)
    tiles = _dot(stacked_t, spread)
    fill(2)
    b_t, il_t, w_t = (tiles[:, i * QK:(i + 1) * QK] for i in range(3))

    finish_groups(1)
    for blk in range(2 * QK // DV):
        csl = slice(blk * DV, (blk + 1) * DV)
        conv = cb_ref[:, csl]
        for j in range(CONV_W):
            conv = conv + cw_ref[j:j + 1, csl] * qk_s[pl.ds(SUBLANES - (CONV_W - 1) + j, tm), csl]
        conv = conv * _sigmoid(conv)
        if blk < QK // DV:
            q_s[:, csl] = conv * (DK ** -0.5)
        else:
            ksl = slice(blk * DV - QK, (blk + 1) * DV - QK)
            k_s[:, ksl] = conv.astype(BF16)
            wk_s[:, ksl] = (w_t[:, ksl] * conv).astype(BF16)
        fill(1)
    tail_new = qk_s[tm:tm + SUBLANES, :]
    qk_s[0:SUBLANES, :] = tail_new
    tail_ref[...] = tail_new

    za = _dot(s1.astype(BF16), a2_ref[...]) + a2b_ref[...]
    fill(1)
    loga = _log_sigmoid(za) / G_TAU
    if meta:
        loga = jnp.where(valid, loga, 0.0)
    hi = loga.astype(BF16)
    r1 = loga - hi.astype(F32)
    mid = r1.astype(BF16)
    lo = (r1 - mid.astype(F32)).astype(BF16)
    bc = jnp.concatenate([_dot(tri_c, hi[rows(c), :]) + _dot(tri_c, mid[rows(c), :]) + _dot(tri_c, lo[rows(c), :])
                          for c in range(nc)], axis=0)
    fill(2)
    e_blk = jnp.exp(bc)
    tot_a = [bc[c * RCHUNK + CHUNK - 1:c * RCHUNK + CHUNK, :] for c in range(nc)]
    tot_b = [bc[(c + 1) * RCHUNK - 1:(c + 1) * RCHUNK, :] for c in range(nc)]
    e_a = [jnp.exp(t) for t in tot_a]
    e_b = [jnp.exp(t) for t in tot_b]
    rest = jnp.concatenate([t - bc[half(c, s), :] for c in range(nc) for s, t in ((0, tot_a[c]), (1, tot_b[c]))],
                           axis=0)
    e_rest = jnp.exp(rest)
    ones_row = jnp.ones((1, QK), F32)
    to_chunk = jnp.concatenate([jnp.broadcast_to(f, (CHUNK, QK)) for c in range(nc) for f in (ones_row, e_a[c])],
                               axis=0)
    from_blk = jnp.concatenate([jnp.broadcast_to(f, (CHUNK, QK)) for c in range(nc) for f in (e_b[c], ones_row)],
                               axis=0)
    finish_groups(3)
    gq = gqk_s[:, :QK] * (DK ** -0.5)
    gk = gqk_s[:, QK:]
    gq64_s[...] = (gq * e_blk).astype(BF16)
    gq128_s[...] = (gq * e_blk * to_chunk).astype(BF16)
    gki_s[...] = (gk * jnp.exp(-bc)).astype(BF16)
    gke64_s[...] = (gk * e_rest).astype(BF16)
    gke128_s[...] = (gk * e_rest * from_blk).astype(BF16)
    fill(2)
    e_rows = [(e_a[c] * e_b[c])[:, h * DK:(h + 1) * DK] for (c, h) in pairs]
    e_cols = jnp.concatenate(e_rows + [jnp.zeros((LANES - len(pairs), DK), F32)], axis=0).T

    finish_groups(2)
    dmat, rmax, sim = {}, {}, {}
    for (c, h) in pairs:
        d = b_t[rows(c), h * LANES:(h + 1) * LANES] + c_all[h:h + 1, rows(c)]
        d = jnp.where(causal, d, -jnp.inf)
        dmat[c, h] = d
        rmax[c, h] = jnp.max(d, axis=-1, keepdims=True)
    fill(1)
    for (c, h) in pairs:
        sim[c, h] = _dot_nt(q_s[rows(c), h * DK:(h + 1) * DK].astype(BF16), k_s[rows(c), h * DK:(h + 1) * DK])
    fill(1)
    lhs, emr, upd = {}, {}, {}
    for (c, h) in pairs:
        il = il_t[rows(c), h * LANES:(h + 1) * LANES]
        m_row = jnp.maximum(il, rmax[c, h])
        wts = jnp.exp(dmat[c, h] - m_row) * sim[c, h]
        aq = jnp.exp(il - m_row) * q_s[rows(c), h * DK:(h + 1) * DK]
        lhs[c, h] = jnp.concatenate([aq.astype(BF16), wts.astype(BF16)], axis=1)
        emr[c, h] = jnp.exp(-m_row)
    fill(1)
    for (c, h) in pairs:
        upd[c, h] = _dot_tn(wk_s[rows(c), h * DK:(h + 1) * DK], vo_s[rows(c), h * CN:(h + 1) * CN])
    fill(1)
    res = {}
    for h in range(HEADS):
        state = cn_ref[h]
        for c in range(nc):
            rhs = jnp.concatenate([state.astype(BF16), vo_s[rows(c), h * CN:(h + 1) * CN]], axis=0)
            res[c, h] = _dot(lhs[c, h], rhs)
            a = a_chunk[c][h:h + 1, :]
            state = jnp.concatenate([a] * (CN // LANES), axis=1) * state + upd[c, h]
        cn_ref[h] = state
        fill(1)
    for (c, h) in pairs:
        inv = 1.0 / jnp.maximum(jnp.abs(res[c, h][:, DV:]), emr[c, h])
        hm_s[rows(c), h * DV:h * DV + LANES] = res[c, h][:, :LANES] * inv
        hm_s[rows(c), h * DV + LANES:(h + 1) * DV] = res[c, h][:, LANES:DV] * inv
    fill(1)

    finish_groups(4)
    att, gupd = {}, {}
    zblk = jnp.zeros((CHUNK, DK), BF16)
    for (c, h) in pairs:
        dsl = slice(h * DK, (h + 1) * DK)
        ra, rb = half(c, 0), half(c, 1)
        qp = jnp.concatenate([jnp.concatenate([gq64_s[ra, dsl], zblk], axis=1),
                              jnp.concatenate([zblk, gq64_s[rb, dsl]], axis=1)], axis=0)
        kp = jnp.concatenate([jnp.concatenate([gki_s[ra, dsl], gke64_s[ra, dsl]], axis=1),
                              jnp.concatenate([zblk, gki_s[rb, dsl]], axis=1)], axis=0)
        att[c, h] = jnp.where(causal, _dot_nt(qp, kp), 0.0).astype(BF16)
    fill(1)
    for (c, h) in pairs:
        gupd[c, h] = _dot_tn(gke128_s[rows(c), h * DK:(h + 1) * DK], gv_s[rows(c), h * DV:(h + 1) * DV])
    fill(1)
    for h in range(HEADS):
        sst = s_ref[h]
        for c in range(nc):
            lhs_g = jnp.concatenate([gq128_s[rows(c), h * DK:(h + 1) * DK], att[c, h]], axis=1)
            rhs_g = jnp.concatenate([sst.astype(BF16), gv_s[rows(c), h * DV:(h + 1) * DV]], axis=0)
            hg_s[rows(c), h * DV:(h + 1) * DV] = _dot(lhs_g, rhs_g)
            i = c * HEADS + h
            sst = e_cols[:, i:i + 1] * sst + gupd[c, h]
        s_ref[h] = sst
        fill(1)

    if meta:
        h1_ref[...] = x
        return

    def head_norm(src, gain_ref):
        outs = []
        for h in range(HEADS):
            t = src[:, h * DV:(h + 1) * DV]
            outs.append(t * lax.rsqrt(jnp.mean(t * t, axis=-1, keepdims=True) + EPS))
        return jnp.concatenate(outs, axis=-1) * gain_ref[...]

    finish_groups(len(groups))
    y_m = head_norm(hm_s, mhg_ref) * ga_s[...]
    merged = gb_s[...] * _dot(y_m.astype(BF16), wbm_ref[...])
    y_g = head_norm(hg_s, ghg_ref) * gc_s[...]
    merged = merged + gd_s[...] * _dot(y_g.astype(BF16), wbg_ref[...])
    h1_ref[...] = x + _dot(merged.astype(BF16), wout_ref[...])


def _const_spec(shape):
    nd = len(shape)
    return pl.BlockSpec(shape, lambda i: (0,) * nd, pipeline_mode=pl.Buffered(1))


def _mixer_call(x2, consts, state, *, tm, meta):
    t = x2.shape[0]
    state_shapes = [jax.ShapeDtypeStruct(a.shape, F32) for a in state]
    row_spec = pl.BlockSpec((tm, D_MODEL), lambda i: (i, 0))
    state_specs = [pl.BlockSpec(a.shape, lambda i, nd=a.ndim: (0,) * nd) for a in state]
    return pl.pallas_call(
        functools.partial(_mixer_kernel, tm=tm, meta=meta),
        grid=(t // tm,),
        in_specs=[row_spec] + [_const_spec(c.shape) for c in consts] + state_specs,
        out_specs=[row_spec] + state_specs,
        out_shape=[jax.ShapeDtypeStruct((t, D_MODEL), F32)] + state_shapes,
        scratch_shapes=[
            pltpu.VMEM((tm, D_MODEL), BF16),
            pltpu.VMEM((tm + SUBLANES, 2 * QK), F32),
            pltpu.VMEM((tm, QK), F32),
            pltpu.VMEM((tm, QK), BF16),
            pltpu.VMEM((tm, QK), BF16),
            pltpu.VMEM((tm, HEADS * CN), BF16),
            pltpu.VMEM((tm, QK), BF16),
            pltpu.VMEM((tm, QK), BF16),
            pltpu.VMEM((tm, QK), BF16),
            pltpu.VMEM((tm, QK), BF16),
            pltpu.VMEM((tm, QK), BF16),
            pltpu.VMEM((tm, D_MODEL), BF16),
            pltpu.VMEM((tm, D_MODEL), F32),
            pltpu.VMEM((tm, D_MODEL), F32),
            pltpu.VMEM((tm, D_MODEL), F32),
            pltpu.VMEM((tm, D_MODEL), F32),
            pltpu.VMEM((tm, D_MODEL), F32),
            pltpu.VMEM((tm, D_MODEL), F32),
            pltpu.VMEM((tm, D_MODEL), F32),
            pltpu.VMEM((RCHUNK, RCHUNK), BF16),
            pltpu.VMEM((tm, tm), BF16),
            pltpu.VMEM((LANES, 3 * HEADS * LANES), BF16),
        ],
        compiler_params=pltpu.CompilerParams(
            dimension_semantics=("arbitrary",), vmem_limit_bytes=VMEM_LIMIT),
        name="mixer_meta" if meta else "mixer",
    )(x2, *consts, *state)


def _ffn_kernel(h_ref, g2_ref, wg_hbm, wu_hbm, wd_hbm, gf_ref, o_ref,
                wg_ref, wu_ref, wd_ref, sg, su, sd, sem, *, fc, parts):
    n_chunks = wg_ref.shape[1] // fc

    def chunk_copies(f, slot):
        cols = pl.ds(f * fc, fc)
        return (pltpu.make_async_copy(wg_hbm.at[:, cols], sg.at[slot], sem.at[slot, 0]),
                pltpu.make_async_copy(wu_hbm.at[:, cols], su.at[slot], sem.at[slot, 1]),
                pltpu.make_async_copy(wd_hbm.at[cols, :], sd.at[slot], sem.at[slot, 2]))

    @pl.when(pl.program_id(0) == 0)
    def _():
        for cp in chunk_copies(0, 0):
            cp.start()
        for f in range(n_chunks):
            slot = f % 2
            if f + 1 < n_chunks:
                for cp in chunk_copies(f + 1, 1 - slot):
                    cp.start()
            for cp in chunk_copies(f, slot):
                cp.wait()
            wg_ref[:, f * fc:(f + 1) * fc] = sg[slot].astype(BF16)
            wu_ref[:, f * fc:(f + 1) * fc] = su[slot].astype(BF16)
            wd_ref[f * fc:(f + 1) * fc, :] = sd[slot].astype(BF16)

    rows = h_ref.shape[0] // parts
    blocks = [slice(p * rows, (p + 1) * rows) for p in range(parts)]
    acc, hn = [], []
    for blk in blocks:
        h = h_ref[blk, :]
        hn.append((h * lax.rsqrt(jnp.mean(h * h, axis=-1, keepdims=True) + EPS) * g2_ref[...]).astype(BF16))
        acc.append(h)
    for f in range(0, wg_ref.shape[1], fc):
        for p in range(parts):
            gate = _dot(hn[p], wg_ref[:, f:f + fc])
            up = _dot(hn[p], wu_ref[:, f:f + fc])
            acc[p] = acc[p] + _dot((gate * _sigmoid(gate) * up).astype(BF16), wd_ref[f:f + fc, :])
    for p, blk in enumerate(blocks):
        a = acc[p]
        o_ref[blk, :] = a * lax.rsqrt(jnp.mean(a * a, axis=-1, keepdims=True) + EPS) * gf_ref[...]


def _ffn_call(h1, g2, wg, wu, wd, gf, *, tm, fc, parts):
    t = h1.shape[0]
    d_ff = wg.shape[1]
    row_spec = pl.BlockSpec((tm, D_MODEL), lambda i: (i, 0))
    hbm_spec = pl.BlockSpec(memory_space=pl.ANY)
    return pl.pallas_call(
        functools.partial(_ffn_kernel, fc=fc, parts=parts),
        grid=(t // tm,),
        in_specs=[row_spec, _const_spec(g2.shape), hbm_spec, hbm_spec, hbm_spec, _const_spec(gf.shape)],
        out_specs=row_spec,
        out_shape=jax.ShapeDtypeStruct((t, D_MODEL), F32),
        scratch_shapes=[
            pltpu.VMEM((D_MODEL, d_ff), BF16),
            pltpu.VMEM((D_MODEL, d_ff), BF16),
            pltpu.VMEM((d_ff, D_MODEL), BF16),
            pltpu.VMEM((2, D_MODEL, fc), F32),
            pltpu.VMEM((2, D_MODEL, fc), F32),
            pltpu.VMEM((2, fc, D_MODEL), F32),
            pltpu.SemaphoreType.DMA((2, 3)),
        ],
        compiler_params=pltpu.CompilerParams(
            dimension_semantics=("arbitrary",), vmem_limit_bytes=VMEM_LIMIT),
        name="ffn",
    )(h1, g2, wg, wu, wd, gf)


def _repack_kernel(wt_hbm, w_ref, wst_ref, stage, narrow, sem, nsem):
    p = pl.program_id(0)
    n = pl.num_programs(0)

    def piece_copy(q, slot):
        row0 = q * PIECE + jnp.where(q >= 8, 8, 0) + jnp.where(q >= 20, 16, 0)
        return pltpu.make_async_copy(wt_hbm.at[pl.ds(pl.multiple_of(row0, SUBLANES), PIECE), :],
                                     stage.at[slot], sem.at[slot])

    def narrow_copies():
        return (pltpu.make_async_copy(wt_hbm.at[pl.ds(MI_COL, SUBLANES), :], narrow.at[pl.ds(0, SUBLANES), :],
                                      nsem.at[0]),
                pltpu.make_async_copy(wt_hbm.at[pl.ds(GA_COL, G_RANK), :], narrow.at[pl.ds(SUBLANES, G_RANK), :],
                                      nsem.at[1]))

    @pl.when(p == 0)
    def _():
        piece_copy(p, 0).start()
        for cp in narrow_copies():
            cp.start()

    @pl.when(p + 1 < n)
    def _():
        piece_copy(p + 1, (p + 1) % 2).start()

    piece_copy(p, p % 2).wait()
    w_ref[...] = stage[p % 2].T.astype(BF16)

    @pl.when(p == 0)
    def _():
        for cp in narrow_copies():
            cp.wait()
        gates = narrow[0:SUBLANES, :]
        zeros = jnp.zeros((SUBLANES - HEADS, D_MODEL), F32)
        wst_ref[...] = jnp.concatenate([gates[:HEADS], zeros, gates[HEADS:], zeros, narrow[SUBLANES:, :]],
                                       axis=0).astype(BF16)


def _repack_w_in(w):
    wt = w.T
    return pl.pallas_call(
        _repack_kernel,
        grid=(N_PACK // PIECE,),
        in_specs=[pl.BlockSpec(memory_space=pl.ANY)],
        out_specs=[pl.BlockSpec((D_MODEL, PIECE), lambda p: (0, p)),
                   pl.BlockSpec((2 * SUBLANES + G_RANK, D_MODEL), lambda p: (0, 0))],
        out_shape=[jax.ShapeDtypeStruct((D_MODEL, N_PACK), BF16),
                   jax.ShapeDtypeStruct((2 * SUBLANES + G_RANK, D_MODEL), BF16)],
        scratch_shapes=[pltpu.VMEM((2, PIECE, D_MODEL), F32), pltpu.VMEM((SUBLANES + G_RANK, D_MODEL), F32),
                        pltpu.SemaphoreType.DMA((2,)), pltpu.SemaphoreType.DMA((2,))],
        compiler_params=pltpu.CompilerParams(dimension_semantics=("arbitrary",)),
        name="repack_w_in",
    )(wt)


def kernel(x, meta_tokens, norm1_g, w_in, conv_w, conv_b, m_gate_b, g_a2, g_a2_b, m_head_g, g_head_g,
           w_branch_m, w_branch_g, w_out, norm2_g, w_ff_gate, w_ff_up, w_ff_down, final_g):
    bsz, seq, d = x.shape
    assert bsz == 1 and d == D_MODEL and norm1_g.shape[0] == 1 and seq % CHUNK == 0
    row = lambda a: a.reshape(1, -1).astype(F32)

    gate_bias = jnp.zeros((2 * SUBLANES, LANES), F32)
    gate_bias = gate_bias.at[:HEADS].set(m_gate_b[0, 0][:, None]).at[SUBLANES:SUBLANES + HEADS].set(m_gate_b[0, 1][:, None])
    a2 = jnp.zeros((LANES, QK), F32).at[GA_LANE:GA_LANE + G_RANK].set(g_a2[0]).astype(BF16)
    conv_w8 = jnp.zeros((SUBLANES, 2 * QK), F32).at[:CONV_W].set(conv_w[0])
    consts = (row(norm1_g[0]), *_repack_w_in(w_in[0]), gate_bias, a2, row(g_a2_b[0]), conv_w8, row(conv_b[0]),
              row(m_head_g[0]), row(g_head_g[0]),
              w_branch_m[0].astype(BF16), w_branch_g[0].astype(BF16), w_out[0].astype(BF16))

    zero_state = (jnp.zeros((SUBLANES, 2 * QK), F32), jnp.zeros((HEADS, DK, CN), F32),
                  jnp.zeros((SUBLANES, LANES), F32), jnp.zeros((HEADS, DK, DV), F32))
    lead = jnp.concatenate([jnp.zeros((RCHUNK - N_META, d), F32), meta_tokens.astype(F32)], axis=0)
    unused = jnp.zeros((SUBLANES, LANES), BF16)
    lead_consts = tuple(unused if i in LEAD_UNUSED else c for i, c in enumerate(consts))
    _, *state = _mixer_call(lead, lead_consts, zero_state, tm=RCHUNK, meta=True)
    h1, *_ = _mixer_call(x[0], consts, tuple(state), tm=512, meta=False)
    out = _ffn_call(h1, row(norm2_g[0]), w_ff_gate[0].astype(F32), w_ff_up[0].astype(F32),
                    w_ff_down[0].astype(F32), row(final_g), tm=1024, fc=256, parts=2)
    return out[None]
```

```python
import functools

import jax
import jax.numpy as jnp
from jax import lax
from jax.experimental import pallas as pl
from jax.experimental.pallas import tpu as pltpu

F32 = jnp.float32
BF16 = jnp.bfloat16

D_MODEL = 1024
N_META = 16
CHUNK = 64
RCHUNK = 128
EPS = 1e-6
HEADS = 4
DV = D_MODEL // HEADS
DK = DV // 2
QK = HEADS * DK
G_RANK = 16
G_TAU = 16.0
CONV_W = 4
LANES = 128
SUBLANES = 8
BF16_ROWS = 16
NEG_BIG = -1e30
VMEM_LIMIT = 60 * 1024 * 1024
CN = DV + LANES

W_IN_RANGES = ((0, 2048), (2056, 5128), (5144, 8216))
N_PACK = sum(b - a for a, b in W_IN_RANGES)
O_MQK, O_MV, O_MO = 0, 1024, 2048
O_GQ, O_GK, O_GV = 3072, 3584, 4096
O_GR, O_GATE_M, O_GATE_G = 5120, 6144, 7168
MI_COL, MF_COL, GA_COL = 2048, 2052, 5128
GA_LANE = 8
PIECE = 256
REPACK_AHEAD = 3
REPACK_SLOTS = REPACK_AHEAD + 1
LEAD_UNUSED = (10, 11, 12)

NT_DIMS = (((1,), (1,)), ((), ()))
TN_DIMS = (((0,), (0,)), ((), ()))


def _dot(a, b):
    return jnp.dot(a, b, preferred_element_type=F32)


def _dot_nt(a, b):
    return lax.dot_general(a, b, NT_DIMS, preferred_element_type=F32)


def _dot_tn(a, b):
    return lax.dot_general(a, b, TN_DIMS, preferred_element_type=F32)


def _pieces(x):
    hi = x.astype(BF16).astype(F32)
    mid = (x - hi).astype(BF16).astype(F32)
    lo = (x - hi - mid).astype(BF16).astype(F32)
    return [hi, mid, lo]


def _log_sigmoid(x):
    return jnp.minimum(x, 0.0) - jnp.log(1.0 + jnp.exp(-jnp.abs(x)))


def _sigmoid(x):
    return 1.0 / (1.0 + jnp.exp(-x))


def _mixer_kernel(x_ref, g1_ref, w_ref, wst_ref, sb_ref, a2_ref, a2b_ref, cw_ref, cb_ref,
                  mhg_ref, ghg_ref, wbm_ref, wbg_ref, wout_ref, tail0_ref, cn0_ref, m0_ref, s0_ref,
                  h1_ref, tail_ref, cn_ref, m_ref, s_ref,
                  xn_s, qk_s, q_s, k_s, wk_s, vo_s, gq64_s, gq128_s, gki_s, gke64_s, gke128_s, gv_s, hm_s, hg_s,
                  ga_s, gb_s, gc_s, gd_s, gqk_s, tric_s, triu_s, spread_s, *, tm, meta):
    nc = tm // RCHUNK
    pairs = [(c, h) for c in range(nc) for h in range(HEADS)]

    @pl.when(pl.program_id(0) == 0)
    def _():
        qk_s[0:SUBLANES, :] = tail0_ref[...]
        cn_ref[...] = cn0_ref[...]
        m_ref[...] = m0_ref[...]
        s_ref[...] = s0_ref[...]
        for h in range(HEADS):
            vo_s[:, h * CN + DV:(h + 1) * CN] = jnp.ones((tm, LANES), BF16)
        row = lax.broadcasted_iota(jnp.int32, (tm, tm), 0)
        col = lax.broadcasted_iota(jnp.int32, (tm, tm), 1)
        tric_s[...] = jnp.where((col <= row) & (col >= (row & -CHUNK)), 1.0, 0.0)[:RCHUNK, :RCHUNK].astype(BF16)
        triu_s[...] = jnp.where((row <= col) & (row >= (col & -RCHUNK)), 1.0, 0.0).astype(BF16)
        sr = lax.broadcasted_iota(jnp.int32, (LANES, 3 * HEADS * LANES), 0)
        sc = lax.broadcasted_iota(jnp.int32, (LANES, 3 * HEADS * LANES), 1)
        blk = (sc >> 9) * (3 * SUBLANES)
        spread_s[...] = jnp.where(
            (sr >= blk) & (sr < blk + 3 * SUBLANES) & ((sr & (SUBLANES - 1)) == ((sc >> 7) & (HEADS - 1))),
            1.0, 0.0).astype(BF16)

    x = x_ref[...]
    xn = x * lax.rsqrt(jnp.mean(x * x, axis=-1, keepdims=True) + EPS) * g1_ref[...]
    if meta:
        valid = lax.broadcasted_iota(jnp.int32, (tm, 1), 0) >= (tm - N_META)
        xn = jnp.where(valid, xn, 0.0)
    xn_s[...] = xn.astype(BF16)

    def proj(off, width):
        return _dot(xn_s[...], w_ref[:, off:off + width])

    def rows(c):
        return slice(c * RCHUNK, (c + 1) * RCHUNK)

    def half(c, second):
        lo = c * RCHUNK + (CHUNK if second else 0)
        return slice(lo, lo + CHUNK)

    crow = lax.broadcasted_iota(jnp.int32, (RCHUNK, RCHUNK), 0)
    ccol = lax.broadcasted_iota(jnp.int32, (RCHUNK, RCHUNK), 1)
    causal = ccol <= crow
    tri_c = tric_s[...]
    tri_up = triu_s[...]
    spread = spread_s[...]

    def st_qk(i, v):
        qk_s[SUBLANES:SUBLANES + tm, i * DV:(i + 1) * DV] = v

    def st_mv(i, v):
        vo_s[:, i * CN:i * CN + DV] = v.astype(BF16)

    def st_gqk(i, v):
        gqk_s[:, i * DV:(i + 1) * DV] = v

    def st_gv(i, v):
        gv_s[:, i * DV:(i + 1) * DV] = v.astype(BF16)

    def st_mo(i, v):
        ga_s[:, i * DV:(i + 1) * DV] = _sigmoid(v)

    def st_gm(i, v):
        gb_s[:, i * DV:(i + 1) * DV] = _sigmoid(v)

    def st_gr(i, v):
        gc_s[:, i * DV:(i + 1) * DV] = v * _sigmoid(v)

    def st_gg(i, v):
        gd_s[:, i * DV:(i + 1) * DV] = _sigmoid(v)

    groups = [(O_MQK, st_qk), (O_MV, st_mv), (O_GQ, st_gqk), (O_GV, st_gv)]
    if not meta:
        groups += [(O_MO, st_mo), (O_GATE_M, st_gm), (O_GR, st_gr), (O_GATE_G, st_gg)]
    queue = [(seg, i, store) for seg, store in groups for i in range(D_MODEL // DV)]
    emitted = [0]

    def fill(n):
        for _ in range(n):
            if emitted[0] < len(queue):
                seg, i, store = queue[emitted[0]]
                store(i, proj(seg + i * DV, DV))
                emitted[0] += 1

    def finish_groups(n):
        fill(n * (D_MODEL // DV) - emitted[0])

    narrow = _dot_nt(wst_ref[...], xn_s[...])
    gates = narrow[:2 * SUBLANES, :] + jnp.concatenate([sb_ref[...]] * (tm // LANES), axis=1)
    fill(2)
    s1 = jnp.concatenate([jnp.zeros((GA_LANE, tm), F32), narrow[2 * SUBLANES:, :],
                          jnp.zeros((LANES - GA_LANE - G_RANK, tm), F32)], axis=0).T
    logi = gates[:SUBLANES, :]
    logf = _log_sigmoid(gates[SUBLANES:, :])
    if meta:
        valid_t = lax.broadcasted_iota(jnp.int32, (1, tm), 1) >= (tm - N_META)
        logi = jnp.where(valid_t, logi, NEG_BIG)
        logf = jnp.where(valid_t, logf, 0.0)
    zero8 = jnp.zeros((SUBLANES, tm), F32)
    b4 = _dot(jnp.concatenate(_pieces(logf) + [zero8], axis=0).astype(BF16), tri_up)
    fill(2)
    b_all = b4[:SUBLANES] + b4[SUBLANES:2 * SUBLANES] + b4[2 * SUBLANES:3 * SUBLANES]
    c_all = logi - b_all
    m_run = m_ref[...]
    inter_parts, w_parts, a_chunk = [], [], []
    for c in range(nc):
        b_c = b_all[:, rows(c)]
        g_c = jnp.broadcast_to(b_c[:, RCHUNK - 1:RCHUNK], (SUBLANES, LANES))
        wlog = g_c + c_all[:, rows(c)]
        m_next = jnp.maximum(g_c + m_run, jnp.max(wlog, axis=1, keepdims=True))
        inter_parts.append(b_c + m_run)
        w_parts.append(jnp.exp(wlog - m_next))
        a_chunk.append(jnp.exp(g_c + m_run - m_next))
        m_run = m_next
    m_ref[...] = m_run
    stacked = jnp.concatenate(
        _pieces(b_all) + _pieces(jnp.concatenate(inter_parts, axis=1)) + _pieces(jnp.concatenate(w_parts, axis=1))
        + [jnp.zeros((LANES - 9 * SUBLANES, tm), F32)], axis=0)
    stacked_t = stacked.T.astype(BF16)
    fill(2)
    tiles = _dot(stacked_t, spread)
    fill(2)
    b_t, il_t, w_t = (tiles[:, i * QK:(i + 1) * QK] for i in range(3))

    finish_groups(1)
    for blk in range(2 * QK // DV):
        csl = slice(blk * DV, (blk + 1) * DV)
        conv = cb_ref[:, csl]
        for j in range(CONV_W):
            conv = conv + cw_ref[j:j + 1, csl] * qk_s[pl.ds(SUBLANES - (CONV_W - 1) + j, tm), csl]
        conv = conv * _sigmoid(conv)
        if blk < QK // DV:
            q_s[:, csl] = conv * (DK ** -0.5)
        else:
            ksl = slice(blk * DV - QK, (blk + 1) * DV - QK)
            k_s[:, ksl] = conv.astype(BF16)
            wk_s[:, ksl] = (w_t[:, ksl] * conv).astype(BF16)
        fill(1)
    tail_new = qk_s[tm:tm + SUBLANES, :]
    qk_s[0:SUBLANES, :] = tail_new
    tail_ref[...] = tail_new

    za = _dot(s1.astype(BF16), a2_ref[...]) + a2b_ref[...]
    fill(1)
    loga = _log_sigmoid(za) / G_TAU
    if meta:
        loga = jnp.where(valid, loga, 0.0)
    hi = loga.astype(BF16)
    r1 = loga - hi.astype(F32)
    mid = r1.astype(BF16)
    lo = (r1 - mid.astype(F32)).astype(BF16)
    bc = jnp.concatenate([_dot(tri_c, hi[rows(c), :]) + _dot(tri_c, mid[rows(c), :]) + _dot(tri_c, lo[rows(c), :])
                          for c in range(nc)], axis=0)
    fill(2)
    e_blk = jnp.exp(bc)
    tot_a = [bc[c * RCHUNK + CHUNK - 1:c * RCHUNK + CHUNK, :] for c in range(nc)]
    tot_b = [bc[(c + 1) * RCHUNK - 1:(c + 1) * RCHUNK, :] for c in range(nc)]
    e_a = [jnp.exp(t) for t in tot_a]
    e_b = [jnp.exp(t) for t in tot_b]
    rest = jnp.concatenate([t - bc[half(c, s), :] for c in range(nc) for s, t in ((0, tot_a[c]), (1, tot_b[c]))],
                           axis=0)
    e_rest = jnp.exp(rest)
    ones_row = jnp.ones((1, QK), F32)
    to_chunk = jnp.concatenate([jnp.broadcast_to(f, (CHUNK, QK)) for c in range(nc) for f in (ones_row, e_a[c])],
                               axis=0)
    from_blk = jnp.concatenate([jnp.broadcast_to(f, (CHUNK, QK)) for c in range(nc) for f in (e_b[c], ones_row)],
                               axis=0)
    finish_groups(3)
    gq = gqk_s[:, :QK] * (DK ** -0.5)
    gk = gqk_s[:, QK:]
    gq64_s[...] = (gq * e_blk).astype(BF16)
    gq128_s[...] = (gq * e_blk * to_chunk).astype(BF16)
    gki_s[...] = (gk * jnp.exp(-bc)).astype(BF16)
    gke64_s[...] = (gk * e_rest).astype(BF16)
    gke128_s[...] = (gk * e_rest * from_blk).astype(BF16)
    fill(2)
    e_rows = [(e_a[c] * e_b[c])[:, h * DK:(h + 1) * DK] for (c, h) in pairs]
    e_cols = jnp.concatenate(e_rows + [jnp.zeros((LANES - len(pairs), DK), F32)], axis=0).T

    finish_groups(2)
    dmat, rmax, sim = {}, {}, {}
    for (c, h) in pairs:
        d = b_t[rows(c), h * LANES:(h + 1) * LANES] + c_all[h:h + 1, rows(c)]
        d = jnp.where(causal, d, -jnp.inf)
        dmat[c, h] = d
        rmax[c, h] = jnp.max(d, axis=-1, keepdims=True)
    fill(1)
    for (c, h) in pairs:
        sim[c, h] = _dot_nt(q_s[rows(c), h * DK:(h + 1) * DK].astype(BF16), k_s[rows(c), h * DK:(h + 1) * DK])
    fill(1)
    lhs, emr, upd = {}, {}, {}
    for (c, h) in pairs:
        il = il_t[rows(c), h * LANES:(h + 1) * LANES]
        m_row = jnp.maximum(il, rmax[c, h])
        wts = jnp.exp(dmat[c, h] - m_row) * sim[c, h]
        aq = jnp.exp(il - m_row) * q_s[rows(c), h * DK:(h + 1) * DK]
        lhs[c, h] = jnp.concatenate([aq.astype(BF16), wts.astype(BF16)], axis=1)
        emr[c, h] = jnp.exp(-m_row)
    fill(1)
    for (c, h) in pairs:
        upd[c, h] = _dot_tn(wk_s[rows(c), h * DK:(h + 1) * DK], vo_s[rows(c), h * CN:(h + 1) * CN])
    fill(1)
    res = {}
    for h in range(HEADS):
        state = cn_ref[h]
        for c in range(nc):
            rhs = jnp.concatenate([state.astype(BF16), vo_s[rows(c), h * CN:(h + 1) * CN]], axis=0)
            res[c, h] = _dot(lhs[c, h], rhs)
            a = a_chunk[c][h:h + 1, :]
            state = jnp.concatenate([a] * (CN // LANES), axis=1) * state + upd[c, h]
        cn_ref[h] = state
        fill(1)
    for (c, h) in pairs:
        inv = 1.0 / jnp.maximum(jnp.abs(res[c, h][:, DV:]), emr[c, h])
        hm_s[rows(c), h * DV:h * DV + LANES] = res[c, h][:, :LANES] * inv
        hm_s[rows(c), h * DV + LANES:(h + 1) * DV] = res[c, h][:, LANES:DV] * inv
    fill(1)

    finish_groups(4)
    att, gupd = {}, {}
    zblk = jnp.zeros((CHUNK, DK), BF16)
    for (c, h) in pairs:
        dsl = slice(h * DK, (h + 1) * DK)
        ra, rb = half(c, 0), half(c, 1)
        qp = jnp.concatenate([jnp.concatenate([gq64_s[ra, dsl], zblk], axis=1),
                              jnp.concatenate([zblk, gq64_s[rb, dsl]], axis=1)], axis=0)
        kp = jnp.concatenate([jnp.concatenate([gki_s[ra, dsl], gke64_s[ra, dsl]], axis=1),
                              jnp.concatenate([zblk, gki_s[rb, dsl]], axis=1)], axis=0)
        att[c, h] = jnp.where(causal, _dot_nt(qp, kp), 0.0).astype(BF16)
    fill(1)
    for (c, h) in pairs:
        gupd[c, h] = _dot_tn(gke128_s[rows(c), h * DK:(h + 1) * DK], gv_s[rows(c), h * DV:(h + 1) * DV])
    fill(1)
    for h in range(HEADS):
        sst = s_ref[h]
        for c in range(nc):
            lhs_g = jnp.concatenate([gq128_s[rows(c), h * DK:(h + 1) * DK], att[c, h]], axis=1)
            rhs_g = jnp.concatenate([sst.astype(BF16), gv_s[rows(c), h * DV:(h + 1) * DV]], axis=0)
            hg_s[rows(c), h * DV:(h + 1) * DV] = _dot(lhs_g, rhs_g)
            i = c * HEADS + h
            sst = e_cols[:, i:i + 1] * sst + gupd[c, h]
        s_ref[h] = sst
        fill(1)

    if meta:
        h1_ref[...] = x
        return

    def head_norm(src, gain_ref):
        outs = []
        for h in range(HEADS):
            t = src[:, h * DV:(h + 1) * DV]
            outs.append(t * lax.rsqrt(jnp.mean(t * t, axis=-1, keepdims=True) + EPS))
        return jnp.concatenate(outs, axis=-1) * gain_ref[...]

    finish_groups(len(groups))
    y_m = head_norm(hm_s, mhg_ref) * ga_s[...]
    merged = gb_s[...] * _dot(y_m.astype(BF16), wbm_ref[...])
    y_g = head_norm(hg_s, ghg_ref) * gc_s[...]
    merged = merged + gd_s[...] * _dot(y_g.astype(BF16), wbg_ref[...])
    h1_ref[...] = x + _dot(merged.astype(BF16), wout_ref[...])


def _const_spec(shape):
    nd = len(shape)
    return pl.BlockSpec(shape, lambda i: (0,) * nd, pipeline_mode=pl.Buffered(1))


def _mixer_call(x2, consts, state, *, tm, meta):
    t = x2.shape[0]
    state_shapes = [jax.ShapeDtypeStruct(a.shape, F32) for a in state]
    row_spec = pl.BlockSpec((tm, D_MODEL), lambda i: (i, 0))
    state_specs = [pl.BlockSpec(a.shape, lambda i, nd=a.ndim: (0,) * nd) for a in state]
    return pl.pallas_call(
        functools.partial(_mixer_kernel, tm=tm, meta=meta),
        grid=(t // tm,),
        in_specs=[row_spec] + [_const_spec(c.shape) for c in consts] + state_specs,
        out_specs=[row_spec] + state_specs,
        out_shape=[jax.ShapeDtypeStruct((t, D_MODEL), F32)] + state_shapes,
        scratch_shapes=[
            pltpu.VMEM((tm, D_MODEL), BF16),
            pltpu.VMEM((tm + SUBLANES, 2 * QK), F32),
            pltpu.VMEM((tm, QK), F32),
            pltpu.VMEM((tm, QK), BF16),
            pltpu.VMEM((tm, QK), BF16),
            pltpu.VMEM((tm, HEADS * CN), BF16),
            pltpu.VMEM((tm, QK), BF16),
            pltpu.VMEM((tm, QK), BF16),
            pltpu.VMEM((tm, QK), BF16),
            pltpu.VMEM((tm, QK), BF16),
            pltpu.VMEM((tm, QK), BF16),
            pltpu.VMEM((tm, D_MODEL), BF16),
            pltpu.VMEM((tm, D_MODEL), F32),
            pltpu.VMEM((tm, D_MODEL), F32),
            pltpu.VMEM((tm, D_MODEL), F32),
            pltpu.VMEM((tm, D_MODEL), F32),
            pltpu.VMEM((tm, D_MODEL), F32),
            pltpu.VMEM((tm, D_MODEL), F32),
            pltpu.VMEM((tm, D_MODEL), F32),
            pltpu.VMEM((RCHUNK, RCHUNK), BF16),
            pltpu.VMEM((tm, tm), BF16),
            pltpu.VMEM((LANES, 3 * HEADS * LANES), BF16),
        ],
        compiler_params=pltpu.CompilerParams(
            dimension_semantics=("arbitrary",), vmem_limit_bytes=VMEM_LIMIT),
        name="mixer_meta" if meta else "mixer",
    )(x2, *consts, *state)


def _ffn_kernel(h_ref, g2_ref, wg_hbm, wu_hbm, wd_hbm, gf_ref, o_ref,
                wg_ref, wu_ref, wd_ref, sg, su, sd, sem, *, fc, parts):
    n_chunks = wg_ref.shape[1] // fc

    def chunk_copies(f, slot):
        cols = pl.ds(f * fc, fc)
        return (pltpu.make_async_copy(wg_hbm.at[:, cols], sg.at[slot], sem.at[slot, 0]),
                pltpu.make_async_copy(wu_hbm.at[:, cols], su.at[slot], sem.at[slot, 1]),
                pltpu.make_async_copy(wd_hbm.at[cols, :], sd.at[slot], sem.at[slot, 2]))

    @pl.when(pl.program_id(0) == 0)
    def _():
        for cp in chunk_copies(0, 0):
            cp.start()
        for f in range(n_chunks):
            slot = f % 2
            if f + 1 < n_chunks:
                for cp in chunk_copies(f + 1, 1 - slot):
                    cp.start()
            for cp in chunk_copies(f, slot):
                cp.wait()
            wg_ref[:, f * fc:(f + 1) * fc] = sg[slot].astype(BF16)
            wu_ref[:, f * fc:(f + 1) * fc] = su[slot].astype(BF16)
            wd_ref[f * fc:(f + 1) * fc, :] = sd[slot].astype(BF16)

    rows = h_ref.shape[0] // parts
    blocks = [slice(p * rows, (p + 1) * rows) for p in range(parts)]
    acc, hn = [], []
    for blk in blocks:
        h = h_ref[blk, :]
        hn.append((h * lax.rsqrt(jnp.mean(h * h, axis=-1, keepdims=True) + EPS) * g2_ref[...]).astype(BF16))
        acc.append(h)
    for f in range(0, wg_ref.shape[1], fc):
        for p in range(parts):
            gate = _dot(hn[p], wg_ref[:, f:f + fc])
            up = _dot(hn[p], wu_ref[:, f:f + fc])
            acc[p] = acc[p] + _dot((gate * _sigmoid(gate) * up).astype(BF16), wd_ref[f:f + fc, :])
    for p, blk in enumerate(blocks):
        a = acc[p]
        o_ref[blk, :] = a * lax.rsqrt(jnp.mean(a * a, axis=-1, keepdims=True) + EPS) * gf_ref[...]


def _ffn_call(h1, g2, wg, wu, wd, gf, *, tm, fc, parts):
    t = h1.shape[0]
    d_ff = wg.shape[1]
    row_spec = pl.BlockSpec((tm, D_MODEL), lambda i: (i, 0))
    hbm_spec = pl.BlockSpec(memory_space=pl.ANY)
    return pl.pallas_call(
        functools.partial(_ffn_kernel, fc=fc, parts=parts),
        grid=(t // tm,),
        in_specs=[row_spec, _const_spec(g2.shape), hbm_spec, hbm_spec, hbm_spec, _const_spec(gf.shape)],
        out_specs=row_spec,
        out_shape=jax.ShapeDtypeStruct((t, D_MODEL), F32),
        scratch_shapes=[
            pltpu.VMEM((D_MODEL, d_ff), BF16),
            pltpu.VMEM((D_MODEL, d_ff), BF16),
            pltpu.VMEM((d_ff, D_MODEL), BF16),
            pltpu.VMEM((2, D_MODEL, fc), F32),
            pltpu.VMEM((2, D_MODEL, fc), F32),
            pltpu.VMEM((2, fc, D_MODEL), F32),
            pltpu.SemaphoreType.DMA((2, 3)),
        ],
        compiler_params=pltpu.CompilerParams(
            dimension_semantics=("arbitrary",), vmem_limit_bytes=VMEM_LIMIT),
        name="ffn",
    )(h1, g2, wg, wu, wd, gf)


def _repack_kernel(wt_hbm, w_ref, wst_ref, stage, narrow, sem, nsem):
    p = pl.program_id(0)
    n = pl.num_programs(0)

    def piece_copy(q, slot):
        row0 = q * PIECE + jnp.where(q >= 8, 8, 0) + jnp.where(q >= 20, 16, 0)
        return pltpu.make_async_copy(wt_hbm.at[pl.ds(pl.multiple_of(row0, SUBLANES), PIECE), :],
                                     stage.at[slot], sem.at[slot])

    def narrow_copies():
        return (pltpu.make_async_copy(wt_hbm.at[pl.ds(MI_COL, SUBLANES), :], narrow.at[pl.ds(0, SUBLANES), :],
                                      nsem.at[0]),
                pltpu.make_async_copy(wt_hbm.at[pl.ds(GA_COL, G_RANK), :], narrow.at[pl.ds(SUBLANES, G_RANK), :],
                                      nsem.at[1]))

    @pl.when(p == 0)
    def _():
        for q in range(REPACK_AHEAD):
            piece_copy(q, q).start()
        for cp in narrow_copies():
            cp.start()

    @pl.when(p + REPACK_AHEAD < n)
    def _():
        piece_copy(p + REPACK_AHEAD, (p + REPACK_AHEAD) % REPACK_SLOTS).start()

    piece_copy(p, p % REPACK_SLOTS).wait()
    w_ref[...] = stage[p % REPACK_SLOTS].T.astype(BF16)

    @pl.when(p == 0)
    def _():
        for cp in narrow_copies():
            cp.wait()
        gates = narrow[0:SUBLANES, :]
        zeros = jnp.zeros((SUBLANES - HEADS, D_MODEL), F32)
        wst_ref[...] = jnp.concatenate([gates[:HEADS], zeros, gates[HEADS:], zeros, narrow[SUBLANES:, :]],
                                       axis=0).astype(BF16)


def _repack_w_in(w):
    wt = w.T
    return pl.pallas_call(
        _repack_kernel,
        grid=(N_PACK // PIECE,),
        in_specs=[pl.BlockSpec(memory_space=pl.ANY)],
        out_specs=[pl.BlockSpec((D_MODEL, PIECE), lambda p: (0, p)),
                   pl.BlockSpec((2 * SUBLANES + G_RANK, D_MODEL), lambda p: (0, 0))],
        out_shape=[jax.ShapeDtypeStruct((D_MODEL, N_PACK), BF16),
                   jax.ShapeDtypeStruct((2 * SUBLANES + G_RANK, D_MODEL), BF16)],
        scratch_shapes=[pltpu.VMEM((REPACK_SLOTS, PIECE, D_MODEL), F32),
                        pltpu.VMEM((SUBLANES + G_RANK, D_MODEL), F32),
                        pltpu.SemaphoreType.DMA((REPACK_SLOTS,)), pltpu.SemaphoreType.DMA((2,))],
        compiler_params=pltpu.CompilerParams(dimension_semantics=("arbitrary",)),
        name="repack_w_in",
    )(wt)


def kernel(x, meta_tokens, norm1_g, w_in, conv_w, conv_b, m_gate_b, g_a2, g_a2_b, m_head_g, g_head_g,
           w_branch_m, w_branch_g, w_out, norm2_g, w_ff_gate, w_ff_up, w_ff_down, final_g):
    bsz, seq, d = x.shape
    assert bsz == 1 and d == D_MODEL and norm1_g.shape[0] == 1 and seq % CHUNK == 0
    row = lambda a: a.reshape(1, -1).astype(F32)

    gate_bias = jnp.zeros((2 * SUBLANES, LANES), F32)
    gate_bias = gate_bias.at[:HEADS].set(m_gate_b[0, 0][:, None]).at[SUBLANES:SUBLANES + HEADS].set(m_gate_b[0, 1][:, None])
    a2 = jnp.zeros((LANES, QK), F32).at[GA_LANE:GA_LANE + G_RANK].set(g_a2[0]).astype(BF16)
    conv_w8 = jnp.zeros((SUBLANES, 2 * QK), F32).at[:CONV_W].set(conv_w[0])
    consts = (row(norm1_g[0]), *_repack_w_in(w_in[0]), gate_bias, a2, row(g_a2_b[0]), conv_w8, row(conv_b[0]),
              row(m_head_g[0]), row(g_head_g[0]),
              w_branch_m[0].astype(BF16), w_branch_g[0].astype(BF16), w_out[0].astype(BF16))

    zero_state = (jnp.zeros((SUBLANES, 2 * QK), F32), jnp.zeros((HEADS, DK, CN), F32),
                  jnp.zeros((SUBLANES, LANES), F32), jnp.zeros((HEADS, DK, DV), F32))
    lead = jnp.concatenate([jnp.zeros((RCHUNK - N_META, d), F32), meta_tokens.astype(F32)], axis=0)
    unused = jnp.zeros((SUBLANES, LANES), BF16)
    lead_consts = tuple(unused if i in LEAD_UNUSED else c for i, c in enumerate(consts))
    _, *state = _mixer_call(lead, lead_consts, zero_state, tm=RCHUNK, meta=True)
    h1, *_ = _mixer_call(x[0], consts, tuple(state), tm=512, meta=False)
    out = _ffn_call(h1, row(norm2_g[0]), w_ff_gate[0].astype(F32), w_ff_up[0].astype(F32),
                    w_ff_down[0].astype(F32), row(final_g), tm=1024, fc=256, parts=2)
    return out[None]
```

```python
import functools

import jax
import jax.numpy as jnp
from jax import lax
from jax.experimental import pallas as pl
from jax.experimental.pallas import tpu as pltpu

F32 = jnp.float32
BF16 = jnp.bfloat16

D_MODEL = 1024
N_META = 16
CHUNK = 64
RCHUNK = 128
EPS = 1e-6
HEADS = 4
DV = D_MODEL // HEADS
DK = DV // 2
QK = HEADS * DK
G_RANK = 16
G_TAU = 16.0
CONV_W = 4
LANES = 128
SUBLANES = 8
BF16_ROWS = 16
NEG_BIG = -1e30
VMEM_LIMIT = 60 * 1024 * 1024
CN = DV + LANES

W_IN_RANGES = ((0, 2048), (2056, 5128), (5144, 8216))
N_PACK = sum(b - a for a, b in W_IN_RANGES)
O_MQK, O_MV, O_MO = 0, 1024, 2048
O_GQ, O_GK, O_GV = 3072, 3584, 4096
O_GR, O_GATE_M, O_GATE_G = 5120, 6144, 7168
MI_COL, MF_COL, GA_COL = 2048, 2052, 5128
GA_LANE = 8
PIECE = 256
REPACK_AHEAD = 3
REPACK_SLOTS = REPACK_AHEAD + 1
LEAD_UNUSED = (10, 11, 12)

NT_DIMS = (((1,), (1,)), ((), ()))
TN_DIMS = (((0,), (0,)), ((), ()))


def _dot(a, b):
    return jnp.dot(a, b, preferred_element_type=F32)


def _dot_nt(a, b):
    return lax.dot_general(a, b, NT_DIMS, preferred_element_type=F32)


def _dot_tn(a, b):
    return lax.dot_general(a, b, TN_DIMS, preferred_element_type=F32)


def _pieces(x):
    hi = x.astype(BF16).astype(F32)
    mid = (x - hi).astype(BF16).astype(F32)
    lo = (x - hi - mid).astype(BF16).astype(F32)
    return [hi, mid, lo]


def _log_sigmoid(x):
    return jnp.minimum(x, 0.0) - jnp.log(1.0 + jnp.exp(-jnp.abs(x)))


def _sigmoid(x):
    return 1.0 / (1.0 + jnp.exp(-x))


def _mixer_kernel(x_ref, g1_ref, w_ref, wst_ref, sb_ref, a2_ref, a2b_ref, cw_ref, cb_ref,
                  mhg_ref, ghg_ref, wbm_ref, wbg_ref, wout_ref, tail0_ref, cn0_ref, m0_ref, s0_ref,
                  h1_ref, tail_ref, cn_ref, m_ref, s_ref,
                  xn_s, qk_s, q_s, k_s, wk_s, vo_s, gq64_s, gq128_s, gki_s, gke64_s, gke128_s, gv_s, hm_s, hg_s,
                  ga_s, gb_s, gc_s, gd_s, gqk_s, tric_s, triu_s, spread_s, *, tm, meta):
    nc = tm // RCHUNK
    pairs = [(c, h) for c in range(nc) for h in range(HEADS)]

    @pl.when(pl.program_id(0) == 0)
    def _():
        qk_s[0:SUBLANES, :] = tail0_ref[...]
        cn_ref[...] = cn0_ref[...]
        m_ref[...] = m0_ref[...]
        s_ref[...] = s0_ref[...]
        for h in range(HEADS):
            vo_s[:, h * CN + DV:(h + 1) * CN] = jnp.ones((tm, LANES), BF16)
        row = lax.broadcasted_iota(jnp.int32, (tm, tm), 0)
        col = lax.broadcasted_iota(jnp.int32, (tm, tm), 1)
        tric_s[...] = jnp.where((col <= row) & (col >= (row & -CHUNK)), 1.0, 0.0)[:RCHUNK, :RCHUNK].astype(BF16)
        triu_s[...] = jnp.where((row <= col) & (row >= (col & -RCHUNK)), 1.0, 0.0).astype(BF16)
        sr = lax.broadcasted_iota(jnp.int32, (LANES, 3 * HEADS * LANES), 0)
        sc = lax.broadcasted_iota(jnp.int32, (LANES, 3 * HEADS * LANES), 1)
        blk = (sc >> 9) * (3 * SUBLANES)
        spread_s[...] = jnp.where(
            (sr >= blk) & (sr < blk + 3 * SUBLANES) & ((sr & (SUBLANES - 1)) == ((sc >> 7) & (HEADS - 1))),
            1.0, 0.0).astype(BF16)

    x = x_ref[...]
    xn = x * lax.rsqrt(jnp.mean(x * x, axis=-1, keepdims=True) + EPS) * g1_ref[...]
    if meta:
        valid = lax.broadcasted_iota(jnp.int32, (tm, 1), 0) >= (tm - N_META)
        xn = jnp.where(valid, xn, 0.0)
    xn_s[...] = xn.astype(BF16)

    def proj(off, width):
        return _dot(xn_s[...], w_ref[:, off:off + width])

    def rows(c):
        return slice(c * RCHUNK, (c + 1) * RCHUNK)

    def half(c, second):
        lo = c * RCHUNK + (CHUNK if second else 0)
        return slice(lo, lo + CHUNK)

    crow = lax.broadcasted_iota(jnp.int32, (RCHUNK, RCHUNK), 0)
    ccol = lax.broadcasted_iota(jnp.int32, (RCHUNK, RCHUNK), 1)
    causal = ccol <= crow
    tri_c = tric_s[...]
    tri_up = triu_s[...]
    spread = spread_s[...]

    def st_qk(i, v):
        qk_s[SUBLANES:SUBLANES + tm, i * DV:(i + 1) * DV] = v

    def st_mv(i, v):
        vo_s[:, i * CN:i * CN + DV] = v.astype(BF16)

    def st_gqk(i, v):
        gqk_s[:, i * DV:(i + 1) * DV] = v

    def st_gv(i, v):
        gv_s[:, i * DV:(i + 1) * DV] = v.astype(BF16)

    def st_mo(i, v):
        ga_s[:, i * DV:(i + 1) * DV] = _sigmoid(v)

    def st_gm(i, v):
        gb_s[:, i * DV:(i + 1) * DV] = _sigmoid(v)

    def st_gr(i, v):
        gc_s[:, i * DV:(i + 1) * DV] = v * _sigmoid(v)

    def st_gg(i, v):
        gd_s[:, i * DV:(i + 1) * DV] = _sigmoid(v)

    groups = [(O_MQK, st_qk), (O_MV, st_mv), (O_GQ, st_gqk), (O_GV, st_gv)]
    if not meta:
        groups += [(O_MO, st_mo), (O_GATE_M, st_gm), (O_GR, st_gr), (O_GATE_G, st_gg)]
    queue = [(seg, i, store) for seg, store in groups for i in range(D_MODEL // DV)]
    emitted = [0]

    def fill(n):
        for _ in range(n):
            if emitted[0] < len(queue):
                seg, i, store = queue[emitted[0]]
                store(i, proj(seg + i * DV, DV))
                emitted[0] += 1

    def finish_groups(n):
        fill(n * (D_MODEL // DV) - emitted[0])

    narrow = _dot_nt(wst_ref[...], xn_s[...])
    gates = narrow[:2 * SUBLANES, :] + jnp.concatenate([sb_ref[...]] * (tm // LANES), axis=1)
    fill(2)
    s1 = jnp.concatenate([jnp.zeros((GA_LANE, tm), F32), narrow[2 * SUBLANES:, :],
                          jnp.zeros((LANES - GA_LANE - G_RANK, tm), F32)], axis=0).T
    logi = gates[:SUBLANES, :]
    logf = _log_sigmoid(gates[SUBLANES:, :])
    if meta:
        valid_t = lax.broadcasted_iota(jnp.int32, (1, tm), 1) >= (tm - N_META)
        logi = jnp.where(valid_t, logi, NEG_BIG)
        logf = jnp.where(valid_t, logf, 0.0)
    zero8 = jnp.zeros((SUBLANES, tm), F32)
    b4 = _dot(jnp.concatenate(_pieces(logf) + [zero8], axis=0).astype(BF16), tri_up)
    fill(2)
    b_all = b4[:SUBLANES] + b4[SUBLANES:2 * SUBLANES] + b4[2 * SUBLANES:3 * SUBLANES]
    c_all = logi - b_all
    m_run = m_ref[...]
    inter_parts, w_parts, a_chunk = [], [], []
    for c in range(nc):
        b_c = b_all[:, rows(c)]
        g_c = jnp.broadcast_to(b_c[:, RCHUNK - 1:RCHUNK], (SUBLANES, LANES))
        wlog = g_c + c_all[:, rows(c)]
        m_next = jnp.maximum(g_c + m_run, jnp.max(wlog, axis=1, keepdims=True))
        inter_parts.append(b_c + m_run)
        w_parts.append(jnp.exp(wlog - m_next))
        a_chunk.append(jnp.exp(g_c + m_run - m_next))
        m_run = m_next
    m_ref[...] = m_run
    stacked = jnp.concatenate(
        _pieces(b_all) + _pieces(jnp.concatenate(inter_parts, axis=1)) + _pieces(jnp.concatenate(w_parts, axis=1))
        + [jnp.zeros((LANES - 9 * SUBLANES, tm), F32)], axis=0)
    stacked_t = stacked.T.astype(BF16)
    fill(2)
    tiles = _dot(stacked_t, spread)
    fill(2)
    b_t, il_t, w_t = (tiles[:, i * QK:(i + 1) * QK] for i in range(3))

    finish_groups(1)
    for blk in range(2 * QK // DV):
        csl = slice(blk * DV, (blk + 1) * DV)
        conv = cb_ref[:, csl]
        for j in range(CONV_W):
            conv = conv + cw_ref[j:j + 1, csl] * qk_s[pl.ds(SUBLANES - (CONV_W - 1) + j, tm), csl]
        conv = conv * _sigmoid(conv)
        if blk < QK // DV:
            q_s[:, csl] = conv * (DK ** -0.5)
        else:
            ksl = slice(blk * DV - QK, (blk + 1) * DV - QK)
            k_s[:, ksl] = conv.astype(BF16)
            wk_s[:, ksl] = (w_t[:, ksl] * conv).astype(BF16)
        fill(1)
    tail_new = qk_s[tm:tm + SUBLANES, :]
    qk_s[0:SUBLANES, :] = tail_new
    tail_ref[...] = tail_new

    za = _dot(s1.astype(BF16), a2_ref[...]) + a2b_ref[...]
    fill(1)
    loga = _log_sigmoid(za) / G_TAU
    if meta:
        loga = jnp.where(valid, loga, 0.0)
    hi = loga.astype(BF16)
    r1 = loga - hi.astype(F32)
    mid = r1.astype(BF16)
    lo = (r1 - mid.astype(F32)).astype(BF16)
    bc = jnp.concatenate([_dot(tri_c, hi[rows(c), :]) + _dot(tri_c, mid[rows(c), :]) + _dot(tri_c, lo[rows(c), :])
                          for c in range(nc)], axis=0)
    fill(2)
    e_blk = jnp.exp(bc)
    tot_a = [bc[c * RCHUNK + CHUNK - 1:c * RCHUNK + CHUNK, :] for c in range(nc)]
    tot_b = [bc[(c + 1) * RCHUNK - 1:(c + 1) * RCHUNK, :] for c in range(nc)]
    e_a = [jnp.exp(t) for t in tot_a]
    e_b = [jnp.exp(t) for t in tot_b]
    rest = jnp.concatenate([t - bc[half(c, s), :] for c in range(nc) for s, t in ((0, tot_a[c]), (1, tot_b[c]))],
                           axis=0)
    e_rest = jnp.exp(rest)
    ones_row = jnp.ones((1, QK), F32)
    to_chunk = jnp.concatenate([jnp.broadcast_to(f, (CHUNK, QK)) for c in range(nc) for f in (ones_row, e_a[c])],
                               axis=0)
    from_blk = jnp.concatenate([jnp.broadcast_to(f, (CHUNK, QK)) for c in range(nc) for f in (e_b[c], ones_row)],
                               axis=0)
    finish_groups(3)
    gq = gqk_s[:, :QK] * (DK ** -0.5)
    gk = gqk_s[:, QK:]
    gq64_s[...] = (gq * e_blk).astype(BF16)
    gq128_s[...] = (gq * e_blk * to_chunk).astype(BF16)
    gki_s[...] = (gk * jnp.exp(-bc)).astype(BF16)
    gke64_s[...] = (gk * e_rest).astype(BF16)
    gke128_s[...] = (gk * e_rest * from_blk).astype(BF16)
    fill(2)
    e_rows = [(e_a[c] * e_b[c])[:, h * DK:(h + 1) * DK] for (c, h) in pairs]
    e_cols = jnp.concatenate(e_rows + [jnp.zeros((LANES - len(pairs), DK), F32)], axis=0).T

    finish_groups(2)
    dmat, rmax, sim = {}, {}, {}
    for (c, h) in pairs:
        d = b_t[rows(c), h * LANES:(h + 1) * LANES] + c_all[h:h + 1, rows(c)]
        d = jnp.where(causal, d, -jnp.inf)
        dmat[c, h] = d
        rmax[c, h] = jnp.max(d, axis=-1, keepdims=True)
    fill(1)
    for (c, h) in pairs:
        sim[c, h] = _dot_nt(q_s[rows(c), h * DK:(h + 1) * DK].astype(BF16), k_s[rows(c), h * DK:(h + 1) * DK])
    fill(1)
    lhs, emr, upd = {}, {}, {}
    for (c, h) in pairs:
        il = il_t[rows(c), h * LANES:(h + 1) * LANES]
        m_row = jnp.maximum(il, rmax[c, h])
        wts = jnp.exp(dmat[c, h] - m_row) * sim[c, h]
        aq = jnp.exp(il - m_row) * q_s[rows(c), h * DK:(h + 1) * DK]
        lhs[c, h] = jnp.concatenate([aq.astype(BF16), wts.astype(BF16)], axis=1)
        emr[c, h] = jnp.exp(-m_row)
    fill(1)
    for (c, h) in pairs:
        upd[c, h] = _dot_tn(wk_s[rows(c), h * DK:(h + 1) * DK], vo_s[rows(c), h * CN:(h + 1) * CN])
    fill(1)
    res = {}
    for h in range(HEADS):
        state = cn_ref[h]
        for c in range(nc):
            rhs = jnp.concatenate([state.astype(BF16), vo_s[rows(c), h * CN:(h + 1) * CN]], axis=0)
            res[c, h] = _dot(lhs[c, h], rhs)
            a = a_chunk[c][h:h + 1, :]
            state = jnp.concatenate([a] * (CN // LANES), axis=1) * state + upd[c, h]
        cn_ref[h] = state
        fill(1)
    for (c, h) in pairs:
        inv = 1.0 / jnp.maximum(jnp.abs(res[c, h][:, DV:]), emr[c, h])
        hm_s[rows(c), h * DV:h * DV + LANES] = res[c, h][:, :LANES] * inv
        hm_s[rows(c), h * DV + LANES:(h + 1) * DV] = res[c, h][:, LANES:DV] * inv
    fill(1)

    finish_groups(4)
    att, gupd = {}, {}
    zblk = jnp.zeros((CHUNK, DK), BF16)
    for (c, h) in pairs:
        dsl = slice(h * DK, (h + 1) * DK)
        ra, rb = half(c, 0), half(c, 1)
        qp = jnp.concatenate([jnp.concatenate([gq64_s[ra, dsl], zblk], axis=1),
                              jnp.concatenate([zblk, gq64_s[rb, dsl]], axis=1)], axis=0)
        kp = jnp.concatenate([jnp.concatenate([gki_s[ra, dsl], gke64_s[ra, dsl]], axis=1),
                              jnp.concatenate([zblk, gki_s[rb, dsl]], axis=1)], axis=0)
        att[c, h] = jnp.where(causal, _dot_nt(qp, kp), 0.0).astype(BF16)
    fill(1)
    for (c, h) in pairs:
        gupd[c, h] = _dot_tn(gke128_s[rows(c), h * DK:(h + 1) * DK], gv_s[rows(c), h * DV:(h + 1) * DV])
    fill(1)
    for h in range(HEADS):
        sst = s_ref[h]
        for c in range(nc):
            lhs_g = jnp.concatenate([gq128_s[rows(c), h * DK:(h + 1) * DK], att[c, h]], axis=1)
            rhs_g = jnp.concatenate([sst.astype(BF16), gv_s[rows(c), h * DV:(h + 1) * DV]], axis=0)
            hg_s[rows(c), h * DV:(h + 1) * DV] = _dot(lhs_g, rhs_g)
            i = c * HEADS + h
            sst = e_cols[:, i:i + 1] * sst + gupd[c, h]
        s_ref[h] = sst
        fill(1)

    if meta:
        h1_ref[...] = x
        return

    def head_norm(src, gain_ref):
        outs = []
        for h in range(HEADS):
            t = src[:, h * DV:(h + 1) * DV]
            outs.append(t * lax.rsqrt(jnp.mean(t * t, axis=-1, keepdims=True) + EPS))
        return jnp.concatenate(outs, axis=-1) * gain_ref[...]

    finish_groups(len(groups))
    y_m = head_norm(hm_s, mhg_ref) * ga_s[...]
    merged = gb_s[...] * _dot(y_m.astype(BF16), wbm_ref[...])
    y_g = head_norm(hg_s, ghg_ref) * gc_s[...]
    merged = merged + gd_s[...] * _dot(y_g.astype(BF16), wbg_ref[...])
    h1_ref[...] = x + _dot(merged.astype(BF16), wout_ref[...])


def _const_spec(shape):
    nd = len(shape)
    return pl.BlockSpec(shape, lambda i: (0,) * nd, pipeline_mode=pl.Buffered(1))


def _mixer_call(x2, consts, state, *, tm, meta):
    t = x2.shape[0]
    state_shapes = [jax.ShapeDtypeStruct(a.shape, F32) for a in state]
    row_spec = pl.BlockSpec((tm, D_MODEL), lambda i: (i, 0))
    state_specs = [pl.BlockSpec(a.shape, lambda i, nd=a.ndim: (0,) * nd) for a in state]
    return pl.pallas_call(
        functools.partial(_mixer_kernel, tm=tm, meta=meta),
        grid=(t // tm,),
        in_specs=[row_spec] + [_const_spec(c.shape) for c in consts] + state_specs,
        out_specs=[row_spec] + state_specs,
        out_shape=[jax.ShapeDtypeStruct((t, D_MODEL), F32)] + state_shapes,
        scratch_shapes=[
            pltpu.VMEM((tm, D_MODEL), BF16),
            pltpu.VMEM((tm + SUBLANES, 2 * QK), F32),
            pltpu.VMEM((tm, QK), F32),
            pltpu.VMEM((tm, QK), BF16),
            pltpu.VMEM((tm, QK), BF16),
            pltpu.VMEM((tm, HEADS * CN), BF16),
            pltpu.VMEM((tm, QK), BF16),
            pltpu.VMEM((tm, QK), BF16),
            pltpu.VMEM((tm, QK), BF16),
            pltpu.VMEM((tm, QK), BF16),
            pltpu.VMEM((tm, QK), BF16),
            pltpu.VMEM((tm, D_MODEL), BF16),
            pltpu.VMEM((tm, D_MODEL), F32),
            pltpu.VMEM((tm, D_MODEL), F32),
            pltpu.VMEM((tm, D_MODEL), F32),
            pltpu.VMEM((tm, D_MODEL), F32),
            pltpu.VMEM((tm, D_MODEL), F32),
            pltpu.VMEM((tm, D_MODEL), F32),
            pltpu.VMEM((tm, D_MODEL), F32),
            pltpu.VMEM((RCHUNK, RCHUNK), BF16),
            pltpu.VMEM((tm, tm), BF16),
            pltpu.VMEM((LANES, 3 * HEADS * LANES), BF16),
        ],
        compiler_params=pltpu.CompilerParams(
            dimension_semantics=("arbitrary",), vmem_limit_bytes=VMEM_LIMIT),
        name="mixer_meta" if meta else "mixer",
    )(x2, *consts, *state)


def _ffn_kernel(h_ref, g2_ref, wg_hbm, wu_hbm, wd_hbm, gf_ref, o_ref,
                wg_ref, wu_ref, wd_ref, sg, su, sd, sem, *, fc, parts):
    n_chunks = wg_ref.shape[1] // fc

    def chunk_copies(f, slot):
        cols = pl.ds(f * fc, fc)
        return (pltpu.make_async_copy(wg_hbm.at[:, cols], sg.at[slot], sem.at[slot, 0]),
                pltpu.make_async_copy(wu_hbm.at[:, cols], su.at[slot], sem.at[slot, 1]),
                pltpu.make_async_copy(wd_hbm.at[cols, :], sd.at[slot], sem.at[slot, 2]))

    def fetch_chunk(f):
        slot = f % 2
        if f == 0:
            for cp in chunk_copies(0, 0):
                cp.start()
        if f + 1 < n_chunks:
            for cp in chunk_copies(f + 1, 1 - slot):
                cp.start()
        for cp in chunk_copies(f, slot):
            cp.wait()
        wg_ref[:, f * fc:(f + 1) * fc] = sg[slot].astype(BF16)
        wu_ref[:, f * fc:(f + 1) * fc] = su[slot].astype(BF16)
        wd_ref[f * fc:(f + 1) * fc, :] = sd[slot].astype(BF16)

    def tile(first):
        rows = h_ref.shape[0] // parts
        blocks = [slice(p * rows, (p + 1) * rows) for p in range(parts)]
        acc, hn = [], []
        for blk in blocks:
            h = h_ref[blk, :]
            hn.append((h * lax.rsqrt(jnp.mean(h * h, axis=-1, keepdims=True) + EPS) * g2_ref[...]).astype(BF16))
            acc.append(h)
        for i in range(n_chunks):
            f = i * fc
            if first:
                fetch_chunk(i)
            for p in range(parts):
                gate = _dot(hn[p], wg_ref[:, f:f + fc])
                up = _dot(hn[p], wu_ref[:, f:f + fc])
                acc[p] = acc[p] + _dot((gate * _sigmoid(gate) * up).astype(BF16), wd_ref[f:f + fc, :])
        for p, blk in enumerate(blocks):
            a = acc[p]
            o_ref[blk, :] = a * lax.rsqrt(jnp.mean(a * a, axis=-1, keepdims=True) + EPS) * gf_ref[...]

    pl.when(pl.program_id(0) == 0)(functools.partial(tile, True))
    pl.when(pl.program_id(0) > 0)(functools.partial(tile, False))


def _ffn_call(h1, g2, wg, wu, wd, gf, *, tm, fc, parts):
    t = h1.shape[0]
    d_ff = wg.shape[1]
    row_spec = pl.BlockSpec((tm, D_MODEL), lambda i: (i, 0))
    hbm_spec = pl.BlockSpec(memory_space=pl.ANY)
    return pl.pallas_call(
        functools.partial(_ffn_kernel, fc=fc, parts=parts),
        grid=(t // tm,),
        in_specs=[row_spec, _const_spec(g2.shape), hbm_spec, hbm_spec, hbm_spec, _const_spec(gf.shape)],
        out_specs=row_spec,
        out_shape=jax.ShapeDtypeStruct((t, D_MODEL), F32),
        scratch_shapes=[
            pltpu.VMEM((D_MODEL, d_ff), BF16),
            pltpu.VMEM((D_MODEL, d_ff), BF16),
            pltpu.VMEM((d_ff, D_MODEL), BF16),
            pltpu.VMEM((2, D_MODEL, fc), F32),
            pltpu.VMEM((2, D_MODEL, fc), F32),
            pltpu.VMEM((2, fc, D_MODEL), F32),
            pltpu.SemaphoreType.DMA((2, 3)),
        ],
        compiler_params=pltpu.CompilerParams(
            dimension_semantics=("arbitrary",), vmem_limit_bytes=VMEM_LIMIT),
        name="ffn",
    )(h1, g2, wg, wu, wd, gf)


def _repack_kernel(wt_hbm, w_ref, wst_ref, stage, narrow, sem, nsem):
    p = pl.program_id(0)
    n = pl.num_programs(0)

    def piece_copy(q, slot):
        row0 = q * PIECE + jnp.where(q >= 8, 8, 0) + jnp.where(q >= 20, 16, 0)
        return pltpu.make_async_copy(wt_hbm.at[pl.ds(pl.multiple_of(row0, SUBLANES), PIECE), :],
                                     stage.at[slot], sem.at[slot])

    def narrow_copies():
        return (pltpu.make_async_copy(wt_hbm.at[pl.ds(MI_COL, SUBLANES), :], narrow.at[pl.ds(0, SUBLANES), :],
                                      nsem.at[0]),
                pltpu.make_async_copy(wt_hbm.at[pl.ds(GA_COL, G_RANK), :], narrow.at[pl.ds(SUBLANES, G_RANK), :],
                                      nsem.at[1]))

    @pl.when(p == 0)
    def _():
        for q in range(REPACK_AHEAD):
            piece_copy(q, q).start()
        for cp in narrow_copies():
            cp.start()

    @pl.when(p + REPACK_AHEAD < n)
    def _():
        piece_copy(p + REPACK_AHEAD, (p + REPACK_AHEAD) % REPACK_SLOTS).start()

    piece_copy(p, p % REPACK_SLOTS).wait()
    w_ref[...] = stage[p % REPACK_SLOTS].T.astype(BF16)

    @pl.when(p == 0)
    def _():
        for cp in narrow_copies():
            cp.wait()
        gates = narrow[0:SUBLANES, :]
        zeros = jnp.zeros((SUBLANES - HEADS, D_MODEL), F32)
        wst_ref[...] = jnp.concatenate([gates[:HEADS], zeros, gates[HEADS:], zeros, narrow[SUBLANES:, :]],
                                       axis=0).astype(BF16)


def _repack_w_in(w):
    wt = w.T
    return pl.pallas_call(
        _repack_kernel,
        grid=(N_PACK // PIECE,),
        in_specs=[pl.BlockSpec(memory_space=pl.ANY)],
        out_specs=[pl.BlockSpec((D_MODEL, PIECE), lambda p: (0, p)),
                   pl.BlockSpec((2 * SUBLANES + G_RANK, D_MODEL), lambda p: (0, 0))],
        out_shape=[jax.ShapeDtypeStruct((D_MODEL, N_PACK), BF16),
                   jax.ShapeDtypeStruct((2 * SUBLANES + G_RANK, D_MODEL), BF16)],
        scratch_shapes=[pltpu.VMEM((REPACK_SLOTS, PIECE, D_MODEL), F32),
                        pltpu.VMEM((SUBLANES + G_RANK, D_MODEL), F32),
                        pltpu.SemaphoreType.DMA((REPACK_SLOTS,)), pltpu.SemaphoreType.DMA((2,))],
        compiler_params=pltpu.CompilerParams(dimension_semantics=("arbitrary",)),
        name="repack_w_in",
    )(wt)


def kernel(x, meta_tokens, norm1_g, w_in, conv_w, conv_b, m_gate_b, g_a2, g_a2_b, m_head_g, g_head_g,
           w_branch_m, w_branch_g, w_out, norm2_g, w_ff_gate, w_ff_up, w_ff_down, final_g):
    bsz, seq, d = x.shape
    assert bsz == 1 and d == D_MODEL and norm1_g.shape[0] == 1 and seq % CHUNK == 0
    row = lambda a: a.reshape(1, -1).astype(F32)

    gate_bias = jnp.zeros((2 * SUBLANES, LANES), F32)
    gate_bias = gate_bias.at[:HEADS].set(m_gate_b[0, 0][:, None]).at[SUBLANES:SUBLANES + HEADS].set(m_gate_b[0, 1][:, None])
    a2 = jnp.zeros((LANES, QK), F32).at[GA_LANE:GA_LANE + G_RANK].set(g_a2[0]).astype(BF16)
    conv_w8 = jnp.zeros((SUBLANES, 2 * QK), F32).at[:CONV_W].set(conv_w[0])
    consts = (row(norm1_g[0]), *_repack_w_in(w_in[0]), gate_bias, a2, row(g_a2_b[0]), conv_w8, row(conv_b[0]),
              row(m_head_g[0]), row(g_head_g[0]),
              w_branch_m[0].astype(BF16), w_branch_g[0].astype(BF16), w_out[0].astype(BF16))

    zero_state = (jnp.zeros((SUBLANES, 2 * QK), F32), jnp.zeros((HEADS, DK, CN), F32),
                  jnp.zeros((SUBLANES, LANES), F32), jnp.zeros((HEADS, DK, DV), F32))
    lead = jnp.concatenate([jnp.zeros((RCHUNK - N_META, d), F32), meta_tokens.astype(F32)], axis=0)
    unused = jnp.zeros((SUBLANES, LANES), BF16)
    lead_consts = tuple(unused if i in LEAD_UNUSED else c for i, c in enumerate(consts))
    _, *state = _mixer_call(lead, lead_consts, zero_state, tm=RCHUNK, meta=True)
    h1, *_ = _mixer_call(x[0], consts, tuple(state), tm=512, meta=False)
    out = _ffn_call(h1, row(norm2_g[0]), w_ff_gate[0].astype(F32), w_ff_up[0].astype(F32),
                    w_ff_down[0].astype(F32), row(final_g), tm=1024, fc=256, parts=2)
    return out[None]
```

```python
import functools

import jax
import jax.numpy as jnp
from jax import lax
from jax.experimental import pallas as pl
from jax.experimental.pallas import tpu as pltpu

F32 = jnp.float32
BF16 = jnp.bfloat16

D_MODEL = 1024
N_META = 16
CHUNK = 64
RCHUNK = 128
EPS = 1e-6
HEADS = 4
DV = D_MODEL // HEADS
DK = DV // 2
QK = HEADS * DK
G_RANK = 16
G_TAU = 16.0
CONV_W = 4
LANES = 128
SUBLANES = 8
BF16_ROWS = 16
NEG_BIG = -1e30
VMEM_LIMIT = 60 * 1024 * 1024
CN = DV + LANES

W_IN_RANGES = ((0, 2048), (2056, 5128), (5144, 8216))
N_PACK = sum(b - a for a, b in W_IN_RANGES)
O_MQK, O_MV, O_MO = 0, 1024, 2048
O_GQ, O_GK, O_GV = 3072, 3584, 4096
O_GR, O_GATE_M, O_GATE_G = 5120, 6144, 7168
MI_COL, MF_COL, GA_COL = 2048, 2052, 5128
GA_LANE = 8
PIECE = 512
MERGE_ROWS = 256
REPACK_AHEAD = 3
REPACK_SLOTS = REPACK_AHEAD + 1
LEAD_UNUSED = (10, 11, 12)

NT_DIMS = (((1,), (1,)), ((), ()))
TN_DIMS = (((0,), (0,)), ((), ()))


def _dot(a, b):
    return jnp.dot(a, b, preferred_element_type=F32)


def _dot_nt(a, b):
    return lax.dot_general(a, b, NT_DIMS, preferred_element_type=F32)


def _dot_tn(a, b):
    return lax.dot_general(a, b, TN_DIMS, preferred_element_type=F32)


def _pieces(x):
    hi = x.astype(BF16).astype(F32)
    mid = (x - hi).astype(BF16).astype(F32)
    lo = (x - hi - mid).astype(BF16).astype(F32)
    return [hi, mid, lo]


def _log_sigmoid(x):
    return jnp.minimum(x, 0.0) - jnp.log(1.0 + jnp.exp(-jnp.abs(x)))


def _sigmoid(x):
    return 1.0 / (1.0 + jnp.exp(-x))


def _mixer_kernel(x_ref, g1_ref, w_ref, wst_ref, sb_ref, a2_ref, a2b_ref, cw_ref, cb_ref,
                  mhg_ref, ghg_ref, wbm_ref, wbg_ref, wout_ref, tail0_ref, cn0_ref, m0_ref, s0_ref,
                  h1_ref, tail_ref, cn_ref, m_ref, s_ref,
                  xn_s, qk_s, q_s, k_s, wk_s, vo_s, gq64_s, gq128_s, gki_s, gke64_s, gke128_s, gv_s, hm_s, hg_s,
                  ga_s, gb_s, gc_s, gd_s, gqk_s, tric_s, triu_s, spread_s, *merge_scratch, tm, meta):
    nc = tm // RCHUNK
    pairs = [(c, h) for c in range(nc) for h in range(HEADS)]
    if not meta:
        *merge_w, wstage, wsem = merge_scratch

    @pl.when(pl.program_id(0) == 0)
    def _():
        qk_s[0:SUBLANES, :] = tail0_ref[...]
        cn_ref[...] = cn0_ref[...]
        m_ref[...] = m0_ref[...]
        s_ref[...] = s0_ref[...]
        for h in range(HEADS):
            vo_s[:, h * CN + DV:(h + 1) * CN] = jnp.ones((tm, LANES), BF16)
        if not meta:
            srcs = (wbm_ref, wbg_ref, wout_ref)
            n_blk = D_MODEL // MERGE_ROWS

            def block_copy(j, slot):
                return pltpu.make_async_copy(srcs[j // n_blk].at[pl.ds((j % n_blk) * MERGE_ROWS, MERGE_ROWS), :],
                                             wstage.at[slot], wsem.at[slot])

            block_copy(0, 0).start()
            for j in range(len(srcs) * n_blk):
                slot = j % 2
                if j + 1 < len(srcs) * n_blk:
                    block_copy(j + 1, 1 - slot).start()
                block_copy(j, slot).wait()
                r0 = (j % n_blk) * MERGE_ROWS
                merge_w[j // n_blk][r0:r0 + MERGE_ROWS, :] = wstage[slot].astype(BF16)
        row = lax.broadcasted_iota(jnp.int32, (tm, tm), 0)
        col = lax.broadcasted_iota(jnp.int32, (tm, tm), 1)
        tric_s[...] = jnp.where((col <= row) & (col >= (row & -CHUNK)), 1.0, 0.0)[:RCHUNK, :RCHUNK].astype(BF16)
        triu_s[...] = jnp.where((row <= col) & (row >= (col & -RCHUNK)), 1.0, 0.0).astype(BF16)
        sr = lax.broadcasted_iota(jnp.int32, (LANES, 3 * HEADS * LANES), 0)
        sc = lax.broadcasted_iota(jnp.int32, (LANES, 3 * HEADS * LANES), 1)
        blk = (sc >> 9) * (3 * SUBLANES)
        spread_s[...] = jnp.where(
            (sr >= blk) & (sr < blk + 3 * SUBLANES) & ((sr & (SUBLANES - 1)) == ((sc >> 7) & (HEADS - 1))),
            1.0, 0.0).astype(BF16)

    x = x_ref[...]
    xn = x * lax.rsqrt(jnp.mean(x * x, axis=-1, keepdims=True) + EPS) * g1_ref[...]
    if meta:
        valid = lax.broadcasted_iota(jnp.int32, (tm, 1), 0) >= (tm - N_META)
        xn = jnp.where(valid, xn, 0.0)
    xn_s[...] = xn.astype(BF16)

    def proj(off, width):
        return _dot(xn_s[...], w_ref[:, off:off + width])

    def rows(c):
        return slice(c * RCHUNK, (c + 1) * RCHUNK)

    def half(c, second):
        lo = c * RCHUNK + (CHUNK if second else 0)
        return slice(lo, lo + CHUNK)

    crow = lax.broadcasted_iota(jnp.int32, (RCHUNK, RCHUNK), 0)
    ccol = lax.broadcasted_iota(jnp.int32, (RCHUNK, RCHUNK), 1)
    causal = ccol <= crow
    tri_c = tric_s[...]
    tri_up = triu_s[...]
    spread = spread_s[...]

    def st_qk(i, v):
        qk_s[SUBLANES:SUBLANES + tm, i * DV:(i + 1) * DV] = v

    def st_mv(i, v):
        vo_s[:, i * CN:i * CN + DV] = v.astype(BF16)

    def st_gqk(i, v):
        gqk_s[:, i * DV:(i + 1) * DV] = v

    def st_gv(i, v):
        gv_s[:, i * DV:(i + 1) * DV] = v.astype(BF16)

    def st_mo(i, v):
        ga_s[:, i * DV:(i + 1) * DV] = _sigmoid(v)

    def st_gm(i, v):
        gb_s[:, i * DV:(i + 1) * DV] = _sigmoid(v)

    def st_gr(i, v):
        gc_s[:, i * DV:(i + 1) * DV] = v * _sigmoid(v)

    def st_gg(i, v):
        gd_s[:, i * DV:(i + 1) * DV] = _sigmoid(v)

    groups = [(O_MQK, st_qk), (O_MV, st_mv), (O_GQ, st_gqk), (O_GV, st_gv)]
    if not meta:
        groups += [(O_MO, st_mo), (O_GATE_M, st_gm), (O_GR, st_gr), (O_GATE_G, st_gg)]
    queue = [(seg, i, store) for seg, store in groups for i in range(D_MODEL // DV)]
    emitted = [0]

    def fill(n):
        for _ in range(n):
            if emitted[0] < len(queue):
                seg, i, store = queue[emitted[0]]
                store(i, proj(seg + i * DV, DV))
                emitted[0] += 1

    def finish_groups(n):
        fill(n * (D_MODEL // DV) - emitted[0])

    narrow = _dot_nt(wst_ref[...], xn_s[...])
    gates = narrow[:2 * SUBLANES, :] + jnp.concatenate([sb_ref[...]] * (tm // LANES), axis=1)
    fill(2)
    s1 = jnp.concatenate([jnp.zeros((GA_LANE, tm), F32), narrow[2 * SUBLANES:, :],
                          jnp.zeros((LANES - GA_LANE - G_RANK, tm), F32)], axis=0).T
    logi = gates[:SUBLANES, :]
    logf = _log_sigmoid(gates[SUBLANES:, :])
    if meta:
        valid_t = lax.broadcasted_iota(jnp.int32, (1, tm), 1) >= (tm - N_META)
        logi = jnp.where(valid_t, logi, NEG_BIG)
        logf = jnp.where(valid_t, logf, 0.0)
    zero8 = jnp.zeros((SUBLANES, tm), F32)
    b4 = _dot(jnp.concatenate(_pieces(logf) + [zero8], axis=0).astype(BF16), tri_up)
    fill(2)
    b_all = b4[:SUBLANES] + b4[SUBLANES:2 * SUBLANES] + b4[2 * SUBLANES:3 * SUBLANES]
    c_all = logi - b_all
    m_run = m_ref[...]
    inter_parts, w_parts, a_chunk = [], [], []
    for c in range(nc):
        b_c = b_all[:, rows(c)]
        g_c = jnp.broadcast_to(b_c[:, RCHUNK - 1:RCHUNK], (SUBLANES, LANES))
        wlog = g_c + c_all[:, rows(c)]
        m_next = jnp.maximum(g_c + m_run, jnp.max(wlog, axis=1, keepdims=True))
        inter_parts.append(b_c + m_run)
        w_parts.append(jnp.exp(wlog - m_next))
        a_chunk.append(jnp.exp(g_c + m_run - m_next))
        m_run = m_next
    m_ref[...] = m_run
    stacked = jnp.concatenate(
        _pieces(b_all) + _pieces(jnp.concatenate(inter_parts, axis=1)) + _pieces(jnp.concatenate(w_parts, axis=1))
        + [jnp.zeros((LANES - 9 * SUBLANES, tm), F32)], axis=0)
    stacked_t = stacked.T.astype(BF16)
    fill(2)
    tiles = _dot(stacked_t, spread)
    fill(2)
    b_t, il_t, w_t = (tiles[:, i * QK:(i + 1) * QK] for i in range(3))

    finish_groups(1)
    for blk in range(2 * QK // DV):
        csl = slice(blk * DV, (blk + 1) * DV)
        conv = cb_ref[:, csl]
        for j in range(CONV_W):
            conv = conv + cw_ref[j:j + 1, csl] * qk_s[pl.ds(SUBLANES - (CONV_W - 1) + j, tm), csl]
        conv = conv * _sigmoid(conv)
        if blk < QK // DV:
            q_s[:, csl] = conv * (DK ** -0.5)
        else:
            ksl = slice(blk * DV - QK, (blk + 1) * DV - QK)
            k_s[:, ksl] = conv.astype(BF16)
            wk_s[:, ksl] = (w_t[:, ksl] * conv).astype(BF16)
        fill(1)
    tail_new = qk_s[tm:tm + SUBLANES, :]
    qk_s[0:SUBLANES, :] = tail_new
    tail_ref[...] = tail_new

    za = _dot(s1.astype(BF16), a2_ref[...]) + a2b_ref[...]
    fill(1)
    loga = _log_sigmoid(za) / G_TAU
    if meta:
        loga = jnp.where(valid, loga, 0.0)
    hi = loga.astype(BF16)
    r1 = loga - hi.astype(F32)
    mid = r1.astype(BF16)
    lo = (r1 - mid.astype(F32)).astype(BF16)
    bc = jnp.concatenate([_dot(tri_c, hi[rows(c), :]) + _dot(tri_c, mid[rows(c), :]) + _dot(tri_c, lo[rows(c), :])
                          for c in range(nc)], axis=0)
    fill(2)
    e_blk = jnp.exp(bc)
    tot_a = [bc[c * RCHUNK + CHUNK - 1:c * RCHUNK + CHUNK, :] for c in range(nc)]
    tot_b = [bc[(c + 1) * RCHUNK - 1:(c + 1) * RCHUNK, :] for c in range(nc)]
    e_a = [jnp.exp(t) for t in tot_a]
    e_b = [jnp.exp(t) for t in tot_b]
    rest = jnp.concatenate([t - bc[half(c, s), :] for c in range(nc) for s, t in ((0, tot_a[c]), (1, tot_b[c]))],
                           axis=0)
    e_rest = jnp.exp(rest)
    ones_row = jnp.ones((1, QK), F32)
    to_chunk = jnp.concatenate([jnp.broadcast_to(f, (CHUNK, QK)) for c in range(nc) for f in (ones_row, e_a[c])],
                               axis=0)
    from_blk = jnp.concatenate([jnp.broadcast_to(f, (CHUNK, QK)) for c in range(nc) for f in (e_b[c], ones_row)],
                               axis=0)
    finish_groups(3)
    gq = gqk_s[:, :QK] * (DK ** -0.5)
    gk = gqk_s[:, QK:]
    gq64_s[...] = (gq * e_blk).astype(BF16)
    gq128_s[...] = (gq * e_blk * to_chunk).astype(BF16)
    gki_s[...] = (gk * jnp.exp(-bc)).astype(BF16)
    gke64_s[...] = (gk * e_rest).astype(BF16)
    gke128_s[...] = (gk * e_rest * from_blk).astype(BF16)
    fill(2)
    e_rows = [(e_a[c] * e_b[c])[:, h * DK:(h + 1) * DK] for (c, h) in pairs]
    e_cols = jnp.concatenate(e_rows + [jnp.zeros((LANES - len(pairs), DK), F32)], axis=0).T

    finish_groups(2)
    dmat, rmax, sim = {}, {}, {}
    for (c, h) in pairs:
        d = b_t[rows(c), h * LANES:(h + 1) * LANES] + c_all[h:h + 1, rows(c)]
        d = jnp.where(causal, d, -jnp.inf)
        dmat[c, h] = d
        rmax[c, h] = jnp.max(d, axis=-1, keepdims=True)
    fill(1)
    for (c, h) in pairs:
        sim[c, h] = _dot_nt(q_s[rows(c), h * DK:(h + 1) * DK].astype(BF16), k_s[rows(c), h * DK:(h + 1) * DK])
    fill(1)
    lhs, emr, upd = {}, {}, {}
    for (c, h) in pairs:
        il = il_t[rows(c), h * LANES:(h + 1) * LANES]
        m_row = jnp.maximum(il, rmax[c, h])
        wts = jnp.exp(dmat[c, h] - m_row) * sim[c, h]
        aq = jnp.exp(il - m_row) * q_s[rows(c), h * DK:(h + 1) * DK]
        lhs[c, h] = jnp.concatenate([aq.astype(BF16), wts.astype(BF16)], axis=1)
        emr[c, h] = jnp.exp(-m_row)
    fill(1)
    for (c, h) in pairs:
        upd[c, h] = _dot_tn(wk_s[rows(c), h * DK:(h + 1) * DK], vo_s[rows(c), h * CN:(h + 1) * CN])
    fill(1)
    res = {}
    for h in range(HEADS):
        state = cn_ref[h]
        for c in range(nc):
            rhs = jnp.concatenate([state.astype(BF16), vo_s[rows(c), h * CN:(h + 1) * CN]], axis=0)
            res[c, h] = _dot(lhs[c, h], rhs)
            a = a_chunk[c][h:h + 1, :]
            state = jnp.concatenate([a] * (CN // LANES), axis=1) * state + upd[c, h]
        cn_ref[h] = state
        fill(1)
    for (c, h) in pairs:
        inv = 1.0 / jnp.maximum(jnp.abs(res[c, h][:, DV:]), emr[c, h])
        hm_s[rows(c), h * DV:h * DV + LANES] = res[c, h][:, :LANES] * inv
        hm_s[rows(c), h * DV + LANES:(h + 1) * DV] = res[c, h][:, LANES:DV] * inv
    fill(1)

    finish_groups(4)
    att, gupd = {}, {}
    zblk = jnp.zeros((CHUNK, DK), BF16)
    for (c, h) in pairs:
        dsl = slice(h * DK, (h + 1) * DK)
        ra, rb = half(c, 0), half(c, 1)
        qp = jnp.concatenate([jnp.concatenate([gq64_s[ra, dsl], zblk], axis=1),
                              jnp.concatenate([zblk, gq64_s[rb, dsl]], axis=1)], axis=0)
        kp = jnp.concatenate([jnp.concatenate([gki_s[ra, dsl], gke64_s[ra, dsl]], axis=1),
                              jnp.concatenate([zblk, gki_s[rb, dsl]], axis=1)], axis=0)
        att[c, h] = jnp.where(causal, _dot_nt(qp, kp), 0.0).astype(BF16)
    fill(1)
    for (c, h) in pairs:
        gupd[c, h] = _dot_tn(gke128_s[rows(c), h * DK:(h + 1) * DK], gv_s[rows(c), h * DV:(h + 1) * DV])
    fill(1)
    for h in range(HEADS):
        sst = s_ref[h]
        for c in range(nc):
            lhs_g = jnp.concatenate([gq128_s[rows(c), h * DK:(h + 1) * DK], att[c, h]], axis=1)
            rhs_g = jnp.concatenate([sst.astype(BF16), gv_s[rows(c), h * DV:(h + 1) * DV]], axis=0)
            hg_s[rows(c), h * DV:(h + 1) * DV] = _dot(lhs_g, rhs_g)
            i = c * HEADS + h
            sst = e_cols[:, i:i + 1] * sst + gupd[c, h]
        s_ref[h] = sst
        fill(1)

    if meta:
        h1_ref[...] = x
        return

    def head_norm(src, gain_ref):
        outs = []
        for h in range(HEADS):
            t = src[:, h * DV:(h + 1) * DV]
            outs.append(t * lax.rsqrt(jnp.mean(t * t, axis=-1, keepdims=True) + EPS))
        return jnp.concatenate(outs, axis=-1) * gain_ref[...]

    finish_groups(len(groups))
    y_m = head_norm(hm_s, mhg_ref) * ga_s[...]
    merged = gb_s[...] * _dot(y_m.astype(BF16), merge_w[0][...])
    y_g = head_norm(hg_s, ghg_ref) * gc_s[...]
    merged = merged + gd_s[...] * _dot(y_g.astype(BF16), merge_w[1][...])
    h1_ref[...] = x + _dot(merged.astype(BF16), merge_w[2][...])


def _const_spec(shape):
    nd = len(shape)
    return pl.BlockSpec(shape, lambda i: (0,) * nd, pipeline_mode=pl.Buffered(1))


def _mixer_call(x2, consts, state, *, tm, meta):
    t = x2.shape[0]
    state_shapes = [jax.ShapeDtypeStruct(a.shape, F32) for a in state]
    row_spec = pl.BlockSpec((tm, D_MODEL), lambda i: (i, 0))
    state_specs = [pl.BlockSpec(a.shape, lambda i, nd=a.ndim: (0,) * nd) for a in state]
    return pl.pallas_call(
        functools.partial(_mixer_kernel, tm=tm, meta=meta),
        grid=(t // tm,),
        in_specs=[row_spec] + [pl.BlockSpec(memory_space=pl.ANY) if (i in LEAD_UNUSED and not meta)
                               else _const_spec(c.shape) for i, c in enumerate(consts)] + state_specs,
        out_specs=[row_spec] + state_specs,
        out_shape=[jax.ShapeDtypeStruct((t, D_MODEL), F32)] + state_shapes,
        scratch_shapes=[
            pltpu.VMEM((tm, D_MODEL), BF16),
            pltpu.VMEM((tm + SUBLANES, 2 * QK), F32),
            pltpu.VMEM((tm, QK), F32),
            pltpu.VMEM((tm, QK), BF16),
            pltpu.VMEM((tm, QK), BF16),
            pltpu.VMEM((tm, HEADS * CN), BF16),
            pltpu.VMEM((tm, QK), BF16),
            pltpu.VMEM((tm, QK), BF16),
            pltpu.VMEM((tm, QK), BF16),
            pltpu.VMEM((tm, QK), BF16),
            pltpu.VMEM((tm, QK), BF16),
            pltpu.VMEM((tm, D_MODEL), BF16),
            pltpu.VMEM((tm, D_MODEL), F32),
            pltpu.VMEM((tm, D_MODEL), F32),
            pltpu.VMEM((tm, D_MODEL), F32),
            pltpu.VMEM((tm, D_MODEL), F32),
            pltpu.VMEM((tm, D_MODEL), F32),
            pltpu.VMEM((tm, D_MODEL), F32),
            pltpu.VMEM((tm, D_MODEL), F32),
            pltpu.VMEM((RCHUNK, RCHUNK), BF16),
            pltpu.VMEM((tm, tm), BF16),
            pltpu.VMEM((LANES, 3 * HEADS * LANES), BF16),
        ] + ([] if meta else [
            pltpu.VMEM((D_MODEL, D_MODEL), BF16),
            pltpu.VMEM((D_MODEL, D_MODEL), BF16),
            pltpu.VMEM((D_MODEL, D_MODEL), BF16),
            pltpu.VMEM((2, MERGE_ROWS, D_MODEL), F32),
            pltpu.SemaphoreType.DMA((2,)),
        ]),
        compiler_params=pltpu.CompilerParams(
            dimension_semantics=("arbitrary",), vmem_limit_bytes=VMEM_LIMIT),
        name="mixer_meta" if meta else "mixer",
    )(x2, *consts, *state)


def _ffn_kernel(h_ref, g2_ref, wg_hbm, wu_hbm, wd_hbm, gf_ref, o_ref,
                wg_ref, wu_ref, wd_ref, sg, su, sd, sem, *, fc, parts):
    n_chunks = wg_ref.shape[1] // fc

    def chunk_copies(f, slot):
        cols = pl.ds(f * fc, fc)
        return (pltpu.make_async_copy(wg_hbm.at[:, cols], sg.at[slot], sem.at[slot, 0]),
                pltpu.make_async_copy(wu_hbm.at[:, cols], su.at[slot], sem.at[slot, 1]),
                pltpu.make_async_copy(wd_hbm.at[cols, :], sd.at[slot], sem.at[slot, 2]))

    @pl.when(pl.program_id(0) == 0)
    def _():
        for cp in chunk_copies(0, 0):
            cp.start()
        for f in range(n_chunks):
            slot = f % 2
            if f + 1 < n_chunks:
                for cp in chunk_copies(f + 1, 1 - slot):
                    cp.start()
            for cp in chunk_copies(f, slot):
                cp.wait()
            wg_ref[:, f * fc:(f + 1) * fc] = sg[slot].astype(BF16)
            wu_ref[:, f * fc:(f + 1) * fc] = su[slot].astype(BF16)
            wd_ref[f * fc:(f + 1) * fc, :] = sd[slot].astype(BF16)

    rows = h_ref.shape[0] // parts
    blocks = [slice(p * rows, (p + 1) * rows) for p in range(parts)]
    acc, hn = [], []
    for blk in blocks:
        h = h_ref[blk, :]
        hn.append((h * lax.rsqrt(jnp.mean(h * h, axis=-1, keepdims=True) + EPS) * g2_ref[...]).astype(BF16))
        acc.append(h)
    for f in range(0, wg_ref.shape[1], fc):
        for p in range(parts):
            gate = _dot(hn[p], wg_ref[:, f:f + fc])
            up = _dot(hn[p], wu_ref[:, f:f + fc])
            acc[p] = acc[p] + _dot((gate * _sigmoid(gate) * up).astype(BF16), wd_ref[f:f + fc, :])
    for p, blk in enumerate(blocks):
        a = acc[p]
        o_ref[blk, :] = a * lax.rsqrt(jnp.mean(a * a, axis=-1, keepdims=True) + EPS) * gf_ref[...]


def _ffn_call(h1, g2, wg, wu, wd, gf, *, tm, fc, parts):
    t = h1.shape[0]
    d_ff = wg.shape[1]
    row_spec = pl.BlockSpec((tm, D_MODEL), lambda i: (i, 0))
    hbm_spec = pl.BlockSpec(memory_space=pl.ANY)
    return pl.pallas_call(
        functools.partial(_ffn_kernel, fc=fc, parts=parts),
        grid=(t // tm,),
        in_specs=[row_spec, _const_spec(g2.shape), hbm_spec, hbm_spec, hbm_spec, _const_spec(gf.shape)],
        out_specs=row_spec,
        out_shape=jax.ShapeDtypeStruct((t, D_MODEL), F32),
        scratch_shapes=[
            pltpu.VMEM((D_MODEL, d_ff), BF16),
            pltpu.VMEM((D_MODEL, d_ff), BF16),
            pltpu.VMEM((d_ff, D_MODEL), BF16),
            pltpu.VMEM((2, D_MODEL, fc), F32),
            pltpu.VMEM((2, D_MODEL, fc), F32),
            pltpu.VMEM((2, fc, D_MODEL), F32),
            pltpu.SemaphoreType.DMA((2, 3)),
        ],
        compiler_params=pltpu.CompilerParams(
            dimension_semantics=("arbitrary",), vmem_limit_bytes=VMEM_LIMIT),
        name="ffn",
    )(h1, g2, wg, wu, wd, gf)


def _repack_kernel(wt_hbm, w_ref, wst_ref, stage, narrow, sem, nsem):
    p = pl.program_id(0)
    n = pl.num_programs(0)

    def piece_copy(q, slot):
        (_, a_end), (b_start, b_end), (c_start, _) = W_IN_RANGES
        first_b, first_c = a_end // PIECE, (a_end + b_end - b_start) // PIECE
        row0 = (q * PIECE + jnp.where(q >= first_b, b_start - a_end, 0)
                + jnp.where(q >= first_c, c_start - b_end, 0))
        return pltpu.make_async_copy(wt_hbm.at[pl.ds(pl.multiple_of(row0, SUBLANES), PIECE), :],
                                     stage.at[slot], sem.at[slot])

    def narrow_copies():
        return (pltpu.make_async_copy(wt_hbm.at[pl.ds(MI_COL, SUBLANES), :], narrow.at[pl.ds(0, SUBLANES), :],
                                      nsem.at[0]),
                pltpu.make_async_copy(wt_hbm.at[pl.ds(GA_COL, G_RANK), :], narrow.at[pl.ds(SUBLANES, G_RANK), :],
                                      nsem.at[1]))

    @pl.when(p == 0)
    def _():
        for q in range(REPACK_AHEAD):
            piece_copy(q, q).start()
        for cp in narrow_copies():
            cp.start()

    @pl.when(p + REPACK_AHEAD < n)
    def _():
        piece_copy(p + REPACK_AHEAD, (p + REPACK_AHEAD) % REPACK_SLOTS).start()

    piece_copy(p, p % REPACK_SLOTS).wait()
    w_ref[...] = stage[p % REPACK_SLOTS].T.astype(BF16)

    @pl.when(p == 0)
    def _():
        for cp in narrow_copies():
            cp.wait()
        gates = narrow[0:SUBLANES, :]
        zeros = jnp.zeros((SUBLANES - HEADS, D_MODEL), F32)
        wst_ref[...] = jnp.concatenate([gates[:HEADS], zeros, gates[HEADS:], zeros, narrow[SUBLANES:, :]],
                                       axis=0).astype(BF16)


def _repack_w_in(w):
    wt = w.T
    return pl.pallas_call(
        _repack_kernel,
        grid=(N_PACK // PIECE,),
        in_specs=[pl.BlockSpec(memory_space=pl.ANY)],
        out_specs=[pl.BlockSpec((D_MODEL, PIECE), lambda p: (0, p)),
                   pl.BlockSpec((2 * SUBLANES + G_RANK, D_MODEL), lambda p: (0, 0))],
        out_shape=[jax.ShapeDtypeStruct((D_MODEL, N_PACK), BF16),
                   jax.ShapeDtypeStruct((2 * SUBLANES + G_RANK, D_MODEL), BF16)],
        scratch_shapes=[pltpu.VMEM((REPACK_SLOTS, PIECE, D_MODEL), F32),
                        pltpu.VMEM((SUBLANES + G_RANK, D_MODEL), F32),
                        pltpu.SemaphoreType.DMA((REPACK_SLOTS,)), pltpu.SemaphoreType.DMA((2,))],
        compiler_params=pltpu.CompilerParams(dimension_semantics=("arbitrary",)),
        name="repack_w_in",
    )(wt)


def kernel(x, meta_tokens, norm1_g, w_in, conv_w, conv_b, m_gate_b, g_a2, g_a2_b, m_head_g, g_head_g,
           w_branch_m, w_branch_g, w_out, norm2_g, w_ff_gate, w_ff_up, w_ff_down, final_g):
    bsz, seq, d = x.shape
    assert bsz == 1 and d == D_MODEL and norm1_g.shape[0] == 1 and seq % CHUNK == 0
    row = lambda a: a.reshape(1, -1).astype(F32)

    gate_bias = jnp.zeros((2 * SUBLANES, LANES), F32)
    gate_bias = gate_bias.at[:HEADS].set(m_gate_b[0, 0][:, None]).at[SUBLANES:SUBLANES + HEADS].set(m_gate_b[0, 1][:, None])
    a2 = jnp.zeros((LANES, QK), F32).at[GA_LANE:GA_LANE + G_RANK].set(g_a2[0]).astype(BF16)
    conv_w8 = jnp.zeros((SUBLANES, 2 * QK), F32).at[:CONV_W].set(conv_w[0])
    consts = (row(norm1_g[0]), *_repack_w_in(w_in[0]), gate_bias, a2, row(g_a2_b[0]), conv_w8, row(conv_b[0]),
              row(m_head_g[0]), row(g_head_g[0]),
              w_branch_m[0].astype(F32), w_branch_g[0].astype(F32), w_out[0].astype(F32))

    zero_state = (jnp.zeros((SUBLANES, 2 * QK), F32), jnp.zeros((HEADS, DK, CN), F32),
                  jnp.zeros((SUBLANES, LANES), F32), jnp.zeros((HEADS, DK, DV), F32))
    lead = jnp.concatenate([jnp.zeros((RCHUNK - N_META, d), F32), meta_tokens.astype(F32)], axis=0)
    unused = jnp.zeros((SUBLANES, LANES), BF16)
    lead_consts = tuple(unused if i in LEAD_UNUSED else c for i, c in enumerate(consts))
    _, *state = _mixer_call(lead, lead_consts, zero_state, tm=RCHUNK, meta=True)
    h1, *_ = _mixer_call(x[0], consts, tuple(state), tm=512, meta=False)
    out = _ffn_call(h1, row(norm2_g[0]), w_ff_gate[0].astype(F32), w_ff_up[0].astype(F32),
                    w_ff_down[0].astype(F32), row(final_g), tm=1024, fc=256, parts=2)
    return out[None]
```

```python
import functools

import jax
import jax.numpy as jnp
from jax import lax
from jax.experimental import pallas as pl
from jax.experimental.pallas import tpu as pltpu

F32 = jnp.float32
BF16 = jnp.bfloat16

D_MODEL = 1024
N_META = 16
CHUNK = 64
RCHUNK = 128
EPS = 1e-6
HEADS = 4
DV = D_MODEL // HEADS
DK = DV // 2
QK = HEADS * DK
G_RANK = 16
G_TAU = 16.0
CONV_W = 4
LANES = 128
SUBLANES = 8
BF16_ROWS = 16
NEG_BIG = -1e30
VMEM_LIMIT = 60 * 1024 * 1024
CN = DV + LANES

W_IN_RANGES = ((0, 2048), (2056, 5128), (5144, 8216))
N_PACK = sum(b - a for a, b in W_IN_RANGES)
O_MQK, O_MV, O_MO = 0, 1024, 2048
O_GQ, O_GK, O_GV = 3072, 3584, 4096
O_GR, O_GATE_M, O_GATE_G = 5120, 6144, 7168
MI_COL, MF_COL, GA_COL = 2048, 2052, 5128
GA_LANE = 8
PIECE = 512
MERGE_ROWS = 256
REPACK_AHEAD = 3
REPACK_SLOTS = REPACK_AHEAD + 1
LEAD_UNUSED = (10, 11, 12)

NT_DIMS = (((1,), (1,)), ((), ()))
TN_DIMS = (((0,), (0,)), ((), ()))


def _dot(a, b):
    return jnp.dot(a, b, preferred_element_type=F32)


def _dot_nt(a, b):
    return lax.dot_general(a, b, NT_DIMS, preferred_element_type=F32)


def _dot_tn(a, b):
    return lax.dot_general(a, b, TN_DIMS, preferred_element_type=F32)


def _pieces(x):
    hi = x.astype(BF16).astype(F32)
    mid = (x - hi).astype(BF16).astype(F32)
    lo = (x - hi - mid).astype(BF16).astype(F32)
    return [hi, mid, lo]


def _log_sigmoid(x):
    return jnp.minimum(x, 0.0) - jnp.log(1.0 + jnp.exp(-jnp.abs(x)))


def _sigmoid(x):
    return 1.0 / (1.0 + jnp.exp(-x))


def _mixer_kernel(x_ref, g1_ref, w_ref, wst_ref, sb_ref, a2_ref, a2b_ref, cw_ref, cb_ref,
                  mhg_ref, ghg_ref, wbm_ref, wbg_ref, wout_ref, tail0_ref, cn0_ref, m0_ref, s0_ref,
                  h1_ref, tail_ref, cn_ref, m_ref, s_ref,
                  xn_s, qk_s, q_s, k_s, wk_s, vo_s, gq64_s, gq128_s, gki_s, gke64_s, gke128_s, gv_s, hm_s, hg_s,
                  ga_s, gb_s, gc_s, gd_s, gqk_s, tric_s, triu_s, spread_s, *merge_scratch, tm, meta):
    nc = tm // RCHUNK
    pairs = [(c, h) for c in range(nc) for h in range(HEADS)]
    if not meta:
        *merge_w, wstage, wsem = merge_scratch

    @pl.when(pl.program_id(0) == 0)
    def _():
        qk_s[0:SUBLANES, :] = tail0_ref[...]
        cn_ref[...] = cn0_ref[...]
        m_ref[...] = m0_ref[...]
        s_ref[...] = s0_ref[...]
        for h in range(HEADS):
            vo_s[:, h * CN + DV:(h + 1) * CN] = jnp.ones((tm, LANES), BF16)
        if not meta:
            srcs = (wbm_ref, wbg_ref, wout_ref)
            n_blk = D_MODEL // MERGE_ROWS

            def block_copy(j, slot):
                return pltpu.make_async_copy(srcs[j // n_blk].at[pl.ds((j % n_blk) * MERGE_ROWS, MERGE_ROWS), :],
                                             wstage.at[slot], wsem.at[slot])

            block_copy(0, 0).start()
            for j in range(len(srcs) * n_blk):
                slot = j % 2
                if j + 1 < len(srcs) * n_blk:
                    block_copy(j + 1, 1 - slot).start()
                block_copy(j, slot).wait()
                r0 = (j % n_blk) * MERGE_ROWS
                merge_w[j // n_blk][r0:r0 + MERGE_ROWS, :] = wstage[slot].astype(BF16)
        row = lax.broadcasted_iota(jnp.int32, (tm, tm), 0)
        col = lax.broadcasted_iota(jnp.int32, (tm, tm), 1)
        tric_s[...] = jnp.where((col <= row) & (col >= (row & -CHUNK)), 1.0, 0.0)[:RCHUNK, :RCHUNK].astype(BF16)
        triu_s[...] = jnp.where((row <= col) & (row >= (col & -RCHUNK)), 1.0, 0.0).astype(BF16)
        sr = lax.broadcasted_iota(jnp.int32, (LANES, 2 * HEADS * LANES), 0)
        sc = lax.broadcasted_iota(jnp.int32, (LANES, 2 * HEADS * LANES), 1)
        blk = (sc >> 9) * (3 * SUBLANES)
        spread_s[...] = jnp.where(
            (sr >= blk) & (sr < blk + 3 * SUBLANES) & ((sr & (SUBLANES - 1)) == ((sc >> 7) & (HEADS - 1))),
            1.0, 0.0).astype(BF16)

    x = x_ref[...]
    xn = x * lax.rsqrt(jnp.mean(x * x, axis=-1, keepdims=True) + EPS) * g1_ref[...]
    if meta:
        valid = lax.broadcasted_iota(jnp.int32, (tm, 1), 0) >= (tm - N_META)
        xn = jnp.where(valid, xn, 0.0)
    xn_s[...] = xn.astype(BF16)

    def proj(off, width):
        return _dot(xn_s[...], w_ref[:, off:off + width])

    def rows(c):
        return slice(c * RCHUNK, (c + 1) * RCHUNK)

    def half(c, second):
        lo = c * RCHUNK + (CHUNK if second else 0)
        return slice(lo, lo + CHUNK)

    crow = lax.broadcasted_iota(jnp.int32, (RCHUNK, RCHUNK), 0)
    ccol = lax.broadcasted_iota(jnp.int32, (RCHUNK, RCHUNK), 1)
    causal = ccol <= crow
    tri_c = tric_s[...]
    tri_up = triu_s[...]
    spread = spread_s[...]

    def st_qk(i, v):
        qk_s[SUBLANES:SUBLANES + tm, i * DV:(i + 1) * DV] = v

    def st_mv(i, v):
        vo_s[:, i * CN:i * CN + DV] = v.astype(BF16)

    def st_gqk(i, v):
        gqk_s[:, i * DV:(i + 1) * DV] = v

    def st_gv(i, v):
        gv_s[:, i * DV:(i + 1) * DV] = v.astype(BF16)

    def st_mo(i, v):
        ga_s[:, i * DV:(i + 1) * DV] = _sigmoid(v)

    def st_gm(i, v):
        gb_s[:, i * DV:(i + 1) * DV] = _sigmoid(v)

    def st_gr(i, v):
        gc_s[:, i * DV:(i + 1) * DV] = v * _sigmoid(v)

    def st_gg(i, v):
        gd_s[:, i * DV:(i + 1) * DV] = _sigmoid(v)

    groups = [(O_MQK, st_qk), (O_MV, st_mv), (O_GQ, st_gqk), (O_GV, st_gv)]
    if not meta:
        groups += [(O_MO, st_mo), (O_GATE_M, st_gm), (O_GR, st_gr), (O_GATE_G, st_gg)]
    queue = [(seg, i, store) for seg, store in groups for i in range(D_MODEL // DV)]
    emitted = [0]

    def fill(n):
        for _ in range(n):
            if emitted[0] < len(queue):
                seg, i, store = queue[emitted[0]]
                store(i, proj(seg + i * DV, DV))
                emitted[0] += 1

    def finish_groups(n):
        fill(n * (D_MODEL // DV) - emitted[0])

    narrow = _dot_nt(wst_ref[...], xn_s[...])
    gates = narrow[:2 * SUBLANES, :] + jnp.concatenate([sb_ref[...]] * (tm // LANES), axis=1)
    fill(2)
    s1 = jnp.concatenate([jnp.zeros((GA_LANE, tm), F32), narrow[2 * SUBLANES:, :],
                          jnp.zeros((LANES - GA_LANE - G_RANK, tm), F32)], axis=0).T
    logi = gates[:SUBLANES, :]
    logf = _log_sigmoid(gates[SUBLANES:, :])
    if meta:
        valid_t = lax.broadcasted_iota(jnp.int32, (1, tm), 1) >= (tm - N_META)
        logi = jnp.where(valid_t, logi, NEG_BIG)
        logf = jnp.where(valid_t, logf, 0.0)
    zero8 = jnp.zeros((SUBLANES, tm), F32)
    b4 = _dot(jnp.concatenate(_pieces(logf) + [zero8], axis=0).astype(BF16), tri_up)
    fill(2)
    b_all = b4[:SUBLANES] + b4[SUBLANES:2 * SUBLANES] + b4[2 * SUBLANES:3 * SUBLANES]
    c_all = logi - b_all
    m_run = m_ref[...]
    m_in, w_parts, a_chunk = [], [], []
    for c in range(nc):
        b_c = b_all[:, rows(c)]
        g_c = jnp.broadcast_to(b_c[:, RCHUNK - 1:RCHUNK], (SUBLANES, LANES))
        wlog = g_c + c_all[:, rows(c)]
        m_next = jnp.maximum(g_c + m_run, jnp.max(wlog, axis=1, keepdims=True))
        m_in.append(m_run)
        w_parts.append(jnp.exp(wlog - m_next))
        a_chunk.append(jnp.exp(g_c + m_run - m_next))
        m_run = m_next
    m_ref[...] = m_run
    stacked = jnp.concatenate(
        _pieces(b_all) + _pieces(jnp.concatenate(w_parts, axis=1))
        + [jnp.zeros((LANES - 6 * SUBLANES, tm), F32)], axis=0)
    stacked_t = stacked.T.astype(BF16)
    fill(2)
    tiles = _dot(stacked_t, spread)
    fill(2)
    b_t, w_t = tiles[:, :QK], tiles[:, QK:]

    finish_groups(1)
    for blk in range(2 * QK // DV):
        csl = slice(blk * DV, (blk + 1) * DV)
        conv = cb_ref[:, csl]
        for j in range(CONV_W):
            conv = conv + cw_ref[j:j + 1, csl] * qk_s[pl.ds(SUBLANES - (CONV_W - 1) + j, tm), csl]
        conv = conv * _sigmoid(conv)
        if blk < QK // DV:
            q_s[:, csl] = conv * (DK ** -0.5)
        else:
            ksl = slice(blk * DV - QK, (blk + 1) * DV - QK)
            k_s[:, ksl] = conv.astype(BF16)
            wk_s[:, ksl] = (w_t[:, ksl] * conv).astype(BF16)
        fill(1)
    tail_new = qk_s[tm:tm + SUBLANES, :]
    qk_s[0:SUBLANES, :] = tail_new
    tail_ref[...] = tail_new

    za = _dot(s1.astype(BF16), a2_ref[...]) + a2b_ref[...]
    fill(1)
    loga = _log_sigmoid(za) / G_TAU
    if meta:
        loga = jnp.where(valid, loga, 0.0)
    hi = loga.astype(BF16)
    r1 = loga - hi.astype(F32)
    mid = r1.astype(BF16)
    bc = jnp.concatenate([_dot(tri_c, hi[rows(c), :]) + _dot(tri_c, mid[rows(c), :])
                          for c in range(nc)], axis=0)
    fill(2)
    e_blk = jnp.exp(bc)
    tot_a = [bc[c * RCHUNK + CHUNK - 1:c * RCHUNK + CHUNK, :] for c in range(nc)]
    tot_b = [bc[(c + 1) * RCHUNK - 1:(c + 1) * RCHUNK, :] for c in range(nc)]
    e_a = [jnp.exp(t) for t in tot_a]
    e_b = [jnp.exp(t) for t in tot_b]
    rest = jnp.concatenate([t - bc[half(c, s), :] for c in range(nc) for s, t in ((0, tot_a[c]), (1, tot_b[c]))],
                           axis=0)
    e_rest = jnp.exp(rest)
    ones_row = jnp.ones((1, QK), F32)
    to_chunk = jnp.concatenate([jnp.broadcast_to(f, (CHUNK, QK)) for c in range(nc) for f in (ones_row, e_a[c])],
                               axis=0)
    from_blk = jnp.concatenate([jnp.broadcast_to(f, (CHUNK, QK)) for c in range(nc) for f in (e_b[c], ones_row)],
                               axis=0)
    finish_groups(3)
    gq = gqk_s[:, :QK] * (DK ** -0.5)
    gk = gqk_s[:, QK:]
    gq64_s[...] = (gq * e_blk).astype(BF16)
    gq128_s[...] = (gq * e_blk * to_chunk).astype(BF16)
    gki_s[...] = (gk * jnp.exp(-bc)).astype(BF16)
    gke64_s[...] = (gk * e_rest).astype(BF16)
    gke128_s[...] = (gk * e_rest * from_blk).astype(BF16)
    fill(2)
    e_rows = [(e_a[c] * e_b[c])[:, h * DK:(h + 1) * DK] for (c, h) in pairs]
    e_cols = jnp.concatenate(e_rows + [jnp.zeros((LANES - len(pairs), DK), F32)], axis=0).T

    finish_groups(2)
    dmat, rmax, sim = {}, {}, {}
    for (c, h) in pairs:
        d = b_t[rows(c), h * LANES:(h + 1) * LANES] + c_all[h:h + 1, rows(c)]
        d = jnp.where(causal, d, -jnp.inf)
        dmat[c, h] = d
        rmax[c, h] = jnp.max(d, axis=-1, keepdims=True)
    fill(1)
    for (c, h) in pairs:
        sim[c, h] = _dot_nt(q_s[rows(c), h * DK:(h + 1) * DK].astype(BF16), k_s[rows(c), h * DK:(h + 1) * DK])
    fill(1)
    lhs, emr, upd = {}, {}, {}
    for (c, h) in pairs:
        il = b_t[rows(c), h * LANES:(h + 1) * LANES] + m_in[c][h:h + 1, :]
        m_row = jnp.maximum(il, rmax[c, h])
        wts = jnp.exp(dmat[c, h] - m_row) * sim[c, h]
        aq = jnp.exp(il - m_row) * q_s[rows(c), h * DK:(h + 1) * DK]
        lhs[c, h] = jnp.concatenate([aq.astype(BF16), wts.astype(BF16)], axis=1)
        emr[c, h] = jnp.exp(-m_row)
    fill(1)
    for (c, h) in pairs:
        upd[c, h] = _dot_tn(wk_s[rows(c), h * DK:(h + 1) * DK], vo_s[rows(c), h * CN:(h + 1) * CN])
    fill(1)
    res = {}
    for h in range(HEADS):
        state = cn_ref[h]
        for c in range(nc):
            rhs = jnp.concatenate([state.astype(BF16), vo_s[rows(c), h * CN:(h + 1) * CN]], axis=0)
            res[c, h] = _dot(lhs[c, h], rhs)
            a = a_chunk[c][h:h + 1, :]
            state = jnp.concatenate([a] * (CN // LANES), axis=1) * state + upd[c, h]
        cn_ref[h] = state
        fill(1)
    for (c, h) in pairs:
        inv = 1.0 / jnp.maximum(jnp.abs(res[c, h][:, DV:]), emr[c, h])
        hm_s[rows(c), h * DV:h * DV + LANES] = res[c, h][:, :LANES] * inv
        hm_s[rows(c), h * DV + LANES:(h + 1) * DV] = res[c, h][:, LANES:DV] * inv
    fill(1)

    finish_groups(4)
    att, gupd = {}, {}
    zblk = jnp.zeros((CHUNK, DK), BF16)
    for (c, h) in pairs:
        dsl = slice(h * DK, (h + 1) * DK)
        ra, rb = half(c, 0), half(c, 1)
        qp = jnp.concatenate([jnp.concatenate([gq64_s[ra, dsl], zblk], axis=1),
                              jnp.concatenate([zblk, gq64_s[rb, dsl]], axis=1)], axis=0)
        kp = jnp.concatenate([jnp.concatenate([gki_s[ra, dsl], gke64_s[ra, dsl]], axis=1),
                              jnp.concatenate([zblk, gki_s[rb, dsl]], axis=1)], axis=0)
        att[c, h] = jnp.where(causal, _dot_nt(qp, kp), 0.0).astype(BF16)
    fill(1)
    for (c, h) in pairs:
        gupd[c, h] = _dot_tn(gke128_s[rows(c), h * DK:(h + 1) * DK], gv_s[rows(c), h * DV:(h + 1) * DV])
    fill(1)
    for h in range(HEADS):
        sst = s_ref[h]
        for c in range(nc):
            lhs_g = jnp.concatenate([gq128_s[rows(c), h * DK:(h + 1) * DK], att[c, h]], axis=1)
            rhs_g = jnp.concatenate([sst.astype(BF16), gv_s[rows(c), h * DV:(h + 1) * DV]], axis=0)
            hg_s[rows(c), h * DV:(h + 1) * DV] = _dot(lhs_g, rhs_g)
            i = c * HEADS + h
            sst = e_cols[:, i:i + 1] * sst + gupd[c, h]
        s_ref[h] = sst
        fill(1)

    if meta:
        h1_ref[...] = x
        return

    def head_norm(src, gain_ref):
        outs = []
        for h in range(HEADS):
            t = src[:, h * DV:(h + 1) * DV]
            outs.append(t * lax.rsqrt(jnp.mean(t * t, axis=-1, keepdims=True) + EPS))
        return jnp.concatenate(outs, axis=-1) * gain_ref[...]

    finish_groups(len(groups))
    y_m = head_norm(hm_s, mhg_ref) * ga_s[...]
    merged = gb_s[...] * _dot(y_m.astype(BF16), merge_w[0][...])
    y_g = head_norm(hg_s, ghg_ref) * gc_s[...]
    merged = merged + gd_s[...] * _dot(y_g.astype(BF16), merge_w[1][...])
    h1_ref[...] = x + _dot(merged.astype(BF16), merge_w[2][...])


def _const_spec(shape):
    nd = len(shape)
    return pl.BlockSpec(shape, lambda i: (0,) * nd, pipeline_mode=pl.Buffered(1))


def _mixer_call(x2, consts, state, *, tm, meta):
    t = x2.shape[0]
    state_shapes = [jax.ShapeDtypeStruct(a.shape, F32) for a in state]
    row_spec = pl.BlockSpec((tm, D_MODEL), lambda i: (i, 0))
    state_specs = [pl.BlockSpec(a.shape, lambda i, nd=a.ndim: (0,) * nd) for a in state]
    return pl.pallas_call(
        functools.partial(_mixer_kernel, tm=tm, meta=meta),
        grid=(t // tm,),
        in_specs=[row_spec] + [pl.BlockSpec(memory_space=pl.ANY) if (i in LEAD_UNUSED and not meta)
                               else _const_spec(c.shape) for i, c in enumerate(consts)] + state_specs,
        out_specs=[row_spec] + state_specs,
        out_shape=[jax.ShapeDtypeStruct((t, D_MODEL), F32)] + state_shapes,
        scratch_shapes=[
            pltpu.VMEM((tm, D_MODEL), BF16),
            pltpu.VMEM((tm + SUBLANES, 2 * QK), F32),
            pltpu.VMEM((tm, QK), F32),
            pltpu.VMEM((tm, QK), BF16),
            pltpu.VMEM((tm, QK), BF16),
            pltpu.VMEM((tm, HEADS * CN), BF16),
            pltpu.VMEM((tm, QK), BF16),
            pltpu.VMEM((tm, QK), BF16),
            pltpu.VMEM((tm, QK), BF16),
            pltpu.VMEM((tm, QK), BF16),
            pltpu.VMEM((tm, QK), BF16),
            pltpu.VMEM((tm, D_MODEL), BF16),
            pltpu.VMEM((tm, D_MODEL), F32),
            pltpu.VMEM((tm, D_MODEL), F32),
            pltpu.VMEM((tm, D_MODEL), F32),
            pltpu.VMEM((tm, D_MODEL), F32),
            pltpu.VMEM((tm, D_MODEL), F32),
            pltpu.VMEM((tm, D_MODEL), F32),
            pltpu.VMEM((tm, D_MODEL), F32),
            pltpu.VMEM((RCHUNK, RCHUNK), BF16),
            pltpu.VMEM((tm, tm), BF16),
            pltpu.VMEM((LANES, 2 * HEADS * LANES), BF16),
        ] + ([] if meta else [
            pltpu.VMEM((D_MODEL, D_MODEL), BF16),
            pltpu.VMEM((D_MODEL, D_MODEL), BF16),
            pltpu.VMEM((D_MODEL, D_MODEL), BF16),
            pltpu.VMEM((2, MERGE_ROWS, D_MODEL), F32),
            pltpu.SemaphoreType.DMA((2,)),
        ]),
        compiler_params=pltpu.CompilerParams(
            dimension_semantics=("arbitrary",), vmem_limit_bytes=VMEM_LIMIT),
        name="mixer_meta" if meta else "mixer",
    )(x2, *consts, *state)


def _ffn_kernel(h_ref, g2_ref, wg_hbm, wu_hbm, wd_hbm, gf_ref, o_ref,
                wg_ref, wu_ref, wd_ref, sg, su, sd, sem, *, fc, parts):
    n_chunks = wg_ref.shape[1] // fc

    def chunk_copies(f, slot):
        cols = pl.ds(f * fc, fc)
        return (pltpu.make_async_copy(wg_hbm.at[:, cols], sg.at[slot], sem.at[slot, 0]),
                pltpu.make_async_copy(wu_hbm.at[:, cols], su.at[slot], sem.at[slot, 1]),
                pltpu.make_async_copy(wd_hbm.at[cols, :], sd.at[slot], sem.at[slot, 2]))

    @pl.when(pl.program_id(0) == 0)
    def _():
        for cp in chunk_copies(0, 0):
            cp.start()
        for f in range(n_chunks):
            slot = f % 2
            if f + 1 < n_chunks:
                for cp in chunk_copies(f + 1, 1 - slot):
                    cp.start()
            for cp in chunk_copies(f, slot):
                cp.wait()
            wg_ref[:, f * fc:(f + 1) * fc] = sg[slot].astype(BF16)
            wu_ref[:, f * fc:(f + 1) * fc] = su[slot].astype(BF16)
            wd_ref[f * fc:(f + 1) * fc, :] = sd[slot].astype(BF16)

    rows = h_ref.shape[0] // parts
    blocks = [slice(p * rows, (p + 1) * rows) for p in range(parts)]
    acc, hn = [], []
    for blk in blocks:
        h = h_ref[blk, :]
        hn.append((h * lax.rsqrt(jnp.mean(h * h, axis=-1, keepdims=True) + EPS) * g2_ref[...]).astype(BF16))
        acc.append(h)
    for f in range(0, wg_ref.shape[1], fc):
        for p in range(parts):
            gate = _dot(hn[p], wg_ref[:, f:f + fc])
            up = _dot(hn[p], wu_ref[:, f:f + fc])
            acc[p] = acc[p] + _dot((gate * _sigmoid(gate) * up).astype(BF16), wd_ref[f:f + fc, :])
    for p, blk in enumerate(blocks):
        a = acc[p]
        o_ref[blk, :] = a * lax.rsqrt(jnp.mean(a * a, axis=-1, keepdims=True) + EPS) * gf_ref[...]


def _ffn_call(h1, g2, wg, wu, wd, gf, *, tm, fc, parts):
    t = h1.shape[0]
    d_ff = wg.shape[1]
    row_spec = pl.BlockSpec((tm, D_MODEL), lambda i: (i, 0))
    hbm_spec = pl.BlockSpec(memory_space=pl.ANY)
    return pl.pallas_call(
        functools.partial(_ffn_kernel, fc=fc, parts=parts),
        grid=(t // tm,),
        in_specs=[row_spec, _const_spec(g2.shape), hbm_spec, hbm_spec, hbm_spec, _const_spec(gf.shape)],
        out_specs=row_spec,
        out_shape=jax.ShapeDtypeStruct((t, D_MODEL), F32),
        scratch_shapes=[
            pltpu.VMEM((D_MODEL, d_ff), BF16),
            pltpu.VMEM((D_MODEL, d_ff), BF16),
            pltpu.VMEM((d_ff, D_MODEL), BF16),
            pltpu.VMEM((2, D_MODEL, fc), F32),
            pltpu.VMEM((2, D_MODEL, fc), F32),
            pltpu.VMEM((2, fc, D_MODEL), F32),
            pltpu.SemaphoreType.DMA((2, 3)),
        ],
        compiler_params=pltpu.CompilerParams(
            dimension_semantics=("arbitrary",), vmem_limit_bytes=VMEM_LIMIT),
        name="ffn",
    )(h1, g2, wg, wu, wd, gf)


def _repack_kernel(wt_hbm, w_ref, wst_ref, stage, narrow, sem, nsem):
    p = pl.program_id(0)
    n = pl.num_programs(0)

    def piece_copy(q, slot):
        (_, a_end), (b_start, b_end), (c_start, _) = W_IN_RANGES
        first_b, first_c = a_end // PIECE, (a_end + b_end - b_start) // PIECE
        row0 = (q * PIECE + jnp.where(q >= first_b, b_start - a_end, 0)
                + jnp.where(q >= first_c, c_start - b_end, 0))
        return pltpu.make_async_copy(wt_hbm.at[pl.ds(pl.multiple_of(row0, SUBLANES), PIECE), :],
                                     stage.at[slot], sem.at[slot])

    def narrow_copies():
        return (pltpu.make_async_copy(wt_hbm.at[pl.ds(MI_COL, SUBLANES), :], narrow.at[pl.ds(0, SUBLANES), :],
                                      nsem.at[0]),
                pltpu.make_async_copy(wt_hbm.at[pl.ds(GA_COL, G_RANK), :], narrow.at[pl.ds(SUBLANES, G_RANK), :],
                                      nsem.at[1]))

    @pl.when(p == 0)
    def _():
        for q in range(REPACK_AHEAD):
            piece_copy(q, q).start()
        for cp in narrow_copies():
            cp.start()

    @pl.when(p + REPACK_AHEAD < n)
    def _():
        piece_copy(p + REPACK_AHEAD, (p + REPACK_AHEAD) % REPACK_SLOTS).start()

    piece_copy(p, p % REPACK_SLOTS).wait()
    w_ref[...] = stage[p % REPACK_SLOTS].T.astype(BF16)

    @pl.when(p == 0)
    def _():
        for cp in narrow_copies():
            cp.wait()
        gates = narrow[0:SUBLANES, :]
        zeros = jnp.zeros((SUBLANES - HEADS, D_MODEL), F32)
        wst_ref[...] = jnp.concatenate([gates[:HEADS], zeros, gates[HEADS:], zeros, narrow[SUBLANES:, :]],
                                       axis=0).astype(BF16)


def _repack_w_in(w):
    wt = w.T
    return pl.pallas_call(
        _repack_kernel,
        grid=(N_PACK // PIECE,),
        in_specs=[pl.BlockSpec(memory_space=pl.ANY)],
        out_specs=[pl.BlockSpec((D_MODEL, PIECE), lambda p: (0, p)),
                   pl.BlockSpec((2 * SUBLANES + G_RANK, D_MODEL), lambda p: (0, 0))],
        out_shape=[jax.ShapeDtypeStruct((D_MODEL, N_PACK), BF16),
                   jax.ShapeDtypeStruct((2 * SUBLANES + G_RANK, D_MODEL), BF16)],
        scratch_shapes=[pltpu.VMEM((REPACK_SLOTS, PIECE, D_MODEL), F32),
                        pltpu.VMEM((SUBLANES + G_RANK, D_MODEL), F32),
                        pltpu.SemaphoreType.DMA((REPACK_SLOTS,)), pltpu.SemaphoreType.DMA((2,))],
        compiler_params=pltpu.CompilerParams(dimension_semantics=("arbitrary",)),
        name="repack_w_in",
    )(wt)


def kernel(x, meta_tokens, norm1_g, w_in, conv_w, conv_b, m_gate_b, g_a2, g_a2_b, m_head_g, g_head_g,
           w_branch_m, w_branch_g, w_out, norm2_g, w_ff_gate, w_ff_up, w_ff_down, final_g):
    bsz, seq, d = x.shape
    assert bsz == 1 and d == D_MODEL and norm1_g.shape[0] == 1 and seq % CHUNK == 0
    row = lambda a: a.reshape(1, -1).astype(F32)

    gate_bias = jnp.zeros((2 * SUBLANES, LANES), F32)
    gate_bias = gate_bias.at[:HEADS].set(m_gate_b[0, 0][:, None]).at[SUBLANES:SUBLANES + HEADS].set(m_gate_b[0, 1][:, None])
    a2 = jnp.zeros((LANES, QK), F32).at[GA_LANE:GA_LANE + G_RANK].set(g_a2[0]).astype(BF16)
    conv_w8 = jnp.zeros((SUBLANES, 2 * QK), F32).at[:CONV_W].set(conv_w[0])
    consts = (row(norm1_g[0]), *_repack_w_in(w_in[0]), gate_bias, a2, row(g_a2_b[0]), conv_w8, row(conv_b[0]),
              row(m_head_g[0]), row(g_head_g[0]),
              w_branch_m[0].astype(F32), w_branch_g[0].astype(F32), w_out[0].astype(F32))

    zero_state = (jnp.zeros((SUBLANES, 2 * QK), F32), jnp.zeros((HEADS, DK, CN), F32),
                  jnp.zeros((SUBLANES, LANES), F32), jnp.zeros((HEADS, DK, DV), F32))
    lead = jnp.concatenate([jnp.zeros((RCHUNK - N_META, d), F32), meta_tokens.astype(F32)], axis=0)
    unused = jnp.zeros((SUBLANES, LANES), BF16)
    lead_consts = tuple(unused if i in LEAD_UNUSED else c for i, c in enumerate(consts))
    _, *state = _mixer_call(lead, lead_consts, zero_state, tm=RCHUNK, meta=True)
    h1, *_ = _mixer_call(x[0], consts, tuple(state), tm=512, meta=False)
    out = _ffn_call(h1, row(norm2_g[0]), w_ff_gate[0].astype(F32), w_ff_up[0].astype(F32),
                    w_ff_down[0].astype(F32), row(final_g), tm=1024, fc=256, parts=2)
    return out[None]
```

```python
import functools

import jax
import jax.numpy as jnp
from jax import lax
from jax.experimental import pallas as pl
from jax.experimental.pallas import tpu as pltpu

F32 = jnp.float32
BF16 = jnp.bfloat16

D_MODEL = 1024
N_META = 16
CHUNK = 64
RCHUNK = 128
EPS = 1e-6
HEADS = 4
DV = D_MODEL // HEADS
DK = DV // 2
QK = HEADS * DK
G_RANK = 16
G_TAU = 16.0
CONV_W = 4
LANES = 128
SUBLANES = 8
BF16_ROWS = 16
NEG_BIG = -1e30
VMEM_LIMIT = 60 * 1024 * 1024
CN = DV + LANES

W_IN_RANGES = ((0, 2048), (2056, 5128), (5144, 8216))
N_PACK = sum(b - a for a, b in W_IN_RANGES)
O_MQK, O_MV, O_MO = 0, 1024, 2048
O_GQ, O_GK, O_GV = 3072, 3584, 4096
O_GR, O_GATE_M, O_GATE_G = 5120, 6144, 7168
MI_COL, MF_COL, GA_COL = 2048, 2052, 5128
GA_LANE = 8
PIECE = 512
MERGE_ROWS = 256
REPACK_AHEAD = 3
REPACK_SLOTS = REPACK_AHEAD + 1
LEAD_UNUSED = (10, 11, 12)

NT_DIMS = (((1,), (1,)), ((), ()))
TN_DIMS = (((0,), (0,)), ((), ()))


def _dot(a, b):
    return jnp.dot(a, b, preferred_element_type=F32)


def _dot_nt(a, b):
    return lax.dot_general(a, b, NT_DIMS, preferred_element_type=F32)


def _dot_tn(a, b):
    return lax.dot_general(a, b, TN_DIMS, preferred_element_type=F32)


def _pieces(x):
    hi = x.astype(BF16).astype(F32)
    mid = (x - hi).astype(BF16).astype(F32)
    lo = (x - hi - mid).astype(BF16).astype(F32)
    return [hi, mid, lo]


def _log_sigmoid(x):
    return jnp.minimum(x, 0.0) - jnp.log(1.0 + jnp.exp(-jnp.abs(x)))


def _sigmoid(x):
    return 1.0 / (1.0 + jnp.exp(-x))


def _repack_w_in(wt_hbm, w_ref, wst_ref, stage, narrow, sem, nsem):
    (_, a_end), (b_start, b_end), (c_start, _) = W_IN_RANGES
    first_b, first_c = a_end // PIECE, (a_end + b_end - b_start) // PIECE
    n_pieces = N_PACK // PIECE

    def piece_copy(q, slot):
        row0 = q * PIECE + (b_start - a_end if q >= first_b else 0) + (c_start - b_end if q >= first_c else 0)
        return pltpu.make_async_copy(wt_hbm.at[pl.ds(row0, PIECE), :], stage.at[slot], sem.at[slot])

    narrow_copies = (
        pltpu.make_async_copy(wt_hbm.at[pl.ds(MI_COL, SUBLANES), :], narrow.at[pl.ds(0, SUBLANES), :], nsem.at[0]),
        pltpu.make_async_copy(wt_hbm.at[pl.ds(GA_COL, G_RANK), :], narrow.at[pl.ds(SUBLANES, G_RANK), :], nsem.at[1]))

    for q in range(REPACK_AHEAD):
        piece_copy(q, q).start()
    for cp in narrow_copies:
        cp.start()
    for q in range(n_pieces):
        if q + REPACK_AHEAD < n_pieces:
            piece_copy(q + REPACK_AHEAD, (q + REPACK_AHEAD) % REPACK_SLOTS).start()
        piece_copy(q, q % REPACK_SLOTS).wait()
        w_ref[:, q * PIECE:(q + 1) * PIECE] = stage[q % REPACK_SLOTS].T.astype(BF16)
    for cp in narrow_copies:
        cp.wait()
    gates = narrow[0:SUBLANES, :]
    zeros = jnp.zeros((SUBLANES - HEADS, D_MODEL), F32)
    wst_ref[...] = jnp.concatenate([gates[:HEADS], zeros, gates[HEADS:], zeros, narrow[SUBLANES:, :]],
                                   axis=0).astype(BF16)


def _mixer_kernel(x_ref, g1_ref, w_ref, wst_ref, sb_ref, a2_ref, a2b_ref, cw_ref, cb_ref,
                  mhg_ref, ghg_ref, wbm_ref, wbg_ref, wout_ref, tail0_ref, cn0_ref, m0_ref, s0_ref,
                  h1_ref, tail_ref, cn_ref, m_ref, s_ref, *rest, tm, meta):
    nc = tm // RCHUNK
    pairs = [(c, h) for c in range(nc) for h in range(HEADS)]
    if meta:
        wt_hbm = w_ref
        w_ref, wst_ref, *rest = rest
        *rest, stage, narrow, sem, nsem = rest
        _repack_w_in(wt_hbm, w_ref, wst_ref, stage, narrow, sem, nsem)
    else:
        *rest, wbm_s, wbg_s, wout_s, wstage, wsem = rest
        merge_w = (wbm_s, wbg_s, wout_s)
    (xn_s, qk_s, q_s, k_s, wk_s, vo_s, gq64_s, gq128_s, gki_s, gke64_s, gke128_s, gv_s, hm_s, hg_s,
     ga_s, gb_s, gc_s, gd_s, gqk_s, tric_s, triu_s, spread_s) = rest

    @pl.when(pl.program_id(0) == 0)
    def _():
        qk_s[0:SUBLANES, :] = tail0_ref[...]
        cn_ref[...] = cn0_ref[...]
        m_ref[...] = m0_ref[...]
        s_ref[...] = s0_ref[...]
        for h in range(HEADS):
            vo_s[:, h * CN + DV:(h + 1) * CN] = jnp.ones((tm, LANES), BF16)
        if not meta:
            srcs = (wbm_ref, wbg_ref, wout_ref)
            n_blk = D_MODEL // MERGE_ROWS

            def block_copy(j, slot):
                return pltpu.make_async_copy(srcs[j // n_blk].at[pl.ds((j % n_blk) * MERGE_ROWS, MERGE_ROWS), :],
                                             wstage.at[slot], wsem.at[slot])

            block_copy(0, 0).start()
            for j in range(len(srcs) * n_blk):
                slot = j % 2
                if j + 1 < len(srcs) * n_blk:
                    block_copy(j + 1, 1 - slot).start()
                block_copy(j, slot).wait()
                r0 = (j % n_blk) * MERGE_ROWS
                merge_w[j // n_blk][r0:r0 + MERGE_ROWS, :] = wstage[slot].astype(BF16)
        row = lax.broadcasted_iota(jnp.int32, (tm, tm), 0)
        col = lax.broadcasted_iota(jnp.int32, (tm, tm), 1)
        tric_s[...] = jnp.where((col <= row) & (col >= (row & -CHUNK)), 1.0, 0.0)[:RCHUNK, :RCHUNK].astype(BF16)
        triu_s[...] = jnp.where((row <= col) & (row >= (col & -RCHUNK)), 1.0, 0.0).astype(BF16)
        sr = lax.broadcasted_iota(jnp.int32, (LANES, 2 * HEADS * LANES), 0)
        sc = lax.broadcasted_iota(jnp.int32, (LANES, 2 * HEADS * LANES), 1)
        blk = (sc >> 9) * (3 * SUBLANES)
        spread_s[...] = jnp.where(
            (sr >= blk) & (sr < blk + 3 * SUBLANES) & ((sr & (SUBLANES - 1)) == ((sc >> 7) & (HEADS - 1))),
            1.0, 0.0).astype(BF16)

    x = x_ref[...]
    xn = x * lax.rsqrt(jnp.mean(x * x, axis=-1, keepdims=True) + EPS) * g1_ref[...]
    if meta:
        valid = lax.broadcasted_iota(jnp.int32, (tm, 1), 0) >= (tm - N_META)
        xn = jnp.where(valid, xn, 0.0)
    xn_s[...] = xn.astype(BF16)

    def proj(off, width):
        return _dot(xn_s[...], w_ref[:, off:off + width])

    def rows(c):
        return slice(c * RCHUNK, (c + 1) * RCHUNK)

    def half(c, second):
        lo = c * RCHUNK + (CHUNK if second else 0)
        return slice(lo, lo + CHUNK)

    crow = lax.broadcasted_iota(jnp.int32, (RCHUNK, RCHUNK), 0)
    ccol = lax.broadcasted_iota(jnp.int32, (RCHUNK, RCHUNK), 1)
    causal = ccol <= crow
    tri_c = tric_s[...]
    tri_up = triu_s[...]
    spread = spread_s[...]

    def st_qk(i, v):
        qk_s[SUBLANES:SUBLANES + tm, i * DV:(i + 1) * DV] = v

    def st_mv(i, v):
        vo_s[:, i * CN:i * CN + DV] = v.astype(BF16)

    def st_gqk(i, v):
        gqk_s[:, i * DV:(i + 1) * DV] = v

    def st_gv(i, v):
        gv_s[:, i * DV:(i + 1) * DV] = v.astype(BF16)

    def st_mo(i, v):
        ga_s[:, i * DV:(i + 1) * DV] = _sigmoid(v)

    def st_gm(i, v):
        gb_s[:, i * DV:(i + 1) * DV] = _sigmoid(v)

    def st_gr(i, v):
        gc_s[:, i * DV:(i + 1) * DV] = v * _sigmoid(v)

    def st_gg(i, v):
        gd_s[:, i * DV:(i + 1) * DV] = _sigmoid(v)

    groups = [(O_MQK, st_qk), (O_MV, st_mv), (O_GQ, st_gqk), (O_GV, st_gv)]
    if not meta:
        groups += [(O_MO, st_mo), (O_GATE_M, st_gm), (O_GR, st_gr), (O_GATE_G, st_gg)]
    queue = [(seg, i, store) for seg, store in groups for i in range(D_MODEL // DV)]
    emitted = [0]

    def fill(n):
        for _ in range(n):
            if emitted[0] < len(queue):
                seg, i, store = queue[emitted[0]]
                store(i, proj(seg + i * DV, DV))
                emitted[0] += 1

    def finish_groups(n):
        fill(n * (D_MODEL // DV) - emitted[0])

    narrow = _dot_nt(wst_ref[...], xn_s[...])
    gates = narrow[:2 * SUBLANES, :] + jnp.concatenate([sb_ref[...]] * (tm // LANES), axis=1)
    fill(2)
    s1 = jnp.concatenate([jnp.zeros((GA_LANE, tm), F32), narrow[2 * SUBLANES:, :],
                          jnp.zeros((LANES - GA_LANE - G_RANK, tm), F32)], axis=0).T
    logi = gates[:SUBLANES, :]
    logf = _log_sigmoid(gates[SUBLANES:, :])
    if meta:
        valid_t = lax.broadcasted_iota(jnp.int32, (1, tm), 1) >= (tm - N_META)
        logi = jnp.where(valid_t, logi, NEG_BIG)
        logf = jnp.where(valid_t, logf, 0.0)
    zero8 = jnp.zeros((SUBLANES, tm), F32)
    b4 = _dot(jnp.concatenate(_pieces(logf) + [zero8], axis=0).astype(BF16), tri_up)
    fill(2)
    b_all = b4[:SUBLANES] + b4[SUBLANES:2 * SUBLANES] + b4[2 * SUBLANES:3 * SUBLANES]
    c_all = logi - b_all
    m_run = m_ref[...]
    m_in, w_parts, a_chunk = [], [], []
    for c in range(nc):
        b_c = b_all[:, rows(c)]
        g_c = jnp.broadcast_to(b_c[:, RCHUNK - 1:RCHUNK], (SUBLANES, LANES))
        wlog = g_c + c_all[:, rows(c)]
        m_next = jnp.maximum(g_c + m_run, jnp.max(wlog, axis=1, keepdims=True))
        m_in.append(m_run)
        w_parts.append(jnp.exp(wlog - m_next))
        a_chunk.append(jnp.exp(g_c + m_run - m_next))
        m_run = m_next
    m_ref[...] = m_run
    stacked = jnp.concatenate(
        _pieces(b_all) + _pieces(jnp.concatenate(w_parts, axis=1))
        + [jnp.zeros((LANES - 6 * SUBLANES, tm), F32)], axis=0)
    stacked_t = stacked.T.astype(BF16)
    fill(2)
    tiles = _dot(stacked_t, spread)
    fill(2)
    b_t, w_t = tiles[:, :QK], tiles[:, QK:]

    finish_groups(1)
    for blk in range(2 * QK // DV):
        csl = slice(blk * DV, (blk + 1) * DV)
        conv = cb_ref[:, csl]
        for j in range(CONV_W):
            conv = conv + cw_ref[j:j + 1, csl] * qk_s[pl.ds(SUBLANES - (CONV_W - 1) + j, tm), csl]
        conv = conv * _sigmoid(conv)
        if blk < QK // DV:
            q_s[:, csl] = conv * (DK ** -0.5)
        else:
            ksl = slice(blk * DV - QK, (blk + 1) * DV - QK)
            k_s[:, ksl] = conv.astype(BF16)
            wk_s[:, ksl] = (w_t[:, ksl] * conv).astype(BF16)
        fill(1)
    tail_new = qk_s[tm:tm + SUBLANES, :]
    qk_s[0:SUBLANES, :] = tail_new
    tail_ref[...] = tail_new

    za = _dot(s1.astype(BF16), a2_ref[...]) + a2b_ref[...]
    fill(1)
    loga = _log_sigmoid(za) / G_TAU
    if meta:
        loga = jnp.where(valid, loga, 0.0)
    hi = loga.astype(BF16)
    r1 = loga - hi.astype(F32)
    mid = r1.astype(BF16)
    bc = jnp.concatenate([_dot(tri_c, hi[rows(c), :]) + _dot(tri_c, mid[rows(c), :])
                          for c in range(nc)], axis=0)
    fill(2)
    e_blk = jnp.exp(bc)
    tot_a = [bc[c * RCHUNK + CHUNK - 1:c * RCHUNK + CHUNK, :] for c in range(nc)]
    tot_b = [bc[(c + 1) * RCHUNK - 1:(c + 1) * RCHUNK, :] for c in range(nc)]
    e_a = [jnp.exp(t) for t in tot_a]
    e_b = [jnp.exp(t) for t in tot_b]
    rest = jnp.concatenate([t - bc[half(c, s), :] for c in range(nc) for s, t in ((0, tot_a[c]), (1, tot_b[c]))],
                           axis=0)
    e_rest = jnp.exp(rest)
    ones_row = jnp.ones((1, QK), F32)
    to_chunk = jnp.concatenate([jnp.broadcast_to(f, (CHUNK, QK)) for c in range(nc) for f in (ones_row, e_a[c])],
                               axis=0)
    from_blk = jnp.concatenate([jnp.broadcast_to(f, (CHUNK, QK)) for c in range(nc) for f in (e_b[c], ones_row)],
                               axis=0)
    finish_groups(3)
    gq = gqk_s[:, :QK] * (DK ** -0.5)
    gk = gqk_s[:, QK:]
    gq64_s[...] = (gq * e_blk).astype(BF16)
    gq128_s[...] = (gq * e_blk * to_chunk).astype(BF16)
    gki_s[...] = (gk * jnp.exp(-bc)).astype(BF16)
    gke64_s[...] = (gk * e_rest).astype(BF16)
    gke128_s[...] = (gk * e_rest * from_blk).astype(BF16)
    fill(2)
    e_rows = [(e_a[c] * e_b[c])[:, h * DK:(h + 1) * DK] for (c, h) in pairs]
    e_cols = jnp.concatenate(e_rows + [jnp.zeros((LANES - len(pairs), DK), F32)], axis=0).T

    finish_groups(2)
    dmat, rmax, sim = {}, {}, {}
    for (c, h) in pairs:
        d = b_t[rows(c), h * LANES:(h + 1) * LANES] + c_all[h:h + 1, rows(c)]
        d = jnp.where(causal, d, -jnp.inf)
        dmat[c, h] = d
        rmax[c, h] = jnp.max(d, axis=-1, keepdims=True)
    fill(1)
    for (c, h) in pairs:
        sim[c, h] = _dot_nt(q_s[rows(c), h * DK:(h + 1) * DK].astype(BF16), k_s[rows(c), h * DK:(h + 1) * DK])
    fill(1)
    lhs, emr, upd = {}, {}, {}
    for (c, h) in pairs:
        il = b_t[rows(c), h * LANES:(h + 1) * LANES] + m_in[c][h:h + 1, :]
        m_row = jnp.maximum(il, rmax[c, h])
        wts = jnp.exp(dmat[c, h] - m_row) * sim[c, h]
        aq = jnp.exp(il - m_row) * q_s[rows(c), h * DK:(h + 1) * DK]
        lhs[c, h] = jnp.concatenate([aq.astype(BF16), wts.astype(BF16)], axis=1)
        emr[c, h] = jnp.exp(-m_row)
    fill(1)
    for (c, h) in pairs:
        upd[c, h] = _dot_tn(wk_s[rows(c), h * DK:(h + 1) * DK], vo_s[rows(c), h * CN:(h + 1) * CN])
    fill(1)
    res = {}
    for h in range(HEADS):
        state = cn_ref[h]
        for c in range(nc):
            rhs = jnp.concatenate([state.astype(BF16), vo_s[rows(c), h * CN:(h + 1) * CN]], axis=0)
            res[c, h] = _dot(lhs[c, h], rhs)
            a = a_chunk[c][h:h + 1, :]
            state = jnp.concatenate([a] * (CN // LANES), axis=1) * state + upd[c, h]
        cn_ref[h] = state
        fill(1)
    for (c, h) in pairs:
        inv = 1.0 / jnp.maximum(jnp.abs(res[c, h][:, DV:]), emr[c, h])
        hm_s[rows(c), h * DV:h * DV + LANES] = res[c, h][:, :LANES] * inv
        hm_s[rows(c), h * DV + LANES:(h + 1) * DV] = res[c, h][:, LANES:DV] * inv
    fill(1)

    finish_groups(4)
    att, gupd = {}, {}
    zblk = jnp.zeros((CHUNK, DK), BF16)
    for (c, h) in pairs:
        dsl = slice(h * DK, (h + 1) * DK)
        ra, rb = half(c, 0), half(c, 1)
        qp = jnp.concatenate([jnp.concatenate([gq64_s[ra, dsl], zblk], axis=1),
                              jnp.concatenate([zblk, gq64_s[rb, dsl]], axis=1)], axis=0)
        kp = jnp.concatenate([jnp.concatenate([gki_s[ra, dsl], gke64_s[ra, dsl]], axis=1),
                              jnp.concatenate([zblk, gki_s[rb, dsl]], axis=1)], axis=0)
        att[c, h] = jnp.where(causal, _dot_nt(qp, kp), 0.0).astype(BF16)
    fill(1)
    for (c, h) in pairs:
        gupd[c, h] = _dot_tn(gke128_s[rows(c), h * DK:(h + 1) * DK], gv_s[rows(c), h * DV:(h + 1) * DV])
    fill(1)
    for h in range(HEADS):
        sst = s_ref[h]
        for c in range(nc):
            lhs_g = jnp.concatenate([gq128_s[rows(c), h * DK:(h + 1) * DK], att[c, h]], axis=1)
            rhs_g = jnp.concatenate([sst.astype(BF16), gv_s[rows(c), h * DV:(h + 1) * DV]], axis=0)
            hg_s[rows(c), h * DV:(h + 1) * DV] = _dot(lhs_g, rhs_g)
            i = c * HEADS + h
            sst = e_cols[:, i:i + 1] * sst + gupd[c, h]
        s_ref[h] = sst
        fill(1)

    if meta:
        h1_ref[...] = x
        return

    def head_norm(src, gain_ref):
        outs = []
        for h in range(HEADS):
            t = src[:, h * DV:(h + 1) * DV]
            outs.append(t * lax.rsqrt(jnp.mean(t * t, axis=-1, keepdims=True) + EPS))
        return jnp.concatenate(outs, axis=-1) * gain_ref[...]

    finish_groups(len(groups))
    y_m = head_norm(hm_s, mhg_ref) * ga_s[...]
    merged = gb_s[...] * _dot(y_m.astype(BF16), merge_w[0][...])
    y_g = head_norm(hg_s, ghg_ref) * gc_s[...]
    merged = merged + gd_s[...] * _dot(y_g.astype(BF16), merge_w[1][...])
    h1_ref[...] = x + _dot(merged.astype(BF16), merge_w[2][...])


def _const_spec(shape):
    nd = len(shape)
    return pl.BlockSpec(shape, lambda i: (0,) * nd, pipeline_mode=pl.Buffered(1))


def _mixer_call(x2, consts, state, *, tm, meta):
    t = x2.shape[0]
    state_shapes = [jax.ShapeDtypeStruct(a.shape, F32) for a in state]
    row_spec = pl.BlockSpec((tm, D_MODEL), lambda i: (i, 0))
    state_specs = [pl.BlockSpec(a.shape, lambda i, nd=a.ndim: (0,) * nd) for a in state]
    hbm_operands = (1,) if meta else LEAD_UNUSED
    packed = [jax.ShapeDtypeStruct((D_MODEL, N_PACK), BF16),
              jax.ShapeDtypeStruct((2 * SUBLANES + G_RANK, D_MODEL), BF16)] if meta else []
    return pl.pallas_call(
        functools.partial(_mixer_kernel, tm=tm, meta=meta),
        grid=(t // tm,),
        in_specs=[row_spec] + [pl.BlockSpec(memory_space=pl.ANY) if i in hbm_operands else _const_spec(c.shape)
                               for i, c in enumerate(consts)] + state_specs,
        out_specs=[row_spec] + state_specs + [pl.BlockSpec(a.shape, lambda i: (0, 0)) for a in packed],
        out_shape=[jax.ShapeDtypeStruct((t, D_MODEL), F32)] + state_shapes + packed,
        scratch_shapes=[
            pltpu.VMEM((tm, D_MODEL), BF16),
            pltpu.VMEM((tm + SUBLANES, 2 * QK), F32),
            pltpu.VMEM((tm, QK), F32),
            pltpu.VMEM((tm, QK), BF16),
            pltpu.VMEM((tm, QK), BF16),
            pltpu.VMEM((tm, HEADS * CN), BF16),
            pltpu.VMEM((tm, QK), BF16),
            pltpu.VMEM((tm, QK), BF16),
            pltpu.VMEM((tm, QK), BF16),
            pltpu.VMEM((tm, QK), BF16),
            pltpu.VMEM((tm, QK), BF16),
            pltpu.VMEM((tm, D_MODEL), BF16),
            pltpu.VMEM((tm, D_MODEL), F32),
            pltpu.VMEM((tm, D_MODEL), F32),
            pltpu.VMEM((tm, D_MODEL), F32),
            pltpu.VMEM((tm, D_MODEL), F32),
            pltpu.VMEM((tm, D_MODEL), F32),
            pltpu.VMEM((tm, D_MODEL), F32),
            pltpu.VMEM((tm, D_MODEL), F32),
            pltpu.VMEM((RCHUNK, RCHUNK), BF16),
            pltpu.VMEM((tm, tm), BF16),
            pltpu.VMEM((LANES, 2 * HEADS * LANES), BF16),
        ] + ([
            pltpu.VMEM((REPACK_SLOTS, PIECE, D_MODEL), F32),
            pltpu.VMEM((SUBLANES + G_RANK, D_MODEL), F32),
            pltpu.SemaphoreType.DMA((REPACK_SLOTS,)),
            pltpu.SemaphoreType.DMA((2,)),
        ] if meta else [
            pltpu.VMEM((D_MODEL, D_MODEL), BF16),
            pltpu.VMEM((D_MODEL, D_MODEL), BF16),
            pltpu.VMEM((D_MODEL, D_MODEL), BF16),
            pltpu.VMEM((2, MERGE_ROWS, D_MODEL), F32),
            pltpu.SemaphoreType.DMA((2,)),
        ]),
        compiler_params=pltpu.CompilerParams(
            dimension_semantics=("arbitrary",), vmem_limit_bytes=VMEM_LIMIT),
        name="mixer_meta" if meta else "mixer",
    )(x2, *consts, *state)


def _ffn_kernel(h_ref, g2_ref, wg_hbm, wu_hbm, wd_hbm, gf_ref, o_ref,
                wg_ref, wu_ref, wd_ref, sg, su, sd, sem, *, fc, parts):
    n_chunks = wg_ref.shape[1] // fc

    def chunk_copies(f, slot):
        cols = pl.ds(f * fc, fc)
        return (pltpu.make_async_copy(wg_hbm.at[:, cols], sg.at[slot], sem.at[slot, 0]),
                pltpu.make_async_copy(wu_hbm.at[:, cols], su.at[slot], sem.at[slot, 1]),
                pltpu.make_async_copy(wd_hbm.at[cols, :], sd.at[slot], sem.at[slot, 2]))

    @pl.when(pl.program_id(0) == 0)
    def _():
        for cp in chunk_copies(0, 0):
            cp.start()
        for f in range(n_chunks):
            slot = f % 2
            if f + 1 < n_chunks:
                for cp in chunk_copies(f + 1, 1 - slot):
                    cp.start()
            for cp in chunk_copies(f, slot):
                cp.wait()
            wg_ref[:, f * fc:(f + 1) * fc] = sg[slot].astype(BF16)
            wu_ref[:, f * fc:(f + 1) * fc] = su[slot].astype(BF16)
            wd_ref[f * fc:(f + 1) * fc, :] = sd[slot].astype(BF16)

    rows = h_ref.shape[0] // parts
    blocks = [slice(p * rows, (p + 1) * rows) for p in range(parts)]
    acc, hn = [], []
    for blk in blocks:
        h = h_ref[blk, :]
        hn.append((h * lax.rsqrt(jnp.mean(h * h, axis=-1, keepdims=True) + EPS) * g2_ref[...]).astype(BF16))
        acc.append(h)
    for f in range(0, wg_ref.shape[1], fc):
        for p in range(parts):
            gate = _dot(hn[p], wg_ref[:, f:f + fc])
            up = _dot(hn[p], wu_ref[:, f:f + fc])
            acc[p] = acc[p] + _dot((gate * _sigmoid(gate) * up).astype(BF16), wd_ref[f:f + fc, :])
    for p, blk in enumerate(blocks):
        a = acc[p]
        o_ref[blk, :] = a * lax.rsqrt(jnp.mean(a * a, axis=-1, keepdims=True) + EPS) * gf_ref[...]


def _ffn_call(h1, g2, wg, wu, wd, gf, *, tm, fc, parts):
    t = h1.shape[0]
    d_ff = wg.shape[1]
    row_spec = pl.BlockSpec((tm, D_MODEL), lambda i: (i, 0))
    hbm_spec = pl.BlockSpec(memory_space=pl.ANY)
    return pl.pallas_call(
        functools.partial(_ffn_kernel, fc=fc, parts=parts),
        grid=(t // tm,),
        in_specs=[row_spec, _const_spec(g2.shape), hbm_spec, hbm_spec, hbm_spec, _const_spec(gf.shape)],
        out_specs=row_spec,
        out_shape=jax.ShapeDtypeStruct((t, D_MODEL), F32),
        scratch_shapes=[
            pltpu.VMEM((D_MODEL, d_ff), BF16),
            pltpu.VMEM((D_MODEL, d_ff), BF16),
            pltpu.VMEM((d_ff, D_MODEL), BF16),
            pltpu.VMEM((2, D_MODEL, fc), F32),
            pltpu.VMEM((2, D_MODEL, fc), F32),
            pltpu.VMEM((2, fc, D_MODEL), F32),
            pltpu.SemaphoreType.DMA((2, 3)),
        ],
        compiler_params=pltpu.CompilerParams(
            dimension_semantics=("arbitrary",), vmem_limit_bytes=VMEM_LIMIT),
        name="ffn",
    )(h1, g2, wg, wu, wd, gf)


def kernel(x, meta_tokens, norm1_g, w_in, conv_w, conv_b, m_gate_b, g_a2, g_a2_b, m_head_g, g_head_g,
           w_branch_m, w_branch_g, w_out, norm2_g, w_ff_gate, w_ff_up, w_ff_down, final_g):
    bsz, seq, d = x.shape
    assert bsz == 1 and d == D_MODEL and norm1_g.shape[0] == 1 and seq % CHUNK == 0
    row = lambda a: a.reshape(1, -1).astype(F32)

    gate_bias = jnp.zeros((2 * SUBLANES, LANES), F32)
    gate_bias = gate_bias.at[:HEADS].set(m_gate_b[0, 0][:, None]).at[SUBLANES:SUBLANES + HEADS].set(m_gate_b[0, 1][:, None])
    a2 = jnp.zeros((LANES, QK), F32).at[GA_LANE:GA_LANE + G_RANK].set(g_a2[0]).astype(BF16)
    conv_w8 = jnp.zeros((SUBLANES, 2 * QK), F32).at[:CONV_W].set(conv_w[0])
    consts = (gate_bias, a2, row(g_a2_b[0]), conv_w8, row(conv_b[0]), row(m_head_g[0]), row(g_head_g[0]),
              w_branch_m[0].astype(F32), w_branch_g[0].astype(F32), w_out[0].astype(F32))

    zero_state = (jnp.zeros((SUBLANES, 2 * QK), F32), jnp.zeros((HEADS, DK, CN), F32),
                  jnp.zeros((SUBLANES, LANES), F32), jnp.zeros((HEADS, DK, DV), F32))
    lead = jnp.concatenate([jnp.zeros((RCHUNK - N_META, d), F32), meta_tokens.astype(F32)], axis=0)
    unused = jnp.zeros((SUBLANES, LANES), BF16)
    g1 = row(norm1_g[0])
    lead_consts = (g1, w_in[0].T.astype(F32), unused) + consts[:-len(LEAD_UNUSED)] + (unused,) * len(LEAD_UNUSED)
    _, *state, w_packed, w_narrow = _mixer_call(lead, lead_consts, zero_state, tm=RCHUNK, meta=True)
    h1, *_ = _mixer_call(x[0], (g1, w_packed, w_narrow) + consts, tuple(state), tm=512, meta=False)
    out = _ffn_call(h1, row(norm2_g[0]), w_ff_gate[0].astype(F32), w_ff_up[0].astype(F32),
                    w_ff_down[0].astype(F32), row(final_g), tm=1024, fc=256, parts=2)
    return out[None]
```

```python
import functools

import jax
import jax.numpy as jnp
from jax import lax
from jax.experimental import pallas as pl
from jax.experimental.pallas import tpu as pltpu

F32 = jnp.float32
BF16 = jnp.bfloat16

D_MODEL = 1024
N_META = 16
CHUNK = 64
RCHUNK = 128
EPS = 1e-6
HEADS = 4
DV = D_MODEL // HEADS
DK = DV // 2
QK = HEADS * DK
G_RANK = 16
G_TAU = 16.0
CONV_W = 4
LANES = 128
SUBLANES = 8
NEG_BIG = -1e30
VMEM_LIMIT = 60 * 1024 * 1024
CN = DV + LANES

W_IN_RANGES = ((0, 2048), (2056, 5128), (5144, 8216))
N_PACK = sum(b - a for a, b in W_IN_RANGES)
O_MQK, O_MV, O_MO = 0, 1024, 2048
O_GQ, O_GV = 3072, 4096
O_GR, O_GATE_M, O_GATE_G = 5120, 6144, 7168
MI_COL, GA_COL = 2048, 5128
GA_LANE = 8
PIECE = 512
MERGE_ROWS = 256
REPACK_AHEAD = 3
REPACK_SLOTS = REPACK_AHEAD + 1
LEAD_UNUSED = (10, 11, 12)

NT_DIMS = (((1,), (1,)), ((), ()))
TN_DIMS = (((0,), (0,)), ((), ()))


def _dot(a, b):
    return jnp.dot(a, b, preferred_element_type=F32)


def _dot_nt(a, b):
    return lax.dot_general(a, b, NT_DIMS, preferred_element_type=F32)


def _dot_tn(a, b):
    return lax.dot_general(a, b, TN_DIMS, preferred_element_type=F32)


def _pieces(x):
    hi = x.astype(BF16).astype(F32)
    mid = (x - hi).astype(BF16).astype(F32)
    lo = (x - hi - mid).astype(BF16).astype(F32)
    return [hi, mid, lo]


def _log_sigmoid(x):
    return jnp.minimum(x, 0.0) - jnp.log(1.0 + jnp.exp(-jnp.abs(x)))


def _sigmoid(x):
    return 1.0 / (1.0 + jnp.exp(-x))


def _repack_w_in(wt_hbm, w_ref, wst_ref, stage, narrow, sem, nsem):
    (_, a_end), (b_start, b_end), (c_start, _) = W_IN_RANGES
    first_b, first_c = a_end // PIECE, (a_end + b_end - b_start) // PIECE
    n_pieces = N_PACK // PIECE

    def piece_copy(q, slot):
        row0 = q * PIECE + (b_start - a_end if q >= first_b else 0) + (c_start - b_end if q >= first_c else 0)
        return pltpu.make_async_copy(wt_hbm.at[pl.ds(row0, PIECE), :], stage.at[slot], sem.at[slot])

    narrow_copies = (
        pltpu.make_async_copy(wt_hbm.at[pl.ds(MI_COL, SUBLANES), :], narrow.at[pl.ds(0, SUBLANES), :], nsem.at[0]),
        pltpu.make_async_copy(wt_hbm.at[pl.ds(GA_COL, G_RANK), :], narrow.at[pl.ds(SUBLANES, G_RANK), :], nsem.at[1]))

    for q in range(REPACK_AHEAD):
        piece_copy(q, q).start()
    for cp in narrow_copies:
        cp.start()
    for q in range(n_pieces):
        if q + REPACK_AHEAD < n_pieces:
            piece_copy(q + REPACK_AHEAD, (q + REPACK_AHEAD) % REPACK_SLOTS).start()
        piece_copy(q, q % REPACK_SLOTS).wait()
        w_ref[:, q * PIECE:(q + 1) * PIECE] = stage[q % REPACK_SLOTS].T.astype(BF16)
    for cp in narrow_copies:
        cp.wait()
    gates = narrow[0:SUBLANES, :]
    zeros = jnp.zeros((SUBLANES - HEADS, D_MODEL), F32)
    wst_ref[...] = jnp.concatenate([gates[:HEADS], zeros, gates[HEADS:], zeros, narrow[SUBLANES:, :]],
                                   axis=0).astype(BF16)


def _mixer_kernel(x_ref, g1_ref, w_ref, wst_ref, sb_ref, a2_ref, a2b_ref, cw_ref, cb_ref,
                  mhg_ref, ghg_ref, wbm_ref, wbg_ref, wout_ref, tail0_ref, cn0_ref, m0_ref, s0_ref,
                  h1_ref, tail_ref, cn_ref, m_ref, s_ref, *rest, tm, meta):
    nc = tm // RCHUNK
    pairs = [(c, h) for c in range(nc) for h in range(HEADS)]
    if meta:
        wt_hbm = w_ref
        w_ref, wst_ref, *rest = rest
        *rest, stage, narrow, sem, nsem = rest
        _repack_w_in(wt_hbm, w_ref, wst_ref, stage, narrow, sem, nsem)
    else:
        *rest, wbm_s, wbg_s, wout_s, wstage, wsem = rest
        merge_w = (wbm_s, wbg_s, wout_s)
    (xn_s, qk_s, q_s, k_s, wk_s, vo_s, gq64_s, gq128_s, gki_s, gke64_s, gke128_s, gv_s, hm_s, hg_s,
     ga_s, gb_s, gc_s, gd_s, gqk_s, tric_s, triu_s, spread_s) = rest

    @pl.when(pl.program_id(0) == 0)
    def _():
        qk_s[0:SUBLANES, :] = tail0_ref[...]
        cn_ref[...] = cn0_ref[...]
        m_ref[...] = m0_ref[...]
        s_ref[...] = s0_ref[...]
        for h in range(HEADS):
            vo_s[:, h * CN + DV:(h + 1) * CN] = jnp.ones((tm, LANES), BF16)
        if not meta:
            srcs = (wbm_ref, wbg_ref, wout_ref)
            n_blk = D_MODEL // MERGE_ROWS

            def block_copy(j, slot):
                return pltpu.make_async_copy(srcs[j // n_blk].at[pl.ds((j % n_blk) * MERGE_ROWS, MERGE_ROWS), :],
                                             wstage.at[slot], wsem.at[slot])

            block_copy(0, 0).start()
            for j in range(len(srcs) * n_blk):
                slot = j % 2
                if j + 1 < len(srcs) * n_blk:
                    block_copy(j + 1, 1 - slot).start()
                block_copy(j, slot).wait()
                r0 = (j % n_blk) * MERGE_ROWS
                merge_w[j // n_blk][r0:r0 + MERGE_ROWS, :] = wstage[slot].astype(BF16)
        row = lax.broadcasted_iota(jnp.int32, (tm, tm), 0)
        col = lax.broadcasted_iota(jnp.int32, (tm, tm), 1)
        tric_s[...] = jnp.where((col <= row) & (col >= (row & -CHUNK)), 1.0, 0.0)[:RCHUNK, :RCHUNK].astype(BF16)
        triu_s[...] = jnp.where((row <= col) & (row >= (col & -RCHUNK)), 1.0, 0.0).astype(BF16)
        sr = lax.broadcasted_iota(jnp.int32, (LANES, 2 * HEADS * LANES), 0)
        sc = lax.broadcasted_iota(jnp.int32, (LANES, 2 * HEADS * LANES), 1)
        blk = (sc >> 9) * (3 * SUBLANES)
        spread_s[...] = jnp.where(
            (sr >= blk) & (sr < blk + 3 * SUBLANES) & ((sr & (SUBLANES - 1)) == ((sc >> 7) & (HEADS - 1))),
            1.0, 0.0).astype(BF16)

    x = x_ref[...]
    xn = x * lax.rsqrt(jnp.mean(x * x, axis=-1, keepdims=True) + EPS) * g1_ref[...]
    if meta:
        valid = lax.broadcasted_iota(jnp.int32, (tm, 1), 0) >= (tm - N_META)
        xn = jnp.where(valid, xn, 0.0)
    xn_s[...] = xn.astype(BF16)

    def proj(off, width):
        return _dot(xn_s[...], w_ref[:, off:off + width])

    def rows(c):
        return slice(c * RCHUNK, (c + 1) * RCHUNK)

    def half(c, second):
        lo = c * RCHUNK + (CHUNK if second else 0)
        return slice(lo, lo + CHUNK)

    crow = lax.broadcasted_iota(jnp.int32, (RCHUNK, RCHUNK), 0)
    ccol = lax.broadcasted_iota(jnp.int32, (RCHUNK, RCHUNK), 1)
    causal = ccol <= crow
    tri_c = tric_s[...]
    tri_up = triu_s[...]
    spread = spread_s[...]

    def st_qk(i, v):
        qk_s[SUBLANES:SUBLANES + tm, i * DV:(i + 1) * DV] = v

    def st_mv(i, v):
        vo_s[:, i * CN:i * CN + DV] = v.astype(BF16)

    def st_gqk(i, v):
        gqk_s[:, i * DV:(i + 1) * DV] = v

    def st_gv(i, v):
        gv_s[:, i * DV:(i + 1) * DV] = v.astype(BF16)

    def st_mo(i, v):
        ga_s[:, i * DV:(i + 1) * DV] = _sigmoid(v)

    def st_gm(i, v):
        gb_s[:, i * DV:(i + 1) * DV] = _sigmoid(v)

    def st_gr(i, v):
        gc_s[:, i * DV:(i + 1) * DV] = v * _sigmoid(v)

    def st_gg(i, v):
        gd_s[:, i * DV:(i + 1) * DV] = _sigmoid(v)

    groups = [(O_MQK, st_qk), (O_MV, st_mv), (O_GQ, st_gqk), (O_GV, st_gv)]
    if not meta:
        groups += [(O_MO, st_mo), (O_GATE_M, st_gm), (O_GR, st_gr), (O_GATE_G, st_gg)]
    queue = [(seg, i, store) for seg, store in groups for i in range(D_MODEL // DV)]
    emitted = [0]

    def fill(n):
        for _ in range(n):
            if emitted[0] < len(queue):
                seg, i, store = queue[emitted[0]]
                store(i, proj(seg + i * DV, DV))
                emitted[0] += 1

    def finish_groups(n):
        fill(n * (D_MODEL // DV) - emitted[0])

    narrow = _dot_nt(wst_ref[...], xn_s[...])
    gates = narrow[:2 * SUBLANES, :] + jnp.concatenate([sb_ref[...]] * (tm // LANES), axis=1)
    fill(2)
    s1 = jnp.concatenate([jnp.zeros((GA_LANE, tm), F32), narrow[2 * SUBLANES:, :],
                          jnp.zeros((LANES - GA_LANE - G_RANK, tm), F32)], axis=0).T
    logi = gates[:SUBLANES, :]
    logf = _log_sigmoid(gates[SUBLANES:, :])
    if meta:
        valid_t = lax.broadcasted_iota(jnp.int32, (1, tm), 1) >= (tm - N_META)
        logi = jnp.where(valid_t, logi, NEG_BIG)
        logf = jnp.where(valid_t, logf, 0.0)
    zero8 = jnp.zeros((SUBLANES, tm), F32)
    b4 = _dot(jnp.concatenate(_pieces(logf) + [zero8], axis=0).astype(BF16), tri_up)
    fill(2)
    b_all = b4[:SUBLANES] + b4[SUBLANES:2 * SUBLANES] + b4[2 * SUBLANES:3 * SUBLANES]
    c_all = logi - b_all
    m_run = m_ref[...]
    m_in, w_parts, a_chunk = [], [], []
    for c in range(nc):
        b_c = b_all[:, rows(c)]
        g_c = jnp.broadcast_to(b_c[:, RCHUNK - 1:RCHUNK], (SUBLANES, LANES))
        wlog = g_c + c_all[:, rows(c)]
        m_next = jnp.maximum(g_c + m_run, jnp.max(wlog, axis=1, keepdims=True))
        m_in.append(m_run)
        w_parts.append(jnp.exp(wlog - m_next))
        a_chunk.append(jnp.exp(g_c + m_run - m_next))
        m_run = m_next
    m_ref[...] = m_run
    stacked = jnp.concatenate(
        _pieces(b_all) + _pieces(jnp.concatenate(w_parts, axis=1))
        + [jnp.zeros((LANES - 6 * SUBLANES, tm), F32)], axis=0)
    stacked_t = stacked.T.astype(BF16)
    fill(2)
    tiles = _dot(stacked_t, spread)
    fill(2)
    b_t, w_t = tiles[:, :QK], tiles[:, QK:]

    finish_groups(1)
    for blk in range(2 * QK // DV):
        csl = slice(blk * DV, (blk + 1) * DV)
        conv = cb_ref[:, csl]
        for j in range(CONV_W):
            conv = conv + cw_ref[j:j + 1, csl] * qk_s[pl.ds(SUBLANES - (CONV_W - 1) + j, tm), csl]
        conv = conv * _sigmoid(conv)
        if blk < QK // DV:
            q_s[:, csl] = conv * (DK ** -0.5)
        else:
            ksl = slice(blk * DV - QK, (blk + 1) * DV - QK)
            k_s[:, ksl] = conv.astype(BF16)
            wk_s[:, ksl] = (w_t[:, ksl] * conv).astype(BF16)
        fill(1)
    tail_new = qk_s[tm:tm + SUBLANES, :]
    qk_s[0:SUBLANES, :] = tail_new
    tail_ref[...] = tail_new

    za = _dot(s1.astype(BF16), a2_ref[...]) + a2b_ref[...]
    fill(1)
    loga = _log_sigmoid(za) / G_TAU
    if meta:
        loga = jnp.where(valid, loga, 0.0)
    hi = loga.astype(BF16)
    r1 = loga - hi.astype(F32)
    mid = r1.astype(BF16)
    bc = jnp.concatenate([_dot(tri_c, hi[rows(c), :]) + _dot(tri_c, mid[rows(c), :])
                          for c in range(nc)], axis=0)
    fill(2)
    e_blk = jnp.exp(bc)
    tot_a = [bc[c * RCHUNK + CHUNK - 1:c * RCHUNK + CHUNK, :] for c in range(nc)]
    tot_b = [bc[(c + 1) * RCHUNK - 1:(c + 1) * RCHUNK, :] for c in range(nc)]
    e_a = [jnp.exp(t) for t in tot_a]
    e_b = [jnp.exp(t) for t in tot_b]
    rest = jnp.concatenate([t - bc[half(c, s), :] for c in range(nc) for s, t in ((0, tot_a[c]), (1, tot_b[c]))],
                           axis=0)
    e_rest = jnp.exp(rest)
    ones_row = jnp.ones((1, QK), F32)
    to_chunk = jnp.concatenate([jnp.broadcast_to(f, (CHUNK, QK)) for c in range(nc) for f in (ones_row, e_a[c])],
                               axis=0)
    from_blk = jnp.concatenate([jnp.broadcast_to(f, (CHUNK, QK)) for c in range(nc) for f in (e_b[c], ones_row)],
                               axis=0)
    finish_groups(3)
    gq = gqk_s[:, :QK] * (DK ** -0.5)
    gk = gqk_s[:, QK:]
    gq64_s[...] = (gq * e_blk).astype(BF16)
    gq128_s[...] = (gq * e_blk * to_chunk).astype(BF16)
    gki_s[...] = (gk * jnp.exp(-bc)).astype(BF16)
    gke64_s[...] = (gk * e_rest).astype(BF16)
    gke128_s[...] = (gk * e_rest * from_blk).astype(BF16)
    fill(2)
    e_rows = [(e_a[c] * e_b[c])[:, h * DK:(h + 1) * DK] for (c, h) in pairs]
    e_cols = jnp.concatenate(e_rows + [jnp.zeros((LANES - len(pairs), DK), F32)], axis=0).T

    finish_groups(2)
    dmat, rmax, sim = {}, {}, {}
    for (c, h) in pairs:
        d = b_t[rows(c), h * LANES:(h + 1) * LANES] + c_all[h:h + 1, rows(c)]
        d = jnp.where(causal, d, -jnp.inf)
        dmat[c, h] = d
        rmax[c, h] = jnp.max(d, axis=-1, keepdims=True)
    fill(1)
    for (c, h) in pairs:
        sim[c, h] = _dot_nt(q_s[rows(c), h * DK:(h + 1) * DK].astype(BF16), k_s[rows(c), h * DK:(h + 1) * DK])
    fill(1)
    lhs, emr, upd = {}, {}, {}
    for (c, h) in pairs:
        il = b_t[rows(c), h * LANES:(h + 1) * LANES] + m_in[c][h:h + 1, :]
        m_row = jnp.maximum(il, rmax[c, h])
        wts = jnp.exp(dmat[c, h] - m_row) * sim[c, h]
        aq = jnp.exp(il - m_row) * q_s[rows(c), h * DK:(h + 1) * DK]
        lhs[c, h] = jnp.concatenate([aq.astype(BF16), wts.astype(BF16)], axis=1)
        emr[c, h] = jnp.exp(-m_row)
    fill(1)
    for (c, h) in pairs:
        upd[c, h] = _dot_tn(wk_s[rows(c), h * DK:(h + 1) * DK], vo_s[rows(c), h * CN:(h + 1) * CN])
    fill(1)
    res = {}
    for h in range(HEADS):
        state = cn_ref[h]
        for c in range(nc):
            rhs = jnp.concatenate([state.astype(BF16), vo_s[rows(c), h * CN:(h + 1) * CN]], axis=0)
            res[c, h] = _dot(lhs[c, h], rhs)
            a = a_chunk[c][h:h + 1, :]
            state = jnp.concatenate([a] * (CN // LANES), axis=1) * state + upd[c, h]
        cn_ref[h] = state
        fill(1)
    for (c, h) in pairs:
        inv = 1.0 / jnp.maximum(jnp.abs(res[c, h][:, DV:]), emr[c, h])
        hm_s[rows(c), h * DV:h * DV + LANES] = res[c, h][:, :LANES] * inv
        hm_s[rows(c), h * DV + LANES:(h + 1) * DV] = res[c, h][:, LANES:DV] * inv
    fill(1)

    finish_groups(4)
    att, gupd = {}, {}
    zblk = jnp.zeros((CHUNK, DK), BF16)
    for (c, h) in pairs:
        dsl = slice(h * DK, (h + 1) * DK)
        ra, rb = half(c, 0), half(c, 1)
        qp = jnp.concatenate([jnp.concatenate([gq64_s[ra, dsl], zblk], axis=1),
                              jnp.concatenate([zblk, gq64_s[rb, dsl]], axis=1)], axis=0)
        kp = jnp.concatenate([jnp.concatenate([gki_s[ra, dsl], gke64_s[ra, dsl]], axis=1),
                              jnp.concatenate([zblk, gki_s[rb, dsl]], axis=1)], axis=0)
        att[c, h] = jnp.where(causal, _dot_nt(qp, kp), 0.0).astype(BF16)
    fill(1)
    for (c, h) in pairs:
        gupd[c, h] = _dot_tn(gke128_s[rows(c), h * DK:(h + 1) * DK], gv_s[rows(c), h * DV:(h + 1) * DV])
    fill(1)
    for h in range(HEADS):
        sst = s_ref[h]
        for c in range(nc):
            lhs_g = jnp.concatenate([gq128_s[rows(c), h * DK:(h + 1) * DK], att[c, h]], axis=1)
            rhs_g = jnp.concatenate([sst.astype(BF16), gv_s[rows(c), h * DV:(h + 1) * DV]], axis=0)
            hg_s[rows(c), h * DV:(h + 1) * DV] = _dot(lhs_g, rhs_g)
            i = c * HEADS + h
            sst = e_cols[:, i:i + 1] * sst + gupd[c, h]
        s_ref[h] = sst
        fill(1)

    if meta:
        h1_ref[...] = x
        return

    def head_norm(src, gain_ref):
        outs = []
        for h in range(HEADS):
            t = src[:, h * DV:(h + 1) * DV]
            outs.append(t * lax.rsqrt(jnp.mean(t * t, axis=-1, keepdims=True) + EPS))
        return jnp.concatenate(outs, axis=-1) * gain_ref[...]

    finish_groups(len(groups))
    y_m = head_norm(hm_s, mhg_ref) * ga_s[...]
    merged = gb_s[...] * _dot(y_m.astype(BF16), merge_w[0][...])
    y_g = head_norm(hg_s, ghg_ref) * gc_s[...]
    merged = merged + gd_s[...] * _dot(y_g.astype(BF16), merge_w[1][...])
    h1_ref[...] = x + _dot(merged.astype(BF16), merge_w[2][...])


def _const_spec(shape):
    nd = len(shape)
    return pl.BlockSpec(shape, lambda i: (0,) * nd, pipeline_mode=pl.Buffered(1))


def _mixer_call(x2, consts, state, *, tm, meta):
    t = x2.shape[0]
    state_shapes = [jax.ShapeDtypeStruct(a.shape, F32) for a in state]
    row_spec = pl.BlockSpec((tm, D_MODEL), lambda i: (i, 0))
    state_specs = [pl.BlockSpec(a.shape, lambda i, nd=a.ndim: (0,) * nd) for a in state]
    hbm_operands = (1,) if meta else LEAD_UNUSED
    packed = [jax.ShapeDtypeStruct((D_MODEL, N_PACK), BF16),
              jax.ShapeDtypeStruct((2 * SUBLANES + G_RANK, D_MODEL), BF16)] if meta else []
    return pl.pallas_call(
        functools.partial(_mixer_kernel, tm=tm, meta=meta),
        grid=(t // tm,),
        in_specs=[row_spec] + [pl.BlockSpec(memory_space=pl.ANY) if i in hbm_operands else _const_spec(c.shape)
                               for i, c in enumerate(consts)] + state_specs,
        out_specs=[row_spec] + state_specs + [pl.BlockSpec(a.shape, lambda i: (0, 0)) for a in packed],
        out_shape=[jax.ShapeDtypeStruct((t, D_MODEL), F32)] + state_shapes + packed,
        scratch_shapes=[
            pltpu.VMEM((tm, D_MODEL), BF16),
            pltpu.VMEM((tm + SUBLANES, 2 * QK), F32),
            pltpu.VMEM((tm, QK), F32),
            pltpu.VMEM((tm, QK), BF16),
            pltpu.VMEM((tm, QK), BF16),
            pltpu.VMEM((tm, HEADS * CN), BF16),
            pltpu.VMEM((tm, QK), BF16),
            pltpu.VMEM((tm, QK), BF16),
            pltpu.VMEM((tm, QK), BF16),
            pltpu.VMEM((tm, QK), BF16),
            pltpu.VMEM((tm, QK), BF16),
            pltpu.VMEM((tm, D_MODEL), BF16),
            pltpu.VMEM((tm, D_MODEL), F32),
            pltpu.VMEM((tm, D_MODEL), F32),
            pltpu.VMEM((tm, D_MODEL), F32),
            pltpu.VMEM((tm, D_MODEL), F32),
            pltpu.VMEM((tm, D_MODEL), F32),
            pltpu.VMEM((tm, D_MODEL), F32),
            pltpu.VMEM((tm, D_MODEL), F32),
            pltpu.VMEM((RCHUNK, RCHUNK), BF16),
            pltpu.VMEM((tm, tm), BF16),
            pltpu.VMEM((LANES, 2 * HEADS * LANES), BF16),
        ] + ([
            pltpu.VMEM((REPACK_SLOTS, PIECE, D_MODEL), F32),
            pltpu.VMEM((SUBLANES + G_RANK, D_MODEL), F32),
            pltpu.SemaphoreType.DMA((REPACK_SLOTS,)),
            pltpu.SemaphoreType.DMA((2,)),
        ] if meta else [
            pltpu.VMEM((D_MODEL, D_MODEL), BF16),
            pltpu.VMEM((D_MODEL, D_MODEL), BF16),
            pltpu.VMEM((D_MODEL, D_MODEL), BF16),
            pltpu.VMEM((2, MERGE_ROWS, D_MODEL), F32),
            pltpu.SemaphoreType.DMA((2,)),
        ]),
        compiler_params=pltpu.CompilerParams(
            dimension_semantics=("arbitrary",), vmem_limit_bytes=VMEM_LIMIT),
        name="mixer_meta" if meta else "mixer",
    )(x2, *consts, *state)


def _ffn_kernel(h_ref, g2_ref, wg_hbm, wu_hbm, wd_hbm, gf_ref, o_ref,
                wg_ref, wu_ref, wd_ref, sg, su, sd, sem, *, fc, parts):
    n_chunks = wg_ref.shape[1] // fc

    def chunk_copies(f, slot):
        cols = pl.ds(f * fc, fc)
        return (pltpu.make_async_copy(wg_hbm.at[:, cols], sg.at[slot], sem.at[slot, 0]),
                pltpu.make_async_copy(wu_hbm.at[:, cols], su.at[slot], sem.at[slot, 1]),
                pltpu.make_async_copy(wd_hbm.at[cols, :], sd.at[slot], sem.at[slot, 2]))

    @pl.when(pl.program_id(0) == 0)
    def _():
        for cp in chunk_copies(0, 0):
            cp.start()
        for f in range(n_chunks):
            slot = f % 2
            if f + 1 < n_chunks:
                for cp in chunk_copies(f + 1, 1 - slot):
                    cp.start()
            for cp in chunk_copies(f, slot):
                cp.wait()
            wg_ref[:, f * fc:(f + 1) * fc] = sg[slot].astype(BF16)
            wu_ref[:, f * fc:(f + 1) * fc] = su[slot].astype(BF16)
            wd_ref[f * fc:(f + 1) * fc, :] = sd[slot].astype(BF16)

    rows = h_ref.shape[0] // parts
    blocks = [slice(p * rows, (p + 1) * rows) for p in range(parts)]
    acc, hn = [], []
    for blk in blocks:
        h = h_ref[blk, :]
        hn.append((h * lax.rsqrt(jnp.mean(h * h, axis=-1, keepdims=True) + EPS) * g2_ref[...]).astype(BF16))
        acc.append(h)
    for f in range(0, wg_ref.shape[1], fc):
        for p in range(parts):
            gate = _dot(hn[p], wg_ref[:, f:f + fc])
            up = _dot(hn[p], wu_ref[:, f:f + fc])
            acc[p] = acc[p] + _dot((gate * _sigmoid(gate) * up).astype(BF16), wd_ref[f:f + fc, :])
    for p, blk in enumerate(blocks):
        a = acc[p]
        o_ref[blk, :] = a * lax.rsqrt(jnp.mean(a * a, axis=-1, keepdims=True) + EPS) * gf_ref[...]


def _ffn_call(h1, g2, wg, wu, wd, gf, *, tm, fc, parts):
    t = h1.shape[0]
    d_ff = wg.shape[1]
    row_spec = pl.BlockSpec((tm, D_MODEL), lambda i: (i, 0))
    hbm_spec = pl.BlockSpec(memory_space=pl.ANY)
    return pl.pallas_call(
        functools.partial(_ffn_kernel, fc=fc, parts=parts),
        grid=(t // tm,),
        in_specs=[row_spec, _const_spec(g2.shape), hbm_spec, hbm_spec, hbm_spec, _const_spec(gf.shape)],
        out_specs=row_spec,
        out_shape=jax.ShapeDtypeStruct((t, D_MODEL), F32),
        scratch_shapes=[
            pltpu.VMEM((D_MODEL, d_ff), BF16),
            pltpu.VMEM((D_MODEL, d_ff), BF16),
            pltpu.VMEM((d_ff, D_MODEL), BF16),
            pltpu.VMEM((2, D_MODEL, fc), F32),
            pltpu.VMEM((2, D_MODEL, fc), F32),
            pltpu.VMEM((2, fc, D_MODEL), F32),
            pltpu.SemaphoreType.DMA((2, 3)),
        ],
        compiler_params=pltpu.CompilerParams(
            dimension_semantics=("arbitrary",), vmem_limit_bytes=VMEM_LIMIT),
        name="ffn",
    )(h1, g2, wg, wu, wd, gf)


def kernel(x, meta_tokens, norm1_g, w_in, conv_w, conv_b, m_gate_b, g_a2, g_a2_b, m_head_g, g_head_g,
           w_branch_m, w_branch_g, w_out, norm2_g, w_ff_gate, w_ff_up, w_ff_down, final_g):
    bsz, seq, d = x.shape
    assert bsz == 1 and d == D_MODEL and norm1_g.shape[0] == 1 and seq % CHUNK == 0
    row = lambda a: a.reshape(1, -1).astype(F32)

    gate_bias = jnp.zeros((2 * SUBLANES, LANES), F32)
    gate_bias = gate_bias.at[:HEADS].set(m_gate_b[0, 0][:, None]).at[SUBLANES:SUBLANES + HEADS].set(m_gate_b[0, 1][:, None])
    a2 = jnp.zeros((LANES, QK), F32).at[GA_LANE:GA_LANE + G_RANK].set(g_a2[0]).astype(BF16)
    conv_w8 = jnp.zeros((SUBLANES, 2 * QK), F32).at[:CONV_W].set(conv_w[0])
    consts = (gate_bias, a2, row(g_a2_b[0]), conv_w8, row(conv_b[0]), row(m_head_g[0]), row(g_head_g[0]),
              w_branch_m[0].astype(F32), w_branch_g[0].astype(F32), w_out[0].astype(F32))

    zero_state = (jnp.zeros((SUBLANES, 2 * QK), F32), jnp.zeros((HEADS, DK, CN), F32),
                  jnp.zeros((SUBLANES, LANES), F32), jnp.zeros((HEADS, DK, DV), F32))
    lead = jnp.concatenate([jnp.zeros((RCHUNK - N_META, d), F32), meta_tokens.astype(F32)], axis=0)
    unused = jnp.zeros((SUBLANES, LANES), BF16)
    g1 = row(norm1_g[0])
    lead_consts = (g1, w_in[0].T.astype(F32), unused) + consts[:-len(LEAD_UNUSED)] + (unused,) * len(LEAD_UNUSED)
    _, *state, w_packed, w_narrow = _mixer_call(lead, lead_consts, zero_state, tm=RCHUNK, meta=True)
    h1, *_ = _mixer_call(x[0], (g1, w_packed, w_narrow) + consts, tuple(state), tm=512, meta=False)
    out = _ffn_call(h1, row(norm2_g[0]), w_ff_gate[0].astype(F32), w_ff_up[0].astype(F32),
                    w_ff_down[0].astype(F32), row(final_g), tm=1024, fc=256, parts=2)
    return out[None]
```

```python
import functools

import jax
import jax.numpy as jnp
from jax import lax
from jax.experimental import pallas as pl
from jax.experimental.pallas import tpu as pltpu

F32 = jnp.float32
BF16 = jnp.bfloat16

D_MODEL = 1024
N_META = 16
CHUNK = 64
RCHUNK = 128
EPS = 1e-6
HEADS = 4
DV = D_MODEL // HEADS
DK = DV // 2
QK = HEADS * DK
G_RANK = 16
G_TAU = 16.0
CONV_W = 4
LANES = 128
SUBLANES = 8
NEG_BIG = -1e30
VMEM_LIMIT = 60 * 1024 * 1024
CN = DV + LANES

W_IN_RANGES = ((0, 2048), (2056, 5128), (5144, 8216))
N_PACK = sum(b - a for a, b in W_IN_RANGES)
O_MQK, O_MV, O_MO = 0, 1024, 2048
O_GQ, O_GV = 3072, 4096
O_GR, O_GATE_M, O_GATE_G = 5120, 6144, 7168
MI_COL, GA_COL = 2048, 5128
GA_LANE = 8
PIECE = 512
MERGE_ROWS = 256
REPACK_AHEAD = 3
REPACK_SLOTS = REPACK_AHEAD + 1
LEAD_UNUSED = (10, 11, 12)

NT_DIMS = (((1,), (1,)), ((), ()))
TN_DIMS = (((0,), (0,)), ((), ()))


def _dot(a, b):
    return jnp.dot(a, b, preferred_element_type=F32)


def _dot_nt(a, b):
    return lax.dot_general(a, b, NT_DIMS, preferred_element_type=F32)


def _dot_tn(a, b):
    return lax.dot_general(a, b, TN_DIMS, preferred_element_type=F32)


def _pieces(x):
    hi = x.astype(BF16).astype(F32)
    mid = (x - hi).astype(BF16).astype(F32)
    lo = (x - hi - mid).astype(BF16).astype(F32)
    return [hi, mid, lo]


def _log_sigmoid(x):
    return jnp.minimum(x, 0.0) - jnp.log(1.0 + jnp.exp(-jnp.abs(x)))


def _sigmoid(x):
    return 1.0 / (1.0 + jnp.exp(-x))


def _repack_w_in(wt_hbm, w_ref, wst_ref, stage, narrow, sem, nsem):
    (_, a_end), (b_start, b_end), (c_start, _) = W_IN_RANGES
    first_b, first_c = a_end // PIECE, (a_end + b_end - b_start) // PIECE
    n_pieces = N_PACK // PIECE

    def piece_copy(q, slot):
        row0 = q * PIECE + (b_start - a_end if q >= first_b else 0) + (c_start - b_end if q >= first_c else 0)
        return pltpu.make_async_copy(wt_hbm.at[pl.ds(row0, PIECE), :], stage.at[slot], sem.at[slot])

    narrow_copies = (
        pltpu.make_async_copy(wt_hbm.at[pl.ds(MI_COL, SUBLANES), :], narrow.at[pl.ds(0, SUBLANES), :], nsem.at[0]),
        pltpu.make_async_copy(wt_hbm.at[pl.ds(GA_COL, G_RANK), :], narrow.at[pl.ds(SUBLANES, G_RANK), :], nsem.at[1]))

    for q in range(REPACK_AHEAD):
        piece_copy(q, q).start()
    for cp in narrow_copies:
        cp.start()
    for q in range(n_pieces):
        if q + REPACK_AHEAD < n_pieces:
            piece_copy(q + REPACK_AHEAD, (q + REPACK_AHEAD) % REPACK_SLOTS).start()
        piece_copy(q, q % REPACK_SLOTS).wait()
        w_ref[:, q * PIECE:(q + 1) * PIECE] = stage[q % REPACK_SLOTS].T.astype(BF16)
    for cp in narrow_copies:
        cp.wait()
    gates = narrow[0:SUBLANES, :]
    zeros = jnp.zeros((SUBLANES - HEADS, D_MODEL), F32)
    wst_ref[...] = jnp.concatenate([gates[:HEADS], zeros, gates[HEADS:], zeros, narrow[SUBLANES:, :]],
                                   axis=0).astype(BF16)


def _mixer_kernel(x_ref, g1_ref, w_ref, wst_ref, sb_ref, a2_ref, a2b_ref, cw_ref, cb_ref,
                  mhg_ref, ghg_ref, wbm_ref, wbg_ref, wout_ref, *rest, tm, meta):
    nc = tm // RCHUNK
    pairs = [(c, h) for c in range(nc) for h in range(HEADS)]
    if not meta:
        tail0_ref, cn0_ref, m0_ref, s0_ref, *rest = rest
    h1_ref, tail_ref, cn_ref, m_ref, s_ref, *rest = rest
    if meta:
        wt_hbm = w_ref
        w_ref, wst_ref, *rest = rest
        *rest, stage, narrow, sem, nsem = rest
        _repack_w_in(wt_hbm, w_ref, wst_ref, stage, narrow, sem, nsem)
    else:
        *rest, wbm_s, wbg_s, wout_s, wstage, wsem = rest
        merge_w = (wbm_s, wbg_s, wout_s)
    (xn_s, qk_s, q_s, k_s, wk_s, vo_s, gq64_s, gq128_s, gki_s, gke64_s, gke128_s, gv_s, hm_s, hg_s,
     ga_s, gb_s, gc_s, gd_s, gqk_s, tric_s, triu_s, spread_s) = rest

    @pl.when(pl.program_id(0) == 0)
    def _():
        if meta:
            qk_s[0:SUBLANES, :] = jnp.zeros((SUBLANES, 2 * QK), F32)
            cn_ref[...] = jnp.zeros(cn_ref.shape, F32)
            m_ref[...] = jnp.zeros(m_ref.shape, F32)
            s_ref[...] = jnp.zeros(s_ref.shape, F32)
        else:
            qk_s[0:SUBLANES, :] = tail0_ref[...]
            cn_ref[...] = cn0_ref[...]
            m_ref[...] = m0_ref[...]
            s_ref[...] = s0_ref[...]
        for h in range(HEADS):
            vo_s[:, h * CN + DV:(h + 1) * CN] = jnp.ones((tm, LANES), BF16)
        if not meta:
            srcs = (wbm_ref, wbg_ref, wout_ref)
            n_blk = D_MODEL // MERGE_ROWS

            def block_copy(j, slot):
                return pltpu.make_async_copy(srcs[j // n_blk].at[pl.ds((j % n_blk) * MERGE_ROWS, MERGE_ROWS), :],
                                             wstage.at[slot], wsem.at[slot])

            block_copy(0, 0).start()
            for j in range(len(srcs) * n_blk):
                slot = j % 2
                if j + 1 < len(srcs) * n_blk:
                    block_copy(j + 1, 1 - slot).start()
                block_copy(j, slot).wait()
                r0 = (j % n_blk) * MERGE_ROWS
                merge_w[j // n_blk][r0:r0 + MERGE_ROWS, :] = wstage[slot].astype(BF16)
        row = lax.broadcasted_iota(jnp.int32, (tm, tm), 0)
        col = lax.broadcasted_iota(jnp.int32, (tm, tm), 1)
        tric_s[...] = jnp.where((col <= row) & (col >= (row & -CHUNK)), 1.0, 0.0)[:RCHUNK, :RCHUNK].astype(BF16)
        triu_s[...] = jnp.where((row <= col) & (row >= (col & -RCHUNK)), 1.0, 0.0).astype(BF16)
        sr = lax.broadcasted_iota(jnp.int32, (LANES, 2 * HEADS * LANES), 0)
        sc = lax.broadcasted_iota(jnp.int32, (LANES, 2 * HEADS * LANES), 1)
        blk = (sc >> 9) * (3 * SUBLANES)
        spread_s[...] = jnp.where(
            (sr >= blk) & (sr < blk + 3 * SUBLANES) & ((sr & (SUBLANES - 1)) == ((sc >> 7) & (HEADS - 1))),
            1.0, 0.0).astype(BF16)

    if meta:
        x = jnp.concatenate([jnp.zeros((tm - N_META, D_MODEL), F32), x_ref[...]], axis=0)
    else:
        x = x_ref[...]
    xn = x * lax.rsqrt(jnp.mean(x * x, axis=-1, keepdims=True) + EPS) * g1_ref[...]
    if meta:
        valid = lax.broadcasted_iota(jnp.int32, (tm, 1), 0) >= (tm - N_META)
        xn = jnp.where(valid, xn, 0.0)
    xn_s[...] = xn.astype(BF16)

    def proj(off, width):
        return _dot(xn_s[...], w_ref[:, off:off + width])

    def rows(c):
        return slice(c * RCHUNK, (c + 1) * RCHUNK)

    def half(c, second):
        lo = c * RCHUNK + (CHUNK if second else 0)
        return slice(lo, lo + CHUNK)

    crow = lax.broadcasted_iota(jnp.int32, (RCHUNK, RCHUNK), 0)
    ccol = lax.broadcasted_iota(jnp.int32, (RCHUNK, RCHUNK), 1)
    causal = ccol <= crow
    tri_c = tric_s[...]
    tri_up = triu_s[...]
    spread = spread_s[...]

    def st_qk(i, v):
        qk_s[SUBLANES:SUBLANES + tm, i * DV:(i + 1) * DV] = v

    def st_mv(i, v):
        vo_s[:, i * CN:i * CN + DV] = v.astype(BF16)

    def st_gqk(i, v):
        gqk_s[:, i * DV:(i + 1) * DV] = v

    def st_gv(i, v):
        gv_s[:, i * DV:(i + 1) * DV] = v.astype(BF16)

    def st_mo(i, v):
        ga_s[:, i * DV:(i + 1) * DV] = _sigmoid(v)

    def st_gm(i, v):
        gb_s[:, i * DV:(i + 1) * DV] = _sigmoid(v)

    def st_gr(i, v):
        gc_s[:, i * DV:(i + 1) * DV] = v * _sigmoid(v)

    def st_gg(i, v):
        gd_s[:, i * DV:(i + 1) * DV] = _sigmoid(v)

    groups = [(O_MQK, st_qk), (O_MV, st_mv), (O_GQ, st_gqk), (O_GV, st_gv)]
    if not meta:
        groups += [(O_MO, st_mo), (O_GATE_M, st_gm), (O_GR, st_gr), (O_GATE_G, st_gg)]
    queue = [(seg, i, store) for seg, store in groups for i in range(D_MODEL // DV)]
    emitted = [0]

    def fill(n):
        for _ in range(n):
            if emitted[0] < len(queue):
                seg, i, store = queue[emitted[0]]
                store(i, proj(seg + i * DV, DV))
                emitted[0] += 1

    def finish_groups(n):
        fill(n * (D_MODEL // DV) - emitted[0])

    narrow = _dot_nt(wst_ref[...], xn_s[...])
    gates = narrow[:2 * SUBLANES, :] + jnp.concatenate([sb_ref[...]] * (tm // LANES), axis=1)
    fill(2)
    s1 = jnp.concatenate([jnp.zeros((GA_LANE, tm), F32), narrow[2 * SUBLANES:, :],
                          jnp.zeros((LANES - GA_LANE - G_RANK, tm), F32)], axis=0).T
    logi = gates[:SUBLANES, :]
    logf = _log_sigmoid(gates[SUBLANES:, :])
    if meta:
        valid_t = lax.broadcasted_iota(jnp.int32, (1, tm), 1) >= (tm - N_META)
        logi = jnp.where(valid_t, logi, NEG_BIG)
        logf = jnp.where(valid_t, logf, 0.0)
    zero8 = jnp.zeros((SUBLANES, tm), F32)
    b4 = _dot(jnp.concatenate(_pieces(logf) + [zero8], axis=0).astype(BF16), tri_up)
    fill(2)
    b_all = b4[:SUBLANES] + b4[SUBLANES:2 * SUBLANES] + b4[2 * SUBLANES:3 * SUBLANES]
    c_all = logi - b_all
    m_run = m_ref[...]
    m_in, w_parts, a_chunk = [], [], []
    for c in range(nc):
        b_c = b_all[:, rows(c)]
        g_c = jnp.broadcast_to(b_c[:, RCHUNK - 1:RCHUNK], (SUBLANES, LANES))
        wlog = g_c + c_all[:, rows(c)]
        m_next = jnp.maximum(g_c + m_run, jnp.max(wlog, axis=1, keepdims=True))
        m_in.append(m_run)
        w_parts.append(jnp.exp(wlog - m_next))
        a_chunk.append(jnp.exp(g_c + m_run - m_next))
        m_run = m_next
    m_ref[...] = m_run
    stacked = jnp.concatenate(
        _pieces(b_all) + _pieces(jnp.concatenate(w_parts, axis=1))
        + [jnp.zeros((LANES - 6 * SUBLANES, tm), F32)], axis=0)
    stacked_t = stacked.T.astype(BF16)
    fill(2)
    tiles = _dot(stacked_t, spread)
    fill(2)
    b_t, w_t = tiles[:, :QK], tiles[:, QK:]

    finish_groups(1)
    for blk in range(2 * QK // DV):
        csl = slice(blk * DV, (blk + 1) * DV)
        conv = cb_ref[:, csl]
        for j in range(CONV_W):
            conv = conv + cw_ref[j:j + 1, csl] * qk_s[pl.ds(SUBLANES - (CONV_W - 1) + j, tm), csl]
        conv = conv * _sigmoid(conv)
        if blk < QK // DV:
            q_s[:, csl] = conv * (DK ** -0.5)
        else:
            ksl = slice(blk * DV - QK, (blk + 1) * DV - QK)
            k_s[:, ksl] = conv.astype(BF16)
            wk_s[:, ksl] = (w_t[:, ksl] * conv).astype(BF16)
        fill(1)
    tail_new = qk_s[tm:tm + SUBLANES, :]
    qk_s[0:SUBLANES, :] = tail_new
    tail_ref[...] = tail_new

    za = _dot(s1.astype(BF16), a2_ref[...]) + a2b_ref[...]
    fill(1)
    loga = _log_sigmoid(za) / G_TAU
    if meta:
        loga = jnp.where(valid, loga, 0.0)
    hi = loga.astype(BF16)
    r1 = loga - hi.astype(F32)
    mid = r1.astype(BF16)
    bc = jnp.concatenate([_dot(tri_c, hi[rows(c), :]) + _dot(tri_c, mid[rows(c), :])
                          for c in range(nc)], axis=0)
    fill(2)
    e_blk = jnp.exp(bc)
    tot_a = [bc[c * RCHUNK + CHUNK - 1:c * RCHUNK + CHUNK, :] for c in range(nc)]
    tot_b = [bc[(c + 1) * RCHUNK - 1:(c + 1) * RCHUNK, :] for c in range(nc)]
    e_a = [jnp.exp(t) for t in tot_a]
    e_b = [jnp.exp(t) for t in tot_b]
    rest = jnp.concatenate([t - bc[half(c, s), :] for c in range(nc) for s, t in ((0, tot_a[c]), (1, tot_b[c]))],
                           axis=0)
    e_rest = jnp.exp(rest)
    ones_row = jnp.ones((1, QK), F32)
    to_chunk = jnp.concatenate([jnp.broadcast_to(f, (CHUNK, QK)) for c in range(nc) for f in (ones_row, e_a[c])],
                               axis=0)
    from_blk = jnp.concatenate([jnp.broadcast_to(f, (CHUNK, QK)) for c in range(nc) for f in (e_b[c], ones_row)],
                               axis=0)
    finish_groups(3)
    gq = gqk_s[:, :QK] * (DK ** -0.5)
    gk = gqk_s[:, QK:]
    gq64_s[...] = (gq * e_blk).astype(BF16)
    gq128_s[...] = (gq * e_blk * to_chunk).astype(BF16)
    gki_s[...] = (gk * jnp.exp(-bc)).astype(BF16)
    gke64_s[...] = (gk * e_rest).astype(BF16)
    gke128_s[...] = (gk * e_rest * from_blk).astype(BF16)
    fill(2)
    e_rows = [(e_a[c] * e_b[c])[:, h * DK:(h + 1) * DK] for (c, h) in pairs]
    e_cols = jnp.concatenate(e_rows + [jnp.zeros((LANES - len(pairs), DK), F32)], axis=0).T

    finish_groups(2)
    dmat, rmax, sim = {}, {}, {}
    for (c, h) in pairs:
        d = b_t[rows(c), h * LANES:(h + 1) * LANES] + c_all[h:h + 1, rows(c)]
        d = jnp.where(causal, d, -jnp.inf)
        dmat[c, h] = d
        rmax[c, h] = jnp.max(d, axis=-1, keepdims=True)
    fill(1)
    for (c, h) in pairs:
        sim[c, h] = _dot_nt(q_s[rows(c), h * DK:(h + 1) * DK].astype(BF16), k_s[rows(c), h * DK:(h + 1) * DK])
    fill(1)
    lhs, emr, upd = {}, {}, {}
    for (c, h) in pairs:
        il = b_t[rows(c), h * LANES:(h + 1) * LANES] + m_in[c][h:h + 1, :]
        m_row = jnp.maximum(il, rmax[c, h])
        wts = jnp.exp(dmat[c, h] - m_row) * sim[c, h]
        aq = jnp.exp(il - m_row) * q_s[rows(c), h * DK:(h + 1) * DK]
        lhs[c, h] = jnp.concatenate([aq.astype(BF16), wts.astype(BF16)], axis=1)
        emr[c, h] = jnp.exp(-m_row)
    fill(1)
    for (c, h) in pairs:
        upd[c, h] = _dot_tn(wk_s[rows(c), h * DK:(h + 1) * DK], vo_s[rows(c), h * CN:(h + 1) * CN])
    fill(1)
    res = {}
    for h in range(HEADS):
        state = cn_ref[h]
        for c in range(nc):
            rhs = jnp.concatenate([state.astype(BF16), vo_s[rows(c), h * CN:(h + 1) * CN]], axis=0)
            res[c, h] = _dot(lhs[c, h], rhs)
            a = a_chunk[c][h:h + 1, :]
            state = jnp.concatenate([a] * (CN // LANES), axis=1) * state + upd[c, h]
        cn_ref[h] = state
        fill(1)
    for (c, h) in pairs:
        inv = 1.0 / jnp.maximum(jnp.abs(res[c, h][:, DV:]), emr[c, h])
        hm_s[rows(c), h * DV:h * DV + LANES] = res[c, h][:, :LANES] * inv
        hm_s[rows(c), h * DV + LANES:(h + 1) * DV] = res[c, h][:, LANES:DV] * inv
    fill(1)

    finish_groups(4)
    att, gupd = {}, {}
    zblk = jnp.zeros((CHUNK, DK), BF16)
    for (c, h) in pairs:
        dsl = slice(h * DK, (h + 1) * DK)
        ra, rb = half(c, 0), half(c, 1)
        qp = jnp.concatenate([jnp.concatenate([gq64_s[ra, dsl], zblk], axis=1),
                              jnp.concatenate([zblk, gq64_s[rb, dsl]], axis=1)], axis=0)
        kp = jnp.concatenate([jnp.concatenate([gki_s[ra, dsl], gke64_s[ra, dsl]], axis=1),
                              jnp.concatenate([zblk, gki_s[rb, dsl]], axis=1)], axis=0)
        att[c, h] = jnp.where(causal, _dot_nt(qp, kp), 0.0).astype(BF16)
    fill(1)
    for (c, h) in pairs:
        gupd[c, h] = _dot_tn(gke128_s[rows(c), h * DK:(h + 1) * DK], gv_s[rows(c), h * DV:(h + 1) * DV])
    fill(1)
    for h in range(HEADS):
        sst = s_ref[h]
        for c in range(nc):
            lhs_g = jnp.concatenate([gq128_s[rows(c), h * DK:(h + 1) * DK], att[c, h]], axis=1)
            rhs_g = jnp.concatenate([sst.astype(BF16), gv_s[rows(c), h * DV:(h + 1) * DV]], axis=0)
            hg_s[rows(c), h * DV:(h + 1) * DV] = _dot(lhs_g, rhs_g)
            i = c * HEADS + h
            sst = e_cols[:, i:i + 1] * sst + gupd[c, h]
        s_ref[h] = sst
        fill(1)

    if meta:
        h1_ref[...] = x
        return

    def head_norm(src, gain_ref):
        outs = []
        for h in range(HEADS):
            t = src[:, h * DV:(h + 1) * DV]
            outs.append(t * lax.rsqrt(jnp.mean(t * t, axis=-1, keepdims=True) + EPS))
        return jnp.concatenate(outs, axis=-1) * gain_ref[...]

    finish_groups(len(groups))
    y_m = head_norm(hm_s, mhg_ref) * ga_s[...]
    merged = gb_s[...] * _dot(y_m.astype(BF16), merge_w[0][...])
    y_g = head_norm(hg_s, ghg_ref) * gc_s[...]
    merged = merged + gd_s[...] * _dot(y_g.astype(BF16), merge_w[1][...])
    h1_ref[...] = x + _dot(merged.astype(BF16), merge_w[2][...])


def _const_spec(shape):
    nd = len(shape)
    return pl.BlockSpec(shape, lambda i: (0,) * nd, pipeline_mode=pl.Buffered(1))


def _mixer_call(x2, consts, state, *, tm, meta):
    t = tm if meta else x2.shape[0]
    state_shapes = [jax.ShapeDtypeStruct(a.shape, F32) for a in state]
    row_spec = pl.BlockSpec((tm, D_MODEL), lambda i: (i, 0))
    x_spec = pl.BlockSpec(x2.shape, lambda i: (0, 0)) if meta else row_spec
    state_specs = [pl.BlockSpec(a.shape, lambda i, nd=len(a.shape): (0,) * nd) for a in state]
    hbm_operands = (1,) if meta else LEAD_UNUSED
    packed = [jax.ShapeDtypeStruct((D_MODEL, N_PACK), BF16),
              jax.ShapeDtypeStruct((2 * SUBLANES + G_RANK, D_MODEL), BF16)] if meta else []
    return pl.pallas_call(
        functools.partial(_mixer_kernel, tm=tm, meta=meta),
        grid=(t // tm,),
        in_specs=[x_spec] + [pl.BlockSpec(memory_space=pl.ANY) if i in hbm_operands else _const_spec(c.shape)
                             for i, c in enumerate(consts)] + ([] if meta else state_specs),
        out_specs=[row_spec] + state_specs + [pl.BlockSpec(a.shape, lambda i: (0, 0)) for a in packed],
        out_shape=[jax.ShapeDtypeStruct((t, D_MODEL), F32)] + state_shapes + packed,
        scratch_shapes=[
            pltpu.VMEM((tm, D_MODEL), BF16),
            pltpu.VMEM((tm + SUBLANES, 2 * QK), F32),
            pltpu.VMEM((tm, QK), F32),
            pltpu.VMEM((tm, QK), BF16),
            pltpu.VMEM((tm, QK), BF16),
            pltpu.VMEM((tm, HEADS * CN), BF16),
            pltpu.VMEM((tm, QK), BF16),
            pltpu.VMEM((tm, QK), BF16),
            pltpu.VMEM((tm, QK), BF16),
            pltpu.VMEM((tm, QK), BF16),
            pltpu.VMEM((tm, QK), BF16),
            pltpu.VMEM((tm, D_MODEL), BF16),
            pltpu.VMEM((tm, D_MODEL), F32),
            pltpu.VMEM((tm, D_MODEL), F32),
            pltpu.VMEM((tm, D_MODEL), F32),
            pltpu.VMEM((tm, D_MODEL), F32),
            pltpu.VMEM((tm, D_MODEL), F32),
            pltpu.VMEM((tm, D_MODEL), F32),
            pltpu.VMEM((tm, D_MODEL), F32),
            pltpu.VMEM((RCHUNK, RCHUNK), BF16),
            pltpu.VMEM((tm, tm), BF16),
            pltpu.VMEM((LANES, 2 * HEADS * LANES), BF16),
        ] + ([
            pltpu.VMEM((REPACK_SLOTS, PIECE, D_MODEL), F32),
            pltpu.VMEM((SUBLANES + G_RANK, D_MODEL), F32),
            pltpu.SemaphoreType.DMA((REPACK_SLOTS,)),
            pltpu.SemaphoreType.DMA((2,)),
        ] if meta else [
            pltpu.VMEM((D_MODEL, D_MODEL), BF16),
            pltpu.VMEM((D_MODEL, D_MODEL), BF16),
            pltpu.VMEM((D_MODEL, D_MODEL), BF16),
            pltpu.VMEM((2, MERGE_ROWS, D_MODEL), F32),
            pltpu.SemaphoreType.DMA((2,)),
        ]),
        compiler_params=pltpu.CompilerParams(
            dimension_semantics=("arbitrary",), vmem_limit_bytes=VMEM_LIMIT),
        name="mixer_meta" if meta else "mixer",
    )(x2, *consts, *([] if meta else state))


def _ffn_kernel(h_ref, g2_ref, wg_hbm, wu_hbm, wd_hbm, gf_ref, o_ref,
                wg_ref, wu_ref, wd_ref, sg, su, sd, sem, *, fc, parts):
    n_chunks = wg_ref.shape[1] // fc

    def chunk_copies(f, slot):
        cols = pl.ds(f * fc, fc)
        return (pltpu.make_async_copy(wg_hbm.at[:, cols], sg.at[slot], sem.at[slot, 0]),
                pltpu.make_async_copy(wu_hbm.at[:, cols], su.at[slot], sem.at[slot, 1]),
                pltpu.make_async_copy(wd_hbm.at[cols, :], sd.at[slot], sem.at[slot, 2]))

    @pl.when(pl.program_id(0) == 0)
    def _():
        for cp in chunk_copies(0, 0):
            cp.start()
        for f in range(n_chunks):
            slot = f % 2
            if f + 1 < n_chunks:
                for cp in chunk_copies(f + 1, 1 - slot):
                    cp.start()
            for cp in chunk_copies(f, slot):
                cp.wait()
            wg_ref[:, f * fc:(f + 1) * fc] = sg[slot].astype(BF16)
            wu_ref[:, f * fc:(f + 1) * fc] = su[slot].astype(BF16)
            wd_ref[f * fc:(f + 1) * fc, :] = sd[slot].astype(BF16)

    rows = h_ref.shape[0] // parts
    blocks = [slice(p * rows, (p + 1) * rows) for p in range(parts)]
    acc, hn = [], []
    for blk in blocks:
        h = h_ref[blk, :]
        hn.append((h * lax.rsqrt(jnp.mean(h * h, axis=-1, keepdims=True) + EPS) * g2_ref[...]).astype(BF16))
        acc.append(h)
    for f in range(0, wg_ref.shape[1], fc):
        for p in range(parts):
            gate = _dot(hn[p], wg_ref[:, f:f + fc])
            up = _dot(hn[p], wu_ref[:, f:f + fc])
            acc[p] = acc[p] + _dot((gate * _sigmoid(gate) * up).astype(BF16), wd_ref[f:f + fc, :])
    for p, blk in enumerate(blocks):
        a = acc[p]
        o_ref[blk, :] = a * lax.rsqrt(jnp.mean(a * a, axis=-1, keepdims=True) + EPS) * gf_ref[...]


def _ffn_call(h1, g2, wg, wu, wd, gf, *, tm, fc, parts):
    t = h1.shape[0]
    d_ff = wg.shape[1]
    row_spec = pl.BlockSpec((tm, D_MODEL), lambda i: (i, 0))
    hbm_spec = pl.BlockSpec(memory_space=pl.ANY)
    return pl.pallas_call(
        functools.partial(_ffn_kernel, fc=fc, parts=parts),
        grid=(t // tm,),
        in_specs=[row_spec, _const_spec(g2.shape), hbm_spec, hbm_spec, hbm_spec, _const_spec(gf.shape)],
        out_specs=row_spec,
        out_shape=jax.ShapeDtypeStruct((t, D_MODEL), F32),
        scratch_shapes=[
            pltpu.VMEM((D_MODEL, d_ff), BF16),
            pltpu.VMEM((D_MODEL, d_ff), BF16),
            pltpu.VMEM((d_ff, D_MODEL), BF16),
            pltpu.VMEM((2, D_MODEL, fc), F32),
            pltpu.VMEM((2, D_MODEL, fc), F32),
            pltpu.VMEM((2, fc, D_MODEL), F32),
            pltpu.SemaphoreType.DMA((2, 3)),
        ],
        compiler_params=pltpu.CompilerParams(
            dimension_semantics=("arbitrary",), vmem_limit_bytes=VMEM_LIMIT),
        name="ffn",
    )(h1, g2, wg, wu, wd, gf)


def kernel(x, meta_tokens, norm1_g, w_in, conv_w, conv_b, m_gate_b, g_a2, g_a2_b, m_head_g, g_head_g,
           w_branch_m, w_branch_g, w_out, norm2_g, w_ff_gate, w_ff_up, w_ff_down, final_g):
    bsz, seq, d = x.shape
    assert bsz == 1 and d == D_MODEL and norm1_g.shape[0] == 1 and seq % CHUNK == 0
    row = lambda a: a.reshape(1, -1).astype(F32)

    gate_bias = jnp.zeros((2 * SUBLANES, LANES), F32)
    gate_bias = gate_bias.at[:HEADS].set(m_gate_b[0, 0][:, None]).at[SUBLANES:SUBLANES + HEADS].set(m_gate_b[0, 1][:, None])
    a2 = jnp.zeros((LANES, QK), F32).at[GA_LANE:GA_LANE + G_RANK].set(g_a2[0]).astype(BF16)
    conv_w8 = jnp.zeros((SUBLANES, 2 * QK), F32).at[:CONV_W].set(conv_w[0])
    consts = (gate_bias, a2, row(g_a2_b[0]), conv_w8, row(conv_b[0]), row(m_head_g[0]), row(g_head_g[0]),
              w_branch_m[0].astype(F32), w_branch_g[0].astype(F32), w_out[0].astype(F32))

    state_shapes = (jax.ShapeDtypeStruct((SUBLANES, 2 * QK), F32), jax.ShapeDtypeStruct((HEADS, DK, CN), F32),
                    jax.ShapeDtypeStruct((SUBLANES, LANES), F32), jax.ShapeDtypeStruct((HEADS, DK, DV), F32))
    unused = jnp.zeros((SUBLANES, LANES), BF16)
    g1 = row(norm1_g[0])
    lead_consts = (g1, w_in[0].T.astype(F32), unused) + consts[:-len(LEAD_UNUSED)] + (unused,) * len(LEAD_UNUSED)
    _, *state, w_packed, w_narrow = _mixer_call(meta_tokens.astype(F32), lead_consts, state_shapes, tm=RCHUNK,
                                                meta=True)
    h1, *_ = _mixer_call(x[0], (g1, w_packed, w_narrow) + consts, tuple(state), tm=512, meta=False)
    out = _ffn_call(h1, row(norm2_g[0]), w_ff_gate[0].astype(F32), w_ff_up[0].astype(F32),
                    w_ff_down[0].astype(F32), row(final_g), tm=1024, fc=256, parts=2)
    return out[None]
```

```python
import functools

import jax
import jax.numpy as jnp
from jax import lax
from jax.experimental import pallas as pl
from jax.experimental.pallas import tpu as pltpu

F32 = jnp.float32
BF16 = jnp.bfloat16

D_MODEL = 1024
N_META = 16
CHUNK = 64
RCHUNK = 128
EPS = 1e-6
HEADS = 4
DV = D_MODEL // HEADS
DK = DV // 2
QK = HEADS * DK
G_RANK = 16
G_TAU = 16.0
CONV_W = 4
LANES = 128
SUBLANES = 8
NEG_BIG = -1e30
VMEM_LIMIT = 60 * 1024 * 1024
CN = DV + LANES

W_IN_RANGES = ((0, 2048), (2056, 5128), (5144, 8216))
N_PACK = sum(b - a for a, b in W_IN_RANGES)
O_MQK, O_MV, O_MO = 0, 1024, 2048
O_GQ, O_GV = 3072, 4096
O_GR, O_GATE_M, O_GATE_G = 5120, 6144, 7168
MI_COL, GA_COL = 2048, 5128
GA_LANE = 8
PIECE = 512
MERGE_ROWS = 256
REPACK_AHEAD = 3
REPACK_SLOTS = REPACK_AHEAD + 1
LEAD_UNUSED = (10, 11, 12)

NT_DIMS = (((1,), (1,)), ((), ()))
TN_DIMS = (((0,), (0,)), ((), ()))


def _dot(a, b):
    return jnp.dot(a, b, preferred_element_type=F32)


def _dot_nt(a, b):
    return lax.dot_general(a, b, NT_DIMS, preferred_element_type=F32)


def _dot_tn(a, b):
    return lax.dot_general(a, b, TN_DIMS, preferred_element_type=F32)


def _pieces(x):
    hi = x.astype(BF16).astype(F32)
    mid = (x - hi).astype(BF16).astype(F32)
    lo = (x - hi - mid).astype(BF16).astype(F32)
    return [hi, mid, lo]


def _log_sigmoid(x):
    return jnp.minimum(x, 0.0) - jnp.log(1.0 + jnp.exp(-jnp.abs(x)))


def _sigmoid(x):
    return 1.0 / (1.0 + jnp.exp(-x))


def _repack_w_in(wt_hbm, w_ref, wst_ref, stage, narrow, sem, nsem):
    (_, a_end), (b_start, b_end), (c_start, _) = W_IN_RANGES
    first_b, first_c = a_end // PIECE, (a_end + b_end - b_start) // PIECE
    n_pieces = N_PACK // PIECE

    def piece_copy(q, slot):
        row0 = q * PIECE + (b_start - a_end if q >= first_b else 0) + (c_start - b_end if q >= first_c else 0)
        return pltpu.make_async_copy(wt_hbm.at[pl.ds(row0, PIECE), :], stage.at[slot], sem.at[slot])

    narrow_copies = (
        pltpu.make_async_copy(wt_hbm.at[pl.ds(MI_COL, SUBLANES), :], narrow.at[pl.ds(0, SUBLANES), :], nsem.at[0]),
        pltpu.make_async_copy(wt_hbm.at[pl.ds(GA_COL, G_RANK), :], narrow.at[pl.ds(SUBLANES, G_RANK), :], nsem.at[1]))

    for q in range(REPACK_AHEAD):
        piece_copy(q, q).start()
    for cp in narrow_copies:
        cp.start()
    for q in range(n_pieces):
        if q + REPACK_AHEAD < n_pieces:
            piece_copy(q + REPACK_AHEAD, (q + REPACK_AHEAD) % REPACK_SLOTS).start()
        piece_copy(q, q % REPACK_SLOTS).wait()
        w_ref[:, q * PIECE:(q + 1) * PIECE] = stage[q % REPACK_SLOTS].T.astype(BF16)
    for cp in narrow_copies:
        cp.wait()
    gates = narrow[0:SUBLANES, :]
    zeros = jnp.zeros((SUBLANES - HEADS, D_MODEL), F32)
    wst_ref[...] = jnp.concatenate([gates[:HEADS], zeros, gates[HEADS:], zeros, narrow[SUBLANES:, :]],
                                   axis=0).astype(BF16)


def _mixer_kernel(x_ref, g1_ref, w_ref, wst_ref, sb_ref, a2_ref, a2b_ref, cw_ref, cb_ref,
                  mhg_ref, ghg_ref, wbm_ref, wbg_ref, wout_ref, *rest, tm, meta):
    nc = tm // RCHUNK
    pairs = [(c, h) for c in range(nc) for h in range(HEADS)]
    if not meta:
        tail0_ref, cn0_ref, m0_ref, s0_ref, *rest = rest
    h1_ref, tail_ref, cn_ref, m_ref, s_ref, *rest = rest
    if meta:
        wt_hbm = w_ref
        w_ref, wst_ref, *rest = rest
        *rest, stage, narrow, sem, nsem = rest
        _repack_w_in(wt_hbm, w_ref, wst_ref, stage, narrow, sem, nsem)
    else:
        *rest, wbm_s, wbg_s, wout_s, wstage, wsem = rest
        merge_w = (wbm_s, wbg_s, wout_s)
    (xn_s, qk_s, q_s, k_s, wk_s, vo_s, gq64_s, gq128_s, gki_s, gke64_s, gke128_s, gv_s, hm_s, hg_s,
     ga_s, gb_s, gc_s, gd_s, gqk_s, tric_s, triu_s, spread_s) = rest

    @pl.when(pl.program_id(0) == 0)
    def _():
        if meta:
            qk_s[0:SUBLANES, :] = jnp.zeros((SUBLANES, 2 * QK), F32)
            cn_ref[...] = jnp.zeros(cn_ref.shape, F32)
            m_ref[...] = jnp.zeros(m_ref.shape, F32)
            s_ref[...] = jnp.zeros(s_ref.shape, F32)
        else:
            qk_s[0:SUBLANES, :] = tail0_ref[...]
            cn_ref[...] = cn0_ref[...]
            m_ref[...] = m0_ref[...]
            s_ref[...] = s0_ref[...]
        for h in range(HEADS):
            vo_s[:, h * CN + DV:(h + 1) * CN] = jnp.ones((tm, LANES), BF16)
        if not meta:
            srcs = (wbm_ref, wbg_ref, wout_ref)
            n_blk = D_MODEL // MERGE_ROWS

            def block_copy(j, slot):
                return pltpu.make_async_copy(srcs[j // n_blk].at[pl.ds((j % n_blk) * MERGE_ROWS, MERGE_ROWS), :],
                                             wstage.at[slot], wsem.at[slot])

            block_copy(0, 0).start()
            for j in range(len(srcs) * n_blk):
                slot = j % 2
                if j + 1 < len(srcs) * n_blk:
                    block_copy(j + 1, 1 - slot).start()
                block_copy(j, slot).wait()
                r0 = (j % n_blk) * MERGE_ROWS
                merge_w[j // n_blk][r0:r0 + MERGE_ROWS, :] = wstage[slot].astype(BF16)
        row = lax.broadcasted_iota(jnp.int32, (tm, tm), 0)
        col = lax.broadcasted_iota(jnp.int32, (tm, tm), 1)
        tric_s[...] = jnp.where((col <= row) & (col >= (row & -CHUNK)), 1.0, 0.0)[:RCHUNK, :RCHUNK].astype(BF16)
        triu_s[...] = jnp.where((row <= col) & (row >= (col & -RCHUNK)), 1.0, 0.0).astype(BF16)
        sr = lax.broadcasted_iota(jnp.int32, (LANES, 2 * HEADS * LANES), 0)
        sc = lax.broadcasted_iota(jnp.int32, (LANES, 2 * HEADS * LANES), 1)
        blk = (sc >> 9) * (3 * SUBLANES)
        spread_s[...] = jnp.where(
            (sr >= blk) & (sr < blk + 3 * SUBLANES) & ((sr & (SUBLANES - 1)) == ((sc >> 7) & (HEADS - 1))),
            1.0, 0.0).astype(BF16)

    if meta:
        x = jnp.concatenate([jnp.zeros((tm - N_META, D_MODEL), F32), x_ref[...]], axis=0)
    else:
        x = x_ref[...]
    xn = x * lax.rsqrt(jnp.mean(x * x, axis=-1, keepdims=True) + EPS) * g1_ref[...]
    if meta:
        valid = lax.broadcasted_iota(jnp.int32, (tm, 1), 0) >= (tm - N_META)
        xn = jnp.where(valid, xn, 0.0)
    xn_s[...] = xn.astype(BF16)

    def proj(off, width):
        return _dot(xn_s[...], w_ref[:, off:off + width])

    def rows(c):
        return slice(c * RCHUNK, (c + 1) * RCHUNK)

    def half(c, second):
        lo = c * RCHUNK + (CHUNK if second else 0)
        return slice(lo, lo + CHUNK)

    crow = lax.broadcasted_iota(jnp.int32, (RCHUNK, RCHUNK), 0)
    ccol = lax.broadcasted_iota(jnp.int32, (RCHUNK, RCHUNK), 1)
    causal = ccol <= crow
    tri_c = tric_s[...]
    tri_up = triu_s[...]
    spread = spread_s[...]

    def st_qk(i, v):
        qk_s[SUBLANES:SUBLANES + tm, i * DV:(i + 1) * DV] = v

    def st_mv(i, v):
        vo_s[:, i * CN:i * CN + DV] = v.astype(BF16)

    def st_gqk(i, v):
        gqk_s[:, i * DV:(i + 1) * DV] = v

    def st_gv(i, v):
        gv_s[:, i * DV:(i + 1) * DV] = v.astype(BF16)

    def st_mo(i, v):
        ga_s[:, i * DV:(i + 1) * DV] = _sigmoid(v)

    def st_gm(i, v):
        gb_s[:, i * DV:(i + 1) * DV] = _sigmoid(v)

    def st_gr(i, v):
        gc_s[:, i * DV:(i + 1) * DV] = v * _sigmoid(v)

    def st_gg(i, v):
        gd_s[:, i * DV:(i + 1) * DV] = _sigmoid(v)

    groups = [(O_MQK, st_qk), (O_MV, st_mv), (O_GQ, st_gqk), (O_GV, st_gv)]
    if not meta:
        groups += [(O_MO, st_mo), (O_GATE_M, st_gm), (O_GR, st_gr), (O_GATE_G, st_gg)]
    queue = [(seg, i, store) for seg, store in groups for i in range(D_MODEL // DV)]
    emitted = [0]

    def fill(n):
        for _ in range(n):
            if emitted[0] < len(queue):
                seg, i, store = queue[emitted[0]]
                store(i, proj(seg + i * DV, DV))
                emitted[0] += 1

    def finish_groups(n):
        fill(n * (D_MODEL // DV) - emitted[0])

    narrow = _dot_nt(wst_ref[...], xn_s[...])
    gates = narrow[:2 * SUBLANES, :] + jnp.concatenate([sb_ref[...]] * (tm // LANES), axis=1)
    fill(2)
    s1 = jnp.concatenate([jnp.zeros((GA_LANE, tm), F32), narrow[2 * SUBLANES:, :],
                          jnp.zeros((LANES - GA_LANE - G_RANK, tm), F32)], axis=0).T
    logi = gates[:SUBLANES, :]
    logf = _log_sigmoid(gates[SUBLANES:, :])
    if meta:
        valid_t = lax.broadcasted_iota(jnp.int32, (1, tm), 1) >= (tm - N_META)
        logi = jnp.where(valid_t, logi, NEG_BIG)
        logf = jnp.where(valid_t, logf, 0.0)
    zero8 = jnp.zeros((SUBLANES, tm), F32)
    b4 = _dot(jnp.concatenate(_pieces(logf) + [zero8], axis=0).astype(BF16), tri_up)
    fill(2)
    b_all = b4[:SUBLANES] + b4[SUBLANES:2 * SUBLANES] + b4[2 * SUBLANES:3 * SUBLANES]
    c_all = logi - b_all
    m_run = m_ref[...]
    m_in, w_parts, a_chunk = [], [], []
    for c in range(nc):
        b_c = b_all[:, rows(c)]
        g_c = jnp.broadcast_to(b_c[:, RCHUNK - 1:RCHUNK], (SUBLANES, LANES))
        wlog = g_c + c_all[:, rows(c)]
        m_next = jnp.maximum(g_c + m_run, jnp.max(wlog, axis=1, keepdims=True))
        m_in.append(m_run)
        w_parts.append(jnp.exp(wlog - m_next))
        a_chunk.append(jnp.exp(g_c + m_run - m_next))
        m_run = m_next
    m_ref[...] = m_run
    stacked = jnp.concatenate(
        _pieces(b_all) + _pieces(jnp.concatenate(w_parts, axis=1))
        + [jnp.zeros((LANES - 6 * SUBLANES, tm), F32)], axis=0)
    stacked_t = stacked.T.astype(BF16)
    fill(2)
    tiles = _dot(stacked_t, spread)
    fill(2)
    b_t, w_t = tiles[:, :QK], tiles[:, QK:]

    finish_groups(1)
    for blk in range(2 * QK // DV):
        csl = slice(blk * DV, (blk + 1) * DV)
        conv = cb_ref[:, csl]
        for j in range(CONV_W):
            conv = conv + cw_ref[j:j + 1, csl] * qk_s[pl.ds(SUBLANES - (CONV_W - 1) + j, tm), csl]
        conv = conv * _sigmoid(conv)
        if blk < QK // DV:
            q_s[:, csl] = conv * (DK ** -0.5)
        else:
            ksl = slice(blk * DV - QK, (blk + 1) * DV - QK)
            k_s[:, ksl] = conv.astype(BF16)
            wk_s[:, ksl] = (w_t[:, ksl] * conv).astype(BF16)
        fill(1)
    tail_new = qk_s[tm:tm + SUBLANES, :]
    qk_s[0:SUBLANES, :] = tail_new
    tail_ref[...] = tail_new

    za = _dot(s1.astype(BF16), a2_ref[...]) + a2b_ref[...]
    fill(1)
    loga = _log_sigmoid(za) / G_TAU
    if meta:
        loga = jnp.where(valid, loga, 0.0)
    hi = loga.astype(BF16)
    r1 = loga - hi.astype(F32)
    mid = r1.astype(BF16)
    bc = jnp.concatenate([_dot(tri_c, hi[rows(c), :]) + _dot(tri_c, mid[rows(c), :])
                          for c in range(nc)], axis=0)
    fill(2)
    e_blk = jnp.exp(bc)
    tot_a = [bc[c * RCHUNK + CHUNK - 1:c * RCHUNK + CHUNK, :] for c in range(nc)]
    tot_b = [bc[(c + 1) * RCHUNK - 1:(c + 1) * RCHUNK, :] for c in range(nc)]
    e_a = [jnp.exp(t) for t in tot_a]
    e_b = [jnp.exp(t) for t in tot_b]
    rest = jnp.concatenate([t - bc[half(c, s), :] for c in range(nc) for s, t in ((0, tot_a[c]), (1, tot_b[c]))],
                           axis=0)
    e_rest = jnp.exp(rest)
    ones_row = jnp.ones((1, QK), F32)
    to_chunk = jnp.concatenate([jnp.broadcast_to(f, (CHUNK, QK)) for c in range(nc) for f in (ones_row, e_a[c])],
                               axis=0)
    from_blk = jnp.concatenate([jnp.broadcast_to(f, (CHUNK, QK)) for c in range(nc) for f in (e_b[c], ones_row)],
                               axis=0)
    finish_groups(3)
    gq = gqk_s[:, :QK] * (DK ** -0.5)
    gk = gqk_s[:, QK:]
    gq64_s[...] = (gq * e_blk).astype(BF16)
    gq128_s[...] = (gq * e_blk * to_chunk).astype(BF16)
    gki_s[...] = (gk * jnp.exp(-bc)).astype(BF16)
    gke64_s[...] = (gk * e_rest).astype(BF16)
    gke128_s[...] = (gk * e_rest * from_blk).astype(BF16)
    fill(2)
    e_rows = [(e_a[c] * e_b[c])[:, h * DK:(h + 1) * DK] for (c, h) in pairs]
    e_cols = jnp.concatenate(e_rows + [jnp.zeros((LANES - len(pairs), DK), F32)], axis=0).T

    finish_groups(2)
    dmat, rmax, sim = {}, {}, {}
    for (c, h) in pairs:
        d = b_t[rows(c), h * LANES:(h + 1) * LANES] + c_all[h:h + 1, rows(c)]
        d = jnp.where(causal, d, -jnp.inf)
        dmat[c, h] = d
        rmax[c, h] = jnp.max(d, axis=-1, keepdims=True)
    fill(1)
    for (c, h) in pairs:
        sim[c, h] = _dot_nt(q_s[rows(c), h * DK:(h + 1) * DK].astype(BF16), k_s[rows(c), h * DK:(h + 1) * DK])
    fill(1)
    lhs, emr, upd = {}, {}, {}
    for (c, h) in pairs:
        il = b_t[rows(c), h * LANES:(h + 1) * LANES] + m_in[c][h:h + 1, :]
        m_row = jnp.maximum(il, rmax[c, h])
        wts = jnp.exp(dmat[c, h] - m_row) * sim[c, h]
        aq = jnp.exp(il - m_row) * q_s[rows(c), h * DK:(h + 1) * DK]
        lhs[c, h] = jnp.concatenate([aq.astype(BF16), wts.astype(BF16)], axis=1)
        emr[c, h] = jnp.exp(-m_row)
    fill(1)
    for (c, h) in pairs:
        upd[c, h] = _dot_tn(wk_s[rows(c), h * DK:(h + 1) * DK], vo_s[rows(c), h * CN:(h + 1) * CN])
    fill(1)
    res = {}
    for h in range(HEADS):
        state = cn_ref[h]
        for c in range(nc):
            rhs = jnp.concatenate([state.astype(BF16), vo_s[rows(c), h * CN:(h + 1) * CN]], axis=0)
            res[c, h] = _dot(lhs[c, h], rhs)
            a = a_chunk[c][h:h + 1, :]
            state = jnp.concatenate([a] * (CN // LANES), axis=1) * state + upd[c, h]
        cn_ref[h] = state
        fill(1)
    for (c, h) in pairs:
        inv = 1.0 / jnp.maximum(jnp.abs(res[c, h][:, DV:]), emr[c, h])
        hm_s[rows(c), h * DV:h * DV + LANES] = res[c, h][:, :LANES] * inv
        hm_s[rows(c), h * DV + LANES:(h + 1) * DV] = res[c, h][:, LANES:DV] * inv
    fill(1)

    finish_groups(4)
    att, gupd = {}, {}
    zblk = jnp.zeros((CHUNK, DK), BF16)
    for (c, h) in pairs:
        dsl = slice(h * DK, (h + 1) * DK)
        ra, rb = half(c, 0), half(c, 1)
        qp = jnp.concatenate([jnp.concatenate([gq64_s[ra, dsl], zblk], axis=1),
                              jnp.concatenate([zblk, gq64_s[rb, dsl]], axis=1)], axis=0)
        kp = jnp.concatenate([jnp.concatenate([gki_s[ra, dsl], gke64_s[ra, dsl]], axis=1),
                              jnp.concatenate([zblk, gki_s[rb, dsl]], axis=1)], axis=0)
        att[c, h] = jnp.where(causal, _dot_nt(qp, kp), 0.0).astype(BF16)
    fill(1)
    for (c, h) in pairs:
        gupd[c, h] = _dot_tn(gke128_s[rows(c), h * DK:(h + 1) * DK], gv_s[rows(c), h * DV:(h + 1) * DV])
    fill(1)
    for h in range(HEADS):
        sst = s_ref[h]
        for c in range(nc):
            lhs_g = jnp.concatenate([gq128_s[rows(c), h * DK:(h + 1) * DK], att[c, h]], axis=1)
            rhs_g = jnp.concatenate([sst.astype(BF16), gv_s[rows(c), h * DV:(h + 1) * DV]], axis=0)
            hg_s[rows(c), h * DV:(h + 1) * DV] = _dot(lhs_g, rhs_g)
            i = c * HEADS + h
            sst = e_cols[:, i:i + 1] * sst + gupd[c, h]
        s_ref[h] = sst
        fill(1)

    if meta:
        h1_ref[...] = x
        return

    def head_norm(src, gain_ref):
        outs = []
        for h in range(HEADS):
            t = src[:, h * DV:(h + 1) * DV]
            outs.append(t * lax.rsqrt(jnp.mean(t * t, axis=-1, keepdims=True) + EPS) * gain_ref[h:h + 1, :])
        return jnp.concatenate(outs, axis=-1)

    finish_groups(len(groups))
    y_m = head_norm(hm_s, mhg_ref) * ga_s[...]
    merged = gb_s[...] * _dot(y_m.astype(BF16), merge_w[0][...])
    y_g = head_norm(hg_s, ghg_ref) * gc_s[...]
    merged = merged + gd_s[...] * _dot(y_g.astype(BF16), merge_w[1][...])
    h1_ref[...] = x + _dot(merged.astype(BF16), merge_w[2][...])


def _const_spec(shape):
    nd = len(shape)
    return pl.BlockSpec(shape, lambda i: (0,) * nd, pipeline_mode=pl.Buffered(1))


def _mixer_call(x2, consts, state, *, tm, meta):
    t = tm if meta else x2.shape[0]
    state_shapes = [jax.ShapeDtypeStruct(a.shape, F32) for a in state]
    row_spec = pl.BlockSpec((tm, D_MODEL), lambda i: (i, 0))
    x_spec = pl.BlockSpec(x2.shape, lambda i: (0, 0)) if meta else row_spec
    state_specs = [pl.BlockSpec(a.shape, lambda i, nd=len(a.shape): (0,) * nd) for a in state]
    hbm_operands = (1,) if meta else LEAD_UNUSED
    packed = [jax.ShapeDtypeStruct((D_MODEL, N_PACK), BF16),
              jax.ShapeDtypeStruct((2 * SUBLANES + G_RANK, D_MODEL), BF16)] if meta else []
    return pl.pallas_call(
        functools.partial(_mixer_kernel, tm=tm, meta=meta),
        grid=(t // tm,),
        in_specs=[x_spec] + [pl.BlockSpec(memory_space=pl.ANY) if i in hbm_operands else _const_spec(c.shape)
                             for i, c in enumerate(consts)] + ([] if meta else state_specs),
        out_specs=[row_spec] + state_specs + [pl.BlockSpec(a.shape, lambda i: (0, 0)) for a in packed],
        out_shape=[jax.ShapeDtypeStruct((t, D_MODEL), F32)] + state_shapes + packed,
        scratch_shapes=[
            pltpu.VMEM((tm, D_MODEL), BF16),
            pltpu.VMEM((tm + SUBLANES, 2 * QK), F32),
            pltpu.VMEM((tm, QK), F32),
            pltpu.VMEM((tm, QK), BF16),
            pltpu.VMEM((tm, QK), BF16),
            pltpu.VMEM((tm, HEADS * CN), BF16),
            pltpu.VMEM((tm, QK), BF16),
            pltpu.VMEM((tm, QK), BF16),
            pltpu.VMEM((tm, QK), BF16),
            pltpu.VMEM((tm, QK), BF16),
            pltpu.VMEM((tm, QK), BF16),
            pltpu.VMEM((tm, D_MODEL), BF16),
            pltpu.VMEM((tm, D_MODEL), F32),
            pltpu.VMEM((tm, D_MODEL), F32),
            pltpu.VMEM((tm, D_MODEL), F32),
            pltpu.VMEM((tm, D_MODEL), F32),
            pltpu.VMEM((tm, D_MODEL), F32),
            pltpu.VMEM((tm, D_MODEL), F32),
            pltpu.VMEM((tm, D_MODEL), F32),
            pltpu.VMEM((RCHUNK, RCHUNK), BF16),
            pltpu.VMEM((tm, tm), BF16),
            pltpu.VMEM((LANES, 2 * HEADS * LANES), BF16),
        ] + ([
            pltpu.VMEM((REPACK_SLOTS, PIECE, D_MODEL), F32),
            pltpu.VMEM((SUBLANES + G_RANK, D_MODEL), F32),
            pltpu.SemaphoreType.DMA((REPACK_SLOTS,)),
            pltpu.SemaphoreType.DMA((2,)),
        ] if meta else [
            pltpu.VMEM((D_MODEL, D_MODEL), BF16),
            pltpu.VMEM((D_MODEL, D_MODEL), BF16),
            pltpu.VMEM((D_MODEL, D_MODEL), BF16),
            pltpu.VMEM((2, MERGE_ROWS, D_MODEL), F32),
            pltpu.SemaphoreType.DMA((2,)),
        ]),
        compiler_params=pltpu.CompilerParams(
            dimension_semantics=("arbitrary",), vmem_limit_bytes=VMEM_LIMIT),
        name="mixer_meta" if meta else "mixer",
    )(x2, *consts, *([] if meta else state))


def _ffn_kernel(h_ref, g2_ref, wg_hbm, wu_hbm, wd_hbm, gf_ref, o_ref,
                wg_ref, wu_ref, wd_ref, sg, su, sd, sem, *, fc, parts):
    n_chunks = wg_ref.shape[1] // fc

    def chunk_copies(f, slot):
        cols = pl.ds(f * fc, fc)
        return (pltpu.make_async_copy(wg_hbm.at[:, cols], sg.at[slot], sem.at[slot, 0]),
                pltpu.make_async_copy(wu_hbm.at[:, cols], su.at[slot], sem.at[slot, 1]),
                pltpu.make_async_copy(wd_hbm.at[cols, :], sd.at[slot], sem.at[slot, 2]))

    @pl.when(pl.program_id(0) == 0)
    def _():
        for cp in chunk_copies(0, 0):
            cp.start()
        for f in range(n_chunks):
            slot = f % 2
            if f + 1 < n_chunks:
                for cp in chunk_copies(f + 1, 1 - slot):
                    cp.start()
            for cp in chunk_copies(f, slot):
                cp.wait()
            wg_ref[:, f * fc:(f + 1) * fc] = sg[slot].astype(BF16)
            wu_ref[:, f * fc:(f + 1) * fc] = su[slot].astype(BF16)
            wd_ref[f * fc:(f + 1) * fc, :] = sd[slot].astype(BF16)

    rows = h_ref.shape[0] // parts
    blocks = [slice(p * rows, (p + 1) * rows) for p in range(parts)]
    acc, hn = [], []
    for blk in blocks:
        h = h_ref[blk, :]
        hn.append((h * lax.rsqrt(jnp.mean(h * h, axis=-1, keepdims=True) + EPS) * g2_ref[...]).astype(BF16))
        acc.append(h)
    for f in range(0, wg_ref.shape[1], fc):
        for p in range(parts):
            gate = _dot(hn[p], wg_ref[:, f:f + fc])
            up = _dot(hn[p], wu_ref[:, f:f + fc])
            acc[p] = acc[p] + _dot((gate * _sigmoid(gate) * up).astype(BF16), wd_ref[f:f + fc, :])
    for p, blk in enumerate(blocks):
        a = acc[p]
        o_ref[blk, :] = a * lax.rsqrt(jnp.mean(a * a, axis=-1, keepdims=True) + EPS) * gf_ref[...]


def _ffn_call(h1, g2, wg, wu, wd, gf, *, tm, fc, parts):
    t = h1.shape[0]
    d_ff = wg.shape[1]
    row_spec = pl.BlockSpec((tm, D_MODEL), lambda i: (i, 0))
    hbm_spec = pl.BlockSpec(memory_space=pl.ANY)
    return pl.pallas_call(
        functools.partial(_ffn_kernel, fc=fc, parts=parts),
        grid=(t // tm,),
        in_specs=[row_spec, _const_spec(g2.shape), hbm_spec, hbm_spec, hbm_spec, _const_spec(gf.shape)],
        out_specs=row_spec,
        out_shape=jax.ShapeDtypeStruct((t, D_MODEL), F32),
        scratch_shapes=[
            pltpu.VMEM((D_MODEL, d_ff), BF16),
            pltpu.VMEM((D_MODEL, d_ff), BF16),
            pltpu.VMEM((d_ff, D_MODEL), BF16),
            pltpu.VMEM((2, D_MODEL, fc), F32),
            pltpu.VMEM((2, D_MODEL, fc), F32),
            pltpu.VMEM((2, fc, D_MODEL), F32),
            pltpu.SemaphoreType.DMA((2, 3)),
        ],
        compiler_params=pltpu.CompilerParams(
            dimension_semantics=("arbitrary",), vmem_limit_bytes=VMEM_LIMIT),
        name="ffn",
    )(h1, g2, wg, wu, wd, gf)


def kernel(x, meta_tokens, norm1_g, w_in, conv_w, conv_b, m_gate_b, g_a2, g_a2_b, m_head_g, g_head_g,
           w_branch_m, w_branch_g, w_out, norm2_g, w_ff_gate, w_ff_up, w_ff_down, final_g):
    bsz, seq, d = x.shape
    assert bsz == 1 and d == D_MODEL and norm1_g.shape[0] == 1 and seq % CHUNK == 0
    row = lambda a: a.reshape(1, -1).astype(F32)

    gate_bias = jnp.zeros((2 * SUBLANES, LANES), F32)
    gate_bias = gate_bias.at[:HEADS].set(m_gate_b[0, 0][:, None]).at[SUBLANES:SUBLANES + HEADS].set(m_gate_b[0, 1][:, None])
    a2 = jnp.zeros((LANES, QK), F32).at[GA_LANE:GA_LANE + G_RANK].set(g_a2[0]).astype(BF16)
    conv_w8 = jnp.zeros((SUBLANES, 2 * QK), F32).at[:CONV_W].set(conv_w[0])
    consts = (gate_bias, a2, row(g_a2_b[0]), conv_w8, row(conv_b[0]), m_head_g[0].astype(F32), g_head_g[0].astype(F32),
              w_branch_m[0].astype(F32), w_branch_g[0].astype(F32), w_out[0].astype(F32))

    state_shapes = (jax.ShapeDtypeStruct((SUBLANES, 2 * QK), F32), jax.ShapeDtypeStruct((HEADS, DK, CN), F32),
                    jax.ShapeDtypeStruct((SUBLANES, LANES), F32), jax.ShapeDtypeStruct((HEADS, DK, DV), F32))
    unused = jnp.zeros((SUBLANES, LANES), BF16)
    g1 = row(norm1_g[0])
    lead_consts = (g1, w_in[0].T.astype(F32), unused) + consts[:-len(LEAD_UNUSED)] + (unused,) * len(LEAD_UNUSED)
    _, *state, w_packed, w_narrow = _mixer_call(meta_tokens.astype(F32), lead_consts, state_shapes, tm=RCHUNK,
                                                meta=True)
    h1, *_ = _mixer_call(x[0], (g1, w_packed, w_narrow) + consts, tuple(state), tm=512, meta=False)
    out = _ffn_call(h1, row(norm2_g[0]), w_ff_gate[0].astype(F32), w_ff_up[0].astype(F32),
                    w_ff_down[0].astype(F32), row(final_g), tm=1024, fc=256, parts=2)
    return out[None]
```

```python
import functools

import jax
import jax.numpy as jnp
from jax import lax
from jax.experimental import pallas as pl
from jax.experimental.pallas import tpu as pltpu

F32 = jnp.float32
BF16 = jnp.bfloat16

D_MODEL = 1024
N_META = 16
CHUNK = 64
RCHUNK = 128
EPS = 1e-6
HEADS = 4
DV = D_MODEL // HEADS
DK = DV // 2
QK = HEADS * DK
G_RANK = 16
G_TAU = 16.0
CONV_W = 4
LANES = 128
SUBLANES = 8
NEG_BIG = -1e30
VMEM_LIMIT = 60 * 1024 * 1024
CN = DV + LANES

W_IN_RANGES = ((0, 2048), (2056, 5128), (5144, 8216))
N_PACK = sum(b - a for a, b in W_IN_RANGES)
O_MQK, O_MV, O_MO = 0, 1024, 2048
O_GQ, O_GV = 3072, 4096
O_GR, O_GATE_M, O_GATE_G = 5120, 6144, 7168
MI_COL, GA_COL = 2048, 5128
GA_LANE = 8
PIECE = 512
MERGE_ROWS = 256
REPACK_AHEAD = 3
REPACK_SLOTS = REPACK_AHEAD + 1
LEAD_UNUSED = (10, 11, 12)

NT_DIMS = (((1,), (1,)), ((), ()))
TN_DIMS = (((0,), (0,)), ((), ()))


def _dot(a, b):
    return jnp.dot(a, b, preferred_element_type=F32)


def _dot_nt(a, b):
    return lax.dot_general(a, b, NT_DIMS, preferred_element_type=F32)


def _dot_tn(a, b):
    return lax.dot_general(a, b, TN_DIMS, preferred_element_type=F32)


def _pieces(x):
    hi = x.astype(BF16).astype(F32)
    mid = (x - hi).astype(BF16).astype(F32)
    lo = (x - hi - mid).astype(BF16).astype(F32)
    return [hi, mid, lo]


def _log_sigmoid(x):
    return jnp.minimum(x, 0.0) - jnp.log(1.0 + jnp.exp(-jnp.abs(x)))


def _sigmoid(x):
    return 1.0 / (1.0 + jnp.exp(-x))


def _repack_w_in(wt_hbm, w_ref, wst_ref, stage, narrow, sem, nsem):
    (_, a_end), (b_start, b_end), (c_start, _) = W_IN_RANGES
    first_b, first_c = a_end // PIECE, (a_end + b_end - b_start) // PIECE
    n_pieces = N_PACK // PIECE

    def piece_copy(q, slot):
        row0 = q * PIECE + (b_start - a_end if q >= first_b else 0) + (c_start - b_end if q >= first_c else 0)
        return pltpu.make_async_copy(wt_hbm.at[pl.ds(row0, PIECE), :], stage.at[slot], sem.at[slot])

    narrow_copies = (
        pltpu.make_async_copy(wt_hbm.at[pl.ds(MI_COL, SUBLANES), :], narrow.at[pl.ds(0, SUBLANES), :], nsem.at[0]),
        pltpu.make_async_copy(wt_hbm.at[pl.ds(GA_COL, G_RANK), :], narrow.at[pl.ds(SUBLANES, G_RANK), :], nsem.at[1]))

    for q in range(REPACK_AHEAD):
        piece_copy(q, q).start()
    for cp in narrow_copies:
        cp.start()
    for q in range(n_pieces):
        if q + REPACK_AHEAD < n_pieces:
            piece_copy(q + REPACK_AHEAD, (q + REPACK_AHEAD) % REPACK_SLOTS).start()
        piece_copy(q, q % REPACK_SLOTS).wait()
        w_ref[:, q * PIECE:(q + 1) * PIECE] = stage[q % REPACK_SLOTS].T.astype(BF16)
    for cp in narrow_copies:
        cp.wait()
    gates = narrow[0:SUBLANES, :]
    zeros = jnp.zeros((SUBLANES - HEADS, D_MODEL), F32)
    wst_ref[...] = jnp.concatenate([gates[:HEADS], zeros, gates[HEADS:], zeros, narrow[SUBLANES:, :]],
                                   axis=0).astype(BF16)


def _mixer_kernel(x_ref, g1_ref, w_ref, wst_ref, sb_ref, a2_ref, a2b_ref, cw_ref, cb_ref,
                  mhg_ref, ghg_ref, wbm_ref, wbg_ref, wout_ref, *rest, tm, meta):
    nc = tm // RCHUNK
    pairs = [(c, h) for c in range(nc) for h in range(HEADS)]
    if not meta:
        tail0_ref, cn0_ref, m0_ref, s0_ref, *rest = rest
    h1_ref, tail_ref, cn_ref, m_ref, s_ref, *rest = rest
    if meta:
        wt_hbm = w_ref
        w_ref, wst_ref, *rest = rest
        *rest, stage, narrow, sem, nsem = rest
        _repack_w_in(wt_hbm, w_ref, wst_ref, stage, narrow, sem, nsem)
    else:
        *rest, wbm_s, wbg_s, wout_s, wstage, wsem = rest
        merge_w = (wbm_s, wbg_s, wout_s)
    (xn_s, qk_s, q_s, k_s, wk_s, vo_s, gq64_s, gq128_s, gki_s, gke64_s, gke128_s, gv_s, hm_s, hg_s,
     ga_s, gb_s, gc_s, gd_s, gqk_s, tric_s, triu_s, spread_s) = rest

    @pl.when(pl.program_id(0) == 0)
    def _():
        if meta:
            qk_s[0:SUBLANES, :] = jnp.zeros((SUBLANES, 2 * QK), F32)
            cn_ref[...] = jnp.zeros(cn_ref.shape, F32)
            m_ref[...] = jnp.zeros(m_ref.shape, F32)
            s_ref[...] = jnp.zeros(s_ref.shape, F32)
        else:
            qk_s[0:SUBLANES, :] = tail0_ref[...]
            cn_ref[...] = cn0_ref[...]
            m_ref[...] = m0_ref[...]
            s_ref[...] = s0_ref[...]
        for h in range(HEADS):
            vo_s[:, h * CN + DV:(h + 1) * CN] = jnp.ones((tm, LANES), BF16)
        if not meta:
            srcs = (wbm_ref, wbg_ref, wout_ref)
            n_blk = D_MODEL // MERGE_ROWS

            def block_copy(j, slot):
                return pltpu.make_async_copy(srcs[j // n_blk].at[pl.ds((j % n_blk) * MERGE_ROWS, MERGE_ROWS), :],
                                             wstage.at[slot], wsem.at[slot])

            for j in range(REPACK_AHEAD):
                block_copy(j, j).start()
            for j in range(len(srcs) * n_blk):
                slot = j % REPACK_SLOTS
                if j + REPACK_AHEAD < len(srcs) * n_blk:
                    block_copy(j + REPACK_AHEAD, (j + REPACK_AHEAD) % REPACK_SLOTS).start()
                block_copy(j, slot).wait()
                r0 = (j % n_blk) * MERGE_ROWS
                merge_w[j // n_blk][r0:r0 + MERGE_ROWS, :] = wstage[slot].astype(BF16)
        row = lax.broadcasted_iota(jnp.int32, (tm, tm), 0)
        col = lax.broadcasted_iota(jnp.int32, (tm, tm), 1)
        tric_s[...] = jnp.where((col <= row) & (col >= (row & -CHUNK)), 1.0, 0.0)[:RCHUNK, :RCHUNK].astype(BF16)
        triu_s[...] = jnp.where((row <= col) & (row >= (col & -RCHUNK)), 1.0, 0.0).astype(BF16)
        sr = lax.broadcasted_iota(jnp.int32, (LANES, 2 * HEADS * LANES), 0)
        sc = lax.broadcasted_iota(jnp.int32, (LANES, 2 * HEADS * LANES), 1)
        blk = (sc >> 9) * (3 * SUBLANES)
        spread_s[...] = jnp.where(
            (sr >= blk) & (sr < blk + 3 * SUBLANES) & ((sr & (SUBLANES - 1)) == ((sc >> 7) & (HEADS - 1))),
            1.0, 0.0).astype(BF16)

    if meta:
        x = jnp.concatenate([jnp.zeros((tm - N_META, D_MODEL), F32), x_ref[...]], axis=0)
    else:
        x = x_ref[...]
    xn = x * lax.rsqrt(jnp.mean(x * x, axis=-1, keepdims=True) + EPS) * g1_ref[...]
    if meta:
        valid = lax.broadcasted_iota(jnp.int32, (tm, 1), 0) >= (tm - N_META)
        xn = jnp.where(valid, xn, 0.0)
    xn_s[...] = xn.astype(BF16)

    def proj(off, width):
        return _dot(xn_s[...], w_ref[:, off:off + width])

    def rows(c):
        return slice(c * RCHUNK, (c + 1) * RCHUNK)

    def half(c, second):
        lo = c * RCHUNK + (CHUNK if second else 0)
        return slice(lo, lo + CHUNK)

    crow = lax.broadcasted_iota(jnp.int32, (RCHUNK, RCHUNK), 0)
    ccol = lax.broadcasted_iota(jnp.int32, (RCHUNK, RCHUNK), 1)
    causal = ccol <= crow
    tri_c = tric_s[...]
    tri_up = triu_s[...]
    spread = spread_s[...]

    def st_qk(i, v):
        qk_s[SUBLANES:SUBLANES + tm, i * DV:(i + 1) * DV] = v

    def st_mv(i, v):
        vo_s[:, i * CN:i * CN + DV] = v.astype(BF16)

    def st_gqk(i, v):
        gqk_s[:, i * DV:(i + 1) * DV] = v

    def st_gv(i, v):
        gv_s[:, i * DV:(i + 1) * DV] = v.astype(BF16)

    def st_mo(i, v):
        ga_s[:, i * DV:(i + 1) * DV] = _sigmoid(v)

    def st_gm(i, v):
        gb_s[:, i * DV:(i + 1) * DV] = _sigmoid(v)

    def st_gr(i, v):
        gc_s[:, i * DV:(i + 1) * DV] = v * _sigmoid(v)

    def st_gg(i, v):
        gd_s[:, i * DV:(i + 1) * DV] = _sigmoid(v)

    groups = [(O_MQK, st_qk), (O_MV, st_mv), (O_GQ, st_gqk), (O_GV, st_gv)]
    if not meta:
        groups += [(O_MO, st_mo), (O_GATE_M, st_gm), (O_GR, st_gr), (O_GATE_G, st_gg)]
    queue = [(seg, i, store) for seg, store in groups for i in range(D_MODEL // DV)]
    emitted = [0]

    def fill(n):
        for _ in range(n):
            if emitted[0] < len(queue):
                seg, i, store = queue[emitted[0]]
                store(i, proj(seg + i * DV, DV))
                emitted[0] += 1

    def finish_groups(n):
        fill(n * (D_MODEL // DV) - emitted[0])

    narrow = _dot_nt(wst_ref[...], xn_s[...])
    gates = narrow[:2 * SUBLANES, :] + jnp.concatenate([sb_ref[...]] * (tm // LANES), axis=1)
    fill(2)
    s1 = jnp.concatenate([jnp.zeros((GA_LANE, tm), F32), narrow[2 * SUBLANES:, :],
                          jnp.zeros((LANES - GA_LANE - G_RANK, tm), F32)], axis=0).T
    logi = gates[:SUBLANES, :]
    logf = _log_sigmoid(gates[SUBLANES:, :])
    if meta:
        valid_t = lax.broadcasted_iota(jnp.int32, (1, tm), 1) >= (tm - N_META)
        logi = jnp.where(valid_t, logi, NEG_BIG)
        logf = jnp.where(valid_t, logf, 0.0)
    zero8 = jnp.zeros((SUBLANES, tm), F32)
    b4 = _dot(jnp.concatenate(_pieces(logf) + [zero8], axis=0).astype(BF16), tri_up)
    fill(2)
    b_all = b4[:SUBLANES] + b4[SUBLANES:2 * SUBLANES] + b4[2 * SUBLANES:3 * SUBLANES]
    c_all = logi - b_all
    m_run = m_ref[...]
    m_in, w_parts, a_chunk = [], [], []
    for c in range(nc):
        b_c = b_all[:, rows(c)]
        g_c = jnp.broadcast_to(b_c[:, RCHUNK - 1:RCHUNK], (SUBLANES, LANES))
        wlog = g_c + c_all[:, rows(c)]
        m_next = jnp.maximum(g_c + m_run, jnp.max(wlog, axis=1, keepdims=True))
        m_in.append(m_run)
        w_parts.append(jnp.exp(wlog - m_next))
        a_chunk.append(jnp.exp(g_c + m_run - m_next))
        m_run = m_next
    m_ref[...] = m_run
    stacked = jnp.concatenate(
        _pieces(b_all) + _pieces(jnp.concatenate(w_parts, axis=1))
        + [jnp.zeros((LANES - 6 * SUBLANES, tm), F32)], axis=0)
    stacked_t = stacked.T.astype(BF16)
    fill(2)
    tiles = _dot(stacked_t, spread)
    fill(2)
    b_t, w_t = tiles[:, :QK], tiles[:, QK:]

    finish_groups(1)
    for blk in range(2 * QK // DV):
        csl = slice(blk * DV, (blk + 1) * DV)
        conv = cb_ref[:, csl]
        for j in range(CONV_W):
            conv = conv + cw_ref[j:j + 1, csl] * qk_s[pl.ds(SUBLANES - (CONV_W - 1) + j, tm), csl]
        conv = conv * _sigmoid(conv)
        if blk < QK // DV:
            q_s[:, csl] = conv * (DK ** -0.5)
        else:
            ksl = slice(blk * DV - QK, (blk + 1) * DV - QK)
            k_s[:, ksl] = conv.astype(BF16)
            wk_s[:, ksl] = (w_t[:, ksl] * conv).astype(BF16)
        fill(1)
    tail_new = qk_s[tm:tm + SUBLANES, :]
    qk_s[0:SUBLANES, :] = tail_new
    tail_ref[...] = tail_new

    za = _dot(s1.astype(BF16), a2_ref[...]) + a2b_ref[...]
    fill(1)
    loga = _log_sigmoid(za) / G_TAU
    if meta:
        loga = jnp.where(valid, loga, 0.0)
    hi = loga.astype(BF16)
    r1 = loga - hi.astype(F32)
    mid = r1.astype(BF16)
    bc = jnp.concatenate([_dot(tri_c, hi[rows(c), :]) + _dot(tri_c, mid[rows(c), :])
                          for c in range(nc)], axis=0)
    fill(2)
    e_blk = jnp.exp(bc)
    tot_a = [bc[c * RCHUNK + CHUNK - 1:c * RCHUNK + CHUNK, :] for c in range(nc)]
    tot_b = [bc[(c + 1) * RCHUNK - 1:(c + 1) * RCHUNK, :] for c in range(nc)]
    e_a = [jnp.exp(t) for t in tot_a]
    e_b = [jnp.exp(t) for t in tot_b]
    rest = jnp.concatenate([t - bc[half(c, s), :] for c in range(nc) for s, t in ((0, tot_a[c]), (1, tot_b[c]))],
                           axis=0)
    e_rest = jnp.exp(rest)
    ones_row = jnp.ones((1, QK), F32)
    to_chunk = jnp.concatenate([jnp.broadcast_to(f, (CHUNK, QK)) for c in range(nc) for f in (ones_row, e_a[c])],
                               axis=0)
    from_blk = jnp.concatenate([jnp.broadcast_to(f, (CHUNK, QK)) for c in range(nc) for f in (e_b[c], ones_row)],
                               axis=0)
    finish_groups(3)
    gq = gqk_s[:, :QK] * (DK ** -0.5)
    gk = gqk_s[:, QK:]
    gq64_s[...] = (gq * e_blk).astype(BF16)
    gq128_s[...] = (gq * e_blk * to_chunk).astype(BF16)
    gki_s[...] = (gk * jnp.exp(-bc)).astype(BF16)
    gke64_s[...] = (gk * e_rest).astype(BF16)
    gke128_s[...] = (gk * e_rest * from_blk).astype(BF16)
    fill(2)
    e_rows = [(e_a[c] * e_b[c])[:, h * DK:(h + 1) * DK] for (c, h) in pairs]
    e_cols = jnp.concatenate(e_rows + [jnp.zeros((LANES - len(pairs), DK), F32)], axis=0).T

    finish_groups(2)
    dmat, rmax, sim = {}, {}, {}
    for (c, h) in pairs:
        d = b_t[rows(c), h * LANES:(h + 1) * LANES] + c_all[h:h + 1, rows(c)]
        d = jnp.where(causal, d, -jnp.inf)
        dmat[c, h] = d
        rmax[c, h] = jnp.max(d, axis=-1, keepdims=True)
    fill(1)
    for (c, h) in pairs:
        sim[c, h] = _dot_nt(q_s[rows(c), h * DK:(h + 1) * DK].astype(BF16), k_s[rows(c), h * DK:(h + 1) * DK])
    fill(1)
    lhs, emr, upd = {}, {}, {}
    for (c, h) in pairs:
        il = b_t[rows(c), h * LANES:(h + 1) * LANES] + m_in[c][h:h + 1, :]
        m_row = jnp.maximum(il, rmax[c, h])
        wts = jnp.exp(dmat[c, h] - m_row) * sim[c, h]
        aq = jnp.exp(il - m_row) * q_s[rows(c), h * DK:(h + 1) * DK]
        lhs[c, h] = jnp.concatenate([aq.astype(BF16), wts.astype(BF16)], axis=1)
        emr[c, h] = jnp.exp(-m_row)
    fill(1)
    for (c, h) in pairs:
        upd[c, h] = _dot_tn(wk_s[rows(c), h * DK:(h + 1) * DK], vo_s[rows(c), h * CN:(h + 1) * CN])
    fill(1)
    res = {}
    for h in range(HEADS):
        state = cn_ref[h]
        for c in range(nc):
            rhs = jnp.concatenate([state.astype(BF16), vo_s[rows(c), h * CN:(h + 1) * CN]], axis=0)
            res[c, h] = _dot(lhs[c, h], rhs)
            a = a_chunk[c][h:h + 1, :]
            state = jnp.concatenate([a] * (CN // LANES), axis=1) * state + upd[c, h]
        cn_ref[h] = state
        fill(1)
    for (c, h) in pairs:
        inv = 1.0 / jnp.maximum(jnp.abs(res[c, h][:, DV:]), emr[c, h])
        hm_s[rows(c), h * DV:h * DV + LANES] = res[c, h][:, :LANES] * inv
        hm_s[rows(c), h * DV + LANES:(h + 1) * DV] = res[c, h][:, LANES:DV] * inv
    fill(1)

    finish_groups(4)
    att, gupd = {}, {}
    zblk = jnp.zeros((CHUNK, DK), BF16)
    for (c, h) in pairs:
        dsl = slice(h * DK, (h + 1) * DK)
        ra, rb = half(c, 0), half(c, 1)
        qp = jnp.concatenate([jnp.concatenate([gq64_s[ra, dsl], zblk], axis=1),
                              jnp.concatenate([zblk, gq64_s[rb, dsl]], axis=1)], axis=0)
        kp = jnp.concatenate([jnp.concatenate([gki_s[ra, dsl], gke64_s[ra, dsl]], axis=1),
                              jnp.concatenate([zblk, gki_s[rb, dsl]], axis=1)], axis=0)
        att[c, h] = jnp.where(causal, _dot_nt(qp, kp), 0.0).astype(BF16)
    fill(1)
    for (c, h) in pairs:
        gupd[c, h] = _dot_tn(gke128_s[rows(c), h * DK:(h + 1) * DK], gv_s[rows(c), h * DV:(h + 1) * DV])
    fill(1)
    for h in range(HEADS):
        sst = s_ref[h]
        for c in range(nc):
            lhs_g = jnp.concatenate([gq128_s[rows(c), h * DK:(h + 1) * DK], att[c, h]], axis=1)
            rhs_g = jnp.concatenate([sst.astype(BF16), gv_s[rows(c), h * DV:(h + 1) * DV]], axis=0)
            hg_s[rows(c), h * DV:(h + 1) * DV] = _dot(lhs_g, rhs_g)
            i = c * HEADS + h
            sst = e_cols[:, i:i + 1] * sst + gupd[c, h]
        s_ref[h] = sst
        fill(1)

    if meta:
        h1_ref[...] = x
        return

    def head_norm(src, gain_ref):
        outs = []
        for h in range(HEADS):
            t = src[:, h * DV:(h + 1) * DV]
            outs.append(t * lax.rsqrt(jnp.mean(t * t, axis=-1, keepdims=True) + EPS) * gain_ref[h:h + 1, :])
        return jnp.concatenate(outs, axis=-1)

    finish_groups(len(groups))
    y_m = head_norm(hm_s, mhg_ref) * ga_s[...]
    merged = gb_s[...] * _dot(y_m.astype(BF16), merge_w[0][...])
    y_g = head_norm(hg_s, ghg_ref) * gc_s[...]
    merged = merged + gd_s[...] * _dot(y_g.astype(BF16), merge_w[1][...])
    h1_ref[...] = x + _dot(merged.astype(BF16), merge_w[2][...])


def _const_spec(shape):
    nd = len(shape)
    return pl.BlockSpec(shape, lambda i: (0,) * nd, pipeline_mode=pl.Buffered(1))


def _mixer_call(x2, consts, state, *, tm, meta):
    t = tm if meta else x2.shape[0]
    state_shapes = [jax.ShapeDtypeStruct(a.shape, F32) for a in state]
    row_spec = pl.BlockSpec((tm, D_MODEL), lambda i: (i, 0))
    x_spec = pl.BlockSpec(x2.shape, lambda i: (0, 0)) if meta else row_spec
    state_specs = [pl.BlockSpec(a.shape, lambda i, nd=len(a.shape): (0,) * nd) for a in state]
    hbm_operands = (1,) if meta else LEAD_UNUSED
    packed = [jax.ShapeDtypeStruct((D_MODEL, N_PACK), BF16),
              jax.ShapeDtypeStruct((2 * SUBLANES + G_RANK, D_MODEL), BF16)] if meta else []
    return pl.pallas_call(
        functools.partial(_mixer_kernel, tm=tm, meta=meta),
        grid=(t // tm,),
        in_specs=[x_spec] + [pl.BlockSpec(memory_space=pl.ANY) if i in hbm_operands else _const_spec(c.shape)
                             for i, c in enumerate(consts)] + ([] if meta else state_specs),
        out_specs=[row_spec] + state_specs + [pl.BlockSpec(a.shape, lambda i: (0, 0)) for a in packed],
        out_shape=[jax.ShapeDtypeStruct((t, D_MODEL), F32)] + state_shapes + packed,
        scratch_shapes=[
            pltpu.VMEM((tm, D_MODEL), BF16),
            pltpu.VMEM((tm + SUBLANES, 2 * QK), F32),
            pltpu.VMEM((tm, QK), F32),
            pltpu.VMEM((tm, QK), BF16),
            pltpu.VMEM((tm, QK), BF16),
            pltpu.VMEM((tm, HEADS * CN), BF16),
            pltpu.VMEM((tm, QK), BF16),
            pltpu.VMEM((tm, QK), BF16),
            pltpu.VMEM((tm, QK), BF16),
            pltpu.VMEM((tm, QK), BF16),
            pltpu.VMEM((tm, QK), BF16),
            pltpu.VMEM((tm, D_MODEL), BF16),
            pltpu.VMEM((tm, D_MODEL), F32),
            pltpu.VMEM((tm, D_MODEL), F32),
            pltpu.VMEM((tm, D_MODEL), F32),
            pltpu.VMEM((tm, D_MODEL), F32),
            pltpu.VMEM((tm, D_MODEL), F32),
            pltpu.VMEM((tm, D_MODEL), F32),
            pltpu.VMEM((tm, D_MODEL), F32),
            pltpu.VMEM((RCHUNK, RCHUNK), BF16),
            pltpu.VMEM((tm, tm), BF16),
            pltpu.VMEM((LANES, 2 * HEADS * LANES), BF16),
        ] + ([
            pltpu.VMEM((REPACK_SLOTS, PIECE, D_MODEL), F32),
            pltpu.VMEM((SUBLANES + G_RANK, D_MODEL), F32),
            pltpu.SemaphoreType.DMA((REPACK_SLOTS,)),
            pltpu.SemaphoreType.DMA((2,)),
        ] if meta else [
            pltpu.VMEM((D_MODEL, D_MODEL), BF16),
            pltpu.VMEM((D_MODEL, D_MODEL), BF16),
            pltpu.VMEM((D_MODEL, D_MODEL), BF16),
            pltpu.VMEM((REPACK_SLOTS, MERGE_ROWS, D_MODEL), F32),
            pltpu.SemaphoreType.DMA((REPACK_SLOTS,)),
        ]),
        compiler_params=pltpu.CompilerParams(
            dimension_semantics=("arbitrary",), vmem_limit_bytes=VMEM_LIMIT),
        name="mixer_meta" if meta else "mixer",
    )(x2, *consts, *([] if meta else state))


def _ffn_kernel(h_ref, g2_ref, wg_hbm, wu_hbm, wd_hbm, gf_ref, o_ref,
                wg_ref, wu_ref, wd_ref, sg, su, sd, sem, *, fc, parts):
    n_chunks = wg_ref.shape[1] // fc

    def chunk_copies(f, slot):
        cols = pl.ds(f * fc, fc)
        return (pltpu.make_async_copy(wg_hbm.at[:, cols], sg.at[slot], sem.at[slot, 0]),
                pltpu.make_async_copy(wu_hbm.at[:, cols], su.at[slot], sem.at[slot, 1]),
                pltpu.make_async_copy(wd_hbm.at[cols, :], sd.at[slot], sem.at[slot, 2]))

    @pl.when(pl.program_id(0) == 0)
    def _():
        for cp in chunk_copies(0, 0):
            cp.start()
        for f in range(n_chunks):
            slot = f % 2
            if f + 1 < n_chunks:
                for cp in chunk_copies(f + 1, 1 - slot):
                    cp.start()
            for cp in chunk_copies(f, slot):
                cp.wait()
            wg_ref[:, f * fc:(f + 1) * fc] = sg[slot].astype(BF16)
            wu_ref[:, f * fc:(f + 1) * fc] = su[slot].astype(BF16)
            wd_ref[f * fc:(f + 1) * fc, :] = sd[slot].astype(BF16)

    rows = h_ref.shape[0] // parts
    blocks = [slice(p * rows, (p + 1) * rows) for p in range(parts)]
    acc, hn = [], []
    for blk in blocks:
        h = h_ref[blk, :]
        hn.append((h * lax.rsqrt(jnp.mean(h * h, axis=-1, keepdims=True) + EPS) * g2_ref[...]).astype(BF16))
        acc.append(h)
    for f in range(0, wg_ref.shape[1], fc):
        for p in range(parts):
            gate = _dot(hn[p], wg_ref[:, f:f + fc])
            up = _dot(hn[p], wu_ref[:, f:f + fc])
            acc[p] = acc[p] + _dot((gate * _sigmoid(gate) * up).astype(BF16), wd_ref[f:f + fc, :])
    for p, blk in enumerate(blocks):
        a = acc[p]
        o_ref[blk, :] = a * lax.rsqrt(jnp.mean(a * a, axis=-1, keepdims=True) + EPS) * gf_ref[...]


def _ffn_call(h1, g2, wg, wu, wd, gf, *, tm, fc, parts):
    t = h1.shape[0]
    d_ff = wg.shape[1]
    row_spec = pl.BlockSpec((tm, D_MODEL), lambda i: (i, 0))
    hbm_spec = pl.BlockSpec(memory_space=pl.ANY)
    return pl.pallas_call(
        functools.partial(_ffn_kernel, fc=fc, parts=parts),
        grid=(t // tm,),
        in_specs=[row_spec, _const_spec(g2.shape), hbm_spec, hbm_spec, hbm_spec, _const_spec(gf.shape)],
        out_specs=row_spec,
        out_shape=jax.ShapeDtypeStruct((t, D_MODEL), F32),
        scratch_shapes=[
            pltpu.VMEM((D_MODEL, d_ff), BF16),
            pltpu.VMEM((D_MODEL, d_ff), BF16),
            pltpu.VMEM((d_ff, D_MODEL), BF16),
            pltpu.VMEM((2, D_MODEL, fc), F32),
            pltpu.VMEM((2, D_MODEL, fc), F32),
            pltpu.VMEM((2, fc, D_MODEL), F32),
            pltpu.SemaphoreType.DMA((2, 3)),
        ],
        compiler_params=pltpu.CompilerParams(
            dimension_semantics=("arbitrary",), vmem_limit_bytes=VMEM_LIMIT),
        name="ffn",
    )(h1, g2, wg, wu, wd, gf)


def kernel(x, meta_tokens, norm1_g, w_in, conv_w, conv_b, m_gate_b, g_a2, g_a2_b, m_head_g, g_head_g,
           w_branch_m, w_branch_g, w_out, norm2_g, w_ff_gate, w_ff_up, w_ff_down, final_g):
    bsz, seq, d = x.shape
    assert bsz == 1 and d == D_MODEL and norm1_g.shape[0] == 1 and seq % CHUNK == 0
    row = lambda a: a.reshape(1, -1).astype(F32)

    gate_bias = jnp.zeros((2 * SUBLANES, LANES), F32)
    gate_bias = gate_bias.at[:HEADS].set(m_gate_b[0, 0][:, None]).at[SUBLANES:SUBLANES + HEADS].set(m_gate_b[0, 1][:, None])
    a2 = jnp.zeros((LANES, QK), F32).at[GA_LANE:GA_LANE + G_RANK].set(g_a2[0]).astype(BF16)
    conv_w8 = jnp.zeros((SUBLANES, 2 * QK), F32).at[:CONV_W].set(conv_w[0])
    consts = (gate_bias, a2, row(g_a2_b[0]), conv_w8, row(conv_b[0]), m_head_g[0].astype(F32), g_head_g[0].astype(F32),
              w_branch_m[0].astype(F32), w_branch_g[0].astype(F32), w_out[0].astype(F32))

    state_shapes = (jax.ShapeDtypeStruct((SUBLANES, 2 * QK), F32), jax.ShapeDtypeStruct((HEADS, DK, CN), F32),
                    jax.ShapeDtypeStruct((SUBLANES, LANES), F32), jax.ShapeDtypeStruct((HEADS, DK, DV), F32))
    unused = jnp.zeros((SUBLANES, LANES), BF16)
    g1 = row(norm1_g[0])
    lead_consts = (g1, w_in[0].T.astype(F32), unused) + consts[:-len(LEAD_UNUSED)] + (unused,) * len(LEAD_UNUSED)
    _, *state, w_packed, w_narrow = _mixer_call(meta_tokens.astype(F32), lead_consts, state_shapes, tm=RCHUNK,
                                                meta=True)
    h1, *_ = _mixer_call(x[0], (g1, w_packed, w_narrow) + consts, tuple(state), tm=512, meta=False)
    out = _ffn_call(h1, row(norm2_g[0]), w_ff_gate[0].astype(F32), w_ff_up[0].astype(F32),
                    w_ff_down[0].astype(F32), row(final_g), tm=1024, fc=256, parts=2)
    return out[None]
```

```python
import functools

import jax
import jax.numpy as jnp
from jax import lax
from jax.experimental import pallas as pl
from jax.experimental.pallas import tpu as pltpu

F32 = jnp.float32
BF16 = jnp.bfloat16

D_MODEL = 1024
N_META = 16
CHUNK = 64
RCHUNK = 128
EPS = 1e-6
HEADS = 4
DV = D_MODEL // HEADS
DK = DV // 2
QK = HEADS * DK
G_RANK = 16
G_TAU = 16.0
CONV_W = 4
LANES = 128
SUBLANES = 8
NEG_BIG = -1e30
VMEM_LIMIT = 60 * 1024 * 1024
CN = DV + LANES

W_IN_RANGES = ((0, 2048), (2056, 5128), (5144, 8216))
N_PACK = sum(b - a for a, b in W_IN_RANGES)
O_MQK, O_MV, O_MO = 0, 1024, 2048
O_GQ, O_GV = 3072, 4096
O_GR, O_GATE_M, O_GATE_G = 5120, 6144, 7168
MI_COL, GA_COL = 2048, 5128
GA_LANE = 8
PIECE = 512
MERGE_ROWS = 256
REPACK_AHEAD = 3
REPACK_SLOTS = REPACK_AHEAD + 1
FFN_SLOTS = 3
LEAD_UNUSED = (10, 11, 12)

NT_DIMS = (((1,), (1,)), ((), ()))
TN_DIMS = (((0,), (0,)), ((), ()))


def _dot(a, b):
    return jnp.dot(a, b, preferred_element_type=F32)


def _dot_nt(a, b):
    return lax.dot_general(a, b, NT_DIMS, preferred_element_type=F32)


def _dot_tn(a, b):
    return lax.dot_general(a, b, TN_DIMS, preferred_element_type=F32)


def _pieces(x):
    hi = x.astype(BF16).astype(F32)
    mid = (x - hi).astype(BF16).astype(F32)
    lo = (x - hi - mid).astype(BF16).astype(F32)
    return [hi, mid, lo]


def _log_sigmoid(x):
    return jnp.minimum(x, 0.0) - jnp.log(1.0 + jnp.exp(-jnp.abs(x)))


def _sigmoid(x):
    return 1.0 / (1.0 + jnp.exp(-x))


def _repack_w_in(wt_hbm, w_ref, wst_ref, stage, narrow, sem, nsem):
    (_, a_end), (b_start, b_end), (c_start, _) = W_IN_RANGES
    first_b, first_c = a_end // PIECE, (a_end + b_end - b_start) // PIECE
    n_pieces = N_PACK // PIECE

    def piece_copy(q, slot):
        row0 = q * PIECE + (b_start - a_end if q >= first_b else 0) + (c_start - b_end if q >= first_c else 0)
        return pltpu.make_async_copy(wt_hbm.at[pl.ds(row0, PIECE), :], stage.at[slot], sem.at[slot])

    narrow_copies = (
        pltpu.make_async_copy(wt_hbm.at[pl.ds(MI_COL, SUBLANES), :], narrow.at[pl.ds(0, SUBLANES), :], nsem.at[0]),
        pltpu.make_async_copy(wt_hbm.at[pl.ds(GA_COL, G_RANK), :], narrow.at[pl.ds(SUBLANES, G_RANK), :], nsem.at[1]))

    for q in range(REPACK_AHEAD):
        piece_copy(q, q).start()
    for cp in narrow_copies:
        cp.start()
    for q in range(n_pieces):
        if q + REPACK_AHEAD < n_pieces:
            piece_copy(q + REPACK_AHEAD, (q + REPACK_AHEAD) % REPACK_SLOTS).start()
        piece_copy(q, q % REPACK_SLOTS).wait()
        w_ref[:, q * PIECE:(q + 1) * PIECE] = stage[q % REPACK_SLOTS].T.astype(BF16)
    for cp in narrow_copies:
        cp.wait()
    gates = narrow[0:SUBLANES, :]
    zeros = jnp.zeros((SUBLANES - HEADS, D_MODEL), F32)
    wst_ref[...] = jnp.concatenate([gates[:HEADS], zeros, gates[HEADS:], zeros, narrow[SUBLANES:, :]],
                                   axis=0).astype(BF16)


def _mixer_kernel(x_ref, g1_ref, w_ref, wst_ref, sb_ref, a2_ref, a2b_ref, cw_ref, cb_ref,
                  mhg_ref, ghg_ref, wbm_ref, wbg_ref, wout_ref, *rest, tm, meta):
    nc = tm // RCHUNK
    pairs = [(c, h) for c in range(nc) for h in range(HEADS)]
    if not meta:
        tail0_ref, cn0_ref, m0_ref, s0_ref, *rest = rest
    h1_ref, tail_ref, cn_ref, m_ref, s_ref, *rest = rest
    if meta:
        wt_hbm = w_ref
        w_ref, wst_ref, *rest = rest
        *rest, stage, narrow, sem, nsem = rest
        _repack_w_in(wt_hbm, w_ref, wst_ref, stage, narrow, sem, nsem)
    else:
        *rest, wbm_s, wbg_s, wout_s, wstage, wsem = rest
        merge_w = (wbm_s, wbg_s, wout_s)
    (xn_s, qk_s, q_s, k_s, wk_s, vo_s, gq64_s, gq128_s, gki_s, gke64_s, gke128_s, gv_s, hm_s, hg_s,
     ga_s, gb_s, gc_s, gd_s, gqk_s, tric_s, triu_s, spread_s) = rest

    @pl.when(pl.program_id(0) == 0)
    def _():
        if meta:
            qk_s[0:SUBLANES, :] = jnp.zeros((SUBLANES, 2 * QK), F32)
            cn_ref[...] = jnp.zeros(cn_ref.shape, F32)
            m_ref[...] = jnp.zeros(m_ref.shape, F32)
            s_ref[...] = jnp.zeros(s_ref.shape, F32)
        else:
            qk_s[0:SUBLANES, :] = tail0_ref[...]
            cn_ref[...] = cn0_ref[...]
            m_ref[...] = m0_ref[...]
            s_ref[...] = s0_ref[...]
        for h in range(HEADS):
            vo_s[:, h * CN + DV:(h + 1) * CN] = jnp.ones((tm, LANES), BF16)
        if not meta:
            srcs = (wbm_ref, wbg_ref, wout_ref)
            n_blk = D_MODEL // MERGE_ROWS

            def block_copy(j, slot):
                return pltpu.make_async_copy(srcs[j // n_blk].at[pl.ds((j % n_blk) * MERGE_ROWS, MERGE_ROWS), :],
                                             wstage.at[slot], wsem.at[slot])

            for j in range(REPACK_AHEAD):
                block_copy(j, j).start()
            for j in range(len(srcs) * n_blk):
                slot = j % REPACK_SLOTS
                if j + REPACK_AHEAD < len(srcs) * n_blk:
                    block_copy(j + REPACK_AHEAD, (j + REPACK_AHEAD) % REPACK_SLOTS).start()
                block_copy(j, slot).wait()
                r0 = (j % n_blk) * MERGE_ROWS
                merge_w[j // n_blk][r0:r0 + MERGE_ROWS, :] = wstage[slot].astype(BF16)
        row = lax.broadcasted_iota(jnp.int32, (tm, tm), 0)
        col = lax.broadcasted_iota(jnp.int32, (tm, tm), 1)
        tric_s[...] = jnp.where((col <= row) & (col >= (row & -CHUNK)), 1.0, 0.0)[:RCHUNK, :RCHUNK].astype(BF16)
        triu_s[...] = jnp.where((row <= col) & (row >= (col & -RCHUNK)), 1.0, 0.0).astype(BF16)
        sr = lax.broadcasted_iota(jnp.int32, (LANES, 2 * HEADS * LANES), 0)
        sc = lax.broadcasted_iota(jnp.int32, (LANES, 2 * HEADS * LANES), 1)
        blk = (sc >> 9) * (3 * SUBLANES)
        spread_s[...] = jnp.where(
            (sr >= blk) & (sr < blk + 3 * SUBLANES) & ((sr & (SUBLANES - 1)) == ((sc >> 7) & (HEADS - 1))),
            1.0, 0.0).astype(BF16)

    if meta:
        x = jnp.concatenate([jnp.zeros((tm - N_META, D_MODEL), F32), x_ref[...]], axis=0)
    else:
        x = x_ref[...]
    xn = x * lax.rsqrt(jnp.mean(x * x, axis=-1, keepdims=True) + EPS) * g1_ref[...]
    if meta:
        valid = lax.broadcasted_iota(jnp.int32, (tm, 1), 0) >= (tm - N_META)
        xn = jnp.where(valid, xn, 0.0)
    xn_s[...] = xn.astype(BF16)

    def proj(off, width):
        return _dot(xn_s[...], w_ref[:, off:off + width])

    def rows(c):
        return slice(c * RCHUNK, (c + 1) * RCHUNK)

    def half(c, second):
        lo = c * RCHUNK + (CHUNK if second else 0)
        return slice(lo, lo + CHUNK)

    crow = lax.broadcasted_iota(jnp.int32, (RCHUNK, RCHUNK), 0)
    ccol = lax.broadcasted_iota(jnp.int32, (RCHUNK, RCHUNK), 1)
    causal = ccol <= crow
    tri_c = tric_s[...]
    tri_up = triu_s[...]
    spread = spread_s[...]

    def st_qk(i, v):
        qk_s[SUBLANES:SUBLANES + tm, i * DV:(i + 1) * DV] = v

    def st_mv(i, v):
        vo_s[:, i * CN:i * CN + DV] = v.astype(BF16)

    def st_gqk(i, v):
        gqk_s[:, i * DV:(i + 1) * DV] = v

    def st_gv(i, v):
        gv_s[:, i * DV:(i + 1) * DV] = v.astype(BF16)

    def st_mo(i, v):
        ga_s[:, i * DV:(i + 1) * DV] = _sigmoid(v)

    def st_gm(i, v):
        gb_s[:, i * DV:(i + 1) * DV] = _sigmoid(v)

    def st_gr(i, v):
        gc_s[:, i * DV:(i + 1) * DV] = v * _sigmoid(v)

    def st_gg(i, v):
        gd_s[:, i * DV:(i + 1) * DV] = _sigmoid(v)

    groups = [(O_MQK, st_qk), (O_MV, st_mv), (O_GQ, st_gqk), (O_GV, st_gv)]
    if not meta:
        groups += [(O_MO, st_mo), (O_GATE_M, st_gm), (O_GR, st_gr), (O_GATE_G, st_gg)]
    queue = [(seg, i, store) for seg, store in groups for i in range(D_MODEL // DV)]
    emitted = [0]

    def fill(n):
        for _ in range(n):
            if emitted[0] < len(queue):
                seg, i, store = queue[emitted[0]]
                store(i, proj(seg + i * DV, DV))
                emitted[0] += 1

    def finish_groups(n):
        fill(n * (D_MODEL // DV) - emitted[0])

    narrow = _dot_nt(wst_ref[...], xn_s[...])
    gates = narrow[:2 * SUBLANES, :] + jnp.concatenate([sb_ref[...]] * (tm // LANES), axis=1)
    fill(2)
    s1 = jnp.concatenate([jnp.zeros((GA_LANE, tm), F32), narrow[2 * SUBLANES:, :],
                          jnp.zeros((LANES - GA_LANE - G_RANK, tm), F32)], axis=0).T
    logi = gates[:SUBLANES, :]
    logf = _log_sigmoid(gates[SUBLANES:, :])
    if meta:
        valid_t = lax.broadcasted_iota(jnp.int32, (1, tm), 1) >= (tm - N_META)
        logi = jnp.where(valid_t, logi, NEG_BIG)
        logf = jnp.where(valid_t, logf, 0.0)
    zero8 = jnp.zeros((SUBLANES, tm), F32)
    b4 = _dot(jnp.concatenate(_pieces(logf) + [zero8], axis=0).astype(BF16), tri_up)
    fill(2)
    b_all = b4[:SUBLANES] + b4[SUBLANES:2 * SUBLANES] + b4[2 * SUBLANES:3 * SUBLANES]
    c_all = logi - b_all
    m_run = m_ref[...]
    m_in, w_parts, a_chunk = [], [], []
    for c in range(nc):
        b_c = b_all[:, rows(c)]
        g_c = jnp.broadcast_to(b_c[:, RCHUNK - 1:RCHUNK], (SUBLANES, LANES))
        wlog = g_c + c_all[:, rows(c)]
        m_next = jnp.maximum(g_c + m_run, jnp.max(wlog, axis=1, keepdims=True))
        m_in.append(m_run)
        w_parts.append(jnp.exp(wlog - m_next))
        a_chunk.append(jnp.exp(g_c + m_run - m_next))
        m_run = m_next
    m_ref[...] = m_run
    stacked = jnp.concatenate(
        _pieces(b_all) + _pieces(jnp.concatenate(w_parts, axis=1))
        + [jnp.zeros((LANES - 6 * SUBLANES, tm), F32)], axis=0)
    stacked_t = stacked.T.astype(BF16)
    fill(2)
    tiles = _dot(stacked_t, spread)
    fill(2)
    b_t, w_t = tiles[:, :QK], tiles[:, QK:]

    finish_groups(1)
    for blk in range(2 * QK // DV):
        csl = slice(blk * DV, (blk + 1) * DV)
        conv = cb_ref[:, csl]
        for j in range(CONV_W):
            conv = conv + cw_ref[j:j + 1, csl] * qk_s[pl.ds(SUBLANES - (CONV_W - 1) + j, tm), csl]
        conv = conv * _sigmoid(conv)
        if blk < QK // DV:
            q_s[:, csl] = conv * (DK ** -0.5)
        else:
            ksl = slice(blk * DV - QK, (blk + 1) * DV - QK)
            k_s[:, ksl] = conv.astype(BF16)
            wk_s[:, ksl] = (w_t[:, ksl] * conv).astype(BF16)
        fill(1)
    tail_new = qk_s[tm:tm + SUBLANES, :]
    qk_s[0:SUBLANES, :] = tail_new
    tail_ref[...] = tail_new

    za = _dot(s1.astype(BF16), a2_ref[...]) + a2b_ref[...]
    fill(1)
    loga = _log_sigmoid(za) / G_TAU
    if meta:
        loga = jnp.where(valid, loga, 0.0)
    hi = loga.astype(BF16)
    r1 = loga - hi.astype(F32)
    mid = r1.astype(BF16)
    bc = jnp.concatenate([_dot(tri_c, hi[rows(c), :]) + _dot(tri_c, mid[rows(c), :])
                          for c in range(nc)], axis=0)
    fill(2)
    e_blk = jnp.exp(bc)
    tot_a = [bc[c * RCHUNK + CHUNK - 1:c * RCHUNK + CHUNK, :] for c in range(nc)]
    tot_b = [bc[(c + 1) * RCHUNK - 1:(c + 1) * RCHUNK, :] for c in range(nc)]
    e_a = [jnp.exp(t) for t in tot_a]
    e_b = [jnp.exp(t) for t in tot_b]
    rest = jnp.concatenate([t - bc[half(c, s), :] for c in range(nc) for s, t in ((0, tot_a[c]), (1, tot_b[c]))],
                           axis=0)
    e_rest = jnp.exp(rest)
    ones_row = jnp.ones((1, QK), F32)
    to_chunk = jnp.concatenate([jnp.broadcast_to(f, (CHUNK, QK)) for c in range(nc) for f in (ones_row, e_a[c])],
                               axis=0)
    from_blk = jnp.concatenate([jnp.broadcast_to(f, (CHUNK, QK)) for c in range(nc) for f in (e_b[c], ones_row)],
                               axis=0)
    finish_groups(3)
    gq = gqk_s[:, :QK] * (DK ** -0.5)
    gk = gqk_s[:, QK:]
    gq64_s[...] = (gq * e_blk).astype(BF16)
    gq128_s[...] = (gq * e_blk * to_chunk).astype(BF16)
    gki_s[...] = (gk * jnp.exp(-bc)).astype(BF16)
    gke64_s[...] = (gk * e_rest).astype(BF16)
    gke128_s[...] = (gk * e_rest * from_blk).astype(BF16)
    fill(2)
    e_rows = [(e_a[c] * e_b[c])[:, h * DK:(h + 1) * DK] for (c, h) in pairs]
    e_cols = jnp.concatenate(e_rows + [jnp.zeros((LANES - len(pairs), DK), F32)], axis=0).T

    finish_groups(2)
    dmat, rmax, sim = {}, {}, {}
    for (c, h) in pairs:
        d = b_t[rows(c), h * LANES:(h + 1) * LANES] + c_all[h:h + 1, rows(c)]
        d = jnp.where(causal, d, -jnp.inf)
        dmat[c, h] = d
        rmax[c, h] = jnp.max(d, axis=-1, keepdims=True)
    fill(1)
    for (c, h) in pairs:
        sim[c, h] = _dot_nt(q_s[rows(c), h * DK:(h + 1) * DK].astype(BF16), k_s[rows(c), h * DK:(h + 1) * DK])
    fill(1)
    lhs, emr, upd = {}, {}, {}
    for (c, h) in pairs:
        il = b_t[rows(c), h * LANES:(h + 1) * LANES] + m_in[c][h:h + 1, :]
        m_row = jnp.maximum(il, rmax[c, h])
        wts = jnp.exp(dmat[c, h] - m_row) * sim[c, h]
        aq = jnp.exp(il - m_row) * q_s[rows(c), h * DK:(h + 1) * DK]
        lhs[c, h] = jnp.concatenate([aq.astype(BF16), wts.astype(BF16)], axis=1)
        emr[c, h] = jnp.exp(-m_row)
    fill(1)
    for (c, h) in pairs:
        upd[c, h] = _dot_tn(wk_s[rows(c), h * DK:(h + 1) * DK], vo_s[rows(c), h * CN:(h + 1) * CN])
    fill(1)
    res = {}
    for h in range(HEADS):
        state = cn_ref[h]
        for c in range(nc):
            rhs = jnp.concatenate([state.astype(BF16), vo_s[rows(c), h * CN:(h + 1) * CN]], axis=0)
            res[c, h] = _dot(lhs[c, h], rhs)
            a = a_chunk[c][h:h + 1, :]
            state = jnp.concatenate([a] * (CN // LANES), axis=1) * state + upd[c, h]
        cn_ref[h] = state
        fill(1)
    for (c, h) in pairs:
        inv = 1.0 / jnp.maximum(jnp.abs(res[c, h][:, DV:]), emr[c, h])
        hm_s[rows(c), h * DV:h * DV + LANES] = res[c, h][:, :LANES] * inv
        hm_s[rows(c), h * DV + LANES:(h + 1) * DV] = res[c, h][:, LANES:DV] * inv
    fill(1)

    finish_groups(4)
    att, gupd = {}, {}
    zblk = jnp.zeros((CHUNK, DK), BF16)
    for (c, h) in pairs:
        dsl = slice(h * DK, (h + 1) * DK)
        ra, rb = half(c, 0), half(c, 1)
        qp = jnp.concatenate([jnp.concatenate([gq64_s[ra, dsl], zblk], axis=1),
                              jnp.concatenate([zblk, gq64_s[rb, dsl]], axis=1)], axis=0)
        kp = jnp.concatenate([jnp.concatenate([gki_s[ra, dsl], gke64_s[ra, dsl]], axis=1),
                              jnp.concatenate([zblk, gki_s[rb, dsl]], axis=1)], axis=0)
        att[c, h] = jnp.where(causal, _dot_nt(qp, kp), 0.0).astype(BF16)
    fill(1)
    for (c, h) in pairs:
        gupd[c, h] = _dot_tn(gke128_s[rows(c), h * DK:(h + 1) * DK], gv_s[rows(c), h * DV:(h + 1) * DV])
    fill(1)
    for h in range(HEADS):
        sst = s_ref[h]
        for c in range(nc):
            lhs_g = jnp.concatenate([gq128_s[rows(c), h * DK:(h + 1) * DK], att[c, h]], axis=1)
            rhs_g = jnp.concatenate([sst.astype(BF16), gv_s[rows(c), h * DV:(h + 1) * DV]], axis=0)
            hg_s[rows(c), h * DV:(h + 1) * DV] = _dot(lhs_g, rhs_g)
            i = c * HEADS + h
            sst = e_cols[:, i:i + 1] * sst + gupd[c, h]
        s_ref[h] = sst
        fill(1)

    if meta:
        h1_ref[...] = x
        return

    def head_norm(src, gain_ref):
        outs = []
        for h in range(HEADS):
            t = src[:, h * DV:(h + 1) * DV]
            outs.append(t * lax.rsqrt(jnp.mean(t * t, axis=-1, keepdims=True) + EPS) * gain_ref[h:h + 1, :])
        return jnp.concatenate(outs, axis=-1)

    finish_groups(len(groups))
    y_m = head_norm(hm_s, mhg_ref) * ga_s[...]
    merged = gb_s[...] * _dot(y_m.astype(BF16), merge_w[0][...])
    y_g = head_norm(hg_s, ghg_ref) * gc_s[...]
    merged = merged + gd_s[...] * _dot(y_g.astype(BF16), merge_w[1][...])
    h1_ref[...] = x + _dot(merged.astype(BF16), merge_w[2][...])


def _const_spec(shape):
    nd = len(shape)
    return pl.BlockSpec(shape, lambda i: (0,) * nd, pipeline_mode=pl.Buffered(1))


def _mixer_call(x2, consts, state, *, tm, meta):
    t = tm if meta else x2.shape[0]
    state_shapes = [jax.ShapeDtypeStruct(a.shape, F32) for a in state]
    row_spec = pl.BlockSpec((tm, D_MODEL), lambda i: (i, 0))
    x_spec = pl.BlockSpec(x2.shape, lambda i: (0, 0)) if meta else row_spec
    state_specs = [pl.BlockSpec(a.shape, lambda i, nd=len(a.shape): (0,) * nd) for a in state]
    hbm_operands = (1,) if meta else LEAD_UNUSED
    packed = [jax.ShapeDtypeStruct((D_MODEL, N_PACK), BF16),
              jax.ShapeDtypeStruct((2 * SUBLANES + G_RANK, D_MODEL), BF16)] if meta else []
    return pl.pallas_call(
        functools.partial(_mixer_kernel, tm=tm, meta=meta),
        grid=(t // tm,),
        in_specs=[x_spec] + [pl.BlockSpec(memory_space=pl.ANY) if i in hbm_operands else _const_spec(c.shape)
                             for i, c in enumerate(consts)] + ([] if meta else state_specs),
        out_specs=[row_spec] + state_specs + [pl.BlockSpec(a.shape, lambda i: (0, 0)) for a in packed],
        out_shape=[jax.ShapeDtypeStruct((t, D_MODEL), F32)] + state_shapes + packed,
        scratch_shapes=[
            pltpu.VMEM((tm, D_MODEL), BF16),
            pltpu.VMEM((tm + SUBLANES, 2 * QK), F32),
            pltpu.VMEM((tm, QK), F32),
            pltpu.VMEM((tm, QK), BF16),
            pltpu.VMEM((tm, QK), BF16),
            pltpu.VMEM((tm, HEADS * CN), BF16),
            pltpu.VMEM((tm, QK), BF16),
            pltpu.VMEM((tm, QK), BF16),
            pltpu.VMEM((tm, QK), BF16),
            pltpu.VMEM((tm, QK), BF16),
            pltpu.VMEM((tm, QK), BF16),
            pltpu.VMEM((tm, D_MODEL), BF16),
            pltpu.VMEM((tm, D_MODEL), F32),
            pltpu.VMEM((tm, D_MODEL), F32),
            pltpu.VMEM((tm, D_MODEL), F32),
            pltpu.VMEM((tm, D_MODEL), F32),
            pltpu.VMEM((tm, D_MODEL), F32),
            pltpu.VMEM((tm, D_MODEL), F32),
            pltpu.VMEM((tm, D_MODEL), F32),
            pltpu.VMEM((RCHUNK, RCHUNK), BF16),
            pltpu.VMEM((tm, tm), BF16),
            pltpu.VMEM((LANES, 2 * HEADS * LANES), BF16),
        ] + ([
            pltpu.VMEM((REPACK_SLOTS, PIECE, D_MODEL), F32),
            pltpu.VMEM((SUBLANES + G_RANK, D_MODEL), F32),
            pltpu.SemaphoreType.DMA((REPACK_SLOTS,)),
            pltpu.SemaphoreType.DMA((2,)),
        ] if meta else [
            pltpu.VMEM((D_MODEL, D_MODEL), BF16),
            pltpu.VMEM((D_MODEL, D_MODEL), BF16),
            pltpu.VMEM((D_MODEL, D_MODEL), BF16),
            pltpu.VMEM((REPACK_SLOTS, MERGE_ROWS, D_MODEL), F32),
            pltpu.SemaphoreType.DMA((REPACK_SLOTS,)),
        ]),
        compiler_params=pltpu.CompilerParams(
            dimension_semantics=("arbitrary",), vmem_limit_bytes=VMEM_LIMIT),
        name="mixer_meta" if meta else "mixer",
    )(x2, *consts, *([] if meta else state))


def _ffn_kernel(h_ref, g2_ref, wg_hbm, wu_hbm, wd_hbm, gf_ref, o_ref,
                wg_ref, wu_ref, wd_ref, sg, su, sd, sem, *, fc, parts):
    n_chunks = wg_ref.shape[1] // fc

    def chunk_copies(f, slot):
        cols = pl.ds(f * fc, fc)
        return (pltpu.make_async_copy(wg_hbm.at[:, cols], sg.at[slot], sem.at[slot, 0]),
                pltpu.make_async_copy(wu_hbm.at[:, cols], su.at[slot], sem.at[slot, 1]),
                pltpu.make_async_copy(wd_hbm.at[cols, :], sd.at[slot], sem.at[slot, 2]))

    @pl.when(pl.program_id(0) == 0)
    def _():
        n_slots = sg.shape[0]
        for f in range(n_slots - 1):
            for cp in chunk_copies(f, f):
                cp.start()
        for f in range(n_chunks):
            slot = f % n_slots
            if f + n_slots - 1 < n_chunks:
                for cp in chunk_copies(f + n_slots - 1, (f + n_slots - 1) % n_slots):
                    cp.start()
            for cp in chunk_copies(f, slot):
                cp.wait()
            wg_ref[:, f * fc:(f + 1) * fc] = sg[slot].astype(BF16)
            wu_ref[:, f * fc:(f + 1) * fc] = su[slot].astype(BF16)
            wd_ref[f * fc:(f + 1) * fc, :] = sd[slot].astype(BF16)

    rows = h_ref.shape[0] // parts
    blocks = [slice(p * rows, (p + 1) * rows) for p in range(parts)]
    acc, hn = [], []
    for blk in blocks:
        h = h_ref[blk, :]
        hn.append((h * lax.rsqrt(jnp.mean(h * h, axis=-1, keepdims=True) + EPS) * g2_ref[...]).astype(BF16))
        acc.append(h)
    for f in range(0, wg_ref.shape[1], fc):
        for p in range(parts):
            gate = _dot(hn[p], wg_ref[:, f:f + fc])
            up = _dot(hn[p], wu_ref[:, f:f + fc])
            acc[p] = acc[p] + _dot((gate * _sigmoid(gate) * up).astype(BF16), wd_ref[f:f + fc, :])
    for p, blk in enumerate(blocks):
        a = acc[p]
        o_ref[blk, :] = a * lax.rsqrt(jnp.mean(a * a, axis=-1, keepdims=True) + EPS) * gf_ref[...]


def _ffn_call(h1, g2, wg, wu, wd, gf, *, tm, fc, parts):
    t = h1.shape[0]
    d_ff = wg.shape[1]
    row_spec = pl.BlockSpec((tm, D_MODEL), lambda i: (i, 0))
    hbm_spec = pl.BlockSpec(memory_space=pl.ANY)
    return pl.pallas_call(
        functools.partial(_ffn_kernel, fc=fc, parts=parts),
        grid=(t // tm,),
        in_specs=[row_spec, _const_spec(g2.shape), hbm_spec, hbm_spec, hbm_spec, _const_spec(gf.shape)],
        out_specs=row_spec,
        out_shape=jax.ShapeDtypeStruct((t, D_MODEL), F32),
        scratch_shapes=[
            pltpu.VMEM((D_MODEL, d_ff), BF16),
            pltpu.VMEM((D_MODEL, d_ff), BF16),
            pltpu.VMEM((d_ff, D_MODEL), BF16),
            pltpu.VMEM((FFN_SLOTS, D_MODEL, fc), F32),
            pltpu.VMEM((FFN_SLOTS, D_MODEL, fc), F32),
            pltpu.VMEM((FFN_SLOTS, fc, D_MODEL), F32),
            pltpu.SemaphoreType.DMA((FFN_SLOTS, 3)),
        ],
        compiler_params=pltpu.CompilerParams(
            dimension_semantics=("arbitrary",), vmem_limit_bytes=VMEM_LIMIT),
        name="ffn",
    )(h1, g2, wg, wu, wd, gf)


def kernel(x, meta_tokens, norm1_g, w_in, conv_w, conv_b, m_gate_b, g_a2, g_a2_b, m_head_g, g_head_g,
           w_branch_m, w_branch_g, w_out, norm2_g, w_ff_gate, w_ff_up, w_ff_down, final_g):
    bsz, seq, d = x.shape
    assert bsz == 1 and d == D_MODEL and norm1_g.shape[0] == 1 and seq % CHUNK == 0
    row = lambda a: a.reshape(1, -1).astype(F32)

    gate_bias = jnp.zeros((2 * SUBLANES, LANES), F32)
    gate_bias = gate_bias.at[:HEADS].set(m_gate_b[0, 0][:, None]).at[SUBLANES:SUBLANES + HEADS].set(m_gate_b[0, 1][:, None])
    a2 = jnp.zeros((LANES, QK), F32).at[GA_LANE:GA_LANE + G_RANK].set(g_a2[0]).astype(BF16)
    conv_w8 = jnp.zeros((SUBLANES, 2 * QK), F32).at[:CONV_W].set(conv_w[0])
    consts = (gate_bias, a2, row(g_a2_b[0]), conv_w8, row(conv_b[0]), m_head_g[0].astype(F32), g_head_g[0].astype(F32),
              w_branch_m[0].astype(F32), w_branch_g[0].astype(F32), w_out[0].astype(F32))

    state_shapes = (jax.ShapeDtypeStruct((SUBLANES, 2 * QK), F32), jax.ShapeDtypeStruct((HEADS, DK, CN), F32),
                    jax.ShapeDtypeStruct((SUBLANES, LANES), F32), jax.ShapeDtypeStruct((HEADS, DK, DV), F32))
    unused = jnp.zeros((SUBLANES, LANES), BF16)
    g1 = row(norm1_g[0])
    lead_consts = (g1, w_in[0].T.astype(F32), unused) + consts[:-len(LEAD_UNUSED)] + (unused,) * len(LEAD_UNUSED)
    _, *state, w_packed, w_narrow = _mixer_call(meta_tokens.astype(F32), lead_consts, state_shapes, tm=RCHUNK,
                                                meta=True)
    h1, *_ = _mixer_call(x[0], (g1, w_packed, w_narrow) + consts, tuple(state), tm=512, meta=False)
    out = _ffn_call(h1, row(norm2_g[0]), w_ff_gate[0].astype(F32), w_ff_up[0].astype(F32),
                    w_ff_down[0].astype(F32), row(final_g), tm=1024, fc=256, parts=2)
    return out[None]
```

```python
import functools

import jax
import jax.numpy as jnp
from jax import lax
from jax.experimental import pallas as pl
from jax.experimental.pallas import tpu as pltpu

F32 = jnp.float32
BF16 = jnp.bfloat16

D_MODEL = 1024
N_META = 16
CHUNK = 64
RCHUNK = 128
EPS = 1e-6
HEADS = 4
DV = D_MODEL // HEADS
DK = DV // 2
QK = HEADS * DK
G_RANK = 16
G_TAU = 16.0
CONV_W = 4
LANES = 128
SUBLANES = 8
NEG_BIG = -1e30
VMEM_LIMIT = 60 * 1024 * 1024
CN = DV + LANES

W_IN_RANGES = ((0, 2048), (2056, 5128), (5144, 8216))
N_PACK = sum(b - a for a, b in W_IN_RANGES)
O_MQK, O_MV, O_MO = 0, 1024, 2048
O_GQ, O_GV = 3072, 4096
O_GR, O_GATE_M, O_GATE_G = 5120, 6144, 7168
MI_COL, GA_COL = 2048, 5128
GA_LANE = 8
PIECE = 512
MERGE_ROWS = 256
REPACK_AHEAD = 3
REPACK_SLOTS = REPACK_AHEAD + 1
DMA_PRIORITIES = 2
FFN_SLOTS = 3
LEAD_UNUSED = (10, 11, 12)

NT_DIMS = (((1,), (1,)), ((), ()))
TN_DIMS = (((0,), (0,)), ((), ()))


def _dot(a, b):
    return jnp.dot(a, b, preferred_element_type=F32)


def _dot_nt(a, b):
    return lax.dot_general(a, b, NT_DIMS, preferred_element_type=F32)


def _dot_tn(a, b):
    return lax.dot_general(a, b, TN_DIMS, preferred_element_type=F32)


def _pieces(x):
    hi = x.astype(BF16).astype(F32)
    mid = (x - hi).astype(BF16).astype(F32)
    lo = (x - hi - mid).astype(BF16).astype(F32)
    return [hi, mid, lo]


def _log_sigmoid(x):
    return jnp.minimum(x, 0.0) - jnp.log(1.0 + jnp.exp(-jnp.abs(x)))


def _sigmoid(x):
    return 1.0 / (1.0 + jnp.exp(-x))


def _repack_w_in(wt_hbm, w_ref, wst_ref, stage, narrow, sem, nsem):
    (_, a_end), (b_start, b_end), (c_start, _) = W_IN_RANGES
    first_b, first_c = a_end // PIECE, (a_end + b_end - b_start) // PIECE
    n_pieces = N_PACK // PIECE

    def piece_copy(q, slot):
        row0 = q * PIECE + (b_start - a_end if q >= first_b else 0) + (c_start - b_end if q >= first_c else 0)
        return pltpu.make_async_copy(wt_hbm.at[pl.ds(row0, PIECE), :], stage.at[slot], sem.at[slot])

    narrow_copies = (
        pltpu.make_async_copy(wt_hbm.at[pl.ds(MI_COL, SUBLANES), :], narrow.at[pl.ds(0, SUBLANES), :], nsem.at[0]),
        pltpu.make_async_copy(wt_hbm.at[pl.ds(GA_COL, G_RANK), :], narrow.at[pl.ds(SUBLANES, G_RANK), :], nsem.at[1]))

    for q in range(REPACK_AHEAD):
        piece_copy(q, q).start(priority=q % DMA_PRIORITIES)
    for cp in narrow_copies:
        cp.start()
    for q in range(n_pieces):
        if q + REPACK_AHEAD < n_pieces:
            piece_copy(q + REPACK_AHEAD, (q + REPACK_AHEAD) % REPACK_SLOTS).start(
                priority=(q + REPACK_AHEAD) % DMA_PRIORITIES)
        piece_copy(q, q % REPACK_SLOTS).wait()
        w_ref[:, q * PIECE:(q + 1) * PIECE] = stage[q % REPACK_SLOTS].T.astype(BF16)
    for cp in narrow_copies:
        cp.wait()
    gates = narrow[0:SUBLANES, :]
    zeros = jnp.zeros((SUBLANES - HEADS, D_MODEL), F32)
    wst_ref[...] = jnp.concatenate([gates[:HEADS], zeros, gates[HEADS:], zeros, narrow[SUBLANES:, :]],
                                   axis=0).astype(BF16)


def _mixer_kernel(x_ref, g1_ref, w_ref, wst_ref, sb_ref, a2_ref, a2b_ref, cw_ref, cb_ref,
                  mhg_ref, ghg_ref, wbm_ref, wbg_ref, wout_ref, *rest, tm, meta):
    nc = tm // RCHUNK
    pairs = [(c, h) for c in range(nc) for h in range(HEADS)]
    if not meta:
        tail0_ref, cn0_ref, m0_ref, s0_ref, *rest = rest
    h1_ref, tail_ref, cn_ref, m_ref, s_ref, *rest = rest
    if meta:
        wt_hbm = w_ref
        w_ref, wst_ref, *rest = rest
        *rest, stage, narrow, sem, nsem = rest
        _repack_w_in(wt_hbm, w_ref, wst_ref, stage, narrow, sem, nsem)
    else:
        *rest, wbm_s, wbg_s, wout_s, wstage, wsem = rest
        merge_w = (wbm_s, wbg_s, wout_s)
    (xn_s, qk_s, q_s, k_s, wk_s, vo_s, gq64_s, gq128_s, gki_s, gke64_s, gke128_s, gv_s, hm_s, hg_s,
     ga_s, gb_s, gc_s, gd_s, gqk_s, tric_s, triu_s, spread_s) = rest

    @pl.when(pl.program_id(0) == 0)
    def _():
        if meta:
            qk_s[0:SUBLANES, :] = jnp.zeros((SUBLANES, 2 * QK), F32)
            cn_ref[...] = jnp.zeros(cn_ref.shape, F32)
            m_ref[...] = jnp.zeros(m_ref.shape, F32)
            s_ref[...] = jnp.zeros(s_ref.shape, F32)
        else:
            qk_s[0:SUBLANES, :] = tail0_ref[...]
            cn_ref[...] = cn0_ref[...]
            m_ref[...] = m0_ref[...]
            s_ref[...] = s0_ref[...]
        for h in range(HEADS):
            vo_s[:, h * CN + DV:(h + 1) * CN] = jnp.ones((tm, LANES), BF16)
        if not meta:
            srcs = (wbm_ref, wbg_ref, wout_ref)
            n_blk = D_MODEL // MERGE_ROWS

            def block_copy(j, slot):
                return pltpu.make_async_copy(srcs[j // n_blk].at[pl.ds((j % n_blk) * MERGE_ROWS, MERGE_ROWS), :],
                                             wstage.at[slot], wsem.at[slot])

            for j in range(REPACK_AHEAD):
                block_copy(j, j).start(priority=j % DMA_PRIORITIES)
            for j in range(len(srcs) * n_blk):
                slot = j % REPACK_SLOTS
                if j + REPACK_AHEAD < len(srcs) * n_blk:
                    block_copy(j + REPACK_AHEAD, (j + REPACK_AHEAD) % REPACK_SLOTS).start(
                        priority=(j + REPACK_AHEAD) % DMA_PRIORITIES)
                block_copy(j, slot).wait()
                r0 = (j % n_blk) * MERGE_ROWS
                merge_w[j // n_blk][r0:r0 + MERGE_ROWS, :] = wstage[slot].astype(BF16)
        row = lax.broadcasted_iota(jnp.int32, (tm, tm), 0)
        col = lax.broadcasted_iota(jnp.int32, (tm, tm), 1)
        tric_s[...] = jnp.where((col <= row) & (col >= (row & -CHUNK)), 1.0, 0.0)[:RCHUNK, :RCHUNK].astype(BF16)
        triu_s[...] = jnp.where((row <= col) & (row >= (col & -RCHUNK)), 1.0, 0.0).astype(BF16)
        sr = lax.broadcasted_iota(jnp.int32, (LANES, 2 * HEADS * LANES), 0)
        sc = lax.broadcasted_iota(jnp.int32, (LANES, 2 * HEADS * LANES), 1)
        blk = (sc >> 9) * (3 * SUBLANES)
        spread_s[...] = jnp.where(
            (sr >= blk) & (sr < blk + 3 * SUBLANES) & ((sr & (SUBLANES - 1)) == ((sc >> 7) & (HEADS - 1))),
            1.0, 0.0).astype(BF16)

    if meta:
        x = jnp.concatenate([jnp.zeros((tm - N_META, D_MODEL), F32), x_ref[...]], axis=0)
    else:
        x = x_ref[...]
    xn = x * lax.rsqrt(jnp.mean(x * x, axis=-1, keepdims=True) + EPS) * g1_ref[...]
    if meta:
        valid = lax.broadcasted_iota(jnp.int32, (tm, 1), 0) >= (tm - N_META)
        xn = jnp.where(valid, xn, 0.0)
    xn_s[...] = xn.astype(BF16)

    def proj(off, width):
        return _dot(xn_s[...], w_ref[:, off:off + width])

    def rows(c):
        return slice(c * RCHUNK, (c + 1) * RCHUNK)

    def half(c, second):
        lo = c * RCHUNK + (CHUNK if second else 0)
        return slice(lo, lo + CHUNK)

    crow = lax.broadcasted_iota(jnp.int32, (RCHUNK, RCHUNK), 0)
    ccol = lax.broadcasted_iota(jnp.int32, (RCHUNK, RCHUNK), 1)
    causal = ccol <= crow
    tri_c = tric_s[...]
    tri_up = triu_s[...]
    spread = spread_s[...]

    def st_qk(i, v):
        qk_s[SUBLANES:SUBLANES + tm, i * DV:(i + 1) * DV] = v

    def st_mv(i, v):
        vo_s[:, i * CN:i * CN + DV] = v.astype(BF16)

    def st_gqk(i, v):
        gqk_s[:, i * DV:(i + 1) * DV] = v

    def st_gv(i, v):
        gv_s[:, i * DV:(i + 1) * DV] = v.astype(BF16)

    def st_mo(i, v):
        ga_s[:, i * DV:(i + 1) * DV] = _sigmoid(v)

    def st_gm(i, v):
        gb_s[:, i * DV:(i + 1) * DV] = _sigmoid(v)

    def st_gr(i, v):
        gc_s[:, i * DV:(i + 1) * DV] = v * _sigmoid(v)

    def st_gg(i, v):
        gd_s[:, i * DV:(i + 1) * DV] = _sigmoid(v)

    groups = [(O_MQK, st_qk), (O_MV, st_mv), (O_GQ, st_gqk), (O_GV, st_gv)]
    if not meta:
        groups += [(O_MO, st_mo), (O_GATE_M, st_gm), (O_GR, st_gr), (O_GATE_G, st_gg)]
    queue = [(seg, i, store) for seg, store in groups for i in range(D_MODEL // DV)]
    emitted = [0]

    def fill(n):
        for _ in range(n):
            if emitted[0] < len(queue):
                seg, i, store = queue[emitted[0]]
                store(i, proj(seg + i * DV, DV))
                emitted[0] += 1

    def finish_groups(n):
        fill(n * (D_MODEL // DV) - emitted[0])

    narrow = _dot_nt(wst_ref[...], xn_s[...])
    gates = narrow[:2 * SUBLANES, :] + jnp.concatenate([sb_ref[...]] * (tm // LANES), axis=1)
    fill(2)
    s1 = jnp.concatenate([jnp.zeros((GA_LANE, tm), F32), narrow[2 * SUBLANES:, :],
                          jnp.zeros((LANES - GA_LANE - G_RANK, tm), F32)], axis=0).T
    logi = gates[:SUBLANES, :]
    logf = _log_sigmoid(gates[SUBLANES:, :])
    if meta:
        valid_t = lax.broadcasted_iota(jnp.int32, (1, tm), 1) >= (tm - N_META)
        logi = jnp.where(valid_t, logi, NEG_BIG)
        logf = jnp.where(valid_t, logf, 0.0)
    zero8 = jnp.zeros((SUBLANES, tm), F32)
    b4 = _dot(jnp.concatenate(_pieces(logf) + [zero8], axis=0).astype(BF16), tri_up)
    fill(2)
    b_all = b4[:SUBLANES] + b4[SUBLANES:2 * SUBLANES] + b4[2 * SUBLANES:3 * SUBLANES]
    c_all = logi - b_all
    m_run = m_ref[...]
    m_in, w_parts, a_chunk = [], [], []
    for c in range(nc):
        b_c = b_all[:, rows(c)]
        g_c = jnp.broadcast_to(b_c[:, RCHUNK - 1:RCHUNK], (SUBLANES, LANES))
        wlog = g_c + c_all[:, rows(c)]
        m_next = jnp.maximum(g_c + m_run, jnp.max(wlog, axis=1, keepdims=True))
        m_in.append(m_run)
        w_parts.append(jnp.exp(wlog - m_next))
        a_chunk.append(jnp.exp(g_c + m_run - m_next))
        m_run = m_next
    m_ref[...] = m_run
    stacked = jnp.concatenate(
        _pieces(b_all) + _pieces(jnp.concatenate(w_parts, axis=1))
        + [jnp.zeros((LANES - 6 * SUBLANES, tm), F32)], axis=0)
    stacked_t = stacked.T.astype(BF16)
    fill(2)
    tiles = _dot(stacked_t, spread)
    fill(2)
    b_t, w_t = tiles[:, :QK], tiles[:, QK:]

    finish_groups(1)
    for blk in range(2 * QK // DV):
        csl = slice(blk * DV, (blk + 1) * DV)
        conv = cb_ref[:, csl]
        for j in range(CONV_W):
            conv = conv + cw_ref[j:j + 1, csl] * qk_s[pl.ds(SUBLANES - (CONV_W - 1) + j, tm), csl]
        conv = conv * _sigmoid(conv)
        if blk < QK // DV:
            q_s[:, csl] = conv * (DK ** -0.5)
        else:
            ksl = slice(blk * DV - QK, (blk + 1) * DV - QK)
            k_s[:, ksl] = conv.astype(BF16)
            wk_s[:, ksl] = (w_t[:, ksl] * conv).astype(BF16)
        fill(1)
    tail_new = qk_s[tm:tm + SUBLANES, :]
    qk_s[0:SUBLANES, :] = tail_new
    tail_ref[...] = tail_new

    za = _dot(s1.astype(BF16), a2_ref[...]) + a2b_ref[...]
    fill(1)
    loga = _log_sigmoid(za) / G_TAU
    if meta:
        loga = jnp.where(valid, loga, 0.0)
    hi = loga.astype(BF16)
    r1 = loga - hi.astype(F32)
    mid = r1.astype(BF16)
    bc = jnp.concatenate([_dot(tri_c, hi[rows(c), :]) + _dot(tri_c, mid[rows(c), :])
                          for c in range(nc)], axis=0)
    fill(2)
    e_blk = jnp.exp(bc)
    tot_a = [bc[c * RCHUNK + CHUNK - 1:c * RCHUNK + CHUNK, :] for c in range(nc)]
    tot_b = [bc[(c + 1) * RCHUNK - 1:(c + 1) * RCHUNK, :] for c in range(nc)]
    e_a = [jnp.exp(t) for t in tot_a]
    e_b = [jnp.exp(t) for t in tot_b]
    rest = jnp.concatenate([t - bc[half(c, s), :] for c in range(nc) for s, t in ((0, tot_a[c]), (1, tot_b[c]))],
                           axis=0)
    e_rest = jnp.exp(rest)
    ones_row = jnp.ones((1, QK), F32)
    to_chunk = jnp.concatenate([jnp.broadcast_to(f, (CHUNK, QK)) for c in range(nc) for f in (ones_row, e_a[c])],
                               axis=0)
    from_blk = jnp.concatenate([jnp.broadcast_to(f, (CHUNK, QK)) for c in range(nc) for f in (e_b[c], ones_row)],
                               axis=0)
    finish_groups(3)
    gq = gqk_s[:, :QK] * (DK ** -0.5)
    gk = gqk_s[:, QK:]
    gq64_s[...] = (gq * e_blk).astype(BF16)
    gq128_s[...] = (gq * e_blk * to_chunk).astype(BF16)
    gki_s[...] = (gk * jnp.exp(-bc)).astype(BF16)
    gke64_s[...] = (gk * e_rest).astype(BF16)
    gke128_s[...] = (gk * e_rest * from_blk).astype(BF16)
    fill(2)
    e_rows = [(e_a[c] * e_b[c])[:, h * DK:(h + 1) * DK] for (c, h) in pairs]
    e_cols = jnp.concatenate(e_rows + [jnp.zeros((LANES - len(pairs), DK), F32)], axis=0).T

    finish_groups(2)
    dmat, rmax, sim = {}, {}, {}
    for (c, h) in pairs:
        d = b_t[rows(c), h * LANES:(h + 1) * LANES] + c_all[h:h + 1, rows(c)]
        d = jnp.where(causal, d, -jnp.inf)
        dmat[c, h] = d
        rmax[c, h] = jnp.max(d, axis=-1, keepdims=True)
    fill(1)
    for (c, h) in pairs:
        sim[c, h] = _dot_nt(q_s[rows(c), h * DK:(h + 1) * DK].astype(BF16), k_s[rows(c), h * DK:(h + 1) * DK])
    fill(1)
    lhs, emr, upd = {}, {}, {}
    for (c, h) in pairs:
        il = b_t[rows(c), h * LANES:(h + 1) * LANES] + m_in[c][h:h + 1, :]
        m_row = jnp.maximum(il, rmax[c, h])
        wts = jnp.exp(dmat[c, h] - m_row) * sim[c, h]
        aq = jnp.exp(il - m_row) * q_s[rows(c), h * DK:(h + 1) * DK]
        lhs[c, h] = jnp.concatenate([aq.astype(BF16), wts.astype(BF16)], axis=1)
        emr[c, h] = jnp.exp(-m_row)
    fill(1)
    for (c, h) in pairs:
        upd[c, h] = _dot_tn(wk_s[rows(c), h * DK:(h + 1) * DK], vo_s[rows(c), h * CN:(h + 1) * CN])
    fill(1)
    res = {}
    for h in range(HEADS):
        state = cn_ref[h]
        for c in range(nc):
            rhs = jnp.concatenate([state.astype(BF16), vo_s[rows(c), h * CN:(h + 1) * CN]], axis=0)
            res[c, h] = _dot(lhs[c, h], rhs)
            a = a_chunk[c][h:h + 1, :]
            state = jnp.concatenate([a] * (CN // LANES), axis=1) * state + upd[c, h]
        cn_ref[h] = state
        fill(1)
    for (c, h) in pairs:
        inv = 1.0 / jnp.maximum(jnp.abs(res[c, h][:, DV:]), emr[c, h])
        hm_s[rows(c), h * DV:h * DV + LANES] = res[c, h][:, :LANES] * inv
        hm_s[rows(c), h * DV + LANES:(h + 1) * DV] = res[c, h][:, LANES:DV] * inv
    fill(1)

    finish_groups(4)
    att, gupd = {}, {}
    zblk = jnp.zeros((CHUNK, DK), BF16)
    for (c, h) in pairs:
        dsl = slice(h * DK, (h + 1) * DK)
        ra, rb = half(c, 0), half(c, 1)
        qp = jnp.concatenate([jnp.concatenate([gq64_s[ra, dsl], zblk], axis=1),
                              jnp.concatenate([zblk, gq64_s[rb, dsl]], axis=1)], axis=0)
        kp = jnp.concatenate([jnp.concatenate([gki_s[ra, dsl], gke64_s[ra, dsl]], axis=1),
                              jnp.concatenate([zblk, gki_s[rb, dsl]], axis=1)], axis=0)
        att[c, h] = jnp.where(causal, _dot_nt(qp, kp), 0.0).astype(BF16)
    fill(1)
    for (c, h) in pairs:
        gupd[c, h] = _dot_tn(gke128_s[rows(c), h * DK:(h + 1) * DK], gv_s[rows(c), h * DV:(h + 1) * DV])
    fill(1)
    for h in range(HEADS):
        sst = s_ref[h]
        for c in range(nc):
            lhs_g = jnp.concatenate([gq128_s[rows(c), h * DK:(h + 1) * DK], att[c, h]], axis=1)
            rhs_g = jnp.concatenate([sst.astype(BF16), gv_s[rows(c), h * DV:(h + 1) * DV]], axis=0)
            hg_s[rows(c), h * DV:(h + 1) * DV] = _dot(lhs_g, rhs_g)
            i = c * HEADS + h
            sst = e_cols[:, i:i + 1] * sst + gupd[c, h]
        s_ref[h] = sst
        fill(1)

    if meta:
        h1_ref[...] = x
        return

    def head_norm(src, gain_ref):
        outs = []
        for h in range(HEADS):
            t = src[:, h * DV:(h + 1) * DV]
            outs.append(t * lax.rsqrt(jnp.mean(t * t, axis=-1, keepdims=True) + EPS) * gain_ref[h:h + 1, :])
        return jnp.concatenate(outs, axis=-1)

    finish_groups(len(groups))
    y_m = head_norm(hm_s, mhg_ref) * ga_s[...]
    merged = gb_s[...] * _dot(y_m.astype(BF16), merge_w[0][...])
    y_g = head_norm(hg_s, ghg_ref) * gc_s[...]
    merged = merged + gd_s[...] * _dot(y_g.astype(BF16), merge_w[1][...])
    h1_ref[...] = x + _dot(merged.astype(BF16), merge_w[2][...])


def _const_spec(shape):
    nd = len(shape)
    return pl.BlockSpec(shape, lambda i: (0,) * nd, pipeline_mode=pl.Buffered(1))


def _mixer_call(x2, consts, state, *, tm, meta):
    t = tm if meta else x2.shape[0]
    state_shapes = [jax.ShapeDtypeStruct(a.shape, F32) for a in state]
    row_spec = pl.BlockSpec((tm, D_MODEL), lambda i: (i, 0))
    x_spec = pl.BlockSpec(x2.shape, lambda i: (0, 0)) if meta else row_spec
    state_specs = [pl.BlockSpec(a.shape, lambda i, nd=len(a.shape): (0,) * nd) for a in state]
    hbm_operands = (1,) if meta else LEAD_UNUSED
    packed = [jax.ShapeDtypeStruct((D_MODEL, N_PACK), BF16),
              jax.ShapeDtypeStruct((2 * SUBLANES + G_RANK, D_MODEL), BF16)] if meta else []
    return pl.pallas_call(
        functools.partial(_mixer_kernel, tm=tm, meta=meta),
        grid=(t // tm,),
        in_specs=[x_spec] + [pl.BlockSpec(memory_space=pl.ANY) if i in hbm_operands else _const_spec(c.shape)
                             for i, c in enumerate(consts)] + ([] if meta else state_specs),
        out_specs=[row_spec] + state_specs + [pl.BlockSpec(a.shape, lambda i: (0, 0)) for a in packed],
        out_shape=[jax.ShapeDtypeStruct((t, D_MODEL), F32)] + state_shapes + packed,
        scratch_shapes=[
            pltpu.VMEM((tm, D_MODEL), BF16),
            pltpu.VMEM((tm + SUBLANES, 2 * QK), F32),
            pltpu.VMEM((tm, QK), F32),
            pltpu.VMEM((tm, QK), BF16),
            pltpu.VMEM((tm, QK), BF16),
            pltpu.VMEM((tm, HEADS * CN), BF16),
            pltpu.VMEM((tm, QK), BF16),
            pltpu.VMEM((tm, QK), BF16),
            pltpu.VMEM((tm, QK), BF16),
            pltpu.VMEM((tm, QK), BF16),
            pltpu.VMEM((tm, QK), BF16),
            pltpu.VMEM((tm, D_MODEL), BF16),
            pltpu.VMEM((tm, D_MODEL), F32),
            pltpu.VMEM((tm, D_MODEL), F32),
            pltpu.VMEM((tm, D_MODEL), F32),
            pltpu.VMEM((tm, D_MODEL), F32),
            pltpu.VMEM((tm, D_MODEL), F32),
            pltpu.VMEM((tm, D_MODEL), F32),
            pltpu.VMEM((tm, D_MODEL), F32),
            pltpu.VMEM((RCHUNK, RCHUNK), BF16),
            pltpu.VMEM((tm, tm), BF16),
            pltpu.VMEM((LANES, 2 * HEADS * LANES), BF16),
        ] + ([
            pltpu.VMEM((REPACK_SLOTS, PIECE, D_MODEL), F32),
            pltpu.VMEM((SUBLANES + G_RANK, D_MODEL), F32),
            pltpu.SemaphoreType.DMA((REPACK_SLOTS,)),
            pltpu.SemaphoreType.DMA((2,)),
        ] if meta else [
            pltpu.VMEM((D_MODEL, D_MODEL), BF16),
            pltpu.VMEM((D_MODEL, D_MODEL), BF16),
            pltpu.VMEM((D_MODEL, D_MODEL), BF16),
            pltpu.VMEM((REPACK_SLOTS, MERGE_ROWS, D_MODEL), F32),
            pltpu.SemaphoreType.DMA((REPACK_SLOTS,)),
        ]),
        compiler_params=pltpu.CompilerParams(
            dimension_semantics=("arbitrary",), vmem_limit_bytes=VMEM_LIMIT),
        name="mixer_meta" if meta else "mixer",
    )(x2, *consts, *([] if meta else state))


def _ffn_kernel(h_ref, g2_ref, wg_hbm, wu_hbm, wd_hbm, gf_ref, o_ref,
                wg_ref, wu_ref, wd_ref, sg, su, sd, sem, *, fc, parts):
    n_chunks = wg_ref.shape[1] // fc

    def chunk_copies(f, slot):
        cols = pl.ds(f * fc, fc)
        return (pltpu.make_async_copy(wg_hbm.at[:, cols], sg.at[slot], sem.at[slot, 0]),
                pltpu.make_async_copy(wu_hbm.at[:, cols], su.at[slot], sem.at[slot, 1]),
                pltpu.make_async_copy(wd_hbm.at[cols, :], sd.at[slot], sem.at[slot, 2]))

    @pl.when(pl.program_id(0) == 0)
    def _():
        n_slots = sg.shape[0]
        for f in range(n_slots - 1):
            for k, cp in enumerate(chunk_copies(f, f)):
                cp.start(priority=(f + k) % DMA_PRIORITIES)
        for f in range(n_chunks):
            slot = f % n_slots
            if f + n_slots - 1 < n_chunks:
                for k, cp in enumerate(chunk_copies(f + n_slots - 1, (f + n_slots - 1) % n_slots)):
                    cp.start(priority=(f + k) % DMA_PRIORITIES)
            for cp in chunk_copies(f, slot):
                cp.wait()
            wg_ref[:, f * fc:(f + 1) * fc] = sg[slot].astype(BF16)
            wu_ref[:, f * fc:(f + 1) * fc] = su[slot].astype(BF16)
            wd_ref[f * fc:(f + 1) * fc, :] = sd[slot].astype(BF16)

    rows = h_ref.shape[0] // parts
    blocks = [slice(p * rows, (p + 1) * rows) for p in range(parts)]
    acc, hn = [], []
    for blk in blocks:
        h = h_ref[blk, :]
        hn.append((h * lax.rsqrt(jnp.mean(h * h, axis=-1, keepdims=True) + EPS) * g2_ref[...]).astype(BF16))
        acc.append(h)
    for f in range(0, wg_ref.shape[1], fc):
        for p in range(parts):
            gate = _dot(hn[p], wg_ref[:, f:f + fc])
            up = _dot(hn[p], wu_ref[:, f:f + fc])
            acc[p] = acc[p] + _dot((gate * _sigmoid(gate) * up).astype(BF16), wd_ref[f:f + fc, :])
    for p, blk in enumerate(blocks):
        a = acc[p]
        o_ref[blk, :] = a * lax.rsqrt(jnp.mean(a * a, axis=-1, keepdims=True) + EPS) * gf_ref[...]


def _ffn_call(h1, g2, wg, wu, wd, gf, *, tm, fc, parts):
    t = h1.shape[0]
    d_ff = wg.shape[1]
    row_spec = pl.BlockSpec((tm, D_MODEL), lambda i: (i, 0))
    hbm_spec = pl.BlockSpec(memory_space=pl.ANY)
    return pl.pallas_call(
        functools.partial(_ffn_kernel, fc=fc, parts=parts),
        grid=(t // tm,),
        in_specs=[row_spec, _const_spec(g2.shape), hbm_spec, hbm_spec, hbm_spec, _const_spec(gf.shape)],
        out_specs=row_spec,
        out_shape=jax.ShapeDtypeStruct((t, D_MODEL), F32),
        scratch_shapes=[
            pltpu.VMEM((D_MODEL, d_ff), BF16),
            pltpu.VMEM((D_MODEL, d_ff), BF16),
            pltpu.VMEM((d_ff, D_MODEL), BF16),
            pltpu.VMEM((FFN_SLOTS, D_MODEL, fc), F32),
            pltpu.VMEM((FFN_SLOTS, D_MODEL, fc), F32),
            pltpu.VMEM((FFN_SLOTS, fc, D_MODEL), F32),
            pltpu.SemaphoreType.DMA((FFN_SLOTS, 3)),
        ],
        compiler_params=pltpu.CompilerParams(
            dimension_semantics=("arbitrary",), vmem_limit_bytes=VMEM_LIMIT),
        name="ffn",
    )(h1, g2, wg, wu, wd, gf)


def kernel(x, meta_tokens, norm1_g, w_in, conv_w, conv_b, m_gate_b, g_a2, g_a2_b, m_head_g, g_head_g,
           w_branch_m, w_branch_g, w_out, norm2_g, w_ff_gate, w_ff_up, w_ff_down, final_g):
    bsz, seq, d = x.shape
    assert bsz == 1 and d == D_MODEL and norm1_g.shape[0] == 1 and seq % CHUNK == 0
    row = lambda a: a.reshape(1, -1).astype(F32)

    gate_bias = jnp.zeros((2 * SUBLANES, LANES), F32)
    gate_bias = gate_bias.at[:HEADS].set(m_gate_b[0, 0][:, None]).at[SUBLANES:SUBLANES + HEADS].set(m_gate_b[0, 1][:, None])
    a2 = jnp.zeros((LANES, QK), F32).at[GA_LANE:GA_LANE + G_RANK].set(g_a2[0]).astype(BF16)
    conv_w8 = jnp.zeros((SUBLANES, 2 * QK), F32).at[:CONV_W].set(conv_w[0])
    consts = (gate_bias, a2, row(g_a2_b[0]), conv_w8, row(conv_b[0]), m_head_g[0].astype(F32), g_head_g[0].astype(F32),
              w_branch_m[0].astype(F32), w_branch_g[0].astype(F32), w_out[0].astype(F32))

    state_shapes = (jax.ShapeDtypeStruct((SUBLANES, 2 * QK), F32), jax.ShapeDtypeStruct((HEADS, DK, CN), F32),
                    jax.ShapeDtypeStruct((SUBLANES, LANES), F32), jax.ShapeDtypeStruct((HEADS, DK, DV), F32))
    unused = jnp.zeros((SUBLANES, LANES), BF16)
    g1 = row(norm1_g[0])
    lead_consts = (g1, w_in[0].T.astype(F32), unused) + consts[:-len(LEAD_UNUSED)] + (unused,) * len(LEAD_UNUSED)
    _, *state, w_packed, w_narrow = _mixer_call(meta_tokens.astype(F32), lead_consts, state_shapes, tm=RCHUNK,
                                                meta=True)
    h1, *_ = _mixer_call(x[0], (g1, w_packed, w_narrow) + consts, tuple(state), tm=512, meta=False)
    out = _ffn_call(h1, row(norm2_g[0]), w_ff_gate[0].astype(F32), w_ff_up[0].astype(F32),
                    w_ff_down[0].astype(F32), row(final_g), tm=1024, fc=256, parts=2)
    return out[None]
```

```python
import functools

import jax
import jax.numpy as jnp
from jax import lax
from jax.experimental import pallas as pl
from jax.experimental.pallas import tpu as pltpu

F32 = jnp.float32
BF16 = jnp.bfloat16

D_MODEL = 1024
N_META = 16
CHUNK = 64
RCHUNK = 128
EPS = 1e-6
HEADS = 4
DV = D_MODEL // HEADS
DK = DV // 2
QK = HEADS * DK
G_RANK = 16
G_TAU = 16.0
CONV_W = 4
LANES = 128
SUBLANES = 8
NEG_BIG = -1e30
VMEM_LIMIT = 60 * 1024 * 1024
CN = DV + LANES

W_IN_RANGES = ((0, 2048), (2056, 5128), (5144, 8216))
N_PACK = sum(b - a for a, b in W_IN_RANGES)
O_MQK, O_MV, O_MO = 0, 1024, 2048
O_GQ, O_GV = 3072, 4096
O_GR, O_GATE_M, O_GATE_G = 5120, 6144, 7168
MI_COL, GA_COL = 2048, 5128
GA_LANE = 8
PIECE = 512
MERGE_ROWS = 256
REPACK_AHEAD = 3
REPACK_SLOTS = REPACK_AHEAD + 1
FFN_SLOTS = 3
LEAD_UNUSED = (10, 11, 12)

NT_DIMS = (((1,), (1,)), ((), ()))
TN_DIMS = (((0,), (0,)), ((), ()))


def _dot(a, b):
    return jnp.dot(a, b, preferred_element_type=F32)


def _dot_nt(a, b):
    return lax.dot_general(a, b, NT_DIMS, preferred_element_type=F32)


def _dot_tn(a, b):
    return lax.dot_general(a, b, TN_DIMS, preferred_element_type=F32)


def _pieces(x):
    hi = x.astype(BF16).astype(F32)
    mid = (x - hi).astype(BF16).astype(F32)
    lo = (x - hi - mid).astype(BF16).astype(F32)
    return [hi, mid, lo]


def _log_sigmoid(x):
    return jnp.minimum(x, 0.0) - jnp.log(1.0 + jnp.exp(-jnp.abs(x)))


def _sigmoid(x):
    return 0.5 * jnp.tanh(0.5 * x) + 0.5


def _repack_w_in(wt_hbm, w_ref, wst_ref, stage, narrow, sem, nsem):
    (_, a_end), (b_start, b_end), (c_start, _) = W_IN_RANGES
    first_b, first_c = a_end // PIECE, (a_end + b_end - b_start) // PIECE
    n_pieces = N_PACK // PIECE

    def piece_copy(q, slot):
        row0 = q * PIECE + (b_start - a_end if q >= first_b else 0) + (c_start - b_end if q >= first_c else 0)
        return pltpu.make_async_copy(wt_hbm.at[pl.ds(row0, PIECE), :], stage.at[slot], sem.at[slot])

    narrow_copies = (
        pltpu.make_async_copy(wt_hbm.at[pl.ds(MI_COL, SUBLANES), :], narrow.at[pl.ds(0, SUBLANES), :], nsem.at[0]),
        pltpu.make_async_copy(wt_hbm.at[pl.ds(GA_COL, G_RANK), :], narrow.at[pl.ds(SUBLANES, G_RANK), :], nsem.at[1]))

    for q in range(REPACK_AHEAD):
        piece_copy(q, q).start()
    for cp in narrow_copies:
        cp.start()
    for q in range(n_pieces):
        if q + REPACK_AHEAD < n_pieces:
            piece_copy(q + REPACK_AHEAD, (q + REPACK_AHEAD) % REPACK_SLOTS).start()
        piece_copy(q, q % REPACK_SLOTS).wait()
        w_ref[:, q * PIECE:(q + 1) * PIECE] = stage[q % REPACK_SLOTS].T.astype(BF16)
    for cp in narrow_copies:
        cp.wait()
    gates = narrow[0:SUBLANES, :]
    zeros = jnp.zeros((SUBLANES - HEADS, D_MODEL), F32)
    wst_ref[...] = jnp.concatenate([gates[:HEADS], zeros, gates[HEADS:], zeros, narrow[SUBLANES:, :]],
                                   axis=0).astype(BF16)


def _mixer_kernel(x_ref, g1_ref, w_ref, wst_ref, sb_ref, a2_ref, a2b_ref, cw_ref, cb_ref,
                  mhg_ref, ghg_ref, wbm_ref, wbg_ref, wout_ref, *rest, tm, meta):
    nc = tm // RCHUNK
    pairs = [(c, h) for c in range(nc) for h in range(HEADS)]
    if not meta:
        tail0_ref, cn0_ref, m0_ref, s0_ref, *rest = rest
    h1_ref, tail_ref, cn_ref, m_ref, s_ref, *rest = rest
    if meta:
        wt_hbm = w_ref
        w_ref, wst_ref, *rest = rest
        *rest, stage, narrow, sem, nsem = rest
        _repack_w_in(wt_hbm, w_ref, wst_ref, stage, narrow, sem, nsem)
    else:
        *rest, wbm_s, wbg_s, wout_s, wstage, wsem = rest
        merge_w = (wbm_s, wbg_s, wout_s)
    (xn_s, qk_s, q_s, k_s, wk_s, vo_s, gq64_s, gq128_s, gki_s, gke64_s, gke128_s, gv_s, hm_s, hg_s,
     ga_s, gb_s, gc_s, gd_s, gqk_s, tric_s, triu_s, spread_s) = rest

    @pl.when(pl.program_id(0) == 0)
    def _():
        if meta:
            qk_s[0:SUBLANES, :] = jnp.zeros((SUBLANES, 2 * QK), F32)
            cn_ref[...] = jnp.zeros(cn_ref.shape, F32)
            m_ref[...] = jnp.zeros(m_ref.shape, F32)
            s_ref[...] = jnp.zeros(s_ref.shape, F32)
        else:
            qk_s[0:SUBLANES, :] = tail0_ref[...]
            cn_ref[...] = cn0_ref[...]
            m_ref[...] = m0_ref[...]
            s_ref[...] = s0_ref[...]
        for h in range(HEADS):
            vo_s[:, h * CN + DV:(h + 1) * CN] = jnp.ones((tm, LANES), BF16)
        if not meta:
            srcs = (wbm_ref, wbg_ref, wout_ref)
            n_blk = D_MODEL // MERGE_ROWS

            def block_copy(j, slot):
                return pltpu.make_async_copy(srcs[j // n_blk].at[pl.ds((j % n_blk) * MERGE_ROWS, MERGE_ROWS), :],
                                             wstage.at[slot], wsem.at[slot])

            for j in range(REPACK_AHEAD):
                block_copy(j, j).start()
            for j in range(len(srcs) * n_blk):
                slot = j % REPACK_SLOTS
                if j + REPACK_AHEAD < len(srcs) * n_blk:
                    block_copy(j + REPACK_AHEAD, (j + REPACK_AHEAD) % REPACK_SLOTS).start()
                block_copy(j, slot).wait()
                r0 = (j % n_blk) * MERGE_ROWS
                merge_w[j // n_blk][r0:r0 + MERGE_ROWS, :] = wstage[slot].astype(BF16)
        row = lax.broadcasted_iota(jnp.int32, (tm, tm), 0)
        col = lax.broadcasted_iota(jnp.int32, (tm, tm), 1)
        tric_s[...] = jnp.where((col <= row) & (col >= (row & -CHUNK)), 1.0, 0.0)[:RCHUNK, :RCHUNK].astype(BF16)
        triu_s[...] = jnp.where((row <= col) & (row >= (col & -RCHUNK)), 1.0, 0.0).astype(BF16)
        sr = lax.broadcasted_iota(jnp.int32, (LANES, 2 * HEADS * LANES), 0)
        sc = lax.broadcasted_iota(jnp.int32, (LANES, 2 * HEADS * LANES), 1)
        blk = (sc >> 9) * (3 * SUBLANES)
        spread_s[...] = jnp.where(
            (sr >= blk) & (sr < blk + 3 * SUBLANES) & ((sr & (SUBLANES - 1)) == ((sc >> 7) & (HEADS - 1))),
            1.0, 0.0).astype(BF16)

    if meta:
        x = jnp.concatenate([jnp.zeros((tm - N_META, D_MODEL), F32), x_ref[...]], axis=0)
    else:
        x = x_ref[...]
    xn = x * lax.rsqrt(jnp.mean(x * x, axis=-1, keepdims=True) + EPS) * g1_ref[...]
    if meta:
        valid = lax.broadcasted_iota(jnp.int32, (tm, 1), 0) >= (tm - N_META)
        xn = jnp.where(valid, xn, 0.0)
    xn_s[...] = xn.astype(BF16)

    def proj(off, width):
        return _dot(xn_s[...], w_ref[:, off:off + width])

    def rows(c):
        return slice(c * RCHUNK, (c + 1) * RCHUNK)

    def half(c, second):
        lo = c * RCHUNK + (CHUNK if second else 0)
        return slice(lo, lo + CHUNK)

    crow = lax.broadcasted_iota(jnp.int32, (RCHUNK, RCHUNK), 0)
    ccol = lax.broadcasted_iota(jnp.int32, (RCHUNK, RCHUNK), 1)
    causal = ccol <= crow
    tri_c = tric_s[...]
    tri_up = triu_s[...]
    spread = spread_s[...]

    def st_qk(i, v):
        qk_s[SUBLANES:SUBLANES + tm, i * DV:(i + 1) * DV] = v

    def st_mv(i, v):
        vo_s[:, i * CN:i * CN + DV] = v.astype(BF16)

    def st_gqk(i, v):
        gqk_s[:, i * DV:(i + 1) * DV] = v

    def st_gv(i, v):
        gv_s[:, i * DV:(i + 1) * DV] = v.astype(BF16)

    def st_mo(i, v):
        ga_s[:, i * DV:(i + 1) * DV] = _sigmoid(v)

    def st_gm(i, v):
        gb_s[:, i * DV:(i + 1) * DV] = _sigmoid(v)

    def st_gr(i, v):
        gc_s[:, i * DV:(i + 1) * DV] = v * _sigmoid(v)

    def st_gg(i, v):
        gd_s[:, i * DV:(i + 1) * DV] = _sigmoid(v)

    groups = [(O_MQK, st_qk), (O_MV, st_mv), (O_GQ, st_gqk), (O_GV, st_gv)]
    if not meta:
        groups += [(O_MO, st_mo), (O_GATE_M, st_gm), (O_GR, st_gr), (O_GATE_G, st_gg)]
    queue = [(seg, i, store) for seg, store in groups for i in range(D_MODEL // DV)]
    emitted = [0]

    def fill(n):
        for _ in range(n):
            if emitted[0] < len(queue):
                seg, i, store = queue[emitted[0]]
                store(i, proj(seg + i * DV, DV))
                emitted[0] += 1

    def finish_groups(n):
        fill(n * (D_MODEL // DV) - emitted[0])

    narrow = _dot_nt(wst_ref[...], xn_s[...])
    gates = narrow[:2 * SUBLANES, :] + jnp.concatenate([sb_ref[...]] * (tm // LANES), axis=1)
    fill(2)
    s1 = jnp.concatenate([jnp.zeros((GA_LANE, tm), F32), narrow[2 * SUBLANES:, :],
                          jnp.zeros((LANES - GA_LANE - G_RANK, tm), F32)], axis=0).T
    logi = gates[:SUBLANES, :]
    logf = _log_sigmoid(gates[SUBLANES:, :])
    if meta:
        valid_t = lax.broadcasted_iota(jnp.int32, (1, tm), 1) >= (tm - N_META)
        logi = jnp.where(valid_t, logi, NEG_BIG)
        logf = jnp.where(valid_t, logf, 0.0)
    zero8 = jnp.zeros((SUBLANES, tm), F32)
    b4 = _dot(jnp.concatenate(_pieces(logf) + [zero8], axis=0).astype(BF16), tri_up)
    fill(2)
    b_all = b4[:SUBLANES] + b4[SUBLANES:2 * SUBLANES] + b4[2 * SUBLANES:3 * SUBLANES]
    c_all = logi - b_all
    m_run = m_ref[...]
    m_in, w_parts, a_chunk = [], [], []
    for c in range(nc):
        b_c = b_all[:, rows(c)]
        g_c = jnp.broadcast_to(b_c[:, RCHUNK - 1:RCHUNK], (SUBLANES, LANES))
        wlog = g_c + c_all[:, rows(c)]
        m_next = jnp.maximum(g_c + m_run, jnp.max(wlog, axis=1, keepdims=True))
        m_in.append(m_run)
        w_parts.append(jnp.exp(wlog - m_next))
        a_chunk.append(jnp.exp(g_c + m_run - m_next))
        m_run = m_next
    m_ref[...] = m_run
    stacked = jnp.concatenate(
        _pieces(b_all) + _pieces(jnp.concatenate(w_parts, axis=1))
        + [jnp.zeros((LANES - 6 * SUBLANES, tm), F32)], axis=0)
    stacked_t = stacked.T.astype(BF16)
    fill(2)
    tiles = _dot(stacked_t, spread)
    fill(2)
    b_t, w_t = tiles[:, :QK], tiles[:, QK:]

    finish_groups(1)
    for blk in range(2 * QK // DV):
        csl = slice(blk * DV, (blk + 1) * DV)
        conv = cb_ref[:, csl]
        for j in range(CONV_W):
            conv = conv + cw_ref[j:j + 1, csl] * qk_s[pl.ds(SUBLANES - (CONV_W - 1) + j, tm), csl]
        conv = conv * _sigmoid(conv)
        if blk < QK // DV:
            q_s[:, csl] = conv * (DK ** -0.5)
        else:
            ksl = slice(blk * DV - QK, (blk + 1) * DV - QK)
            k_s[:, ksl] = conv.astype(BF16)
            wk_s[:, ksl] = (w_t[:, ksl] * conv).astype(BF16)
        fill(1)
    tail_new = qk_s[tm:tm + SUBLANES, :]
    qk_s[0:SUBLANES, :] = tail_new
    tail_ref[...] = tail_new

    za = _dot(s1.astype(BF16), a2_ref[...]) + a2b_ref[...]
    fill(1)
    loga = _log_sigmoid(za) / G_TAU
    if meta:
        loga = jnp.where(valid, loga, 0.0)
    hi = loga.astype(BF16)
    r1 = loga - hi.astype(F32)
    mid = r1.astype(BF16)
    bc = jnp.concatenate([_dot(tri_c, hi[rows(c), :]) + _dot(tri_c, mid[rows(c), :])
                          for c in range(nc)], axis=0)
    fill(2)
    e_blk = jnp.exp(bc)
    tot_a = [bc[c * RCHUNK + CHUNK - 1:c * RCHUNK + CHUNK, :] for c in range(nc)]
    tot_b = [bc[(c + 1) * RCHUNK - 1:(c + 1) * RCHUNK, :] for c in range(nc)]
    e_a = [jnp.exp(t) for t in tot_a]
    e_b = [jnp.exp(t) for t in tot_b]
    rest = jnp.concatenate([t - bc[half(c, s), :] for c in range(nc) for s, t in ((0, tot_a[c]), (1, tot_b[c]))],
                           axis=0)
    e_rest = jnp.exp(rest)
    ones_row = jnp.ones((1, QK), F32)
    to_chunk = jnp.concatenate([jnp.broadcast_to(f, (CHUNK, QK)) for c in range(nc) for f in (ones_row, e_a[c])],
                               axis=0)
    from_blk = jnp.concatenate([jnp.broadcast_to(f, (CHUNK, QK)) for c in range(nc) for f in (e_b[c], ones_row)],
                               axis=0)
    finish_groups(3)
    gq = gqk_s[:, :QK] * (DK ** -0.5)
    gk = gqk_s[:, QK:]
    gq64_s[...] = (gq * e_blk).astype(BF16)
    gq128_s[...] = (gq * e_blk * to_chunk).astype(BF16)
    gki_s[...] = (gk * jnp.exp(-bc)).astype(BF16)
    gke64_s[...] = (gk * e_rest).astype(BF16)
    gke128_s[...] = (gk * e_rest * from_blk).astype(BF16)
    fill(2)
    e_rows = [(e_a[c] * e_b[c])[:, h * DK:(h + 1) * DK] for (c, h) in pairs]
    e_cols = jnp.concatenate(e_rows + [jnp.zeros((LANES - len(pairs), DK), F32)], axis=0).T

    finish_groups(2)
    dmat, rmax, sim = {}, {}, {}
    for (c, h) in pairs:
        d = b_t[rows(c), h * LANES:(h + 1) * LANES] + c_all[h:h + 1, rows(c)]
        d = jnp.where(causal, d, -jnp.inf)
        dmat[c, h] = d
        rmax[c, h] = jnp.max(d, axis=-1, keepdims=True)
    fill(1)
    for (c, h) in pairs:
        sim[c, h] = _dot_nt(q_s[rows(c), h * DK:(h + 1) * DK].astype(BF16), k_s[rows(c), h * DK:(h + 1) * DK])
    fill(1)
    lhs, emr, upd = {}, {}, {}
    for (c, h) in pairs:
        il = b_t[rows(c), h * LANES:(h + 1) * LANES] + m_in[c][h:h + 1, :]
        m_row = jnp.maximum(il, rmax[c, h])
        wts = jnp.exp(dmat[c, h] - m_row) * sim[c, h]
        aq = jnp.exp(il - m_row) * q_s[rows(c), h * DK:(h + 1) * DK]
        lhs[c, h] = jnp.concatenate([aq.astype(BF16), wts.astype(BF16)], axis=1)
        emr[c, h] = jnp.exp(-m_row)
    fill(1)
    for (c, h) in pairs:
        upd[c, h] = _dot_tn(wk_s[rows(c), h * DK:(h + 1) * DK], vo_s[rows(c), h * CN:(h + 1) * CN])
    fill(1)
    res = {}
    for h in range(HEADS):
        state = cn_ref[h]
        for c in range(nc):
            rhs = jnp.concatenate([state.astype(BF16), vo_s[rows(c), h * CN:(h + 1) * CN]], axis=0)
            res[c, h] = _dot(lhs[c, h], rhs)
            a = a_chunk[c][h:h + 1, :]
            state = jnp.concatenate([a] * (CN // LANES), axis=1) * state + upd[c, h]
        cn_ref[h] = state
        fill(1)
    for (c, h) in pairs:
        inv = 1.0 / jnp.maximum(jnp.abs(res[c, h][:, DV:]), emr[c, h])
        hm_s[rows(c), h * DV:h * DV + LANES] = res[c, h][:, :LANES] * inv
        hm_s[rows(c), h * DV + LANES:(h + 1) * DV] = res[c, h][:, LANES:DV] * inv
    fill(1)

    finish_groups(4)
    att, gupd = {}, {}
    zblk = jnp.zeros((CHUNK, DK), BF16)
    for (c, h) in pairs:
        dsl = slice(h * DK, (h + 1) * DK)
        ra, rb = half(c, 0), half(c, 1)
        qp = jnp.concatenate([jnp.concatenate([gq64_s[ra, dsl], zblk], axis=1),
                              jnp.concatenate([zblk, gq64_s[rb, dsl]], axis=1)], axis=0)
        kp = jnp.concatenate([jnp.concatenate([gki_s[ra, dsl], gke64_s[ra, dsl]], axis=1),
                              jnp.concatenate([zblk, gki_s[rb, dsl]], axis=1)], axis=0)
        att[c, h] = jnp.where(causal, _dot_nt(qp, kp), 0.0).astype(BF16)
    fill(1)
    for (c, h) in pairs:
        gupd[c, h] = _dot_tn(gke128_s[rows(c), h * DK:(h + 1) * DK], gv_s[rows(c), h * DV:(h + 1) * DV])
    fill(1)
    for h in range(HEADS):
        sst = s_ref[h]
        for c in range(nc):
            lhs_g = jnp.concatenate([gq128_s[rows(c), h * DK:(h + 1) * DK], att[c, h]], axis=1)
            rhs_g = jnp.concatenate([sst.astype(BF16), gv_s[rows(c), h * DV:(h + 1) * DV]], axis=0)
            hg_s[rows(c), h * DV:(h + 1) * DV] = _dot(lhs_g, rhs_g)
            i = c * HEADS + h
            sst = e_cols[:, i:i + 1] * sst + gupd[c, h]
        s_ref[h] = sst
        fill(1)

    if meta:
        h1_ref[...] = x
        return

    def head_norm(src, gain_ref):
        outs = []
        for h in range(HEADS):
            t = src[:, h * DV:(h + 1) * DV]
            outs.append(t * lax.rsqrt(jnp.mean(t * t, axis=-1, keepdims=True) + EPS) * gain_ref[h:h + 1, :])
        return jnp.concatenate(outs, axis=-1)

    finish_groups(len(groups))
    y_m = head_norm(hm_s, mhg_ref) * ga_s[...]
    merged = gb_s[...] * _dot(y_m.astype(BF16), merge_w[0][...])
    y_g = head_norm(hg_s, ghg_ref) * gc_s[...]
    merged = merged + gd_s[...] * _dot(y_g.astype(BF16), merge_w[1][...])
    h1_ref[...] = x + _dot(merged.astype(BF16), merge_w[2][...])


def _const_spec(shape):
    nd = len(shape)
    return pl.BlockSpec(shape, lambda i: (0,) * nd, pipeline_mode=pl.Buffered(1))


def _mixer_call(x2, consts, state, *, tm, meta):
    t = tm if meta else x2.shape[0]
    state_shapes = [jax.ShapeDtypeStruct(a.shape, F32) for a in state]
    row_spec = pl.BlockSpec((tm, D_MODEL), lambda i: (i, 0))
    x_spec = pl.BlockSpec(x2.shape, lambda i: (0, 0)) if meta else row_spec
    state_specs = [pl.BlockSpec(a.shape, lambda i, nd=len(a.shape): (0,) * nd) for a in state]
    hbm_operands = (1,) if meta else LEAD_UNUSED
    packed = [jax.ShapeDtypeStruct((D_MODEL, N_PACK), BF16),
              jax.ShapeDtypeStruct((2 * SUBLANES + G_RANK, D_MODEL), BF16)] if meta else []
    return pl.pallas_call(
        functools.partial(_mixer_kernel, tm=tm, meta=meta),
        grid=(t // tm,),
        in_specs=[x_spec] + [pl.BlockSpec(memory_space=pl.ANY) if i in hbm_operands else _const_spec(c.shape)
                             for i, c in enumerate(consts)] + ([] if meta else state_specs),
        out_specs=[row_spec] + state_specs + [pl.BlockSpec(a.shape, lambda i: (0, 0)) for a in packed],
        out_shape=[jax.ShapeDtypeStruct((t, D_MODEL), F32)] + state_shapes + packed,
        scratch_shapes=[
            pltpu.VMEM((tm, D_MODEL), BF16),
            pltpu.VMEM((tm + SUBLANES, 2 * QK), F32),
            pltpu.VMEM((tm, QK), F32),
            pltpu.VMEM((tm, QK), BF16),
            pltpu.VMEM((tm, QK), BF16),
            pltpu.VMEM((tm, HEADS * CN), BF16),
            pltpu.VMEM((tm, QK), BF16),
            pltpu.VMEM((tm, QK), BF16),
            pltpu.VMEM((tm, QK), BF16),
            pltpu.VMEM((tm, QK), BF16),
            pltpu.VMEM((tm, QK), BF16),
            pltpu.VMEM((tm, D_MODEL), BF16),
            pltpu.VMEM((tm, D_MODEL), F32),
            pltpu.VMEM((tm, D_MODEL), F32),
            pltpu.VMEM((tm, D_MODEL), F32),
            pltpu.VMEM((tm, D_MODEL), F32),
            pltpu.VMEM((tm, D_MODEL), F32),
            pltpu.VMEM((tm, D_MODEL), F32),
            pltpu.VMEM((tm, D_MODEL), F32),
            pltpu.VMEM((RCHUNK, RCHUNK), BF16),
            pltpu.VMEM((tm, tm), BF16),
            pltpu.VMEM((LANES, 2 * HEADS * LANES), BF16),
        ] + ([
            pltpu.VMEM((REPACK_SLOTS, PIECE, D_MODEL), F32),
            pltpu.VMEM((SUBLANES + G_RANK, D_MODEL), F32),
            pltpu.SemaphoreType.DMA((REPACK_SLOTS,)),
            pltpu.SemaphoreType.DMA((2,)),
        ] if meta else [
            pltpu.VMEM((D_MODEL, D_MODEL), BF16),
            pltpu.VMEM((D_MODEL, D_MODEL), BF16),
            pltpu.VMEM((D_MODEL, D_MODEL), BF16),
            pltpu.VMEM((REPACK_SLOTS, MERGE_ROWS, D_MODEL), F32),
            pltpu.SemaphoreType.DMA((REPACK_SLOTS,)),
        ]),
        compiler_params=pltpu.CompilerParams(
            dimension_semantics=("arbitrary",), vmem_limit_bytes=VMEM_LIMIT),
        name="mixer_meta" if meta else "mixer",
    )(x2, *consts, *([] if meta else state))


def _ffn_kernel(h_ref, g2_ref, wg_hbm, wu_hbm, wd_hbm, gf_ref, o_ref,
                wg_ref, wu_ref, wd_ref, sg, su, sd, sem, *, fc, parts):
    n_chunks = wg_ref.shape[1] // fc

    def chunk_copies(f, slot):
        cols = pl.ds(f * fc, fc)
        return (pltpu.make_async_copy(wg_hbm.at[:, cols], sg.at[slot], sem.at[slot, 0]),
                pltpu.make_async_copy(wu_hbm.at[:, cols], su.at[slot], sem.at[slot, 1]),
                pltpu.make_async_copy(wd_hbm.at[cols, :], sd.at[slot], sem.at[slot, 2]))

    @pl.when(pl.program_id(0) == 0)
    def _():
        n_slots = sg.shape[0]
        for f in range(n_slots - 1):
            for cp in chunk_copies(f, f):
                cp.start()
        for f in range(n_chunks):
            slot = f % n_slots
            if f + n_slots - 1 < n_chunks:
                for cp in chunk_copies(f + n_slots - 1, (f + n_slots - 1) % n_slots):
                    cp.start()
            for cp in chunk_copies(f, slot):
                cp.wait()
            wg_ref[:, f * fc:(f + 1) * fc] = sg[slot].astype(BF16)
            wu_ref[:, f * fc:(f + 1) * fc] = su[slot].astype(BF16)
            wd_ref[f * fc:(f + 1) * fc, :] = sd[slot].astype(BF16)

    rows = h_ref.shape[0] // parts
    blocks = [slice(p * rows, (p + 1) * rows) for p in range(parts)]
    acc, hn = [], []
    for blk in blocks:
        h = h_ref[blk, :]
        hn.append((h * lax.rsqrt(jnp.mean(h * h, axis=-1, keepdims=True) + EPS) * g2_ref[...]).astype(BF16))
        acc.append(h)
    for f in range(0, wg_ref.shape[1], fc):
        for p in range(parts):
            gate = _dot(hn[p], wg_ref[:, f:f + fc])
            up = _dot(hn[p], wu_ref[:, f:f + fc])
            acc[p] = acc[p] + _dot((gate * _sigmoid(gate) * up).astype(BF16), wd_ref[f:f + fc, :])
    for p, blk in enumerate(blocks):
        a = acc[p]
        o_ref[blk, :] = a * lax.rsqrt(jnp.mean(a * a, axis=-1, keepdims=True) + EPS) * gf_ref[...]


def _ffn_call(h1, g2, wg, wu, wd, gf, *, tm, fc, parts):
    t = h1.shape[0]
    d_ff = wg.shape[1]
    row_spec = pl.BlockSpec((tm, D_MODEL), lambda i: (i, 0))
    hbm_spec = pl.BlockSpec(memory_space=pl.ANY)
    return pl.pallas_call(
        functools.partial(_ffn_kernel, fc=fc, parts=parts),
        grid=(t // tm,),
        in_specs=[row_spec, _const_spec(g2.shape), hbm_spec, hbm_spec, hbm_spec, _const_spec(gf.shape)],
        out_specs=row_spec,
        out_shape=jax.ShapeDtypeStruct((t, D_MODEL), F32),
        scratch_shapes=[
            pltpu.VMEM((D_MODEL, d_ff), BF16),
            pltpu.VMEM((D_MODEL, d_ff), BF16),
            pltpu.VMEM((d_ff, D_MODEL), BF16),
            pltpu.VMEM((FFN_SLOTS, D_MODEL, fc), F32),
            pltpu.VMEM((FFN_SLOTS, D_MODEL, fc), F32),
            pltpu.VMEM((FFN_SLOTS, fc, D_MODEL), F32),
            pltpu.SemaphoreType.DMA((FFN_SLOTS, 3)),
        ],
        compiler_params=pltpu.CompilerParams(
            dimension_semantics=("arbitrary",), vmem_limit_bytes=VMEM_LIMIT),
        name="ffn",
    )(h1, g2, wg, wu, wd, gf)


def kernel(x, meta_tokens, norm1_g, w_in, conv_w, conv_b, m_gate_b, g_a2, g_a2_b, m_head_g, g_head_g,
           w_branch_m, w_branch_g, w_out, norm2_g, w_ff_gate, w_ff_up, w_ff_down, final_g):
    bsz, seq, d = x.shape
    assert bsz == 1 and d == D_MODEL and norm1_g.shape[0] == 1 and seq % CHUNK == 0
    row = lambda a: a.reshape(1, -1).astype(F32)

    gate_bias = jnp.zeros((2 * SUBLANES, LANES), F32)
    gate_bias = gate_bias.at[:HEADS].set(m_gate_b[0, 0][:, None]).at[SUBLANES:SUBLANES + HEADS].set(m_gate_b[0, 1][:, None])
    a2 = jnp.zeros((LANES, QK), F32).at[GA_LANE:GA_LANE + G_RANK].set(g_a2[0]).astype(BF16)
    conv_w8 = jnp.zeros((SUBLANES, 2 * QK), F32).at[:CONV_W].set(conv_w[0])
    consts = (gate_bias, a2, row(g_a2_b[0]), conv_w8, row(conv_b[0]), m_head_g[0].astype(F32), g_head_g[0].astype(F32),
              w_branch_m[0].astype(F32), w_branch_g[0].astype(F32), w_out[0].astype(F32))

    state_shapes = (jax.ShapeDtypeStruct((SUBLANES, 2 * QK), F32), jax.ShapeDtypeStruct((HEADS, DK, CN), F32),
                    jax.ShapeDtypeStruct((SUBLANES, LANES), F32), jax.ShapeDtypeStruct((HEADS, DK, DV), F32))
    unused = jnp.zeros((SUBLANES, LANES), BF16)
    g1 = row(norm1_g[0])
    lead_consts = (g1, w_in[0].T.astype(F32), unused) + consts[:-len(LEAD_UNUSED)] + (unused,) * len(LEAD_UNUSED)
    _, *state, w_packed, w_narrow = _mixer_call(meta_tokens.astype(F32), lead_consts, state_shapes, tm=RCHUNK,
                                                meta=True)
    h1, *_ = _mixer_call(x[0], (g1, w_packed, w_narrow) + consts, tuple(state), tm=512, meta=False)
    out = _ffn_call(h1, row(norm2_g[0]), w_ff_gate[0].astype(F32), w_ff_up[0].astype(F32),
                    w_ff_down[0].astype(F32), row(final_g), tm=1024, fc=256, parts=2)
    return out[None]
```
